```python
import math
import jax, jax.numpy as jnp
from jax import lax
import numpy as np

D_MODEL = 1024
BATCH = 8
SEQ = 2048
DEPTH = 1
DEC_BATCH = 128
DEC_SEQ = 1
PAST_LEN = 16384
PAGE_SIZE = 128

N_MEM = 256
RWKV_HEAD = 64
RWKV_HEADS = D_MODEL // RWKV_HEAD
RWKV_WIDTH = RWKV_HEADS * RWKV_HEAD
DECAY_LORA = 64
AAA_LORA = 64
GATE_LORA = 128
RWKV_PROJ = 3 * RWKV_WIDTH + DECAY_LORA + AAA_LORA + GATE_LORA
GN_EPS = 64e-5
LRU_WIDTH = D_MODEL
LRU_BLOCKS = 16
LRU_BLOCK = LRU_WIDTH // LRU_BLOCKS
CONV_WIDTH = 4
LRU_C = 8.0
N_BRANCH = 2
PROJ_WIDTH = RWKV_PROJ + 2 * LRU_WIDTH + N_BRANCH * D_MODEL
XA_HEADS = 4
XA_HEAD = D_MODEL // XA_HEADS
D_FF = 2816
DN_ALPHA = (2.0 * DEPTH) ** 0.25
DN_BETA = (8.0 * DEPTH) ** -0.25
LN_EPS = 1e-5

kernel_name = "hybrid_rwkv7_rglru_gated_decoder_step"

F32 = jnp.float32


def _layer_norm(x, g, b):
    xf = x.astype(F32)
    mu = jnp.mean(xf, axis=-1, keepdims=True)
    var = jnp.mean(jnp.square(xf - mu), axis=-1, keepdims=True)
    return ((xf - mu) * lax.rsqrt(var + LN_EPS) * g.astype(F32) + b.astype(F32)).astype(x.dtype)


def _swiglu(h, wi, wo):
    u = h @ wi
    gate, up = u[..., :D_FF], u[..., D_FF:]
    return (jax.nn.silu(gate) * up) @ wo


def _rwkv_scan(S0, r, w, k, v, kk, a):
    def step(S, inp):
        r_t, w_t, k_t, v_t, kk_t, a_t = inp
        sa = jnp.einsum('bhvk,bhk->bhv', S, -kk_t)
        S = S * w_t[:, :, None, :] + sa[..., None] * (kk_t * a_t)[:, :, None, :] + v_t[..., None] * k_t[:, :, None, :]
        y = jnp.einsum('bhvk,bhk->bhv', S, r_t)
        return S, y
    xs = tuple(jnp.moveaxis(t, 1, 0) for t in (r, w, k, v, kk, a))
    S, ys = lax.scan(step, S0, xs)
    return S, jnp.moveaxis(ys, 0, 1)


def _lin_combine(c1, c2):
    a1, b1 = c1
    a2, b2 = c2
    return a1 * a2, a2 * b1 + b2


def _mixer(h, S0, shift0, h0, buf0, P):
    B, T, _ = h.shape
    H, N = RWKV_HEADS, RWKV_HEAD
    proj = h @ P['w_in']
    p_rwkv = proj[..., :RWKV_PROJ]
    p_lru = proj[..., RWKV_PROJ:RWKV_PROJ + LRU_WIDTH]
    p_gelu = proj[..., RWKV_PROJ + LRU_WIDTH:RWKV_PROJ + 2 * LRU_WIDTH]
    p_gate = proj[..., RWKV_PROJ + 2 * LRU_WIDTH:]

    prev = jnp.concatenate([shift0[:, None].astype(p_rwkv.dtype), p_rwkv[:, :-1]], axis=1)
    xs = p_rwkv + (prev - p_rwkv) * P['shift_mu']
    W = RWKV_WIDTH
    r = xs[..., :W]
    k = xs[..., W:2 * W]
    v = xs[..., 2 * W:3 * W]
    xw = xs[..., 3 * W:3 * W + DECAY_LORA]
    xa = xs[..., 3 * W + DECAY_LORA:3 * W + DECAY_LORA + AAA_LORA]
    xg = xs[..., 3 * W + DECAY_LORA + AAA_LORA:]
    wlog = -jax.nn.softplus(-(P['decay_w0'] + jnp.tanh(xw) @ P['decay_w2']).astype(F32)) - 0.5
    decay = jnp.exp(-jnp.exp(wlog))
    a = jax.nn.sigmoid((P['aaa_a0'] + xa @ P['aaa_a2']).astype(F32))
    g = jax.nn.sigmoid(xg) @ P['gate_g2']
    rf, kf, vf = r.astype(F32), k.astype(F32), v.astype(F32)
    kk = (kf * P['k_k'].astype(F32)).reshape(B, T, H, N)
    kk = kk / jnp.maximum(jnp.sqrt(jnp.sum(kk * kk, axis=-1, keepdims=True)), 1e-12)
    kf = kf * (1.0 + (a - 1.0) * P['k_a'].astype(F32))
    hd = lambda t: t.reshape(B, T, H, N)
    rh, kh, vh = hd(rf), hd(kf), hd(vf)
    S_new, y = _rwkv_scan(S0.astype(F32), rh, hd(decay), kh, vh, kk, hd(a))
    ym = jnp.mean(y, axis=-1, keepdims=True)
    yv = jnp.mean(jnp.square(y - ym), axis=-1, keepdims=True)
    yn = ((y - ym) * lax.rsqrt(yv + GN_EPS)).reshape(B, T, W) * P['gn_g'].astype(F32) + P['gn_b'].astype(F32)
    bonus = (jnp.sum(rh * kh * P['r_k'].astype(F32), axis=-1, keepdims=True) * vh).reshape(B, T, W)
    rwkv_out = ((yn + bonus) * g.astype(F32)).astype(h.dtype)

    conv_in = jnp.concatenate([buf0.astype(p_lru.dtype), p_lru], axis=1)
    xc = P['conv_b'] + sum(P['conv_w'][j] * conv_in[:, j:j + T] for j in range(CONV_WIDTH))
    new_buf = conv_in[:, -(CONV_WIDTH - 1):]
    xcb = xc.reshape(B, T, LRU_BLOCKS, LRU_BLOCK)
    gr = jax.nn.sigmoid((jnp.einsum('btnc,ncd->btnd', xcb, P['lru_wr']).reshape(B, T, LRU_WIDTH) + P['lru_br']).astype(F32))
    gi = jax.nn.sigmoid((jnp.einsum('btnc,ncd->btnd', xcb, P['lru_wi']).reshape(B, T, LRU_WIDTH) + P['lru_bi']).astype(F32))
    log_a = -LRU_C * gr * jax.nn.softplus(-P['lru_lambda'].astype(F32))
    a_t = jnp.exp(log_a)
    b_t = jnp.sqrt(-jnp.expm1(2.0 * log_a)) * gi * xc.astype(F32)
    b_t = b_t.at[:, 0].add(a_t[:, 0] * h0.astype(F32))
    _, hs = lax.associative_scan(_lin_combine, (a_t, b_t), axis=1)
    lru_out = (hs * jax.nn.gelu(p_gelu.astype(F32))).astype(h.dtype)

    gates = jax.nn.sigmoid(p_gate).reshape(B, T, N_BRANCH, D_MODEL)
    merged = gates[:, :, 0] * rwkv_out + gates[:, :, 1] * lru_out
    out = merged @ P['w_mix_out']
    return out, S_new.astype(h.dtype), p_rwkv[:, -1], hs[:, -1].astype(h.dtype), new_buf


def _xattn(h, mk, mv, wq, wo):
    B, T, _ = h.shape
    q = (h @ wq).reshape(B, T, XA_HEADS, XA_HEAD)
    s = jnp.einsum('bthd,bmhd->bhtm', q, mk).astype(F32) * (XA_HEAD ** -0.5)
    p = jax.nn.softmax(s, axis=-1).astype(h.dtype)
    o = jnp.einsum('bhtm,bmhd->bthd', p, mv).reshape(B, T, D_MODEL)
    return o @ wo


def _layer(h, mk, mv, S0, shift0, h0, buf0, P):
    h = _layer_norm(DN_ALPHA * h + 0.5 * _swiglu(h, P['ffn1_wi'], P['ffn1_wo']), P['ln_g'][0], P['ln_b'][0])
    mix, S_new, shift_new, h_new, buf_new = _mixer(h, S0, shift0, h0, buf0, P)
    h = _layer_norm(DN_ALPHA * h + mix, P['ln_g'][1], P['ln_b'][1])
    h = _layer_norm(DN_ALPHA * h + _xattn(h, mk, mv, P['xa_wq'], P['xa_wo']), P['ln_g'][2], P['ln_b'][2])
    h = _layer_norm(DN_ALPHA * h + 0.5 * _swiglu(h, P['ffn2_wi'], P['ffn2_wo']), P['ln_g'][3], P['ln_b'][3])
    return h, S_new, shift_new, h_new, buf_new


def setup_inputs(seed: int = 0) -> dict:
    key = jax.random.key(seed)
    ks = iter(jax.random.split(key, 64))
    nrm = lambda shape, scale: jax.random.normal(next(ks), shape, F32) * scale
    uni = lambda shape, lo, hi: jax.random.uniform(next(ks), shape, F32, minval=lo, maxval=hi)
    L, D, W = DEPTH, D_MODEL, RWKV_WIDTH
    inv = lambda n: float(n) ** -0.5
    lam_s = uni((L, LRU_WIDTH), 0.9, 0.999) ** (1.0 / LRU_C)
    return {
        'x_prompt': nrm((BATCH, SEQ, D), 1.0),
        'x_sample': nrm((DEC_BATCH, DEC_SEQ, D), 1.0),
        'mem_prompt': nrm((BATCH, N_MEM, D), 1.0),
        'cache_mem_k': nrm((L, DEC_BATCH, N_MEM, XA_HEADS, XA_HEAD), 1.0),
        'cache_mem_v': nrm((L, DEC_BATCH, N_MEM, XA_HEADS, XA_HEAD), 1.0),
        'state_rwkv': nrm((L, DEC_BATCH, RWKV_HEADS, RWKV_HEAD, RWKV_HEAD), 0.5),
        'state_rwkv_shift': nrm((L, DEC_BATCH, RWKV_PROJ), 1.0),
        'state_lru': nrm((L, DEC_BATCH, LRU_WIDTH), 0.5),
        'state_conv': nrm((L, DEC_BATCH, CONV_WIDTH - 1, LRU_WIDTH), 1.0),
        'ln_g': 1.0 + nrm((L, 4, D), 0.02),
        'ln_b': nrm((L, 4, D), 0.02),
        'ffn1_wi': nrm((L, D, 2 * D_FF), inv(D)),
        'ffn1_wo': nrm((L, D_FF, D), inv(D_FF) * DN_BETA),
        'ffn2_wi': nrm((L, D, 2 * D_FF), inv(D)),
        'ffn2_wo': nrm((L, D_FF, D), inv(D_FF) * DN_BETA),
        'w_in': nrm((L, D, PROJ_WIDTH), inv(D)),
        'shift_mu': uni((L, RWKV_PROJ), 0.0, 1.0),
        'decay_w0': uni((L, W), -6.0, -1.0),
        'decay_w2': nrm((L, DECAY_LORA, W), 0.1 * inv(DECAY_LORA)),
        'aaa_a0': nrm((L, W), 0.1),
        'aaa_a2': nrm((L, AAA_LORA, W), 0.1 * inv(AAA_LORA)),
        'gate_g2': nrm((L, GATE_LORA, W), inv(GATE_LORA)),
        'k_k': 0.85 + nrm((L, W), 0.02),
        'k_a': 1.0 + nrm((L, W), 0.02),
        'r_k': nrm((L, RWKV_HEADS, RWKV_HEAD), 0.1),
        'gn_g': 1.0 + nrm((L, W), 0.02),
        'gn_b': nrm((L, W), 0.02),
        'conv_w': nrm((L, CONV_WIDTH, LRU_WIDTH), inv(CONV_WIDTH)),
        'conv_b': nrm((L, LRU_WIDTH), 0.02),
        'lru_wr': nrm((L, LRU_BLOCKS, LRU_BLOCK, LRU_BLOCK), inv(LRU_BLOCK)),
        'lru_br': nrm((L, LRU_WIDTH), 0.02),
        'lru_wi': nrm((L, LRU_BLOCKS, LRU_BLOCK, LRU_BLOCK), inv(LRU_BLOCK)),
        'lru_bi': nrm((L, LRU_WIDTH), 0.02),
        'lru_lambda': jnp.log(lam_s) - jnp.log1p(-lam_s),
        'w_mix_out': nrm((L, D, D), inv(D) * DN_BETA),
        'xa_wq': nrm((L, D, D), inv(D)),
        'xa_wk': nrm((L, D, D), inv(D)),
        'xa_wv': nrm((L, D, D), inv(D)),
        'xa_wo': nrm((L, D, D), inv(D) * DN_BETA),
    }


def reference(x_prompt, x_sample, mem_prompt, cache_mem_k, cache_mem_v, state_rwkv, state_rwkv_shift,
              state_lru, state_conv, ln_g, ln_b, ffn1_wi, ffn1_wo, ffn2_wi, ffn2_wo, w_in, shift_mu,
              decay_w0, decay_w2, aaa_a0, aaa_a2, gate_g2, k_k, k_a, r_k, gn_g, gn_b, conv_w, conv_b,
              lru_wr, lru_br, lru_wi, lru_bi, lru_lambda, w_mix_out, xa_wq, xa_wk, xa_wv, xa_wo):
    dt = x_prompt.dtype
    hp, hs = x_prompt, x_sample
    p_mk, p_mv, p_S, p_sh, p_h, p_cv = [], [], [], [], [], []
    s_S, s_sh, s_h, s_cv = [], [], [], []
    for l in range(DEPTH):
        P = dict(ln_g=ln_g[l], ln_b=ln_b[l], ffn1_wi=ffn1_wi[l], ffn1_wo=ffn1_wo[l], ffn2_wi=ffn2_wi[l],
                 ffn2_wo=ffn2_wo[l], w_in=w_in[l], shift_mu=shift_mu[l], decay_w0=decay_w0[l],
                 decay_w2=decay_w2[l], aaa_a0=aaa_a0[l], aaa_a2=aaa_a2[l], gate_g2=gate_g2[l], k_k=k_k[l],
                 k_a=k_a[l], r_k=r_k[l], gn_g=gn_g[l], gn_b=gn_b[l], conv_w=conv_w[l], conv_b=conv_b[l],
                 lru_wr=lru_wr[l], lru_br=lru_br[l], lru_wi=lru_wi[l], lru_bi=lru_bi[l],
                 lru_lambda=lru_lambda[l], w_mix_out=w_mix_out[l], xa_wq=xa_wq[l], xa_wo=xa_wo[l])
        mk = (mem_prompt @ xa_wk[l]).reshape(BATCH, N_MEM, XA_HEADS, XA_HEAD)
        mv = (mem_prompt @ xa_wv[l]).reshape(BATCH, N_MEM, XA_HEADS, XA_HEAD)
        hp, S1, sh1, h1, b1 = _layer(
            hp, mk, mv,
            jnp.zeros((BATCH, RWKV_HEADS, RWKV_HEAD, RWKV_HEAD), F32),
            jnp.zeros((BATCH, RWKV_PROJ), dt),
            jnp.zeros((BATCH, LRU_WIDTH), F32),
            jnp.zeros((BATCH, CONV_WIDTH - 1, LRU_WIDTH), dt), P)
        p_mk.append(mk); p_mv.append(mv); p_S.append(S1); p_sh.append(sh1); p_h.append(h1); p_cv.append(b1)
        hs, S2, sh2, h2, b2 = _layer(hs, cache_mem_k[l], cache_mem_v[l], state_rwkv[l], state_rwkv_shift[l],
                                     state_lru[l], state_conv[l], P)
        s_S.append(S2); s_sh.append(sh2); s_h.append(h2); s_cv.append(b2)
    return (hp, hs,
            jnp.stack(p_mk), jnp.stack(p_mv), jnp.stack(p_S), jnp.stack(p_sh), jnp.stack(p_h), jnp.stack(p_cv),
            jnp.stack(s_S), jnp.stack(s_sh), jnp.stack(s_h), jnp.stack(s_cv))
```

```python
import functools

import jax
import jax.numpy as jnp
from jax import lax
from jax.experimental import pallas as pl
from jax.experimental.pallas import tpu as pltpu

F32 = jnp.float32
BF16 = jnp.bfloat16

RWKV_HEAD = 64
DECAY_LORA = 64
AAA_LORA = 64
GATE_LORA = 128
GN_EPS = 64e-5
LRU_BLOCK = 64
CONV_WIDTH = 4
LRU_C = 8.0
LN_EPS = 1e-5

V7X_LANES = 128
V7X_SUBLANES = 8
V7X_MXU_DIM = 256
V7X_SCOPED_VMEM_BYTES = 60000 * 1024

RWKV_CHUNK = 64
HEAD_PAIR = 2 * RWKV_HEAD
LRU_GROUP = V7X_MXU_DIM


def _cparams(semantics):
    return pltpu.CompilerParams(dimension_semantics=semantics, vmem_limit_bytes=V7X_SCOPED_VMEM_BYTES)


def _const_spec(shape):
    zeros = (0,) * len(shape)
    return pl.BlockSpec(shape, lambda *_: zeros)


def _dot(a, b):
    return jnp.dot(a.astype(BF16), b.astype(BF16), preferred_element_type=F32)


def _dot_dims(a, b, dims):
    return lax.dot_general(a.astype(BF16), b.astype(BF16), (dims, ((), ())), preferred_element_type=F32)


_NN = ((1,), (0,))
_NT = ((1,), (1,))
_TN = ((0,), (0,))


def _split2(x):
    hi = x.astype(BF16)
    lo = (x - hi.astype(F32)).astype(BF16)
    return hi, lo


def _dot3(a, b, dims=_NN):
    a_hi, a_lo = _split2(a)
    b_hi, b_lo = _split2(b)
    d = lambda x, y: lax.dot_general(x, y, (dims, ((), ())), preferred_element_type=F32)
    return d(a_hi, b_hi) + (d(a_hi, b_lo) + d(a_lo, b_hi))


def _dot_exact_lhs(a_bf16, b):
    hi, lo = _split2(b)
    lo2 = (b - hi.astype(F32) - lo.astype(F32)).astype(BF16)
    d = lambda y: jnp.dot(a_bf16, y, preferred_element_type=F32)
    return d(hi) + (d(lo) + d(lo2))


def _dot_exact_rhs(a, b_bf16):
    hi, lo = _split2(a)
    d = lambda x: jnp.dot(x, b_bf16, preferred_element_type=F32)
    return d(hi) + d(lo)


def _layer_norm(x, g, b):
    mu = jnp.mean(x, axis=-1, keepdims=True)
    xc = x - mu
    var = jnp.mean(xc * xc, axis=-1, keepdims=True)
    return xc * lax.rsqrt(var + LN_EPS) * g + b


def _softplus(z):
    return jnp.maximum(z, 0.0) + jnp.log(1.0 + jnp.exp(-jnp.abs(z)))


def _head_ones(width):
    r = lax.broadcasted_iota(jnp.int32, (width, width), 0) // RWKV_HEAD
    c = lax.broadcasted_iota(jnp.int32, (width, width), 1) // RWKV_HEAD
    return (r == c).astype(BF16)


def _mm_kernel(x_ref, w_ref, o_ref):
    o_ref[...] = jnp.dot(x_ref[...].astype(BF16), w_ref[...], preferred_element_type=F32)


def _matmul(x, w, *, tm, tn):
    n, k = x.shape
    m = w.shape[1]
    assert n % tm == 0 and m % tn == 0
    return pl.pallas_call(
        _mm_kernel,
        grid=(m // tn, n // tm),
        in_specs=[pl.BlockSpec((tm, k), lambda j, i: (i, 0)),
                  pl.BlockSpec((k, tn), lambda j, i: (0, j))],
        out_specs=pl.BlockSpec((tm, tn), lambda j, i: (i, j)),
        out_shape=jax.ShapeDtypeStruct((n, m), F32),
        compiler_params=_cparams(("parallel", "parallel")),
        name="matmul",
    )(x, w)


def _ffn_kernel(x_ref, wg_ref, wu_ref, wo_ref, g_ref, b_ref, o_ref, *, alpha, n_chunks):
    x = x_ref[...]
    xb = x.astype(BF16)
    ck = wg_ref.shape[1] // n_chunks
    acc = jnp.zeros(x.shape, F32)
    for c in range(n_chunks):
        sl = slice(c * ck, (c + 1) * ck)
        gate = jnp.dot(xb, wg_ref[:, sl], preferred_element_type=F32)
        up = jnp.dot(xb, wu_ref[:, sl], preferred_element_type=F32)
        mid = (gate * jax.nn.sigmoid(gate) * up).astype(BF16)
        acc = acc + jnp.dot(mid, wo_ref[sl, :], preferred_element_type=F32)
    o_ref[...] = _layer_norm(alpha * x + 0.5 * acc, g_ref[...], b_ref[...])


def _ffn_ln(x, wg, wu, wo, g, b, *, alpha, tm):
    n, d = x.shape
    assert n % tm == 0
    d_ff = wg.shape[1]
    n_chunks = 2 if d_ff % (2 * V7X_LANES) == 0 else 1
    return pl.pallas_call(
        functools.partial(_ffn_kernel, alpha=alpha, n_chunks=n_chunks),
        grid=(n // tm,),
        in_specs=[pl.BlockSpec((tm, d), lambda i: (i, 0)),
                  _const_spec(wg.shape), _const_spec(wu.shape), _const_spec(wo.shape),
                  _const_spec(g.shape), _const_spec(b.shape)],
        out_specs=pl.BlockSpec((tm, d), lambda i: (i, 0)),
        out_shape=jax.ShapeDtypeStruct((n, d), F32),
        compiler_params=_cparams(("parallel",)),
        name="ffn_ln",
    )(x, wg, wu, wo, g, b)


def _rwkv_pre(r, k, v, xx, w0, w2, a0, a2, g2, k_k, k_a, r_k, ones):
    xw = xx[:, :DECAY_LORA]
    xa = xx[:, DECAY_LORA:DECAY_LORA + AAA_LORA]
    xg = xx[:, DECAY_LORA + AAA_LORA:]
    z = w0 + _dot(jnp.tanh(xw), w2)
    lw = -jnp.exp(-_softplus(-z) - 0.5)
    a = jax.nn.sigmoid(a0 + _dot(xa, a2))
    g = _dot(jax.nn.sigmoid(xg), g2)
    kkr = k * k_k
    ss = _dot_exact_rhs(kkr * kkr, ones)
    kk = kkr / jnp.maximum(jnp.sqrt(ss), 1e-12)
    kf = k * (1.0 + (a - 1.0) * k_a)
    bonus = _dot_exact_rhs(r * kf * r_k, ones) * v
    return lw, a, g, kk, kf, bonus


def _rwkv_post(y, bonus, g, gn_g, gn_b, ones):
    inv_n = 1.0 / RWKV_HEAD
    ym = _dot_exact_rhs(y, ones) * inv_n
    yc = y - ym
    yv = _dot_exact_rhs(yc * yc, ones) * inv_n
    yn = yc * lax.rsqrt(yv + GN_EPS) * gn_g + gn_b
    return (yn + bonus) * g


def _chunk_pair(S, r, kf, v, kk, a, lw, ltri):
    C = RWKV_CHUNK
    L = _dot_exact_lhs(ltri, lw)
    Lc = L[C - 1:C, :]
    e_l = jnp.exp(L)
    e_lp = jnp.exp(L - lw)
    e_nl = jnp.exp(-L)
    e_c = jnp.exp(Lc - L)
    wc = jnp.exp(Lc)
    bb = kk * a
    lane = lax.broadcasted_iota(jnp.int32, (C, HEAD_PAIR), 1)
    first = lane < RWKV_HEAD

    def stack(x):
        return jnp.concatenate([jnp.where(first, x, 0.0), jnp.where(first, 0.0, x)], axis=0)

    ar = jnp.concatenate([stack(-kk * e_lp), stack(r * e_l)], axis=0)
    bk = jnp.concatenate([stack(bb * e_nl), stack(kf * e_nl)], axis=0)
    vs = stack(v)
    bkh = jnp.concatenate([stack(bb * e_c), stack(kf * e_c)], axis=0)
    G = _dot3(ar, bk, _NT)
    n2 = 2 * C
    ri = lax.broadcasted_iota(jnp.int32, (n2, n2), 0)
    ci = lax.broadcasted_iota(jnp.int32, (n2, n2), 1)
    a_ab = jnp.where(ri > ci, G[:n2, :n2], 0.0)
    a_ak = jnp.where(ri > ci, G[:n2, n2:], 0.0)
    a_rb = jnp.where(ri >= ci, G[n2:, :n2], 0.0)
    a_rk = jnp.where(ri >= ci, G[n2:, n2:], 0.0)
    N = a_ab
    P = jnp.where(ri == ci, 1.0, 0.0) + N
    steps = 1
    while 2 * steps < C:
        N = _dot3(N, N)
        P = P + _dot3(P, N)
        steps *= 2
    rs = _dot3(ar, S, _NT)
    u = _dot3(P, rs[:n2] + _dot3(a_ak, vs))
    uv = jnp.concatenate([u, vs], axis=0)
    ys = rs[n2:] + _dot3(jnp.concatenate([a_rb, a_rk], axis=1), uv)
    S_new = S * wc + _dot3(uv, bkh, _TN)
    return ys[:C] + ys[C:], S_new


def _rwkv_chunk_kernel(pr_ref, pk_ref, pv_ref, px_ref, sr_ref, sk_ref, sv_ref, sx_ref,
                       mur_ref, muk_ref, muv_ref, mux_ref, w0_ref, w2_ref, a0_ref, a2_ref, g2_ref,
                       kk_ref, ka_ref, rk_ref, gng_ref, gnb_ref,
                       out_ref, s_out_ref,
                       s_scr, cr_scr, ck_scr, cv_scr, cx_scr, y_scr):
    ti = pl.program_id(2)
    tt, hw = pr_ref.shape
    n_pairs = hw // HEAD_PAIR

    @pl.when(ti == 0)
    def _init():
        s_scr[...] = jnp.zeros(s_scr.shape, F32)
        cr_scr[...] = sr_ref[...]
        ck_scr[...] = sk_ref[...]
        cv_scr[...] = sv_ref[...]
        cx_scr[...] = sx_ref[...]

    def shifted(p_ref, c_scr, mu_ref):
        p = p_ref[...]
        row = lax.broadcasted_iota(jnp.int32, p.shape, 0)
        prev = jnp.where(row == 0, c_scr[...], pltpu.roll(p, 1, 0))
        c_scr[...] = p_ref[tt - 1:tt, :]
        return p + (prev - p) * mu_ref[...]

    r = shifted(pr_ref, cr_scr, mur_ref)
    k = shifted(pk_ref, ck_scr, muk_ref)
    v = shifted(pv_ref, cv_scr, muv_ref)
    xx = shifted(px_ref, cx_scr, mux_ref)
    ones = _head_ones(hw)
    lw, a, g, kk, kf, bonus = _rwkv_pre(r, k, v, xx, w0_ref[...], w2_ref[...], a0_ref[...], a2_ref[...],
                                        g2_ref[...], kk_ref[...], ka_ref[...], rk_ref[...], ones)
    C = RWKV_CHUNK
    ltri = (lax.broadcasted_iota(jnp.int32, (C, C), 0) >= lax.broadcasted_iota(jnp.int32, (C, C), 1)).astype(BF16)
    for c in range(tt // C):
        rows = slice(c * C, (c + 1) * C)
        for p in range(n_pairs):
            lanes = slice(p * HEAD_PAIR, (p + 1) * HEAD_PAIR)
            y, s_new = _chunk_pair(s_scr[p], r[rows, lanes], kf[rows, lanes], v[rows, lanes],
                                   kk[rows, lanes], a[rows, lanes], lw[rows, lanes], ltri)
            s_scr[p] = s_new
            y_scr[rows, lanes] = y
    out_ref[...] = _rwkv_post(y_scr[...], bonus, g, gng_ref[...], gnb_ref[...], ones)

    @pl.when(ti == pl.num_programs(2) - 1)
    def _emit_state():
        for p in range(n_pairs):
            s = s_scr[p]
            s_out_ref[2 * p] = s[:RWKV_HEAD, :RWKV_HEAD]
            s_out_ref[2 * p + 1] = s[RWKV_HEAD:, RWKV_HEAD:]


def _rwkv_chunked(pa, shift0, W, *, tt, hw):
    B, T, _ = pa.shape
    width = W['decay_w0'].shape[1]
    heads = width // RWKV_HEAD
    assert T % tt == 0 and tt % RWKV_CHUNK == 0 and width % hw == 0 and hw % HEAD_PAIR == 0
    nb = width // hw
    xw = DECAY_LORA + AAA_LORA + GATE_LORA
    assert (3 * width) % xw == 0
    xblk = 3 * width // xw
    col = lambda off: (lambda b, h, t: (b, t, off + h))
    vec = lambda: pl.BlockSpec((1, hw), lambda b, h, t: (0, h))
    in_specs = [
        pl.BlockSpec((None, tt, hw), col(0)), pl.BlockSpec((None, tt, hw), col(nb)),
        pl.BlockSpec((None, tt, hw), col(2 * nb)), pl.BlockSpec((None, tt, xw), lambda b, h, t: (b, t, xblk)),
        pl.BlockSpec((None, 1, hw), lambda b, h, t: (b, 0, h)), pl.BlockSpec((None, 1, hw), lambda b, h, t: (b, 0, nb + h)),
        pl.BlockSpec((None, 1, hw), lambda b, h, t: (b, 0, 2 * nb + h)), pl.BlockSpec((None, 1, xw), lambda b, h, t: (b, 0, xblk)),
        vec(), vec(), vec(), _const_spec((1, xw)),
        vec(), pl.BlockSpec((DECAY_LORA, hw), lambda b, h, t: (0, h)),
        vec(), pl.BlockSpec((AAA_LORA, hw), lambda b, h, t: (0, h)),
        pl.BlockSpec((GATE_LORA, hw), lambda b, h, t: (0, h)),
        vec(), vec(), vec(), vec(), vec(),
    ]
    out, s_new = pl.pallas_call(
        _rwkv_chunk_kernel,
        grid=(B, nb, T // tt),
        in_specs=in_specs,
        out_specs=[pl.BlockSpec((None, tt, hw), lambda b, h, t: (b, t, h)),
                   pl.BlockSpec((None, hw // RWKV_HEAD, RWKV_HEAD, RWKV_HEAD), lambda b, h, t: (b, h, 0, 0))],
        out_shape=[jax.ShapeDtypeStruct((B, T, width), F32),
                   jax.ShapeDtypeStruct((B, heads, RWKV_HEAD, RWKV_HEAD), F32)],
        scratch_shapes=[pltpu.VMEM((hw // HEAD_PAIR, HEAD_PAIR, HEAD_PAIR), F32),
                        pltpu.VMEM((1, hw), F32), pltpu.VMEM((1, hw), F32), pltpu.VMEM((1, hw), F32),
                        pltpu.VMEM((1, xw), F32), pltpu.VMEM((tt, hw), F32)],
        compiler_params=_cparams(("parallel", "parallel", "arbitrary")),
        name="rwkv_chunked",
    )(pa, pa, pa, pa, shift0, shift0, shift0, shift0,
      W['mu_r'], W['mu_k'], W['mu_v'], W['mu_x'], W['decay_w0'], W['decay_w2'], W['aaa_a0'], W['aaa_a2'],
      W['gate_g2'], W['k_k'], W['k_a'], W['r_k'], W['gn_g'], W['gn_b'])
    return out, s_new


def _rwkv_step_kernel(p_ref, s0_ref, st_ref, mu_ref, w0_ref, w2_ref, a0_ref, a2_ref, g2_ref,
                      kk_ref, ka_ref, rk_ref, gng_ref, gnb_ref,
                      out_ref, st_out_ref, vec_scr, y_scr):
    bb, width = out_ref.shape
    heads = width // RWKV_HEAD
    p = p_ref[...]
    xs = p + (s0_ref[...] - p) * mu_ref[...]
    r, k, v, xx = xs[:, :width], xs[:, width:2 * width], xs[:, 2 * width:3 * width], xs[:, 3 * width:]
    ones = _head_ones(width)
    lw, a, g, kk, kf, bonus = _rwkv_pre(r, k, v, xx, w0_ref[...], w2_ref[...], a0_ref[...], a2_ref[...],
                                        g2_ref[...], kk_ref[...], ka_ref[...], rk_ref[...], ones)
    vec_scr[0] = r
    vec_scr[1] = kf
    vec_scr[2] = v
    vec_scr[3] = kk
    vec_scr[4] = kk * a
    vec_scr[5] = jnp.exp(lw)
    eye = (lax.broadcasted_iota(jnp.int32, (RWKV_HEAD, RWKV_HEAD), 0)
           == lax.broadcasted_iota(jnp.int32, (RWKV_HEAD, RWKV_HEAD), 1))

    def body(b, carry):
        full = [vec_scr[i, pl.ds(b, 1), :] for i in range(6)]
        ys = []
        for h in range(heads):
            hs = slice(h * RWKV_HEAD, (h + 1) * RWKV_HEAD)
            row = lambda i: full[i][:, hs]
            S = st_ref[b, h]
            sa = jnp.sum(S * row(3), axis=1, keepdims=True)
            v_col = jnp.sum(jnp.where(eye, row(2), 0.0), axis=1, keepdims=True)
            S2 = S * row(5) - sa * row(4) + v_col * row(1)
            y_col = jnp.sum(S2 * row(0), axis=1, keepdims=True)
            ys.append(jnp.sum(jnp.where(eye, y_col, 0.0), axis=0, keepdims=True))
            st_out_ref[b, h] = S2
        y_scr[pl.ds(b, 1), :] = jnp.concatenate(ys, axis=1)
        return carry

    lax.fori_loop(0, bb, body, 0)
    out_ref[...] = _rwkv_post(y_scr[...], bonus, g, gng_ref[...], gnb_ref[...], ones)


def _rwkv_step(pa, shift0, state, W, *, bb):
    B, proj = pa.shape
    width = W['decay_w0'].shape[1]
    heads = width // RWKV_HEAD
    assert B % bb == 0
    row = lambda w: pl.BlockSpec((bb, w), lambda i: (i, 0))
    st_spec = pl.BlockSpec((bb, heads, RWKV_HEAD, RWKV_HEAD), lambda i: (i, 0, 0, 0))
    names = ('decay_w0', 'decay_w2', 'aaa_a0', 'aaa_a2', 'gate_g2', 'k_k', 'k_a', 'r_k', 'gn_g', 'gn_b')
    out, st_new = pl.pallas_call(
        _rwkv_step_kernel,
        grid=(B // bb,),
        in_specs=[row(proj), row(proj), st_spec, _const_spec(W['shift_mu'].shape)]
                 + [_const_spec(W[n].shape) for n in names],
        out_specs=[row(width), st_spec],
        out_shape=[jax.ShapeDtypeStruct((B, width), F32), jax.ShapeDtypeStruct(state.shape, F32)],
        scratch_shapes=[pltpu.VMEM((6, bb, width), F32), pltpu.VMEM((bb, width), F32)],
        compiler_params=_cparams(("parallel",)),
        name="rwkv_step",
    )(pa, shift0, state, W['shift_mu'], *[W[n] for n in names])
    return out, st_new


def _lru_kernel(x_ref, pg_ref, g0_ref, g1_ref, rw_ref, h_ref, buf_ref, h0_ref,
                cw_ref, cb_ref, wr_ref, br_ref, wi_ref, bi_ref, lam_ref, wmix_ref, lng_ref, lnb_ref,
                out_ref, hlast_ref, tail_scr, hc_scr, *, alpha):
    ti = pl.program_id(1)
    tt, width = x_ref.shape
    S8 = V7X_SUBLANES

    @pl.when(ti == 0)
    def _init():
        tail_scr[...] = buf_ref[...]
        hc_scr[...] = h0_ref[...]

    x = x_ref[...]
    tail = tail_scr[...]
    r8 = lax.broadcasted_iota(jnp.int32, (S8, width), 0)

    def delayed(d):
        if tt < S8:
            assert tt == 1
            return tail[S8 - d:S8 - d + 1, :]
        rolled = pltpu.roll(x, d, 0)
        head = jnp.where(r8 < d, pltpu.roll(tail, d, 0), rolled[:S8])
        return jnp.concatenate([head, rolled[S8:]], axis=0)

    cw = cw_ref[...]
    conv = cw[0:1] * delayed(3)
    for j in range(1, CONV_WIDTH - 1):
        conv = conv + cw[j:j + 1] * delayed(CONV_WIDTH - 1 - j)
    xc = cb_ref[...] + (conv + cw[CONV_WIDTH - 1:CONV_WIDTH] * x)
    if tt < S8:
        tail_scr[...] = jnp.where(r8 == S8 - 1, x, pltpu.roll(tail, S8 - 1, 0))
    else:
        tail_scr[...] = x_ref[tt - S8:tt, :]

    xcb = xc.astype(BF16)
    n_grp = width // LRU_GROUP
    grp = lambda w_ref: jnp.concatenate(
        [jnp.dot(xcb[:, i * LRU_GROUP:(i + 1) * LRU_GROUP], w_ref[i], preferred_element_type=F32)
         for i in range(n_grp)], axis=1)
    gr = jax.nn.sigmoid(grp(wr_ref) + br_ref[...])
    gi = jax.nn.sigmoid(grp(wi_ref) + bi_ref[...])
    log_a = -LRU_C * gr * _softplus(-lam_ref[...])
    A = jnp.exp(log_a)
    Bv = jnp.sqrt(1.0 - jnp.exp(2.0 * log_a)) * gi * xc
    if tt > 1:
        rowi = lax.broadcasted_iota(jnp.int32, (tt, width), 0)
        s = 1
        while s < tt:
            keep = rowi >= s
            Bv = Bv + A * jnp.where(keep, pltpu.roll(Bv, s, 0), 0.0)
            A = A * jnp.where(keep, pltpu.roll(A, s, 0), 1.0)
            s *= 2
    hs = Bv + A * hc_scr[...]
    h_last = hs[tt - 1:tt, :]
    hc_scr[...] = h_last
    hlast_ref[...] = h_last

    lru_out = hs * jax.nn.gelu(pg_ref[...])
    merged = jax.nn.sigmoid(g0_ref[...]) * rw_ref[...] + jax.nn.sigmoid(g1_ref[...]) * lru_out
    mix = jnp.dot(merged.astype(BF16), wmix_ref[...], preferred_element_type=F32)
    out_ref[...] = _layer_norm(alpha * h_ref[...] + mix, lng_ref[...], lnb_ref[...])


def _lru_mix_ln(pb, rw, h, buf8, h0, W, ln_g, ln_b, *, alpha, tt):
    B, T, width = rw.shape
    assert T % tt == 0
    blk = lambda j: pl.BlockSpec((None, tt, width), lambda b, t: (b, t, j))
    names = ('conv_w', 'conv_b', 'lru_wr', 'lru_br', 'lru_wi', 'lru_bi', 'lru_lambda', 'w_mix_out')
    out, h_last = pl.pallas_call(
        functools.partial(_lru_kernel, alpha=alpha),
        grid=(B, T // tt),
        in_specs=[blk(0), blk(1), blk(2), blk(3), blk(0), blk(0),
                  pl.BlockSpec((None, V7X_SUBLANES, width), lambda b, t: (b, 0, 0)),
                  pl.BlockSpec((None, 1, width), lambda b, t: (b, 0, 0))]
                 + [_const_spec(W[n].shape) for n in names] + [_const_spec(ln_g.shape), _const_spec(ln_b.shape)],
        out_specs=[blk(0), pl.BlockSpec((None, 1, width), lambda b, t: (b, 0, 0))],
        out_shape=[jax.ShapeDtypeStruct((B, T, width), F32), jax.ShapeDtypeStruct((B, 1, width), F32)],
        scratch_shapes=[pltpu.VMEM((V7X_SUBLANES, width), F32), pltpu.VMEM((1, width), F32)],
        compiler_params=_cparams(("parallel", "arbitrary")),
        name="lru_mix_ln",
    )(pb, pb, pb, pb, rw, h, buf8, h0, *[W[n] for n in names], ln_g, ln_b)
    return out, h_last


def _xattn_kernel(h_ref, mk_ref, mv_ref, wq_ref, wo_ref, g_ref, b_ref, o_ref, *, alpha, heads):
    h = h_ref[...]
    d = h.shape[1]
    hd = d // heads
    q = jnp.dot(h.astype(BF16), wq_ref[...], preferred_element_type=F32)
    outs = []
    for i in range(heads):
        sl = slice(i * hd, (i + 1) * hd)
        s = _dot_dims(q[:, sl], mk_ref[:, sl], _NT) * (hd ** -0.5)
        e = jnp.exp(s - jnp.max(s, axis=-1, keepdims=True))
        p = e / jnp.sum(e, axis=-1, keepdims=True)
        outs.append(_dot(p, mv_ref[:, sl]))
    o = jnp.concatenate(outs, axis=1)
    out = jnp.dot(o.astype(BF16), wo_ref[...], preferred_element_type=F32)
    o_ref[...] = _layer_norm(alpha * h + out, g_ref[...], b_ref[...])


def _xattn_ln(h, mk, mv, wq, wo, g, b, *, alpha, heads, tt):
    B, T, d = h.shape
    m = mk.shape[1]
    assert T % tt == 0
    return pl.pallas_call(
        functools.partial(_xattn_kernel, alpha=alpha, heads=heads),
        grid=(B, T // tt),
        in_specs=[pl.BlockSpec((None, tt, d), lambda bi, t: (bi, t, 0)),
                  pl.BlockSpec((None, m, d), lambda bi, t: (bi, 0, 0)),
                  pl.BlockSpec((None, m, d), lambda bi, t: (bi, 0, 0)),
                  _const_spec(wq.shape), _const_spec(wo.shape), _const_spec(g.shape), _const_spec(b.shape)],
        out_specs=pl.BlockSpec((None, tt, d), lambda bi, t: (bi, t, 0)),
        out_shape=jax.ShapeDtypeStruct((B, T, d), F32),
        compiler_params=_cparams(("parallel", "parallel")),
        name="xattn_ln",
    )(h, mk, mv, wq, wo, g, b)


def _row(v):
    return v.reshape(1, -1)


def _block_diag_groups(w):
    n, c, _ = w.shape
    per = LRU_GROUP // c
    w4 = w.reshape(n // per, per, c, c)
    bd = jnp.einsum('gjcd,jk->gjckd', w4, jnp.eye(per, dtype=w.dtype))
    return bd.reshape(n // per, LRU_GROUP, LRU_GROUP).astype(BF16)


def _prep_layer(l, ln_g, ln_b, ffn1_wi, ffn1_wo, ffn2_wi, ffn2_wo, w_in, shift_mu, decay_w0, decay_w2,
                aaa_a0, aaa_a2, gate_g2, k_k, k_a, r_k, gn_g, gn_b, conv_w, conv_b, lru_wr, lru_br,
                lru_wi, lru_bi, lru_lambda, w_mix_out, xa_wq, xa_wk, xa_wv, xa_wo):
    width = decay_w0.shape[1]
    rp = shift_mu.shape[1]
    d_ff = ffn1_wo.shape[1]
    bf = lambda w: w.astype(BF16)
    mu = shift_mu[l]
    return dict(
        ln_g=[_row(ln_g[l, i]) for i in range(4)], ln_b=[_row(ln_b[l, i]) for i in range(4)],
        ffn1=(bf(ffn1_wi[l][:, :d_ff]), bf(ffn1_wi[l][:, d_ff:]), bf(ffn1_wo[l])),
        ffn2=(bf(ffn2_wi[l][:, :d_ff]), bf(ffn2_wi[l][:, d_ff:]), bf(ffn2_wo[l])),
        w_in_a=bf(w_in[l][:, :rp]), w_in_b=bf(w_in[l][:, rp:]),
        shift_mu=_row(mu), mu_r=_row(mu[:width]), mu_k=_row(mu[width:2 * width]),
        mu_v=_row(mu[2 * width:3 * width]), mu_x=_row(mu[3 * width:]),
        decay_w0=_row(decay_w0[l]), decay_w2=bf(decay_w2[l]), aaa_a0=_row(aaa_a0[l]), aaa_a2=bf(aaa_a2[l]),
        gate_g2=bf(gate_g2[l]), k_k=_row(k_k[l]), k_a=_row(k_a[l]), r_k=_row(r_k[l]),
        gn_g=_row(gn_g[l]), gn_b=_row(gn_b[l]),
        conv_w=conv_w[l], conv_b=_row(conv_b[l]),
        lru_wr=_block_diag_groups(lru_wr[l]), lru_br=_row(lru_br[l]),
        lru_wi=_block_diag_groups(lru_wi[l]), lru_bi=_row(lru_bi[l]), lru_lambda=_row(lru_lambda[l]),
        w_mix_out=bf(w_mix_out[l]), xa_wq=bf(xa_wq[l]), xa_wo=bf(xa_wo[l]),
        xa_wkv=bf(jnp.concatenate([xa_wk[l], xa_wv[l]], axis=1)),
    )


def _tile(n, pref):
    return pref if n % pref == 0 else n


def _layer(h, mk, mv, state, shift0, h0, buf0, W, *, alpha, xa_heads):
    B, T, d = h.shape
    n = B * T
    tm = _tile(n, 512)
    h1 = _ffn_ln(h.reshape(n, d), *W['ffn1'], W['ln_g'][0], W['ln_b'][0], alpha=alpha, tm=tm)
    pa = _matmul(h1, W['w_in_a'], tm=tm, tn=W['w_in_a'].shape[1])
    pb = _matmul(h1, W['w_in_b'], tm=tm, tn=W['w_in_b'].shape[1] // 2)
    width = W['decay_w0'].shape[1]
    lru_w = W['conv_b'].shape[1]
    pa3 = pa.reshape(B, T, -1)
    pb3 = pb.reshape(B, T, -1)
    if state is None:
        rw, s_new = _rwkv_chunked(pa3, shift0.reshape(B, 1, -1), W, tt=_tile(T, 256), hw=V7X_MXU_DIM)
    else:
        assert T == 1
        rw, s_new = _rwkv_step(pa, shift0, state, W, bb=_tile(B, V7X_SUBLANES))
        rw = rw.reshape(B, T, width)
    buf8 = jnp.concatenate([jnp.zeros((B, V7X_SUBLANES - (CONV_WIDTH - 1), lru_w), F32), buf0], axis=1)
    h2, h_last = _lru_mix_ln(pb3, rw, h1.reshape(B, T, d), buf8, h0.reshape(B, 1, lru_w), W,
                             W['ln_g'][1], W['ln_b'][1], alpha=alpha, tt=_tile(T, 256))
    conv_in_tail = jnp.concatenate([buf0, pb3[:, :, :lru_w]], axis=1)[:, -(CONV_WIDTH - 1):]
    h3 = _xattn_ln(h2, mk, mv, W['xa_wq'], W['xa_wo'], W['ln_g'][2], W['ln_b'][2],
                   alpha=alpha, heads=xa_heads, tt=_tile(T, 512))
    h4 = _ffn_ln(h3.reshape(n, d), *W['ffn2'], W['ln_g'][3], W['ln_b'][3], alpha=alpha, tm=tm)
    return h4.reshape(B, T, d), s_new, pa3[:, -1], h_last.reshape(B, lru_w), conv_in_tail


def kernel(x_prompt, x_sample, mem_prompt, cache_mem_k, cache_mem_v, state_rwkv, state_rwkv_shift, state_lru, state_conv, ln_g, ln_b, ffn1_wi, ffn1_wo, ffn2_wi, ffn2_wo, w_in, shift_mu, decay_w0, decay_w2, aaa_a0, aaa_a2, gate_g2, k_k, k_a, r_k, gn_g, gn_b, conv_w, conv_b, lru_wr, lru_br, lru_wi, lru_bi, lru_lambda, w_mix_out, xa_wq, xa_wk, xa_wv, xa_wo):
    depth = ln_g.shape[0]
    alpha = (2.0 * depth) ** 0.25
    B, _, d = x_prompt.shape
    n_mem, xa_heads, xa_head = cache_mem_k.shape[2:]
    rp = shift_mu.shape[1]
    lru_w = conv_b.shape[1]
    hp, hs = x_prompt, x_sample
    outs = [[] for _ in range(10)]
    for l in range(depth):
        W = _prep_layer(l, ln_g, ln_b, ffn1_wi, ffn1_wo, ffn2_wi, ffn2_wo, w_in, shift_mu, decay_w0, decay_w2,
                        aaa_a0, aaa_a2, gate_g2, k_k, k_a, r_k.reshape(depth, -1), gn_g, gn_b, conv_w, conv_b,
                        lru_wr, lru_br, lru_wi, lru_bi, lru_lambda, w_mix_out, xa_wq, xa_wk, xa_wv, xa_wo)
        mem2 = mem_prompt.reshape(B * n_mem, d)
        mkv = _matmul(mem2, W['xa_wkv'], tm=_tile(B * n_mem, 512), tn=d).reshape(B, n_mem, 2 * d)
        mk, mv = mkv[:, :, :d], mkv[:, :, d:]
        hp, S1, sh1, h1, b1 = _layer(
            hp, mk, mv, None, jnp.zeros((B, rp), F32), jnp.zeros((B, lru_w), F32),
            jnp.zeros((B, CONV_WIDTH - 1, lru_w), F32), W, alpha=alpha, xa_heads=xa_heads)
        Bs = hs.shape[0]
        hs, S2, sh2, h2, b2 = _layer(
            hs, cache_mem_k[l].reshape(Bs, n_mem, d), cache_mem_v[l].reshape(Bs, n_mem, d),
            state_rwkv[l], state_rwkv_shift[l], state_lru[l], state_conv[l], W, alpha=alpha, xa_heads=xa_heads)
        for lst, val in zip(outs, (mk.reshape(B, n_mem, xa_heads, xa_head), mv.reshape(B, n_mem, xa_heads, xa_head),
                                   S1, sh1, h1, b1, S2, sh2, h2, b2)):
            lst.append(val)
    return (hp, hs) + tuple(jnp.stack(o) for o in outs)
```

```python
import functools

import jax
import jax.numpy as jnp
from jax import lax
from jax.experimental import pallas as pl
from jax.experimental.pallas import tpu as pltpu

F32 = jnp.float32
BF16 = jnp.bfloat16

RWKV_HEAD = 64
DECAY_LORA = 64
AAA_LORA = 64
GATE_LORA = 128
GN_EPS = 64e-5
LRU_BLOCK = 64
CONV_WIDTH = 4
LRU_C = 8.0
LN_EPS = 1e-5

V7X_LANES = 128
V7X_SUBLANES = 8
V7X_MXU_DIM = 256
V7X_SCOPED_VMEM_BYTES = 60000 * 1024

RWKV_CHUNK = 64
SOLVE_PASSES = 1
STATE_PASSES = 1
HEAD_PAIR = 2 * RWKV_HEAD
LRU_GROUP = V7X_MXU_DIM


def _cparams(semantics):
    return pltpu.CompilerParams(dimension_semantics=semantics, vmem_limit_bytes=V7X_SCOPED_VMEM_BYTES)


def _const_spec(shape):
    zeros = (0,) * len(shape)
    return pl.BlockSpec(shape, lambda *_: zeros)


def _dot(a, b):
    return jnp.dot(a.astype(BF16), b.astype(BF16), preferred_element_type=F32)


def _dot_dims(a, b, dims):
    return lax.dot_general(a.astype(BF16), b.astype(BF16), (dims, ((), ())), preferred_element_type=F32)


_NN = ((1,), (0,))
_NT = ((1,), (1,))
_TN = ((0,), (0,))


def _split2(x):
    hi = x.astype(BF16)
    lo = (x - hi.astype(F32)).astype(BF16)
    return hi, lo


def _dot3(a, b, dims=_NN):
    a_hi, a_lo = _split2(a)
    b_hi, b_lo = _split2(b)
    d = lambda x, y: lax.dot_general(x, y, (dims, ((), ())), preferred_element_type=F32)
    return d(a_hi, b_hi) + (d(a_hi, b_lo) + d(a_lo, b_hi))


def _dot_exact_lhs(a_bf16, b):
    hi, lo = _split2(b)
    lo2 = (b - hi.astype(F32) - lo.astype(F32)).astype(BF16)
    d = lambda y: jnp.dot(a_bf16, y, preferred_element_type=F32)
    return d(hi) + (d(lo) + d(lo2))


def _dot_exact_rhs(a, b_bf16):
    hi, lo = _split2(a)
    d = lambda x: jnp.dot(x, b_bf16, preferred_element_type=F32)
    return d(hi) + d(lo)


def _layer_norm(x, g, b):
    mu = jnp.mean(x, axis=-1, keepdims=True)
    xc = x - mu
    var = jnp.mean(xc * xc, axis=-1, keepdims=True)
    return xc * lax.rsqrt(var + LN_EPS) * g + b


def _softplus(z):
    return jnp.maximum(z, 0.0) + jnp.log(1.0 + jnp.exp(-jnp.abs(z)))


def _head_ones(width):
    r = lax.broadcasted_iota(jnp.int32, (width, width), 0) // RWKV_HEAD
    c = lax.broadcasted_iota(jnp.int32, (width, width), 1) // RWKV_HEAD
    return (r == c).astype(BF16)


def _mm_kernel(x_ref, w_ref, o_ref):
    o_ref[...] = jnp.dot(x_ref[...].astype(BF16), w_ref[...], preferred_element_type=F32)


def _matmul(x, w, *, tm, tn):
    n, k = x.shape
    m = w.shape[1]
    assert n % tm == 0 and m % tn == 0
    return pl.pallas_call(
        _mm_kernel,
        grid=(m // tn, n // tm),
        in_specs=[pl.BlockSpec((tm, k), lambda j, i: (i, 0)),
                  pl.BlockSpec((k, tn), lambda j, i: (0, j))],
        out_specs=pl.BlockSpec((tm, tn), lambda j, i: (i, j)),
        out_shape=jax.ShapeDtypeStruct((n, m), F32),
        compiler_params=_cparams(("parallel", "parallel")),
        name="matmul",
    )(x, w)


def _ffn_kernel(x_ref, wg_ref, wu_ref, wo_ref, g_ref, b_ref, o_ref, *, alpha, n_chunks):
    x = x_ref[...]
    xb = x.astype(BF16)
    ck = wg_ref.shape[1] // n_chunks
    acc = jnp.zeros(x.shape, F32)
    for c in range(n_chunks):
        sl = slice(c * ck, (c + 1) * ck)
        gate = jnp.dot(xb, wg_ref[:, sl], preferred_element_type=F32)
        up = jnp.dot(xb, wu_ref[:, sl], preferred_element_type=F32)
        mid = (gate * jax.nn.sigmoid(gate) * up).astype(BF16)
        acc = acc + jnp.dot(mid, wo_ref[sl, :], preferred_element_type=F32)
    o_ref[...] = _layer_norm(alpha * x + 0.5 * acc, g_ref[...], b_ref[...])


def _ffn_ln(x, wg, wu, wo, g, b, *, alpha, tm):
    n, d = x.shape
    assert n % tm == 0
    d_ff = wg.shape[1]
    n_chunks = 2 if d_ff % (2 * V7X_LANES) == 0 else 1
    return pl.pallas_call(
        functools.partial(_ffn_kernel, alpha=alpha, n_chunks=n_chunks),
        grid=(n // tm,),
        in_specs=[pl.BlockSpec((tm, d), lambda i: (i, 0)),
                  _const_spec(wg.shape), _const_spec(wu.shape), _const_spec(wo.shape),
                  _const_spec(g.shape), _const_spec(b.shape)],
        out_specs=pl.BlockSpec((tm, d), lambda i: (i, 0)),
        out_shape=jax.ShapeDtypeStruct((n, d), F32),
        compiler_params=_cparams(("parallel",)),
        name="ffn_ln",
    )(x, wg, wu, wo, g, b)


def _rwkv_pre(r, k, v, xx, w0, w2, a0, a2, g2, k_k, k_a, r_k, ones):
    xw = xx[:, :DECAY_LORA]
    xa = xx[:, DECAY_LORA:DECAY_LORA + AAA_LORA]
    xg = xx[:, DECAY_LORA + AAA_LORA:]
    z = w0 + _dot(jnp.tanh(xw), w2)
    lw = -jnp.exp(-_softplus(-z) - 0.5)
    a = jax.nn.sigmoid(a0 + _dot(xa, a2))
    g = _dot(jax.nn.sigmoid(xg), g2)
    kkr = k * k_k
    ss = _dot_exact_rhs(kkr * kkr, ones)
    kk = kkr / jnp.maximum(jnp.sqrt(ss), 1e-12)
    kf = k * (1.0 + (a - 1.0) * k_a)
    bonus = _dot_exact_rhs(r * kf * r_k, ones) * v
    return lw, a, g, kk, kf, bonus


def _rwkv_post(y, bonus, g, gn_g, gn_b, ones):
    inv_n = 1.0 / RWKV_HEAD
    ym = _dot_exact_rhs(y, ones) * inv_n
    yc = y - ym
    yv = _dot_exact_rhs(yc * yc, ones) * inv_n
    yn = yc * lax.rsqrt(yv + GN_EPS) * gn_g + gn_b
    return (yn + bonus) * g


def _bdot(a, b, dims, passes):
    dn = ((tuple(d + 1 for d in dims[0]), tuple(d + 1 for d in dims[1])), ((0,), (0,)))
    d = lambda x, y: lax.dot_general(x, y, dn, preferred_element_type=F32)
    if passes == 1:
        return d(a.astype(BF16), b.astype(BF16))
    a_hi, a_lo = _split2(a)
    b_hi, b_lo = _split2(b)
    return d(a_hi, b_hi) + (d(a_hi, b_lo) + d(a_lo, b_hi))


def _scan_operands(r, kf, v, kk, a, lw):
    tt, hw = r.shape
    C = RWKV_CHUNK
    n_pairs = hw // HEAD_PAIR
    ltri = (lax.broadcasted_iota(jnp.int32, (C, C), 0) >= lax.broadcasted_iota(jnp.int32, (C, C), 1)).astype(BF16)
    first = lax.broadcasted_iota(jnp.int32, (C, HEAD_PAIR), 1) < RWKV_HEAD

    def stack(x):
        return jnp.concatenate([jnp.where(first, x, 0.0), jnp.where(first, 0.0, x)], axis=0)

    names = ('a', 'r', 'b', 'k', 'v', 'bh', 'kh')
    ops = {n: [] for n in names}
    wcs = []
    for c in range(tt // C):
        rows = slice(c * C, (c + 1) * C)
        lw_c = lw[rows]
        L = _dot_exact_lhs(ltri, lw_c)
        Lc = L[C - 1:C, :]
        e_nl = jnp.exp(-L)
        e_c = jnp.exp(Lc - L)
        bb = kk[rows] * a[rows]
        tile = dict(a=-kk[rows] * jnp.exp(L - lw_c), r=r[rows] * jnp.exp(L), b=bb * e_nl, k=kf[rows] * e_nl,
                    v=v[rows], bh=bb * e_c, kh=kf[rows] * e_c)
        wc = jnp.exp(Lc)
        for p in range(n_pairs):
            lanes = slice(p * HEAD_PAIR, (p + 1) * HEAD_PAIR)
            for n in names:
                ops[n].append(stack(tile[n][:, lanes]))
            wcs.append(wc[:, lanes])
    As, Rs, Bs, Ks, Vs, Bh, Kh = (jnp.stack(ops[n]) for n in names)
    n2 = 2 * C
    G = _bdot(jnp.concatenate([As, Rs], axis=1), jnp.concatenate([Bs, Ks], axis=1), _NT, SOLVE_PASSES)
    ri = lax.broadcasted_iota(jnp.int32, (n2, n2), 0)
    ci = lax.broadcasted_iota(jnp.int32, (n2, n2), 1)
    a_ab = jnp.where(ri > ci, G[:, :n2, :n2], 0.0)
    a_ak = jnp.where(ri > ci, G[:, :n2, n2:], 0.0)
    a_rb = jnp.where(ri >= ci, G[:, n2:, :n2], 0.0)
    a_rk = jnp.where(ri >= ci, G[:, n2:, n2:], 0.0)
    P = jnp.where(ri == ci, 1.0, 0.0) + a_ab
    N = _bdot(a_ab, a_ab, _NN, SOLVE_PASSES)
    steps = 2
    while 2 * steps < C:
        NP = _bdot(N, jnp.concatenate([N, P], axis=2), _NN, SOLVE_PASSES)
        N = NP[:, :, :n2]
        P = P + NP[:, :, n2:]
        steps *= 2
    P = P + _bdot(N, P, _NN, SOLVE_PASSES)
    XU = _bdot(P, jnp.concatenate([As, _bdot(a_ak, Vs, _NN, SOLVE_PASSES)], axis=2), _NN, SOLVE_PASSES)
    X1 = XU[:, :, :HEAD_PAIR]
    Uloc = XU[:, :, HEAD_PAIR:]
    UV = jnp.concatenate([Uloc, Vs], axis=1)
    Q = Rs + _bdot(a_rb, X1, _NN, STATE_PASSES)
    Yloc = _bdot(jnp.concatenate([a_rb, a_rk], axis=2), UV, _NN, STATE_PASSES)
    Pm = _bdot(X1, Bh, _TN, STATE_PASSES)
    Sloc = _bdot(UV, jnp.concatenate([Bh, Kh], axis=1), _TN, STATE_PASSES)
    return Q, Yloc, Pm, Sloc, jnp.stack(wcs)


def _rwkv_chunk_kernel(pr_ref, pk_ref, pv_ref, px_ref, sr_ref, sk_ref, sv_ref, sx_ref,
                       mur_ref, muk_ref, muv_ref, mux_ref, w0_ref, w2_ref, a0_ref, a2_ref, g2_ref,
                       kk_ref, ka_ref, rk_ref, gng_ref, gnb_ref,
                       out_ref, s_out_ref,
                       s_scr, cr_scr, ck_scr, cv_scr, cx_scr, y_scr):
    ti = pl.program_id(2)
    tt, hw = pr_ref.shape
    n_pairs = hw // HEAD_PAIR

    @pl.when(ti == 0)
    def _init():
        s_scr[...] = jnp.zeros(s_scr.shape, F32)
        cr_scr[...] = sr_ref[...]
        ck_scr[...] = sk_ref[...]
        cv_scr[...] = sv_ref[...]
        cx_scr[...] = sx_ref[...]

    def shifted(p_ref, c_scr, mu_ref):
        p = p_ref[...]
        row = lax.broadcasted_iota(jnp.int32, p.shape, 0)
        prev = jnp.where(row == 0, c_scr[...], pltpu.roll(p, 1, 0))
        c_scr[...] = p_ref[tt - 1:tt, :]
        return p + (prev - p) * mu_ref[...]

    r = shifted(pr_ref, cr_scr, mur_ref)
    k = shifted(pk_ref, ck_scr, muk_ref)
    v = shifted(pv_ref, cv_scr, muv_ref)
    xx = shifted(px_ref, cx_scr, mux_ref)
    ones = _head_ones(hw)
    lw, a, g, kk, kf, bonus = _rwkv_pre(r, k, v, xx, w0_ref[...], w2_ref[...], a0_ref[...], a2_ref[...],
                                        g2_ref[...], kk_ref[...], ka_ref[...], rk_ref[...], ones)
    C = RWKV_CHUNK
    Q, Yloc, Pm, Sloc, wc = _scan_operands(r, kf, v, kk, a, lw)
    S = s_scr[...]
    for c in range(tt // C):
        inst = slice(c * n_pairs, (c + 1) * n_pairs)
        ys = _bdot(Q[inst], S, _NT, STATE_PASSES) + Yloc[inst]
        y_c = ys[:, :C, :] + ys[:, C:, :]
        for p in range(n_pairs):
            y_scr[c * C:(c + 1) * C, p * HEAD_PAIR:(p + 1) * HEAD_PAIR] = y_c[p]
        S = S * wc[inst] + _bdot(S, Pm[inst], _NN, STATE_PASSES) + Sloc[inst]
    s_scr[...] = S
    out_ref[...] = _rwkv_post(y_scr[...], bonus, g, gng_ref[...], gnb_ref[...], ones)

    @pl.when(ti == pl.num_programs(2) - 1)
    def _emit_state():
        for p in range(n_pairs):
            s_out_ref[2 * p] = S[p, :RWKV_HEAD, :RWKV_HEAD]
            s_out_ref[2 * p + 1] = S[p, RWKV_HEAD:, RWKV_HEAD:]


def _rwkv_chunked(pa, shift0, W, *, tt, hw):
    B, T, _ = pa.shape
    width = W['decay_w0'].shape[1]
    heads = width // RWKV_HEAD
    assert T % tt == 0 and tt % RWKV_CHUNK == 0 and width % hw == 0 and hw % HEAD_PAIR == 0
    nb = width // hw
    xw = DECAY_LORA + AAA_LORA + GATE_LORA
    assert (3 * width) % xw == 0
    xblk = 3 * width // xw
    col = lambda off: (lambda b, h, t: (b, t, off + h))
    vec = lambda: pl.BlockSpec((1, hw), lambda b, h, t: (0, h))
    in_specs = [
        pl.BlockSpec((None, tt, hw), col(0)), pl.BlockSpec((None, tt, hw), col(nb)),
        pl.BlockSpec((None, tt, hw), col(2 * nb)), pl.BlockSpec((None, tt, xw), lambda b, h, t: (b, t, xblk)),
        pl.BlockSpec((None, 1, hw), lambda b, h, t: (b, 0, h)), pl.BlockSpec((None, 1, hw), lambda b, h, t: (b, 0, nb + h)),
        pl.BlockSpec((None, 1, hw), lambda b, h, t: (b, 0, 2 * nb + h)), pl.BlockSpec((None, 1, xw), lambda b, h, t: (b, 0, xblk)),
        vec(), vec(), vec(), _const_spec((1, xw)),
        vec(), pl.BlockSpec((DECAY_LORA, hw), lambda b, h, t: (0, h)),
        vec(), pl.BlockSpec((AAA_LORA, hw), lambda b, h, t: (0, h)),
        pl.BlockSpec((GATE_LORA, hw), lambda b, h, t: (0, h)),
        vec(), vec(), vec(), vec(), vec(),
    ]
    out, s_new = pl.pallas_call(
        _rwkv_chunk_kernel,
        grid=(B, nb, T // tt),
        in_specs=in_specs,
        out_specs=[pl.BlockSpec((None, tt, hw), lambda b, h, t: (b, t, h)),
                   pl.BlockSpec((None, hw // RWKV_HEAD, RWKV_HEAD, RWKV_HEAD), lambda b, h, t: (b, h, 0, 0))],
        out_shape=[jax.ShapeDtypeStruct((B, T, width), F32),
                   jax.ShapeDtypeStruct((B, heads, RWKV_HEAD, RWKV_HEAD), F32)],
        scratch_shapes=[pltpu.VMEM((hw // HEAD_PAIR, HEAD_PAIR, HEAD_PAIR), F32),
                        pltpu.VMEM((1, hw), F32), pltpu.VMEM((1, hw), F32), pltpu.VMEM((1, hw), F32),
                        pltpu.VMEM((1, xw), F32), pltpu.VMEM((tt, hw), F32)],
        compiler_params=_cparams(("parallel", "parallel", "arbitrary")),
        name="rwkv_chunked",
    )(pa, pa, pa, pa, shift0, shift0, shift0, shift0,
      W['mu_r'], W['mu_k'], W['mu_v'], W['mu_x'], W['decay_w0'], W['decay_w2'], W['aaa_a0'], W['aaa_a2'],
      W['gate_g2'], W['k_k'], W['k_a'], W['r_k'], W['gn_g'], W['gn_b'])
    return out, s_new


def _rwkv_step_kernel(p_ref, s0_ref, st_ref, mu_ref, w0_ref, w2_ref, a0_ref, a2_ref, g2_ref,
                      kk_ref, ka_ref, rk_ref, gng_ref, gnb_ref,
                      out_ref, st_out_ref, vec_scr, y_scr):
    bb, width = out_ref.shape
    heads = width // RWKV_HEAD
    p = p_ref[...]
    xs = p + (s0_ref[...] - p) * mu_ref[...]
    r, k, v, xx = xs[:, :width], xs[:, width:2 * width], xs[:, 2 * width:3 * width], xs[:, 3 * width:]
    ones = _head_ones(width)
    lw, a, g, kk, kf, bonus = _rwkv_pre(r, k, v, xx, w0_ref[...], w2_ref[...], a0_ref[...], a2_ref[...],
                                        g2_ref[...], kk_ref[...], ka_ref[...], rk_ref[...], ones)
    vec_scr[0] = r
    vec_scr[1] = kf
    vec_scr[2] = v
    vec_scr[3] = kk
    vec_scr[4] = kk * a
    vec_scr[5] = jnp.exp(lw)
    eye = (lax.broadcasted_iota(jnp.int32, (RWKV_HEAD, RWKV_HEAD), 0)
           == lax.broadcasted_iota(jnp.int32, (RWKV_HEAD, RWKV_HEAD), 1))

    def body(b, carry):
        def per_head(i):
            full = vec_scr[i, pl.ds(b, 1), :]
            return jnp.stack([full[:, h * RWKV_HEAD:(h + 1) * RWKV_HEAD] for h in range(heads)], axis=0)

        S = st_ref[b]
        sa = jnp.sum(S * per_head(3), axis=2, keepdims=True)
        v_col = jnp.sum(jnp.where(eye, per_head(2), 0.0), axis=2, keepdims=True)
        S2 = S * per_head(5) - sa * per_head(4) + v_col * per_head(1)
        st_out_ref[b] = S2
        y_col = jnp.sum(S2 * per_head(0), axis=2, keepdims=True)
        y_row = jnp.sum(jnp.where(eye, y_col, 0.0), axis=1, keepdims=True)
        y_scr[pl.ds(b, 1), :] = jnp.concatenate([y_row[h] for h in range(heads)], axis=1)
        return carry

    lax.fori_loop(0, bb, body, 0)
    out_ref[...] = _rwkv_post(y_scr[...], bonus, g, gng_ref[...], gnb_ref[...], ones)


def _rwkv_step(pa, shift0, state, W, *, bb):
    B, proj = pa.shape
    width = W['decay_w0'].shape[1]
    heads = width // RWKV_HEAD
    assert B % bb == 0
    row = lambda w: pl.BlockSpec((bb, w), lambda i: (i, 0))
    st_spec = pl.BlockSpec((bb, heads, RWKV_HEAD, RWKV_HEAD), lambda i: (i, 0, 0, 0))
    names = ('decay_w0', 'decay_w2', 'aaa_a0', 'aaa_a2', 'gate_g2', 'k_k', 'k_a', 'r_k', 'gn_g', 'gn_b')
    out, st_new = pl.pallas_call(
        _rwkv_step_kernel,
        grid=(B // bb,),
        in_specs=[row(proj), row(proj), st_spec, _const_spec(W['shift_mu'].shape)]
                 + [_const_spec(W[n].shape) for n in names],
        out_specs=[row(width), st_spec],
        out_shape=[jax.ShapeDtypeStruct((B, width), F32), jax.ShapeDtypeStruct(state.shape, F32)],
        scratch_shapes=[pltpu.VMEM((6, bb, width), F32), pltpu.VMEM((bb, width), F32)],
        compiler_params=_cparams(("parallel",)),
        name="rwkv_step",
    )(pa, shift0, state, W['shift_mu'], *[W[n] for n in names])
    return out, st_new


def _lru_coeffs(xc, wr_ref, br_ref, wi_ref, bi_ref, lam_ref):
    xcb = xc.astype(BF16)
    n_grp = xc.shape[1] // LRU_GROUP
    grp = lambda w_ref: jnp.concatenate(
        [jnp.dot(xcb[:, i * LRU_GROUP:(i + 1) * LRU_GROUP], w_ref[i], preferred_element_type=F32)
         for i in range(n_grp)], axis=1)
    gr = jax.nn.sigmoid(grp(wr_ref) + br_ref[...])
    gi = jax.nn.sigmoid(grp(wi_ref) + bi_ref[...])
    log_a = -LRU_C * gr * _softplus(-lam_ref[...])
    return jnp.exp(log_a), jnp.sqrt(1.0 - jnp.exp(2.0 * log_a)) * gi * xc


def _lru_finish(hs, pg_ref, g0_ref, g1_ref, rw_ref, h_ref, wmix_ref, lng_ref, lnb_ref, alpha):
    lru_out = hs * jax.nn.gelu(pg_ref[...])
    merged = jax.nn.sigmoid(g0_ref[...]) * rw_ref[...] + jax.nn.sigmoid(g1_ref[...]) * lru_out
    mix = jnp.dot(merged.astype(BF16), wmix_ref[...], preferred_element_type=F32)
    return _layer_norm(alpha * h_ref[...] + mix, lng_ref[...], lnb_ref[...])


def _lru_kernel(x_ref, pg_ref, g0_ref, g1_ref, rw_ref, h_ref, buf_ref, h0_ref,
                cw_ref, cb_ref, wr_ref, br_ref, wi_ref, bi_ref, lam_ref, wmix_ref, lng_ref, lnb_ref,
                out_ref, hlast_ref, tail_scr, hc_scr, *, alpha):
    ti = pl.program_id(1)
    tt, width = x_ref.shape
    S8 = V7X_SUBLANES

    @pl.when(ti == 0)
    def _init():
        tail_scr[...] = buf_ref[...]
        hc_scr[...] = h0_ref[...]

    x = x_ref[...]
    tail = tail_scr[...]
    r8 = lax.broadcasted_iota(jnp.int32, (S8, width), 0)

    def delayed(d):
        rolled = pltpu.roll(x, d, 0)
        head = jnp.where(r8 < d, pltpu.roll(tail, d, 0), rolled[:S8])
        return jnp.concatenate([head, rolled[S8:]], axis=0)

    cw = cw_ref[...]
    conv = cw[0:1] * delayed(CONV_WIDTH - 1)
    for j in range(1, CONV_WIDTH - 1):
        conv = conv + cw[j:j + 1] * delayed(CONV_WIDTH - 1 - j)
    xc = cb_ref[...] + (conv + cw[CONV_WIDTH - 1:CONV_WIDTH] * x)
    tail_scr[...] = x_ref[tt - S8:tt, :]

    A, Bv = _lru_coeffs(xc, wr_ref, br_ref, wi_ref, bi_ref, lam_ref)
    rowi = lax.broadcasted_iota(jnp.int32, (tt, width), 0)
    s = 1
    while s < tt:
        keep = rowi >= s
        Bv = Bv + A * jnp.where(keep, pltpu.roll(Bv, s, 0), 0.0)
        A = A * jnp.where(keep, pltpu.roll(A, s, 0), 1.0)
        s *= 2
    hs = Bv + A * hc_scr[...]
    h_last = hs[tt - 1:tt, :]
    hc_scr[...] = h_last
    hlast_ref[...] = h_last
    out_ref[...] = _lru_finish(hs, pg_ref, g0_ref, g1_ref, rw_ref, h_ref, wmix_ref, lng_ref, lnb_ref, alpha)


def _lru_step_kernel(x_ref, pg_ref, g0_ref, g1_ref, rw_ref, h_ref, buf_ref, h0_ref,
                     cw_ref, cb_ref, wr_ref, br_ref, wi_ref, bi_ref, lam_ref, wmix_ref, lng_ref, lnb_ref,
                     out_ref, hnew_ref, *, alpha):
    cw = cw_ref[...]
    conv = cw[0:1] * buf_ref[0]
    for j in range(1, CONV_WIDTH - 1):
        conv = conv + cw[j:j + 1] * buf_ref[j]
    xc = cb_ref[...] + (conv + cw[CONV_WIDTH - 1:CONV_WIDTH] * x_ref[...])
    A, Bv = _lru_coeffs(xc, wr_ref, br_ref, wi_ref, bi_ref, lam_ref)
    hs = Bv + A * h0_ref[...]
    hnew_ref[...] = hs
    out_ref[...] = _lru_finish(hs, pg_ref, g0_ref, g1_ref, rw_ref, h_ref, wmix_ref, lng_ref, lnb_ref, alpha)


_LRU_WEIGHTS = ('conv_w', 'conv_b', 'lru_wr', 'lru_br', 'lru_wi', 'lru_bi', 'lru_lambda', 'w_mix_out')


def _lru_step_mix_ln(pb, rw, h, buf, h0, W, ln_g, ln_b, *, alpha, tm):
    B, width = rw.shape
    assert B % tm == 0
    blk = lambda j: pl.BlockSpec((tm, width), lambda i: (i, j))
    out, h_new = pl.pallas_call(
        functools.partial(_lru_step_kernel, alpha=alpha),
        grid=(B // tm,),
        in_specs=[blk(0), blk(1), blk(2), blk(3), blk(0), blk(0),
                  pl.BlockSpec((CONV_WIDTH - 1, tm, width), lambda i: (0, i, 0)), blk(0)]
                 + [_const_spec(W[n].shape) for n in _LRU_WEIGHTS] + [_const_spec(ln_g.shape), _const_spec(ln_b.shape)],
        out_specs=[blk(0), blk(0)],
        out_shape=[jax.ShapeDtypeStruct((B, width), F32), jax.ShapeDtypeStruct((B, width), F32)],
        compiler_params=_cparams(("parallel",)),
        name="lru_step_mix_ln",
    )(pb, pb, pb, pb, rw, h, buf, h0, *[W[n] for n in _LRU_WEIGHTS], ln_g, ln_b)
    return out, h_new


def _lru_mix_ln(pb, rw, h, buf8, h0, W, ln_g, ln_b, *, alpha, tt):
    B, T, width = rw.shape
    assert T % tt == 0 and tt % V7X_SUBLANES == 0
    blk = lambda j: pl.BlockSpec((None, tt, width), lambda b, t: (b, t, j))
    names = _LRU_WEIGHTS
    out, h_last = pl.pallas_call(
        functools.partial(_lru_kernel, alpha=alpha),
        grid=(B, T // tt),
        in_specs=[blk(0), blk(1), blk(2), blk(3), blk(0), blk(0),
                  pl.BlockSpec((None, V7X_SUBLANES, width), lambda b, t: (b, 0, 0)),
                  pl.BlockSpec((None, 1, width), lambda b, t: (b, 0, 0))]
                 + [_const_spec(W[n].shape) for n in names] + [_const_spec(ln_g.shape), _const_spec(ln_b.shape)],
        out_specs=[blk(0), pl.BlockSpec((None, 1, width), lambda b, t: (b, 0, 0))],
        out_shape=[jax.ShapeDtypeStruct((B, T, width), F32), jax.ShapeDtypeStruct((B, 1, width), F32)],
        scratch_shapes=[pltpu.VMEM((V7X_SUBLANES, width), F32), pltpu.VMEM((1, width), F32)],
        compiler_params=_cparams(("parallel", "arbitrary")),
        name="lru_mix_ln",
    )(pb, pb, pb, pb, rw, h, buf8, h0, *[W[n] for n in names], ln_g, ln_b)
    return out, h_last


def _xattn_kernel(h_ref, kv_ref, wq_ref, wo_ref, g_ref, b_ref, o_ref, *, alpha, heads):
    h = h_ref[...]
    d = h.shape[1]
    hd = d // heads
    q = jnp.dot(h.astype(BF16), wq_ref[...], preferred_element_type=F32)
    outs = []
    for j in range(heads):
        sl = slice(j * hd, (j + 1) * hd)
        s = _dot_dims(q[:, sl], kv_ref[:, sl], _NT) * (hd ** -0.5)
        e = jnp.exp(s - jnp.max(s, axis=-1, keepdims=True))
        p = e / jnp.sum(e, axis=-1, keepdims=True)
        outs.append(_dot(p, kv_ref[:, d + j * hd:d + (j + 1) * hd]))
    out = jnp.dot(jnp.concatenate(outs, axis=1).astype(BF16), wo_ref[...], preferred_element_type=F32)
    o_ref[...] = _layer_norm(alpha * h + out, g_ref[...], b_ref[...])


def _xattn_rows_kernel(h_ref, mk_ref, mv_ref, wq_ref, wo_ref, g_ref, b_ref, o_ref, *, alpha):
    h = h_ref[...]
    nb, m, heads, hd = mk_ref.shape
    q = jnp.dot(h.astype(BF16), wq_ref[...], preferred_element_type=F32)
    col = lax.broadcasted_iota(jnp.int32, (heads, m * heads), 1)
    own = (col % heads) == lax.broadcasted_iota(jnp.int32, (heads, m * heads), 0)
    rows = []
    for i in range(nb):
        k2 = mk_ref[i].reshape(m * heads, hd)
        v2 = mv_ref[i].reshape(m * heads, hd)
        q4 = jnp.concatenate([q[i:i + 1, j * hd:(j + 1) * hd] for j in range(heads)], axis=0)
        s = jnp.where(own, _dot_dims(q4, k2, _NT) * (hd ** -0.5), -jnp.inf)
        e = jnp.exp(s - jnp.max(s, axis=-1, keepdims=True))
        p = e / jnp.sum(e, axis=-1, keepdims=True)
        o4 = _dot(p, v2)
        rows.append(jnp.concatenate([o4[j:j + 1] for j in range(heads)], axis=1))
    out = jnp.dot(jnp.concatenate(rows, axis=0).astype(BF16), wo_ref[...], preferred_element_type=F32)
    o_ref[...] = _layer_norm(alpha * h + out, g_ref[...], b_ref[...])


def _xattn_ln(h, kv, wq, wo, g, b, *, alpha, heads, rows_per_batch, tile_rows):
    n, d = h.shape
    assert n % tile_rows == 0 and rows_per_batch % tile_rows == 0
    per = rows_per_batch // tile_rows
    m = kv.shape[0] // (n // rows_per_batch)
    return pl.pallas_call(
        functools.partial(_xattn_kernel, alpha=alpha, heads=heads),
        grid=(n // tile_rows,),
        in_specs=[pl.BlockSpec((tile_rows, d), lambda i: (i, 0)),
                  pl.BlockSpec((m, 2 * d), lambda i: (i // per, 0)),
                  _const_spec(wq.shape), _const_spec(wo.shape), _const_spec(g.shape), _const_spec(b.shape)],
        out_specs=pl.BlockSpec((tile_rows, d), lambda i: (i, 0)),
        out_shape=jax.ShapeDtypeStruct((n, d), F32),
        compiler_params=_cparams(("parallel",)),
        name="xattn_ln",
    )(h, kv, wq, wo, g, b)


def _xattn_rows_ln(h, mk, mv, wq, wo, g, b, *, alpha, nb):
    n, d = h.shape
    _, m, heads, hd = mk.shape
    assert n % nb == 0
    kv_spec = pl.BlockSpec((nb, m, heads, hd), lambda i: (i, 0, 0, 0))
    return pl.pallas_call(
        functools.partial(_xattn_rows_kernel, alpha=alpha),
        grid=(n // nb,),
        in_specs=[pl.BlockSpec((nb, d), lambda i: (i, 0)), kv_spec, kv_spec,
                  _const_spec(wq.shape), _const_spec(wo.shape), _const_spec(g.shape), _const_spec(b.shape)],
        out_specs=pl.BlockSpec((nb, d), lambda i: (i, 0)),
        out_shape=jax.ShapeDtypeStruct((n, d), F32),
        compiler_params=_cparams(("parallel",)),
        name="xattn_rows_ln",
    )(h, mk, mv, wq, wo, g, b)


def _kv_proj_kernel(x_ref, w_ref, kv_ref, k_ref, v_ref):
    heads, hd = k_ref.shape[1:]
    d = heads * hd
    kv = jnp.dot(x_ref[...].astype(BF16), w_ref[...], preferred_element_type=F32)
    kv_ref[...] = kv
    for j in range(heads):
        k_ref[:, j, :] = kv[:, j * hd:(j + 1) * hd]
        v_ref[:, j, :] = kv[:, d + j * hd:d + (j + 1) * hd]


def _kv_proj(x, wkv, *, heads, tm):
    n, d = x.shape
    assert n % tm == 0
    hd = d // heads
    out = jax.ShapeDtypeStruct((n, heads, hd), F32)
    return pl.pallas_call(
        _kv_proj_kernel,
        grid=(n // tm,),
        in_specs=[pl.BlockSpec((tm, d), lambda i: (i, 0)), _const_spec(wkv.shape)],
        out_specs=[pl.BlockSpec((tm, 2 * d), lambda i: (i, 0))] + [pl.BlockSpec((tm, heads, hd), lambda i: (i, 0, 0))] * 2,
        out_shape=[jax.ShapeDtypeStruct((n, 2 * d), F32), out, out],
        compiler_params=_cparams(("parallel",)),
        name="kv_proj",
    )(x, wkv)


def _row(v):
    return v.reshape(1, -1)


def _block_diag_groups(w):
    n, c, _ = w.shape
    per = LRU_GROUP // c
    w4 = w.reshape(n // per, per, c, c)
    bd = jnp.einsum('gjcd,jk->gjckd', w4, jnp.eye(per, dtype=w.dtype))
    return bd.reshape(n // per, LRU_GROUP, LRU_GROUP).astype(BF16)


def _prep_layer(l, ln_g, ln_b, ffn1_wi, ffn1_wo, ffn2_wi, ffn2_wo, w_in, shift_mu, decay_w0, decay_w2,
                aaa_a0, aaa_a2, gate_g2, k_k, k_a, r_k, gn_g, gn_b, conv_w, conv_b, lru_wr, lru_br,
                lru_wi, lru_bi, lru_lambda, w_mix_out, xa_wq, xa_wk, xa_wv, xa_wo):
    width = decay_w0.shape[1]
    rp = shift_mu.shape[1]
    d_ff = ffn1_wo.shape[1]
    bf = lambda w: w.astype(BF16)
    mu = shift_mu[l]
    return dict(
        ln_g=[_row(ln_g[l, i]) for i in range(4)], ln_b=[_row(ln_b[l, i]) for i in range(4)],
        ffn1=(bf(ffn1_wi[l][:, :d_ff]), bf(ffn1_wi[l][:, d_ff:]), bf(ffn1_wo[l])),
        ffn2=(bf(ffn2_wi[l][:, :d_ff]), bf(ffn2_wi[l][:, d_ff:]), bf(ffn2_wo[l])),
        w_in_a=bf(w_in[l][:, :rp]), w_in_b=bf(w_in[l][:, rp:]),
        shift_mu=_row(mu), mu_r=_row(mu[:width]), mu_k=_row(mu[width:2 * width]),
        mu_v=_row(mu[2 * width:3 * width]), mu_x=_row(mu[3 * width:]),
        decay_w0=_row(decay_w0[l]), decay_w2=bf(decay_w2[l]), aaa_a0=_row(aaa_a0[l]), aaa_a2=bf(aaa_a2[l]),
        gate_g2=bf(gate_g2[l]), k_k=_row(k_k[l]), k_a=_row(k_a[l]), r_k=_row(r_k[l]),
        gn_g=_row(gn_g[l]), gn_b=_row(gn_b[l]),
        conv_w=conv_w[l], conv_b=_row(conv_b[l]),
        lru_wr=_block_diag_groups(lru_wr[l]), lru_br=_row(lru_br[l]),
        lru_wi=_block_diag_groups(lru_wi[l]), lru_bi=_row(lru_bi[l]), lru_lambda=_row(lru_lambda[l]),
        w_mix_out=bf(w_mix_out[l]), xa_wq=bf(xa_wq[l]), xa_wo=bf(xa_wo[l]),
        xa_wkv=bf(jnp.concatenate([xa_wk[l], xa_wv[l]], axis=1)),
    )


def _tile(n, pref):
    return pref if n % pref == 0 else n


def _layer(h, mem, state, shift0, h0, buf0, W, *, alpha, xa_heads):
    B, T, d = h.shape
    n = B * T
    tm = _tile(n, 512)
    h1 = _ffn_ln(h.reshape(n, d), *W['ffn1'], W['ln_g'][0], W['ln_b'][0], alpha=alpha, tm=tm)
    pa = _matmul(h1, W['w_in_a'], tm=tm, tn=W['w_in_a'].shape[1])
    pb = _matmul(h1, W['w_in_b'], tm=tm, tn=W['w_in_b'].shape[1] // 2)
    width = W['decay_w0'].shape[1]
    lru_w = W['conv_b'].shape[1]
    pa3 = pa.reshape(B, T, -1)
    pb3 = pb.reshape(B, T, -1)
    if T > 1:
        assert state is None
        rw, s_new = _rwkv_chunked(pa3, shift0.reshape(B, 1, -1), W, tt=_tile(T, 256), hw=_tile(width, 512))
        buf8 = jnp.concatenate([jnp.zeros((B, V7X_SUBLANES - (CONV_WIDTH - 1), lru_w), F32), buf0], axis=1)
        h2, h_last = _lru_mix_ln(pb3, rw, h1.reshape(B, T, d), buf8, h0.reshape(B, 1, lru_w), W,
                                 W['ln_g'][1], W['ln_b'][1], alpha=alpha, tt=_tile(T, 256))
    else:
        rw, s_new = _rwkv_step(pa, shift0, state, W, bb=_tile(B, 2 * V7X_SUBLANES))
        h2, h_last = _lru_step_mix_ln(pb, rw, h1, jnp.swapaxes(buf0, 0, 1), h0, W,
                                      W['ln_g'][1], W['ln_b'][1], alpha=alpha, tm=_tile(B, 128))
    conv_in_tail = jnp.concatenate([buf0, pb3[:, :, :lru_w]], axis=1)[:, -(CONV_WIDTH - 1):]
    xa = (W['xa_wq'], W['xa_wo'], W['ln_g'][2], W['ln_b'][2])
    if T > 1:
        h3 = _xattn_ln(h2.reshape(n, d), mem, *xa, alpha=alpha, heads=xa_heads, rows_per_batch=T,
                       tile_rows=_tile(T, 512))
    else:
        h3 = _xattn_rows_ln(h2, *mem, *xa, alpha=alpha, nb=_tile(B, V7X_SUBLANES))
    h4 = _ffn_ln(h3, *W['ffn2'], W['ln_g'][3], W['ln_b'][3], alpha=alpha, tm=tm)
    return h4.reshape(B, T, d), s_new, pa3[:, -1], h_last.reshape(B, lru_w), conv_in_tail


def kernel(x_prompt, x_sample, mem_prompt, cache_mem_k, cache_mem_v, state_rwkv, state_rwkv_shift, state_lru, state_conv, ln_g, ln_b, ffn1_wi, ffn1_wo, ffn2_wi, ffn2_wo, w_in, shift_mu, decay_w0, decay_w2, aaa_a0, aaa_a2, gate_g2, k_k, k_a, r_k, gn_g, gn_b, conv_w, conv_b, lru_wr, lru_br, lru_wi, lru_bi, lru_lambda, w_mix_out, xa_wq, xa_wk, xa_wv, xa_wo):
    depth = ln_g.shape[0]
    alpha = (2.0 * depth) ** 0.25
    B, _, d = x_prompt.shape
    n_mem, xa_heads, xa_head = cache_mem_k.shape[2:]
    rp = shift_mu.shape[1]
    lru_w = conv_b.shape[1]
    hp, hs = x_prompt, x_sample
    outs = [[] for _ in range(10)]
    for l in range(depth):
        W = _prep_layer(l, ln_g, ln_b, ffn1_wi, ffn1_wo, ffn2_wi, ffn2_wo, w_in, shift_mu, decay_w0, decay_w2,
                        aaa_a0, aaa_a2, gate_g2, k_k, k_a, r_k.reshape(depth, -1), gn_g, gn_b, conv_w, conv_b,
                        lru_wr, lru_br, lru_wi, lru_bi, lru_lambda, w_mix_out, xa_wq, xa_wk, xa_wv, xa_wo)
        kv, mk, mv = _kv_proj(mem_prompt.reshape(B * n_mem, d), W['xa_wkv'], heads=xa_heads,
                              tm=_tile(B * n_mem, 512))
        mk = mk.reshape(B, n_mem, xa_heads, xa_head)
        mv = mv.reshape(B, n_mem, xa_heads, xa_head)
        hp, S1, sh1, h1, b1 = _layer(
            hp, kv, None, jnp.zeros((B, rp), F32), jnp.zeros((B, lru_w), F32),
            jnp.zeros((B, CONV_WIDTH - 1, lru_w), F32), W, alpha=alpha, xa_heads=xa_heads)
        hs, S2, sh2, h2, b2 = _layer(
            hs, (cache_mem_k[l], cache_mem_v[l]),
            state_rwkv[l], state_rwkv_shift[l], state_lru[l], state_conv[l], W, alpha=alpha, xa_heads=xa_heads)
        for lst, val in zip(outs, (mk, mv, S1, sh1, h1, b1, S2, sh2, h2, b2)):
            lst.append(val)
    return (hp, hs) + tuple(jnp.stack(o) for o in outs)
```

```python
import functools

import jax
import jax.numpy as jnp
from jax import lax
from jax.experimental import pallas as pl
from jax.experimental.pallas import tpu as pltpu

F32 = jnp.float32
BF16 = jnp.bfloat16

RWKV_HEAD = 64
DECAY_LORA = 64
AAA_LORA = 64
GATE_LORA = 128
GN_EPS = 64e-5
LRU_BLOCK = 64
CONV_WIDTH = 4
LRU_C = 8.0
LN_EPS = 1e-5

V7X_LANES = 128
V7X_SUBLANES = 8
V7X_MXU_DIM = 256
V7X_SCOPED_VMEM_BYTES = 60000 * 1024

RWKV_CHUNK = 64
SOLVE_PASSES = 1
STATE_PASSES = 1
HEAD_PAIR = 2 * RWKV_HEAD
LRU_GROUP = V7X_MXU_DIM


def _cparams(semantics):
    return pltpu.CompilerParams(dimension_semantics=semantics, vmem_limit_bytes=V7X_SCOPED_VMEM_BYTES)


def _const_spec(shape):
    zeros = (0,) * len(shape)
    return pl.BlockSpec(shape, lambda *_: zeros)


def _dot(a, b):
    return jnp.dot(a.astype(BF16), b.astype(BF16), preferred_element_type=F32)


def _dot_dims(a, b, dims):
    return lax.dot_general(a.astype(BF16), b.astype(BF16), (dims, ((), ())), preferred_element_type=F32)


_NN = ((1,), (0,))
_NT = ((1,), (1,))
_TN = ((0,), (0,))


def _split2(x):
    hi = x.astype(BF16)
    lo = (x - hi.astype(F32)).astype(BF16)
    return hi, lo


def _dot_exact_lhs(a_bf16, b):
    hi, lo = _split2(b)
    lo2 = (b - hi.astype(F32) - lo.astype(F32)).astype(BF16)
    d = lambda y: jnp.dot(a_bf16, y, preferred_element_type=F32)
    return d(hi) + (d(lo) + d(lo2))


def _dot_exact_rhs(a, b_bf16):
    hi, lo = _split2(a)
    d = lambda x: jnp.dot(x, b_bf16, preferred_element_type=F32)
    return d(hi) + d(lo)


def _layer_norm(x, g, b):
    mu = jnp.mean(x, axis=-1, keepdims=True)
    xc = x - mu
    var = jnp.mean(xc * xc, axis=-1, keepdims=True)
    return xc * lax.rsqrt(var + LN_EPS) * g + b


def _softplus(z):
    return jnp.maximum(z, 0.0) + jnp.log(1.0 + jnp.exp(-jnp.abs(z)))


def _head_ones(width):
    r = lax.broadcasted_iota(jnp.int32, (width, width), 0) // RWKV_HEAD
    c = lax.broadcasted_iota(jnp.int32, (width, width), 1) // RWKV_HEAD
    return (r == c).astype(BF16)


def _mm_kernel(x_ref, w_ref, o_ref):
    o_ref[...] = jnp.dot(x_ref[...].astype(BF16), w_ref[...], preferred_element_type=F32)


def _matmul(x, w, *, tm, tn):
    n, k = x.shape
    m = w.shape[1]
    assert n % tm == 0 and m % tn == 0
    return pl.pallas_call(
        _mm_kernel,
        grid=(m // tn, n // tm),
        in_specs=[pl.BlockSpec((tm, k), lambda j, i: (i, 0)),
                  pl.BlockSpec((k, tn), lambda j, i: (0, j))],
        out_specs=pl.BlockSpec((tm, tn), lambda j, i: (i, j)),
        out_shape=jax.ShapeDtypeStruct((n, m), F32),
        compiler_params=_cparams(("parallel", "parallel")),
        name="matmul",
    )(x, w)


def _ffn_kernel(x_ref, wg_ref, wu_ref, wo_ref, g_ref, b_ref, o_ref, *, alpha, n_chunks):
    x = x_ref[...]
    xb = x.astype(BF16)
    ck = wg_ref.shape[1] // n_chunks
    acc = jnp.zeros(x.shape, F32)
    for c in range(n_chunks):
        sl = slice(c * ck, (c + 1) * ck)
        gate = jnp.dot(xb, wg_ref[:, sl], preferred_element_type=F32)
        up = jnp.dot(xb, wu_ref[:, sl], preferred_element_type=F32)
        mid = (gate * jax.nn.sigmoid(gate) * up).astype(BF16)
        acc = acc + jnp.dot(mid, wo_ref[sl, :], preferred_element_type=F32)
    o_ref[...] = _layer_norm(alpha * x + 0.5 * acc, g_ref[...], b_ref[...])


def _ffn_ln(x, wg, wu, wo, g, b, *, alpha, tm):
    n, d = x.shape
    assert n % tm == 0
    d_ff = wg.shape[1]
    n_chunks = 2 if d_ff % (2 * V7X_LANES) == 0 else 1
    return pl.pallas_call(
        functools.partial(_ffn_kernel, alpha=alpha, n_chunks=n_chunks),
        grid=(n // tm,),
        in_specs=[pl.BlockSpec((tm, d), lambda i: (i, 0)),
                  _const_spec(wg.shape), _const_spec(wu.shape), _const_spec(wo.shape),
                  _const_spec(g.shape), _const_spec(b.shape)],
        out_specs=pl.BlockSpec((tm, d), lambda i: (i, 0)),
        out_shape=jax.ShapeDtypeStruct((n, d), F32),
        compiler_params=_cparams(("parallel",)),
        name="ffn_ln",
    )(x, wg, wu, wo, g, b)


def _rwkv_pre(r, k, v, xx, w0, w2, a0, a2, g2, k_k, k_a, r_k, ones):
    xw = xx[:, :DECAY_LORA]
    xa = xx[:, DECAY_LORA:DECAY_LORA + AAA_LORA]
    xg = xx[:, DECAY_LORA + AAA_LORA:]
    z = w0 + _dot(jnp.tanh(xw), w2)
    lw = -jnp.exp(-_softplus(-z) - 0.5)
    a = jax.nn.sigmoid(a0 + _dot(xa, a2))
    g = _dot(jax.nn.sigmoid(xg), g2)
    kkr = k * k_k
    ss = _dot(kkr * kkr, ones)
    kk = kkr / jnp.maximum(jnp.sqrt(ss), 1e-12)
    kf = k * (1.0 + (a - 1.0) * k_a)
    bonus = _dot(r * kf * r_k, ones) * v
    return lw, a, g, kk, kf, bonus


def _rwkv_post(y, bonus, g, gn_g, gn_b, ones):
    inv_n = 1.0 / RWKV_HEAD
    ym = _dot_exact_rhs(y, ones) * inv_n
    yc = y - ym
    yv = _dot(yc * yc, ones) * inv_n
    yn = yc * lax.rsqrt(yv + GN_EPS) * gn_g + gn_b
    return (yn + bonus) * g


def _bdot(a, b, dims, passes):
    dn = ((tuple(d + 1 for d in dims[0]), tuple(d + 1 for d in dims[1])), ((0,), (0,)))
    d = lambda x, y: lax.dot_general(x, y, dn, preferred_element_type=F32)
    if passes == 1:
        return d(a.astype(BF16), b.astype(BF16))
    a_hi, a_lo = _split2(a)
    b_hi, b_lo = _split2(b)
    return d(a_hi, b_hi) + (d(a_hi, b_lo) + d(a_lo, b_hi))


def _scan_operands(r, kf, v, kk, a, lw):
    tt, hw = r.shape
    C = RWKV_CHUNK
    n_pairs = hw // HEAD_PAIR
    ltri = (lax.broadcasted_iota(jnp.int32, (C, C), 0) >= lax.broadcasted_iota(jnp.int32, (C, C), 1)).astype(BF16)
    first = lax.broadcasted_iota(jnp.int32, (C, HEAD_PAIR), 1) < RWKV_HEAD

    def bd(x):
        return jnp.concatenate([jnp.where(first, x, 0.0), jnp.where(first, 0.0, x)], axis=1)

    names = ('a', 'r', 'b', 'k', 'v', 'bh', 'kh')
    ops = {n: [] for n in names}
    wcs = []
    for c in range(tt // C):
        rows = slice(c * C, (c + 1) * C)
        lw_c = lw[rows]
        L = _dot_exact_lhs(ltri, lw_c)
        Lc = L[C - 1:C, :]
        e_nl = jnp.exp(-L)
        e_c = jnp.exp(Lc - L)
        bb = kk[rows] * a[rows]
        tile = dict(a=-kk[rows] * jnp.exp(L - lw_c), r=r[rows] * jnp.exp(L), b=bb * e_nl, k=kf[rows] * e_nl,
                    v=v[rows], bh=bb * e_c, kh=kf[rows] * e_c)
        wc = jnp.exp(Lc)
        for p in range(n_pairs):
            lanes = slice(p * HEAD_PAIR, (p + 1) * HEAD_PAIR)
            for n in names:
                ops[n].append(tile[n][:, lanes])
            wcs.append(wc[:, lanes])
    A, R, B, K, V, Bh, Kh = (jnp.stack(ops[n]) for n in names)
    Vbd = bd(V)
    G = _bdot(jnp.concatenate([A, R], axis=1), jnp.concatenate([bd(B), bd(K)], axis=1), _NT, SOLVE_PASSES)
    tok = lax.broadcasted_iota(jnp.int32, (C, HEAD_PAIR), 0)
    src = lax.broadcasted_iota(jnp.int32, (C, HEAD_PAIR), 1) % RWKV_HEAD
    a_ab = jnp.where(tok > src, G[:, :C, :HEAD_PAIR], 0.0)
    a_ak = jnp.where(tok > src, G[:, :C, HEAD_PAIR:], 0.0)
    a_rb = jnp.where(tok >= src, G[:, C:, :HEAD_PAIR], 0.0)
    a_rk = jnp.where(tok >= src, G[:, C:, HEAD_PAIR:], 0.0)
    P = jnp.where(tok == src, 1.0, 0.0) + a_ab
    N = _bdot(a_ab, bd(a_ab), _NN, SOLVE_PASSES)
    steps = 2
    while 2 * steps < C:
        NP = _bdot(jnp.concatenate([N, P], axis=1), bd(N), _NN, SOLVE_PASSES)
        N = NP[:, :C]
        P = P + NP[:, C:]
        steps *= 2
    P = P + _bdot(P, bd(N), _NN, SOLVE_PASSES)
    aV = _bdot(a_ak, Vbd, _NN, SOLVE_PASSES)
    XU = _bdot(P, jnp.concatenate([bd(A), bd(aV)], axis=2), _NN, SOLVE_PASSES)
    X1 = XU[:, :, :HEAD_PAIR]
    Uloc = XU[:, :, HEAD_PAIR:]
    Q = R + _bdot(a_rb, bd(X1), _NN, STATE_PASSES)
    Yloc = _bdot(jnp.concatenate([a_rb, a_rk], axis=2), jnp.concatenate([bd(Uloc), Vbd], axis=1),
                 _NN, STATE_PASSES)
    ri = lax.broadcasted_iota(jnp.int32, (HEAD_PAIR, HEAD_PAIR), 0) // RWKV_HEAD
    ci = lax.broadcasted_iota(jnp.int32, (HEAD_PAIR, HEAD_PAIR), 1) // RWKV_HEAD
    same_head = ri == ci
    Pm = jnp.where(same_head, _bdot(X1, Bh, _TN, STATE_PASSES), 0.0)
    Sloc = jnp.where(same_head, _bdot(jnp.concatenate([Uloc, V], axis=1), jnp.concatenate([Bh, Kh], axis=1),
                                      _TN, STATE_PASSES), 0.0)
    return Q, Yloc, Pm, Sloc, jnp.stack(wcs)


def _rwkv_chunk_kernel(pr_ref, pk_ref, pv_ref, px_ref, sr_ref, sk_ref, sv_ref, sx_ref,
                       mur_ref, muk_ref, muv_ref, mux_ref, w0_ref, w2_ref, a0_ref, a2_ref, g2_ref,
                       kk_ref, ka_ref, rk_ref, gng_ref, gnb_ref,
                       out_ref, s_out_ref,
                       s_scr, cr_scr, ck_scr, cv_scr, cx_scr, y_scr):
    ti = pl.program_id(2)
    tt, hw = pr_ref.shape
    n_pairs = hw // HEAD_PAIR

    @pl.when(ti == 0)
    def _init():
        s_scr[...] = jnp.zeros(s_scr.shape, F32)
        cr_scr[...] = sr_ref[...]
        ck_scr[...] = sk_ref[...]
        cv_scr[...] = sv_ref[...]
        cx_scr[...] = sx_ref[...]

    def shifted(p_ref, c_scr, mu_ref):
        p = p_ref[...]
        row = lax.broadcasted_iota(jnp.int32, p.shape, 0)
        prev = jnp.where(row == 0, c_scr[...], pltpu.roll(p, 1, 0))
        c_scr[...] = p_ref[tt - 1:tt, :]
        return p + (prev - p) * mu_ref[...]

    r = shifted(pr_ref, cr_scr, mur_ref)
    k = shifted(pk_ref, ck_scr, muk_ref)
    v = shifted(pv_ref, cv_scr, muv_ref)
    xx = shifted(px_ref, cx_scr, mux_ref)
    ones = _head_ones(hw)
    lw, a, g, kk, kf, bonus = _rwkv_pre(r, k, v, xx, w0_ref[...], w2_ref[...], a0_ref[...], a2_ref[...],
                                        g2_ref[...], kk_ref[...], ka_ref[...], rk_ref[...], ones)
    C = RWKV_CHUNK
    Q, Yloc, Pm, Sloc, wc = _scan_operands(r, kf, v, kk, a, lw)
    S = s_scr[...]
    for c in range(tt // C):
        inst = slice(c * n_pairs, (c + 1) * n_pairs)
        y_c = _bdot(Q[inst], S, _NT, STATE_PASSES) + Yloc[inst]
        for p in range(n_pairs):
            y_scr[c * C:(c + 1) * C, p * HEAD_PAIR:(p + 1) * HEAD_PAIR] = y_c[p]
        S = S * wc[inst] + _bdot(S, Pm[inst], _NN, STATE_PASSES) + Sloc[inst]
    s_scr[...] = S
    out_ref[...] = _rwkv_post(y_scr[...], bonus, g, gng_ref[...], gnb_ref[...], ones)

    @pl.when(ti == pl.num_programs(2) - 1)
    def _emit_state():
        for p in range(n_pairs):
            s_out_ref[2 * p] = S[p, :RWKV_HEAD, :RWKV_HEAD]
            s_out_ref[2 * p + 1] = S[p, RWKV_HEAD:, RWKV_HEAD:]


def _rwkv_chunked(pa, shift0, W, *, tt, hw):
    B, T, _ = pa.shape
    width = W['decay_w0'].shape[1]
    heads = width // RWKV_HEAD
    assert T % tt == 0 and tt % RWKV_CHUNK == 0 and width % hw == 0 and hw % HEAD_PAIR == 0
    nb = width // hw
    xw = DECAY_LORA + AAA_LORA + GATE_LORA
    assert (3 * width) % xw == 0
    xblk = 3 * width // xw
    col = lambda off: (lambda b, h, t: (b, t, off + h))
    vec = lambda: pl.BlockSpec((1, hw), lambda b, h, t: (0, h))
    in_specs = [
        pl.BlockSpec((None, tt, hw), col(0)), pl.BlockSpec((None, tt, hw), col(nb)),
        pl.BlockSpec((None, tt, hw), col(2 * nb)), pl.BlockSpec((None, tt, xw), lambda b, h, t: (b, t, xblk)),
        pl.BlockSpec((None, 1, hw), lambda b, h, t: (b, 0, h)), pl.BlockSpec((None, 1, hw), lambda b, h, t: (b, 0, nb + h)),
        pl.BlockSpec((None, 1, hw), lambda b, h, t: (b, 0, 2 * nb + h)), pl.BlockSpec((None, 1, xw), lambda b, h, t: (b, 0, xblk)),
        vec(), vec(), vec(), _const_spec((1, xw)),
        vec(), pl.BlockSpec((DECAY_LORA, hw), lambda b, h, t: (0, h)),
        vec(), pl.BlockSpec((AAA_LORA, hw), lambda b, h, t: (0, h)),
        pl.BlockSpec((GATE_LORA, hw), lambda b, h, t: (0, h)),
        vec(), vec(), vec(), vec(), vec(),
    ]
    out, s_new = pl.pallas_call(
        _rwkv_chunk_kernel,
        grid=(B, nb, T // tt),
        in_specs=in_specs,
        out_specs=[pl.BlockSpec((None, tt, hw), lambda b, h, t: (b, t, h)),
                   pl.BlockSpec((None, hw // RWKV_HEAD, RWKV_HEAD, RWKV_HEAD), lambda b, h, t: (b, h, 0, 0))],
        out_shape=[jax.ShapeDtypeStruct((B, T, width), F32),
                   jax.ShapeDtypeStruct((B, heads, RWKV_HEAD, RWKV_HEAD), F32)],
        scratch_shapes=[pltpu.VMEM((hw // HEAD_PAIR, HEAD_PAIR, HEAD_PAIR), F32),
                        pltpu.VMEM((1, hw), F32), pltpu.VMEM((1, hw), F32), pltpu.VMEM((1, hw), F32),
                        pltpu.VMEM((1, xw), F32), pltpu.VMEM((tt, hw), F32)],
        compiler_params=_cparams(("parallel", "parallel", "arbitrary")),
        name="rwkv_chunked",
    )(pa, pa, pa, pa, shift0, shift0, shift0, shift0,
      W['mu_r'], W['mu_k'], W['mu_v'], W['mu_x'], W['decay_w0'], W['decay_w2'], W['aaa_a0'], W['aaa_a2'],
      W['gate_g2'], W['k_k'], W['k_a'], W['r_k'], W['gn_g'], W['gn_b'])
    return out, s_new


def _rwkv_step_kernel(p_ref, s0_ref, st_ref, mu_ref, w0_ref, w2_ref, a0_ref, a2_ref, g2_ref,
                      kk_ref, ka_ref, rk_ref, gng_ref, gnb_ref,
                      out_ref, st_out_ref, vec_scr, y_scr):
    bb, width = out_ref.shape
    heads = width // RWKV_HEAD
    p = p_ref[...]
    xs = p + (s0_ref[...] - p) * mu_ref[...]
    r, k, v, xx = xs[:, :width], xs[:, width:2 * width], xs[:, 2 * width:3 * width], xs[:, 3 * width:]
    ones = _head_ones(width)
    lw, a, g, kk, kf, bonus = _rwkv_pre(r, k, v, xx, w0_ref[...], w2_ref[...], a0_ref[...], a2_ref[...],
                                        g2_ref[...], kk_ref[...], ka_ref[...], rk_ref[...], ones)
    vec_scr[0] = r
    vec_scr[1] = kf
    vec_scr[2] = v
    vec_scr[3] = kk
    vec_scr[4] = kk * a
    vec_scr[5] = jnp.exp(lw)
    eye = (lax.broadcasted_iota(jnp.int32, (RWKV_HEAD, RWKV_HEAD), 0)
           == lax.broadcasted_iota(jnp.int32, (RWKV_HEAD, RWKV_HEAD), 1))

    def body(b, carry):
        def per_head(i):
            full = vec_scr[i, pl.ds(b, 1), :]
            return jnp.stack([full[:, h * RWKV_HEAD:(h + 1) * RWKV_HEAD] for h in range(heads)], axis=0)

        S = st_ref[b]
        sa = jnp.sum(S * per_head(3), axis=2, keepdims=True)
        v_col = jnp.sum(jnp.where(eye, per_head(2), 0.0), axis=2, keepdims=True)
        S2 = S * per_head(5) - sa * per_head(4) + v_col * per_head(1)
        st_out_ref[b] = S2
        y_col = jnp.sum(S2 * per_head(0), axis=2, keepdims=True)
        y_row = jnp.sum(jnp.where(eye, y_col, 0.0), axis=1, keepdims=True)
        y_scr[pl.ds(b, 1), :] = jnp.concatenate([y_row[h] for h in range(heads)], axis=1)
        return carry

    lax.fori_loop(0, bb, body, 0)
    out_ref[...] = _rwkv_post(y_scr[...], bonus, g, gng_ref[...], gnb_ref[...], ones)


def _rwkv_step(pa, shift0, state, W, *, bb):
    B, proj = pa.shape
    width = W['decay_w0'].shape[1]
    heads = width // RWKV_HEAD
    assert B % bb == 0
    row = lambda w: pl.BlockSpec((bb, w), lambda i: (i, 0))
    st_spec = pl.BlockSpec((bb, heads, RWKV_HEAD, RWKV_HEAD), lambda i: (i, 0, 0, 0))
    names = ('decay_w0', 'decay_w2', 'aaa_a0', 'aaa_a2', 'gate_g2', 'k_k', 'k_a', 'r_k', 'gn_g', 'gn_b')
    out, st_new = pl.pallas_call(
        _rwkv_step_kernel,
        grid=(B // bb,),
        in_specs=[row(proj), row(proj), st_spec, _const_spec(W['shift_mu'].shape)]
                 + [_const_spec(W[n].shape) for n in names],
        out_specs=[row(width), st_spec],
        out_shape=[jax.ShapeDtypeStruct((B, width), F32), jax.ShapeDtypeStruct(state.shape, F32)],
        scratch_shapes=[pltpu.VMEM((6, bb, width), F32), pltpu.VMEM((bb, width), F32)],
        compiler_params=_cparams(("parallel",)),
        name="rwkv_step",
    )(pa, shift0, state, W['shift_mu'], *[W[n] for n in names])
    return out, st_new


def _lru_coeffs(xc, wr_ref, br_ref, wi_ref, bi_ref, lam_ref):
    xcb = xc.astype(BF16)
    n_grp = xc.shape[1] // LRU_GROUP
    grp = lambda w_ref: jnp.concatenate(
        [jnp.dot(xcb[:, i * LRU_GROUP:(i + 1) * LRU_GROUP], w_ref[i], preferred_element_type=F32)
         for i in range(n_grp)], axis=1)
    gr = jax.nn.sigmoid(grp(wr_ref) + br_ref[...])
    gi = jax.nn.sigmoid(grp(wi_ref) + bi_ref[...])
    a_t = jnp.exp(-LRU_C * gr * _softplus(-lam_ref[...]))
    return a_t, jnp.sqrt(1.0 - a_t * a_t) * gi * xc


def _lru_finish(hs, pg_ref, g0_ref, g1_ref, rw_ref, h_ref, wmix_ref, lng_ref, lnb_ref, alpha):
    lru_out = hs * jax.nn.gelu(pg_ref[...])
    merged = jax.nn.sigmoid(g0_ref[...]) * rw_ref[...] + jax.nn.sigmoid(g1_ref[...]) * lru_out
    mix = jnp.dot(merged.astype(BF16), wmix_ref[...], preferred_element_type=F32)
    return _layer_norm(alpha * h_ref[...] + mix, lng_ref[...], lnb_ref[...])


def _lru_kernel(x_ref, pg_ref, g0_ref, g1_ref, rw_ref, h_ref, buf_ref, h0_ref,
                cw_ref, cb_ref, wr_ref, br_ref, wi_ref, bi_ref, lam_ref, wmix_ref, lng_ref, lnb_ref,
                out_ref, hlast_ref, tail_scr, hc_scr, *, alpha):
    ti = pl.program_id(1)
    tt, width = x_ref.shape
    S8 = V7X_SUBLANES

    @pl.when(ti == 0)
    def _init():
        tail_scr[...] = buf_ref[...]
        hc_scr[...] = h0_ref[...]

    x = x_ref[...]
    tail = tail_scr[...]
    r8 = lax.broadcasted_iota(jnp.int32, (S8, width), 0)

    def delayed(d):
        head = jnp.where(r8 < d, pltpu.roll(tail, d, 0), pltpu.roll(x[:S8], d, 0))
        return jnp.concatenate([head, x_ref[S8 - d:tt - d, :]], axis=0)

    cw = cw_ref[...]
    conv = cw[0:1] * delayed(CONV_WIDTH - 1)
    for j in range(1, CONV_WIDTH - 1):
        conv = conv + cw[j:j + 1] * delayed(CONV_WIDTH - 1 - j)
    xc = cb_ref[...] + (conv + cw[CONV_WIDTH - 1:CONV_WIDTH] * x)
    tail_scr[...] = x_ref[tt - S8:tt, :]

    A, Bv = _lru_coeffs(xc, wr_ref, br_ref, wi_ref, bi_ref, lam_ref)
    A = A.reshape(tt // S8, S8, width)
    Bv = Bv.reshape(tt // S8, S8, width)
    in_grp = lax.broadcasted_iota(jnp.int32, (S8, width), 0)
    s = 1
    while s < S8:
        keep = in_grp >= s
        Bv = Bv + A * jnp.where(keep, pltpu.roll(Bv, s, 1), 0.0)
        A = A * jnp.where(keep, pltpu.roll(A, s, 1), 1.0)
        s *= 2
    carry = hc_scr[...]
    groups = []
    for gi in range(tt // S8):
        hg = Bv[gi] + A[gi] * carry
        groups.append(hg)
        carry = hg[S8 - 1:S8, :]
    hs = jnp.concatenate(groups, axis=0)
    h_last = carry
    hc_scr[...] = h_last
    hlast_ref[...] = h_last
    out_ref[...] = _lru_finish(hs, pg_ref, g0_ref, g1_ref, rw_ref, h_ref, wmix_ref, lng_ref, lnb_ref, alpha)


def _lru_step_kernel(x_ref, pg_ref, g0_ref, g1_ref, rw_ref, h_ref, buf_ref, h0_ref,
                     cw_ref, cb_ref, wr_ref, br_ref, wi_ref, bi_ref, lam_ref, wmix_ref, lng_ref, lnb_ref,
                     out_ref, hnew_ref, *, alpha):
    cw = cw_ref[...]
    conv = cw[0:1] * buf_ref[0]
    for j in range(1, CONV_WIDTH - 1):
        conv = conv + cw[j:j + 1] * buf_ref[j]
    xc = cb_ref[...] + (conv + cw[CONV_WIDTH - 1:CONV_WIDTH] * x_ref[...])
    A, Bv = _lru_coeffs(xc, wr_ref, br_ref, wi_ref, bi_ref, lam_ref)
    hs = Bv + A * h0_ref[...]
    hnew_ref[...] = hs
    out_ref[...] = _lru_finish(hs, pg_ref, g0_ref, g1_ref, rw_ref, h_ref, wmix_ref, lng_ref, lnb_ref, alpha)


_LRU_WEIGHTS = ('conv_w', 'conv_b', 'lru_wr', 'lru_br', 'lru_wi', 'lru_bi', 'lru_lambda', 'w_mix_out')


def _lru_step_mix_ln(pb, rw, h, buf, h0, W, ln_g, ln_b, *, alpha, tm):
    B, width = rw.shape
    assert B % tm == 0
    blk = lambda j: pl.BlockSpec((tm, width), lambda i: (i, j))
    out, h_new = pl.pallas_call(
        functools.partial(_lru_step_kernel, alpha=alpha),
        grid=(B // tm,),
        in_specs=[blk(0), blk(1), blk(2), blk(3), blk(0), blk(0),
                  pl.BlockSpec((CONV_WIDTH - 1, tm, width), lambda i: (0, i, 0)), blk(0)]
                 + [_const_spec(W[n].shape) for n in _LRU_WEIGHTS] + [_const_spec(ln_g.shape), _const_spec(ln_b.shape)],
        out_specs=[blk(0), blk(0)],
        out_shape=[jax.ShapeDtypeStruct((B, width), F32), jax.ShapeDtypeStruct((B, width), F32)],
        compiler_params=_cparams(("parallel",)),
        name="lru_step_mix_ln",
    )(pb, pb, pb, pb, rw, h, buf, h0, *[W[n] for n in _LRU_WEIGHTS], ln_g, ln_b)
    return out, h_new


def _lru_mix_ln(pb, rw, h, buf8, h0, W, ln_g, ln_b, *, alpha, tt):
    B, T, width = rw.shape
    assert T % tt == 0 and tt % V7X_SUBLANES == 0
    blk = lambda j: pl.BlockSpec((None, tt, width), lambda b, t: (b, t, j))
    names = _LRU_WEIGHTS
    out, h_last = pl.pallas_call(
        functools.partial(_lru_kernel, alpha=alpha),
        grid=(B, T // tt),
        in_specs=[blk(0), blk(1), blk(2), blk(3), blk(0), blk(0),
                  pl.BlockSpec((None, V7X_SUBLANES, width), lambda b, t: (b, 0, 0)),
                  pl.BlockSpec((None, 1, width), lambda b, t: (b, 0, 0))]
                 + [_const_spec(W[n].shape) for n in names] + [_const_spec(ln_g.shape), _const_spec(ln_b.shape)],
        out_specs=[blk(0), pl.BlockSpec((None, 1, width), lambda b, t: (b, 0, 0))],
        out_shape=[jax.ShapeDtypeStruct((B, T, width), F32), jax.ShapeDtypeStruct((B, 1, width), F32)],
        scratch_shapes=[pltpu.VMEM((V7X_SUBLANES, width), F32), pltpu.VMEM((1, width), F32)],
        compiler_params=_cparams(("parallel", "arbitrary")),
        name="lru_mix_ln",
    )(pb, pb, pb, pb, rw, h, buf8, h0, *[W[n] for n in names], ln_g, ln_b)
    return out, h_last


def _xattn_kernel(h_ref, kv_ref, wq_ref, wo_ref, g_ref, b_ref, o_ref, *, alpha, heads):
    h = h_ref[...]
    d = h.shape[1]
    hd = d // heads
    q = jnp.dot(h.astype(BF16), wq_ref[...], preferred_element_type=F32)
    outs = []
    for j in range(heads):
        sl = slice(j * hd, (j + 1) * hd)
        s = _dot_dims(q[:, sl], kv_ref[:, sl], _NT) * (hd ** -0.5)
        e = jnp.exp(s - jnp.max(s, axis=-1, keepdims=True))
        p = e / jnp.sum(e, axis=-1, keepdims=True)
        outs.append(_dot(p, kv_ref[:, d + j * hd:d + (j + 1) * hd]))
    out = jnp.dot(jnp.concatenate(outs, axis=1).astype(BF16), wo_ref[...], preferred_element_type=F32)
    o_ref[...] = _layer_norm(alpha * h + out, g_ref[...], b_ref[...])


def _xattn_rows_kernel(h_ref, mk_ref, mv_ref, wq_ref, wo_ref, g_ref, b_ref, o_ref, *, alpha):
    h = h_ref[...]
    nb, m, heads, hd = mk_ref.shape
    q = jnp.dot(h.astype(BF16), wq_ref[...], preferred_element_type=F32)
    col = lax.broadcasted_iota(jnp.int32, (heads, m * heads), 1)
    own = (col % heads) == lax.broadcasted_iota(jnp.int32, (heads, m * heads), 0)
    rows = []
    for i in range(nb):
        k2 = mk_ref[i].reshape(m * heads, hd)
        v2 = mv_ref[i].reshape(m * heads, hd)
        q4 = jnp.concatenate([q[i:i + 1, j * hd:(j + 1) * hd] for j in range(heads)], axis=0)
        s = jnp.where(own, _dot_dims(q4, k2, _NT) * (hd ** -0.5), -jnp.inf)
        e = jnp.exp(s - jnp.max(s, axis=-1, keepdims=True))
        p = e / jnp.sum(e, axis=-1, keepdims=True)
        o4 = _dot(p, v2)
        rows.append(jnp.concatenate([o4[j:j + 1] for j in range(heads)], axis=1))
    out = jnp.dot(jnp.concatenate(rows, axis=0).astype(BF16), wo_ref[...], preferred_element_type=F32)
    o_ref[...] = _layer_norm(alpha * h + out, g_ref[...], b_ref[...])


def _xattn_ln(h, kv, wq, wo, g, b, *, alpha, heads, rows_per_batch, tile_rows):
    n, d = h.shape
    assert n % tile_rows == 0 and rows_per_batch % tile_rows == 0
    per = rows_per_batch // tile_rows
    m = kv.shape[0] // (n // rows_per_batch)
    return pl.pallas_call(
        functools.partial(_xattn_kernel, alpha=alpha, heads=heads),
        grid=(n // tile_rows,),
        in_specs=[pl.BlockSpec((tile_rows, d), lambda i: (i, 0)),
                  pl.BlockSpec((m, 2 * d), lambda i: (i // per, 0)),
                  _const_spec(wq.shape), _const_spec(wo.shape), _const_spec(g.shape), _const_spec(b.shape)],
        out_specs=pl.BlockSpec((tile_rows, d), lambda i: (i, 0)),
        out_shape=jax.ShapeDtypeStruct((n, d), F32),
        compiler_params=_cparams(("parallel",)),
        name="xattn_ln",
    )(h, kv, wq, wo, g, b)


def _xattn_rows_ln(h, mk, mv, wq, wo, g, b, *, alpha, nb):
    n, d = h.shape
    _, m, heads, hd = mk.shape
    assert n % nb == 0
    kv_spec = pl.BlockSpec((nb, m, heads, hd), lambda i: (i, 0, 0, 0))
    return pl.pallas_call(
        functools.partial(_xattn_rows_kernel, alpha=alpha),
        grid=(n // nb,),
        in_specs=[pl.BlockSpec((nb, d), lambda i: (i, 0)), kv_spec, kv_spec,
                  _const_spec(wq.shape), _const_spec(wo.shape), _const_spec(g.shape), _const_spec(b.shape)],
        out_specs=pl.BlockSpec((nb, d), lambda i: (i, 0)),
        out_shape=jax.ShapeDtypeStruct((n, d), F32),
        compiler_params=_cparams(("parallel",)),
        name="xattn_rows_ln",
    )(h, mk, mv, wq, wo, g, b)


def _kv_proj_kernel(x_ref, w_ref, kv_ref, k_ref, v_ref):
    heads, hd = k_ref.shape[1:]
    d = heads * hd
    kv = jnp.dot(x_ref[...].astype(BF16), w_ref[...], preferred_element_type=F32)
    kv_ref[...] = kv
    for j in range(heads):
        k_ref[:, j, :] = kv[:, j * hd:(j + 1) * hd]
        v_ref[:, j, :] = kv[:, d + j * hd:d + (j + 1) * hd]


def _kv_proj(x, wkv, *, heads, tm):
    n, d = x.shape
    assert n % tm == 0
    hd = d // heads
    out = jax.ShapeDtypeStruct((n, heads, hd), F32)
    return pl.pallas_call(
        _kv_proj_kernel,
        grid=(n // tm,),
        in_specs=[pl.BlockSpec((tm, d), lambda i: (i, 0)), _const_spec(wkv.shape)],
        out_specs=[pl.BlockSpec((tm, 2 * d), lambda i: (i, 0))] + [pl.BlockSpec((tm, heads, hd), lambda i: (i, 0, 0))] * 2,
        out_shape=[jax.ShapeDtypeStruct((n, 2 * d), F32), out, out],
        compiler_params=_cparams(("parallel",)),
        name="kv_proj",
    )(x, wkv)


def _row(v):
    return v.reshape(1, -1)


def _block_diag_groups(w):
    n, c, _ = w.shape
    per = LRU_GROUP // c
    w4 = w.reshape(n // per, per, c, c)
    bd = jnp.einsum('gjcd,jk->gjckd', w4, jnp.eye(per, dtype=w.dtype))
    return bd.reshape(n // per, LRU_GROUP, LRU_GROUP).astype(BF16)


def _prep_layer(l, ln_g, ln_b, ffn1_wi, ffn1_wo, ffn2_wi, ffn2_wo, w_in, shift_mu, decay_w0, decay_w2,
                aaa_a0, aaa_a2, gate_g2, k_k, k_a, r_k, gn_g, gn_b, conv_w, conv_b, lru_wr, lru_br,
                lru_wi, lru_bi, lru_lambda, w_mix_out, xa_wq, xa_wk, xa_wv, xa_wo):
    width = decay_w0.shape[1]
    rp = shift_mu.shape[1]
    d_ff = ffn1_wo.shape[1]
    bf = lambda w: w.astype(BF16)
    mu = shift_mu[l]
    return dict(
        ln_g=[_row(ln_g[l, i]) for i in range(4)], ln_b=[_row(ln_b[l, i]) for i in range(4)],
        ffn1=(bf(ffn1_wi[l][:, :d_ff]), bf(ffn1_wi[l][:, d_ff:]), bf(ffn1_wo[l])),
        ffn2=(bf(ffn2_wi[l][:, :d_ff]), bf(ffn2_wi[l][:, d_ff:]), bf(ffn2_wo[l])),
        w_in_a=bf(w_in[l][:, :rp]), w_in_b=bf(w_in[l][:, rp:]),
        shift_mu=_row(mu), mu_r=_row(mu[:width]), mu_k=_row(mu[width:2 * width]),
        mu_v=_row(mu[2 * width:3 * width]), mu_x=_row(mu[3 * width:]),
        decay_w0=_row(decay_w0[l]), decay_w2=bf(decay_w2[l]), aaa_a0=_row(aaa_a0[l]), aaa_a2=bf(aaa_a2[l]),
        gate_g2=bf(gate_g2[l]), k_k=_row(k_k[l]), k_a=_row(k_a[l]), r_k=_row(r_k[l]),
        gn_g=_row(gn_g[l]), gn_b=_row(gn_b[l]),
        conv_w=conv_w[l], conv_b=_row(conv_b[l]),
        lru_wr=_block_diag_groups(lru_wr[l]), lru_br=_row(lru_br[l]),
        lru_wi=_block_diag_groups(lru_wi[l]), lru_bi=_row(lru_bi[l]), lru_lambda=_row(lru_lambda[l]),
        w_mix_out=bf(w_mix_out[l]), xa_wq=bf(xa_wq[l]), xa_wo=bf(xa_wo[l]),
        xa_wkv=bf(jnp.concatenate([xa_wk[l], xa_wv[l]], axis=1)),
    )


def _tile(n, pref):
    return pref if n % pref == 0 else n


def _layer(h, mem, state, shift0, h0, buf0, W, *, alpha, xa_heads):
    B, T, d = h.shape
    n = B * T
    tm = _tile(n, 512)
    h1 = _ffn_ln(h.reshape(n, d), *W['ffn1'], W['ln_g'][0], W['ln_b'][0], alpha=alpha, tm=tm)
    pa = _matmul(h1, W['w_in_a'], tm=tm, tn=W['w_in_a'].shape[1])
    pb = _matmul(h1, W['w_in_b'], tm=tm, tn=W['w_in_b'].shape[1] // 2)
    width = W['decay_w0'].shape[1]
    lru_w = W['conv_b'].shape[1]
    pa3 = pa.reshape(B, T, -1)
    pb3 = pb.reshape(B, T, -1)
    if T > 1:
        assert state is None
        rw, s_new = _rwkv_chunked(pa3, shift0.reshape(B, 1, -1), W, tt=_tile(T, 512), hw=_tile(width, 512))
        buf8 = jnp.concatenate([jnp.zeros((B, V7X_SUBLANES - (CONV_WIDTH - 1), lru_w), F32), buf0], axis=1)
        h2, h_last = _lru_mix_ln(pb3, rw, h1.reshape(B, T, d), buf8, h0.reshape(B, 1, lru_w), W,
                                 W['ln_g'][1], W['ln_b'][1], alpha=alpha, tt=_tile(T, 256))
    else:
        rw, s_new = _rwkv_step(pa, shift0, state, W, bb=_tile(B, 2 * V7X_SUBLANES))
        h2, h_last = _lru_step_mix_ln(pb, rw, h1, jnp.swapaxes(buf0, 0, 1), h0, W,
                                      W['ln_g'][1], W['ln_b'][1], alpha=alpha, tm=_tile(B, 128))
    hist = CONV_WIDTH - 1
    if T >= hist:
        conv_in_tail = pb3[:, T - hist:, :lru_w]
    else:
        conv_in_tail = jnp.concatenate([buf0[:, T:], pb3[:, :, :lru_w]], axis=1)
    xa = (W['xa_wq'], W['xa_wo'], W['ln_g'][2], W['ln_b'][2])
    if T > 1:
        h3 = _xattn_ln(h2.reshape(n, d), mem, *xa, alpha=alpha, heads=xa_heads, rows_per_batch=T,
                       tile_rows=_tile(T, 512))
    else:
        h3 = _xattn_rows_ln(h2, *mem, *xa, alpha=alpha, nb=_tile(B, V7X_SUBLANES))
    h4 = _ffn_ln(h3, *W['ffn2'], W['ln_g'][3], W['ln_b'][3], alpha=alpha, tm=tm)
    return h4.reshape(B, T, d), s_new, pa3[:, -1], h_last.reshape(B, lru_w), conv_in_tail


def kernel(x_prompt, x_sample, mem_prompt, cache_mem_k, cache_mem_v, state_rwkv, state_rwkv_shift, state_lru, state_conv, ln_g, ln_b, ffn1_wi, ffn1_wo, ffn2_wi, ffn2_wo, w_in, shift_mu, decay_w0, decay_w2, aaa_a0, aaa_a2, gate_g2, k_k, k_a, r_k, gn_g, gn_b, conv_w, conv_b, lru_wr, lru_br, lru_wi, lru_bi, lru_lambda, w_mix_out, xa_wq, xa_wk, xa_wv, xa_wo):
    depth = ln_g.shape[0]
    alpha = (2.0 * depth) ** 0.25
    B, _, d = x_prompt.shape
    n_mem, xa_heads, xa_head = cache_mem_k.shape[2:]
    rp = shift_mu.shape[1]
    lru_w = conv_b.shape[1]
    hp, hs = x_prompt, x_sample
    outs = [[] for _ in range(10)]
    for l in range(depth):
        W = _prep_layer(l, ln_g, ln_b, ffn1_wi, ffn1_wo, ffn2_wi, ffn2_wo, w_in, shift_mu, decay_w0, decay_w2,
                        aaa_a0, aaa_a2, gate_g2, k_k, k_a, r_k.reshape(depth, -1), gn_g, gn_b, conv_w, conv_b,
                        lru_wr, lru_br, lru_wi, lru_bi, lru_lambda, w_mix_out, xa_wq, xa_wk, xa_wv, xa_wo)
        kv, mk, mv = _kv_proj(mem_prompt.reshape(B * n_mem, d), W['xa_wkv'], heads=xa_heads,
                              tm=_tile(B * n_mem, 512))
        mk = mk.reshape(B, n_mem, xa_heads, xa_head)
        mv = mv.reshape(B, n_mem, xa_heads, xa_head)
        hp, S1, sh1, h1, b1 = _layer(
            hp, kv, None, jnp.zeros((B, rp), F32), jnp.zeros((B, lru_w), F32),
            jnp.zeros((B, CONV_WIDTH - 1, lru_w), F32), W, alpha=alpha, xa_heads=xa_heads)
        hs, S2, sh2, h2, b2 = _layer(
            hs, (cache_mem_k[l], cache_mem_v[l]),
            state_rwkv[l], state_rwkv_shift[l], state_lru[l], state_conv[l], W, alpha=alpha, xa_heads=xa_heads)
        for lst, val in zip(outs, (mk, mv, S1, sh1, h1, b1, S2, sh2, h2, b2)):
            lst.append(val)
    return (hp, hs) + tuple(jnp.stack(o) for o in outs)
```

```python
import functools

import jax
import jax.numpy as jnp
from jax import lax
from jax.experimental import pallas as pl
from jax.experimental.pallas import tpu as pltpu

F32 = jnp.float32
BF16 = jnp.bfloat16

RWKV_HEAD = 64
DECAY_LORA = 64
AAA_LORA = 64
GATE_LORA = 128
GN_EPS = 64e-5
LRU_BLOCK = 64
CONV_WIDTH = 4
LRU_C = 8.0
LN_EPS = 1e-5

V7X_LANES = 128
V7X_SUBLANES = 8
V7X_MXU_DIM = 256
V7X_SCOPED_VMEM_BYTES = 60000 * 1024

RWKV_CHUNK = 64
SOLVE_PASSES = 1
STATE_PASSES = 1
HEAD_PAIR = 2 * RWKV_HEAD
LRU_GROUP = V7X_MXU_DIM


def _cparams(semantics):
    return pltpu.CompilerParams(dimension_semantics=semantics, vmem_limit_bytes=V7X_SCOPED_VMEM_BYTES)


def _const_spec(shape):
    zeros = (0,) * len(shape)
    return pl.BlockSpec(shape, lambda *_: zeros)


def _dot(a, b):
    return jnp.dot(a.astype(BF16), b.astype(BF16), preferred_element_type=F32)


def _dot_dims(a, b, dims):
    return lax.dot_general(a.astype(BF16), b.astype(BF16), (dims, ((), ())), preferred_element_type=F32)


_NN = ((1,), (0,))
_NT = ((1,), (1,))
_TN = ((0,), (0,))


def _split2(x):
    hi = x.astype(BF16)
    lo = (x - hi.astype(F32)).astype(BF16)
    return hi, lo


def _dot_exact_lhs(a_bf16, b):
    hi, lo = _split2(b)
    lo2 = (b - hi.astype(F32) - lo.astype(F32)).astype(BF16)
    d = lambda y: jnp.dot(a_bf16, y, preferred_element_type=F32)
    return d(hi) + (d(lo) + d(lo2))


def _dot_exact_rhs(a, b_bf16):
    hi, lo = _split2(a)
    d = lambda x: jnp.dot(x, b_bf16, preferred_element_type=F32)
    return d(hi) + d(lo)


def _layer_norm(x, g, b):
    mu = jnp.mean(x, axis=-1, keepdims=True)
    xc = x - mu
    var = jnp.mean(xc * xc, axis=-1, keepdims=True)
    return xc * lax.rsqrt(var + LN_EPS) * g + b


def _softplus(z):
    return jnp.maximum(z, 0.0) + jnp.log(1.0 + jnp.exp(-jnp.abs(z)))


def _head_ones(width):
    r = lax.broadcasted_iota(jnp.int32, (width, width), 0) // RWKV_HEAD
    c = lax.broadcasted_iota(jnp.int32, (width, width), 1) // RWKV_HEAD
    return (r == c).astype(BF16)


def _mm_kernel(x_ref, w_ref, o_ref):
    o_ref[...] = jnp.dot(x_ref[...].astype(BF16), w_ref[...], preferred_element_type=F32)


def _matmul(x, w, *, tm, tn):
    n, k = x.shape
    m = w.shape[1]
    assert n % tm == 0 and m % tn == 0
    return pl.pallas_call(
        _mm_kernel,
        grid=(m // tn, n // tm),
        in_specs=[pl.BlockSpec((tm, k), lambda j, i: (i, 0)),
                  pl.BlockSpec((k, tn), lambda j, i: (0, j))],
        out_specs=pl.BlockSpec((tm, tn), lambda j, i: (i, j)),
        out_shape=jax.ShapeDtypeStruct((n, m), F32),
        compiler_params=_cparams(("parallel", "parallel")),
        name="matmul",
    )(x, w)


def _ffn_kernel(x_ref, wg_ref, wu_ref, wo_ref, g_ref, b_ref, o_ref, *, alpha, n_chunks):
    x = x_ref[...]
    xb = x.astype(BF16)
    ck = wg_ref.shape[1] // n_chunks
    acc = jnp.zeros(x.shape, F32)
    for c in range(n_chunks):
        sl = slice(c * ck, (c + 1) * ck)
        gate = jnp.dot(xb, wg_ref[:, sl], preferred_element_type=F32)
        up = jnp.dot(xb, wu_ref[:, sl], preferred_element_type=F32)
        mid = (gate * jax.nn.sigmoid(gate) * up).astype(BF16)
        acc = acc + jnp.dot(mid, wo_ref[sl, :], preferred_element_type=F32)
    o_ref[...] = _layer_norm(alpha * x + 0.5 * acc, g_ref[...], b_ref[...])


def _ffn_ln(x, wg, wu, wo, g, b, *, alpha, tm):
    n, d = x.shape
    assert n % tm == 0
    d_ff = wg.shape[1]
    n_chunks = 1
    return pl.pallas_call(
        functools.partial(_ffn_kernel, alpha=alpha, n_chunks=n_chunks),
        grid=(n // tm,),
        in_specs=[pl.BlockSpec((tm, d), lambda i: (i, 0)),
                  _const_spec(wg.shape), _const_spec(wu.shape), _const_spec(wo.shape),
                  _const_spec(g.shape), _const_spec(b.shape)],
        out_specs=pl.BlockSpec((tm, d), lambda i: (i, 0)),
        out_shape=jax.ShapeDtypeStruct((n, d), F32),
        compiler_params=_cparams(("parallel",)),
        name="ffn_ln",
    )(x, wg, wu, wo, g, b)


def _rwkv_pre(r, k, v, xx, w0, w2, a0, a2, g2, k_k, k_a, r_k, ones):
    xw = xx[:, :DECAY_LORA]
    xa = xx[:, DECAY_LORA:DECAY_LORA + AAA_LORA]
    xg = xx[:, DECAY_LORA + AAA_LORA:]
    z = w0 + _dot(jnp.tanh(xw), w2)
    lw = -jnp.exp(-_softplus(-z) - 0.5)
    a = jax.nn.sigmoid(a0 + _dot(xa, a2))
    g = _dot(jax.nn.sigmoid(xg), g2)
    kkr = k * k_k
    ss = _dot(kkr * kkr, ones)
    kk = kkr / jnp.maximum(jnp.sqrt(ss), 1e-12)
    kf = k * (1.0 + (a - 1.0) * k_a)
    bonus = _dot(r * kf * r_k, ones) * v
    return lw, a, g, kk, kf, bonus


def _rwkv_post(y, bonus, g, gn_g, gn_b, ones):
    inv_n = 1.0 / RWKV_HEAD
    ym = _dot_exact_rhs(y, ones) * inv_n
    yc = y - ym
    yv = _dot(yc * yc, ones) * inv_n
    yn = yc * lax.rsqrt(yv + GN_EPS) * gn_g + gn_b
    return (yn + bonus) * g


def _bdot(a, b, dims, passes):
    dn = ((tuple(d + 1 for d in dims[0]), tuple(d + 1 for d in dims[1])), ((0,), (0,)))
    d = lambda x, y: lax.dot_general(x, y, dn, preferred_element_type=F32)
    if passes == 1:
        return d(a.astype(BF16), b.astype(BF16))
    a_hi, a_lo = _split2(a)
    b_hi, b_lo = _split2(b)
    return d(a_hi, b_hi) + (d(a_hi, b_lo) + d(a_lo, b_hi))


def _scan_operands(r, kf, v, kk, a, lw):
    tt, hw = r.shape
    C = RWKV_CHUNK
    n_pairs = hw // HEAD_PAIR
    ltri = (lax.broadcasted_iota(jnp.int32, (C, C), 0) >= lax.broadcasted_iota(jnp.int32, (C, C), 1)).astype(BF16)
    first = lax.broadcasted_iota(jnp.int32, (C, HEAD_PAIR), 1) < RWKV_HEAD

    def bd(x):
        return jnp.concatenate([jnp.where(first, x, 0.0), jnp.where(first, 0.0, x)], axis=1)

    names = ('a', 'r', 'b', 'k', 'v', 'bh', 'kh')
    ops = {n: [] for n in names}
    wcs = []
    for c in range(tt // C):
        rows = slice(c * C, (c + 1) * C)
        lw_c = lw[rows]
        L = _dot_exact_lhs(ltri, lw_c)
        Lc = L[C - 1:C, :]
        e_nl = jnp.exp(-L)
        e_c = jnp.exp(Lc - L)
        bb = kk[rows] * a[rows]
        tile = dict(a=-kk[rows] * jnp.exp(L - lw_c), r=r[rows] * jnp.exp(L), b=bb * e_nl, k=kf[rows] * e_nl,
                    v=v[rows], bh=bb * e_c, kh=kf[rows] * e_c)
        wc = jnp.exp(Lc)
        for p in range(n_pairs):
            lanes = slice(p * HEAD_PAIR, (p + 1) * HEAD_PAIR)
            for n in names:
                ops[n].append(tile[n][:, lanes])
            wcs.append(wc[:, lanes])
    A, R, B, K, V, Bh, Kh = (jnp.stack(ops[n]) for n in names)
    Vbd = bd(V)
    G = _bdot(jnp.concatenate([A, R], axis=1), jnp.concatenate([bd(B), bd(K)], axis=1), _NT, SOLVE_PASSES)
    tok = lax.broadcasted_iota(jnp.int32, (C, HEAD_PAIR), 0)
    src = lax.broadcasted_iota(jnp.int32, (C, HEAD_PAIR), 1) % RWKV_HEAD
    a_ab = jnp.where(tok > src, G[:, :C, :HEAD_PAIR], 0.0)
    a_ak = jnp.where(tok > src, G[:, :C, HEAD_PAIR:], 0.0)
    a_rb = jnp.where(tok >= src, G[:, C:, :HEAD_PAIR], 0.0)
    a_rk = jnp.where(tok >= src, G[:, C:, HEAD_PAIR:], 0.0)
    P = jnp.where(tok == src, 1.0, 0.0) + a_ab
    N = _bdot(a_ab, bd(a_ab), _NN, SOLVE_PASSES)
    steps = 2
    while 2 * steps < C:
        NP = _bdot(jnp.concatenate([N, P], axis=1), bd(N), _NN, SOLVE_PASSES)
        N = NP[:, :C]
        P = P + NP[:, C:]
        steps *= 2
    P = P + _bdot(P, bd(N), _NN, SOLVE_PASSES)
    aV = _bdot(a_ak, Vbd, _NN, SOLVE_PASSES)
    XU = _bdot(P, jnp.concatenate([bd(A), bd(aV)], axis=2), _NN, SOLVE_PASSES)
    X1 = XU[:, :, :HEAD_PAIR]
    Uloc = XU[:, :, HEAD_PAIR:]
    Q = R + _bdot(a_rb, bd(X1), _NN, STATE_PASSES)
    Yloc = _bdot(jnp.concatenate([a_rb, a_rk], axis=2), jnp.concatenate([bd(Uloc), Vbd], axis=1),
                 _NN, STATE_PASSES)
    ri = lax.broadcasted_iota(jnp.int32, (HEAD_PAIR, HEAD_PAIR), 0) // RWKV_HEAD
    ci = lax.broadcasted_iota(jnp.int32, (HEAD_PAIR, HEAD_PAIR), 1) // RWKV_HEAD
    same_head = ri == ci
    Pm = jnp.where(same_head, _bdot(X1, Bh, _TN, STATE_PASSES), 0.0)
    Sloc = jnp.where(same_head, _bdot(jnp.concatenate([Uloc, V], axis=1), jnp.concatenate([Bh, Kh], axis=1),
                                      _TN, STATE_PASSES), 0.0)
    return Q, Yloc, Pm, Sloc, jnp.stack(wcs)


def _rwkv_chunk_kernel(pr_ref, pk_ref, pv_ref, px_ref, sr_ref, sk_ref, sv_ref, sx_ref,
                       mur_ref, muk_ref, muv_ref, mux_ref, w0_ref, w2_ref, a0_ref, a2_ref, g2_ref,
                       kk_ref, ka_ref, rk_ref, gng_ref, gnb_ref,
                       out_ref, s_out_ref,
                       s_scr, cr_scr, ck_scr, cv_scr, cx_scr, y_scr):
    ti = pl.program_id(2)
    tt, hw = pr_ref.shape
    n_pairs = hw // HEAD_PAIR

    @pl.when(ti == 0)
    def _init():
        s_scr[...] = jnp.zeros(s_scr.shape, F32)
        cr_scr[...] = sr_ref[...]
        ck_scr[...] = sk_ref[...]
        cv_scr[...] = sv_ref[...]
        cx_scr[...] = sx_ref[...]

    def shifted(p_ref, c_scr, mu_ref):
        p = p_ref[...]
        row = lax.broadcasted_iota(jnp.int32, p.shape, 0)
        prev = jnp.where(row == 0, c_scr[...], pltpu.roll(p, 1, 0))
        c_scr[...] = p_ref[tt - 1:tt, :]
        return p + (prev - p) * mu_ref[...]

    r = shifted(pr_ref, cr_scr, mur_ref)
    k = shifted(pk_ref, ck_scr, muk_ref)
    v = shifted(pv_ref, cv_scr, muv_ref)
    xx = shifted(px_ref, cx_scr, mux_ref)
    ones = _head_ones(hw)
    lw, a, g, kk, kf, bonus = _rwkv_pre(r, k, v, xx, w0_ref[...], w2_ref[...], a0_ref[...], a2_ref[...],
                                        g2_ref[...], kk_ref[...], ka_ref[...], rk_ref[...], ones)
    C = RWKV_CHUNK
    Q, Yloc, Pm, Sloc, wc = _scan_operands(r, kf, v, kk, a, lw)
    S = s_scr[...]
    for c in range(tt // C):
        inst = slice(c * n_pairs, (c + 1) * n_pairs)
        y_c = _bdot(Q[inst], S, _NT, STATE_PASSES) + Yloc[inst]
        for p in range(n_pairs):
            y_scr[c * C:(c + 1) * C, p * HEAD_PAIR:(p + 1) * HEAD_PAIR] = y_c[p]
        S = S * wc[inst] + _bdot(S, Pm[inst], _NN, STATE_PASSES) + Sloc[inst]
    s_scr[...] = S
    out_ref[...] = _rwkv_post(y_scr[...], bonus, g, gng_ref[...], gnb_ref[...], ones)

    @pl.when(ti == pl.num_programs(2) - 1)
    def _emit_state():
        for p in range(n_pairs):
            s_out_ref[2 * p] = S[p, :RWKV_HEAD, :RWKV_HEAD]
            s_out_ref[2 * p + 1] = S[p, RWKV_HEAD:, RWKV_HEAD:]


def _rwkv_chunked(pa, shift0, W, *, tt, hw):
    B, T, _ = pa.shape
    width = W['decay_w0'].shape[1]
    heads = width // RWKV_HEAD
    assert T % tt == 0 and tt % RWKV_CHUNK == 0 and width % hw == 0 and hw % HEAD_PAIR == 0
    nb = width // hw
    xw = DECAY_LORA + AAA_LORA + GATE_LORA
    assert (3 * width) % xw == 0
    xblk = 3 * width // xw
    col = lambda off: (lambda b, h, t: (b, t, off + h))
    vec = lambda: pl.BlockSpec((1, hw), lambda b, h, t: (0, h))
    in_specs = [
        pl.BlockSpec((None, tt, hw), col(0)), pl.BlockSpec((None, tt, hw), col(nb)),
        pl.BlockSpec((None, tt, hw), col(2 * nb)), pl.BlockSpec((None, tt, xw), lambda b, h, t: (b, t, xblk)),
        pl.BlockSpec((None, 1, hw), lambda b, h, t: (b, 0, h)), pl.BlockSpec((None, 1, hw), lambda b, h, t: (b, 0, nb + h)),
        pl.BlockSpec((None, 1, hw), lambda b, h, t: (b, 0, 2 * nb + h)), pl.BlockSpec((None, 1, xw), lambda b, h, t: (b, 0, xblk)),
        vec(), vec(), vec(), _const_spec((1, xw)),
        vec(), pl.BlockSpec((DECAY_LORA, hw), lambda b, h, t: (0, h)),
        vec(), pl.BlockSpec((AAA_LORA, hw), lambda b, h, t: (0, h)),
        pl.BlockSpec((GATE_LORA, hw), lambda b, h, t: (0, h)),
        vec(), vec(), vec(), vec(), vec(),
    ]
    out, s_new = pl.pallas_call(
        _rwkv_chunk_kernel,
        grid=(B, nb, T // tt),
        in_specs=in_specs,
        out_specs=[pl.BlockSpec((None, tt, hw), lambda b, h, t: (b, t, h)),
                   pl.BlockSpec((None, hw // RWKV_HEAD, RWKV_HEAD, RWKV_HEAD), lambda b, h, t: (b, h, 0, 0))],
        out_shape=[jax.ShapeDtypeStruct((B, T, width), F32),
                   jax.ShapeDtypeStruct((B, heads, RWKV_HEAD, RWKV_HEAD), F32)],
        scratch_shapes=[pltpu.VMEM((hw // HEAD_PAIR, HEAD_PAIR, HEAD_PAIR), F32),
                        pltpu.VMEM((1, hw), F32), pltpu.VMEM((1, hw), F32), pltpu.VMEM((1, hw), F32),
                        pltpu.VMEM((1, xw), F32), pltpu.VMEM((tt, hw), F32)],
        compiler_params=_cparams(("parallel", "parallel", "arbitrary")),
        name="rwkv_chunked",
    )(pa, pa, pa, pa, shift0, shift0, shift0, shift0,
      W['mu_r'], W['mu_k'], W['mu_v'], W['mu_x'], W['decay_w0'], W['decay_w2'], W['aaa_a0'], W['aaa_a2'],
      W['gate_g2'], W['k_k'], W['k_a'], W['r_k'], W['gn_g'], W['gn_b'])
    return out, s_new


def _rwkv_step_kernel(p_ref, s0_ref, st_ref, mu_ref, w0_ref, w2_ref, a0_ref, a2_ref, g2_ref,
                      kk_ref, ka_ref, rk_ref, gng_ref, gnb_ref,
                      out_ref, st_out_ref, vec_scr, y_scr, bonus_scr, g_scr):
    i = pl.program_id(0)
    B, width = out_ref.shape
    hg = st_ref.shape[0]
    H = RWKV_HEAD
    ones = _head_ones(width)

    @pl.when(i == 0)
    def _prologue():
        p = p_ref[...]
        xs = p + (s0_ref[...] - p) * mu_ref[...]
        r, k, v, xx = xs[:, :width], xs[:, width:2 * width], xs[:, 2 * width:3 * width], xs[:, 3 * width:]
        lw, a, g, kk, kf, bonus = _rwkv_pre(r, k, v, xx, w0_ref[...], w2_ref[...], a0_ref[...], a2_ref[...],
                                            g2_ref[...], kk_ref[...], ka_ref[...], rk_ref[...], ones)
        for j, vec in enumerate((r, kf, v, kk, kk * a, jnp.exp(lw))):
            vec_scr[j] = vec.T
        bonus_scr[...] = bonus
        g_scr[...] = g

    for hl in range(hg):
        base = pl.multiple_of((i * hg + hl) * H, H)
        r_h, kf_h, kk_h, kka_h, w_h = (vec_scr[j, pl.ds(base, H), :] for j in (0, 1, 3, 4, 5))

        def body(vi, carry):
            S = st_ref[hl, vi]
            sa = jnp.sum(S * kk_h, axis=0, keepdims=True)
            v_row = vec_scr[2, pl.ds(base + vi, 1), :]
            S2 = S * w_h - sa * kka_h + v_row * kf_h
            st_out_ref[hl, vi] = S2
            y_scr[pl.ds(base + vi, 1), :] = jnp.sum(S2 * r_h, axis=0, keepdims=True)
            return carry

        lax.fori_loop(0, H, body, 0, unroll=8)

    @pl.when(i == pl.num_programs(0) - 1)
    def _epilogue():
        out_ref[...] = _rwkv_post(y_scr[...].T, bonus_scr[...], g_scr[...], gng_ref[...], gnb_ref[...], ones)


def _rwkv_step(pa, shift0, state_t, W, *, hg):
    B, proj = pa.shape
    width = W['decay_w0'].shape[1]
    heads = width // RWKV_HEAD
    assert heads % hg == 0 and state_t.shape == (heads, RWKV_HEAD, RWKV_HEAD, B)
    st_spec = pl.BlockSpec((hg, RWKV_HEAD, RWKV_HEAD, B), lambda i: (i, 0, 0, 0))
    names = ('decay_w0', 'decay_w2', 'aaa_a0', 'aaa_a2', 'gate_g2', 'k_k', 'k_a', 'r_k', 'gn_g', 'gn_b')
    out, st_new = pl.pallas_call(
        _rwkv_step_kernel,
        grid=(heads // hg,),
        in_specs=[_const_spec((B, proj)), _const_spec((B, proj)), st_spec, _const_spec(W['shift_mu'].shape)]
                 + [_const_spec(W[n].shape) for n in names],
        out_specs=[_const_spec((B, width)), st_spec],
        out_shape=[jax.ShapeDtypeStruct((B, width), F32), jax.ShapeDtypeStruct(state_t.shape, F32)],
        scratch_shapes=[pltpu.VMEM((6, width, B), F32), pltpu.VMEM((width, B), F32),
                        pltpu.VMEM((B, width), F32), pltpu.VMEM((B, width), F32)],
        compiler_params=_cparams(("arbitrary",)),
        name="rwkv_step",
    )(pa, shift0, state_t, W['shift_mu'], *[W[n] for n in names])
    return out, st_new


def _lru_coeffs(xc, wr_ref, br_ref, wi_ref, bi_ref, lam_ref):
    xcb = xc.astype(BF16)
    n_grp = xc.shape[1] // LRU_GROUP
    grp = lambda w_ref: jnp.concatenate(
        [jnp.dot(xcb[:, i * LRU_GROUP:(i + 1) * LRU_GROUP], w_ref[i], preferred_element_type=F32)
         for i in range(n_grp)], axis=1)
    gr = jax.nn.sigmoid(grp(wr_ref) + br_ref[...])
    gi = jax.nn.sigmoid(grp(wi_ref) + bi_ref[...])
    a_t = jnp.exp(-LRU_C * gr * _softplus(-lam_ref[...]))
    return a_t, jnp.sqrt(1.0 - a_t * a_t) * gi * xc


def _lru_finish(hs, pg_ref, g0_ref, g1_ref, rw_ref, h_ref, wmix_ref, lng_ref, lnb_ref, alpha):
    lru_out = hs * jax.nn.gelu(pg_ref[...])
    merged = jax.nn.sigmoid(g0_ref[...]) * rw_ref[...] + jax.nn.sigmoid(g1_ref[...]) * lru_out
    mix = jnp.dot(merged.astype(BF16), wmix_ref[...], preferred_element_type=F32)
    return _layer_norm(alpha * h_ref[...] + mix, lng_ref[...], lnb_ref[...])


def _lru_kernel(x_ref, pg_ref, g0_ref, g1_ref, rw_ref, h_ref, buf_ref, h0_ref,
                cw_ref, cb_ref, wr_ref, br_ref, wi_ref, bi_ref, lam_ref, wmix_ref, lng_ref, lnb_ref,
                out_ref, hlast_ref, tail_scr, hc_scr, *, alpha):
    ti = pl.program_id(1)
    tt, width = x_ref.shape
    S8 = V7X_SUBLANES

    @pl.when(ti == 0)
    def _init():
        tail_scr[...] = buf_ref[...]
        hc_scr[...] = h0_ref[...]

    x = x_ref[...]
    tail = tail_scr[...]
    r8 = lax.broadcasted_iota(jnp.int32, (S8, width), 0)

    def delayed(d):
        head = jnp.where(r8 < d, pltpu.roll(tail, d, 0), pltpu.roll(x[:S8], d, 0))
        return jnp.concatenate([head, x_ref[S8 - d:tt - d, :]], axis=0)

    cw = cw_ref[...]
    conv = cw[0:1] * delayed(CONV_WIDTH - 1)
    for j in range(1, CONV_WIDTH - 1):
        conv = conv + cw[j:j + 1] * delayed(CONV_WIDTH - 1 - j)
    xc = cb_ref[...] + (conv + cw[CONV_WIDTH - 1:CONV_WIDTH] * x)
    tail_scr[...] = x_ref[tt - S8:tt, :]

    A, Bv = _lru_coeffs(xc, wr_ref, br_ref, wi_ref, bi_ref, lam_ref)
    A = A.reshape(tt // S8, S8, width)
    Bv = Bv.reshape(tt // S8, S8, width)
    in_grp = lax.broadcasted_iota(jnp.int32, (S8, width), 0)
    s = 1
    while s < S8:
        keep = in_grp >= s
        Bv = Bv + A * jnp.where(keep, pltpu.roll(Bv, s, 1), 0.0)
        A = A * jnp.where(keep, pltpu.roll(A, s, 1), 1.0)
        s *= 2
    carry = hc_scr[...]
    groups = []
    for gi in range(tt // S8):
        hg = Bv[gi] + A[gi] * carry
        groups.append(hg)
        carry = hg[S8 - 1:S8, :]
    hs = jnp.concatenate(groups, axis=0)
    h_last = carry
    hc_scr[...] = h_last
    hlast_ref[...] = h_last
    out_ref[...] = _lru_finish(hs, pg_ref, g0_ref, g1_ref, rw_ref, h_ref, wmix_ref, lng_ref, lnb_ref, alpha)


def _lru_step_kernel(x_ref, pg_ref, g0_ref, g1_ref, rw_ref, h_ref, buf_ref, h0_ref,
                     cw_ref, cb_ref, wr_ref, br_ref, wi_ref, bi_ref, lam_ref, wmix_ref, lng_ref, lnb_ref,
                     out_ref, hnew_ref, *, alpha):
    cw = cw_ref[...]
    conv = cw[0:1] * buf_ref[0]
    for j in range(1, CONV_WIDTH - 1):
        conv = conv + cw[j:j + 1] * buf_ref[j]
    xc = cb_ref[...] + (conv + cw[CONV_WIDTH - 1:CONV_WIDTH] * x_ref[...])
    A, Bv = _lru_coeffs(xc, wr_ref, br_ref, wi_ref, bi_ref, lam_ref)
    hs = Bv + A * h0_ref[...]
    hnew_ref[...] = hs
    out_ref[...] = _lru_finish(hs, pg_ref, g0_ref, g1_ref, rw_ref, h_ref, wmix_ref, lng_ref, lnb_ref, alpha)


_LRU_WEIGHTS = ('conv_w', 'conv_b', 'lru_wr', 'lru_br', 'lru_wi', 'lru_bi', 'lru_lambda', 'w_mix_out')


def _lru_step_mix_ln(pb, rw, h, buf, h0, W, ln_g, ln_b, *, alpha, tm):
    B, width = rw.shape
    assert B % tm == 0
    blk = lambda j: pl.BlockSpec((tm, width), lambda i: (i, j))
    out, h_new = pl.pallas_call(
        functools.partial(_lru_step_kernel, alpha=alpha),
        grid=(B // tm,),
        in_specs=[blk(0), blk(1), blk(2), blk(3), blk(0), blk(0),
                  pl.BlockSpec((CONV_WIDTH - 1, tm, width), lambda i: (0, i, 0)), blk(0)]
                 + [_const_spec(W[n].shape) for n in _LRU_WEIGHTS] + [_const_spec(ln_g.shape), _const_spec(ln_b.shape)],
        out_specs=[blk(0), blk(0)],
        out_shape=[jax.ShapeDtypeStruct((B, width), F32), jax.ShapeDtypeStruct((B, width), F32)],
        compiler_params=_cparams(("parallel",)),
        name="lru_step_mix_ln",
    )(pb, pb, pb, pb, rw, h, buf, h0, *[W[n] for n in _LRU_WEIGHTS], ln_g, ln_b)
    return out, h_new


def _lru_mix_ln(pb, rw, h, buf8, h0, W, ln_g, ln_b, *, alpha, tt):
    B, T, width = rw.shape
    assert T % tt == 0 and tt % V7X_SUBLANES == 0
    blk = lambda j: pl.BlockSpec((None, tt, width), lambda b, t: (b, t, j))
    names = _LRU_WEIGHTS
    out, h_last = pl.pallas_call(
        functools.partial(_lru_kernel, alpha=alpha),
        grid=(B, T // tt),
        in_specs=[blk(0), blk(1), blk(2), blk(3), blk(0), blk(0),
                  pl.BlockSpec((None, V7X_SUBLANES, width), lambda b, t: (b, 0, 0)),
                  pl.BlockSpec((None, 1, width), lambda b, t: (b, 0, 0))]
                 + [_const_spec(W[n].shape) for n in names] + [_const_spec(ln_g.shape), _const_spec(ln_b.shape)],
        out_specs=[blk(0), pl.BlockSpec((None, 1, width), lambda b, t: (b, 0, 0))],
        out_shape=[jax.ShapeDtypeStruct((B, T, width), F32), jax.ShapeDtypeStruct((B, 1, width), F32)],
        scratch_shapes=[pltpu.VMEM((V7X_SUBLANES, width), F32), pltpu.VMEM((1, width), F32)],
        compiler_params=_cparams(("parallel", "arbitrary")),
        name="lru_mix_ln",
    )(pb, pb, pb, pb, rw, h, buf8, h0, *[W[n] for n in names], ln_g, ln_b)
    return out, h_last


def _xattn_kernel(h_ref, kv_ref, wq_ref, wo_ref, g_ref, b_ref, o_ref, *, alpha, heads):
    h = h_ref[...]
    d = h.shape[1]
    hd = d // heads
    q = jnp.dot(h.astype(BF16), wq_ref[...], preferred_element_type=F32)
    outs = []
    for j in range(heads):
        sl = slice(j * hd, (j + 1) * hd)
        s = _dot_dims(q[:, sl], kv_ref[:, sl], _NT) * (hd ** -0.5)
        e = jnp.exp(s - jnp.max(s, axis=-1, keepdims=True))
        p = e / jnp.sum(e, axis=-1, keepdims=True)
        outs.append(_dot(p, kv_ref[:, d + j * hd:d + (j + 1) * hd]))
    out = jnp.dot(jnp.concatenate(outs, axis=1).astype(BF16), wo_ref[...], preferred_element_type=F32)
    o_ref[...] = _layer_norm(alpha * h + out, g_ref[...], b_ref[...])


def _xattn_rows_kernel(h_ref, mk_ref, mv_ref, wq_ref, wo_ref, g_ref, b_ref, o_ref, *, alpha):
    h = h_ref[...]
    nb, m, heads, hd = mk_ref.shape
    q = jnp.dot(h.astype(BF16), wq_ref[...], preferred_element_type=F32)
    col = lax.broadcasted_iota(jnp.int32, (heads, m * heads), 1)
    own = (col % heads) == lax.broadcasted_iota(jnp.int32, (heads, m * heads), 0)
    rows = []
    for i in range(nb):
        k2 = mk_ref[i].reshape(m * heads, hd)
        v2 = mv_ref[i].reshape(m * heads, hd)
        q4 = jnp.concatenate([q[i:i + 1, j * hd:(j + 1) * hd] for j in range(heads)], axis=0)
        s = jnp.where(own, _dot_dims(q4, k2, _NT) * (hd ** -0.5), -jnp.inf)
        e = jnp.exp(s - jnp.max(s, axis=-1, keepdims=True))
        p = e / jnp.sum(e, axis=-1, keepdims=True)
        o4 = _dot(p, v2)
        rows.append(jnp.concatenate([o4[j:j + 1] for j in range(heads)], axis=1))
    out = jnp.dot(jnp.concatenate(rows, axis=0).astype(BF16), wo_ref[...], preferred_element_type=F32)
    o_ref[...] = _layer_norm(alpha * h + out, g_ref[...], b_ref[...])


def _xattn_ln(h, kv, wq, wo, g, b, *, alpha, heads, rows_per_batch, tile_rows):
    n, d = h.shape
    assert n % tile_rows == 0 and rows_per_batch % tile_rows == 0
    per = rows_per_batch // tile_rows
    m = kv.shape[0] // (n // rows_per_batch)
    return pl.pallas_call(
        functools.partial(_xattn_kernel, alpha=alpha, heads=heads),
        grid=(n // tile_rows,),
        in_specs=[pl.BlockSpec((tile_rows, d), lambda i: (i, 0)),
                  pl.BlockSpec((m, 2 * d), lambda i: (i // per, 0)),
                  _const_spec(wq.shape), _const_spec(wo.shape), _const_spec(g.shape), _const_spec(b.shape)],
        out_specs=pl.BlockSpec((tile_rows, d), lambda i: (i, 0)),
        out_shape=jax.ShapeDtypeStruct((n, d), F32),
        compiler_params=_cparams(("parallel",)),
        name="xattn_ln",
    )(h, kv, wq, wo, g, b)


def _xattn_rows_ln(h, mk, mv, wq, wo, g, b, *, alpha, nb):
    n, d = h.shape
    _, m, heads, hd = mk.shape
    assert n % nb == 0
    kv_spec = pl.BlockSpec((nb, m, heads, hd), lambda i: (i, 0, 0, 0))
    return pl.pallas_call(
        functools.partial(_xattn_rows_kernel, alpha=alpha),
        grid=(n // nb,),
        in_specs=[pl.BlockSpec((nb, d), lambda i: (i, 0)), kv_spec, kv_spec,
                  _const_spec(wq.shape), _const_spec(wo.shape), _const_spec(g.shape), _const_spec(b.shape)],
        out_specs=pl.BlockSpec((nb, d), lambda i: (i, 0)),
        out_shape=jax.ShapeDtypeStruct((n, d), F32),
        compiler_params=_cparams(("parallel",)),
        name="xattn_rows_ln",
    )(h, mk, mv, wq, wo, g, b)


def _kv_proj_kernel(x_ref, w_ref, kv_ref, k_ref, v_ref):
    heads, hd = k_ref.shape[1:]
    d = heads * hd
    kv = jnp.dot(x_ref[...].astype(BF16), w_ref[...], preferred_element_type=F32)
    kv_ref[...] = kv
    for j in range(heads):
        k_ref[:, j, :] = kv[:, j * hd:(j + 1) * hd]
        v_ref[:, j, :] = kv[:, d + j * hd:d + (j + 1) * hd]


def _kv_proj(x, wkv, *, heads, tm):
    n, d = x.shape
    assert n % tm == 0
    hd = d // heads
    out = jax.ShapeDtypeStruct((n, heads, hd), F32)
    return pl.pallas_call(
        _kv_proj_kernel,
        grid=(n // tm,),
        in_specs=[pl.BlockSpec((tm, d), lambda i: (i, 0)), _const_spec(wkv.shape)],
        out_specs=[pl.BlockSpec((tm, 2 * d), lambda i: (i, 0))] + [pl.BlockSpec((tm, heads, hd), lambda i: (i, 0, 0))] * 2,
        out_shape=[jax.ShapeDtypeStruct((n, 2 * d), F32), out, out],
        compiler_params=_cparams(("parallel",)),
        name="kv_proj",
    )(x, wkv)


def _row(v):
    return v.reshape(1, -1)


def _block_diag_groups(w):
    n, c, _ = w.shape
    per = LRU_GROUP // c
    w4 = w.reshape(n // per, per, c, c)
    bd = jnp.einsum('gjcd,jk->gjckd', w4, jnp.eye(per, dtype=w.dtype))
    return bd.reshape(n // per, LRU_GROUP, LRU_GROUP).astype(BF16)


def _prep_layer(l, ln_g, ln_b, ffn1_wi, ffn1_wo, ffn2_wi, ffn2_wo, w_in, shift_mu, decay_w0, decay_w2,
                aaa_a0, aaa_a2, gate_g2, k_k, k_a, r_k, gn_g, gn_b, conv_w, conv_b, lru_wr, lru_br,
                lru_wi, lru_bi, lru_lambda, w_mix_out, xa_wq, xa_wk, xa_wv, xa_wo):
    width = decay_w0.shape[1]
    rp = shift_mu.shape[1]
    d_ff = ffn1_wo.shape[1]
    bf = lambda w: w.astype(BF16)
    mu = shift_mu[l]
    return dict(
        ln_g=[_row(ln_g[l, i]) for i in range(4)], ln_b=[_row(ln_b[l, i]) for i in range(4)],
        ffn1=(bf(ffn1_wi[l][:, :d_ff]), bf(ffn1_wi[l][:, d_ff:]), bf(ffn1_wo[l])),
        ffn2=(bf(ffn2_wi[l][:, :d_ff]), bf(ffn2_wi[l][:, d_ff:]), bf(ffn2_wo[l])),
        w_in_a=bf(w_in[l][:, :rp]), w_in_b=bf(w_in[l][:, rp:]),
        shift_mu=_row(mu), mu_r=_row(mu[:width]), mu_k=_row(mu[width:2 * width]),
        mu_v=_row(mu[2 * width:3 * width]), mu_x=_row(mu[3 * width:]),
        decay_w0=_row(decay_w0[l]), decay_w2=bf(decay_w2[l]), aaa_a0=_row(aaa_a0[l]), aaa_a2=bf(aaa_a2[l]),
        gate_g2=bf(gate_g2[l]), k_k=_row(k_k[l]), k_a=_row(k_a[l]), r_k=_row(r_k[l]),
        gn_g=_row(gn_g[l]), gn_b=_row(gn_b[l]),
        conv_w=conv_w[l], conv_b=_row(conv_b[l]),
        lru_wr=_block_diag_groups(lru_wr[l]), lru_br=_row(lru_br[l]),
        lru_wi=_block_diag_groups(lru_wi[l]), lru_bi=_row(lru_bi[l]), lru_lambda=_row(lru_lambda[l]),
        w_mix_out=bf(w_mix_out[l]), xa_wq=bf(xa_wq[l]), xa_wo=bf(xa_wo[l]),
        xa_wkv=bf(jnp.concatenate([xa_wk[l], xa_wv[l]], axis=1)),
    )


def _tile(n, pref):
    return pref if n % pref == 0 else n


def _layer(h, mem, state, shift0, h0, buf0, W, *, alpha, xa_heads):
    B, T, d = h.shape
    n = B * T
    tm = _tile(n, 512)
    h1 = _ffn_ln(h.reshape(n, d), *W['ffn1'], W['ln_g'][0], W['ln_b'][0], alpha=alpha, tm=tm)
    pa = _matmul(h1, W['w_in_a'], tm=tm, tn=W['w_in_a'].shape[1])
    pb = _matmul(h1, W['w_in_b'], tm=tm, tn=W['w_in_b'].shape[1])
    width = W['decay_w0'].shape[1]
    lru_w = W['conv_b'].shape[1]
    pa3 = pa.reshape(B, T, -1)
    pb3 = pb.reshape(B, T, -1)
    if T > 1:
        assert state is None
        rw, s_new = _rwkv_chunked(pa3, shift0.reshape(B, 1, -1), W, tt=_tile(T, 512), hw=_tile(width, 512))
        buf8 = jnp.concatenate([jnp.zeros((B, V7X_SUBLANES - (CONV_WIDTH - 1), lru_w), F32), buf0], axis=1)
        h2, h_last = _lru_mix_ln(pb3, rw, h1.reshape(B, T, d), buf8, h0.reshape(B, 1, lru_w), W,
                                 W['ln_g'][1], W['ln_b'][1], alpha=alpha, tt=_tile(T, 256))
    else:
        rw, s_new = _rwkv_step(pa, shift0, jnp.transpose(state, (1, 2, 3, 0)), W, hg=2)
        s_new = jnp.transpose(s_new, (3, 0, 1, 2))
        h2, h_last = _lru_step_mix_ln(pb, rw, h1, jnp.swapaxes(buf0, 0, 1), h0, W,
                                      W['ln_g'][1], W['ln_b'][1], alpha=alpha, tm=_tile(B, 128))
    hist = CONV_WIDTH - 1
    if T >= hist:
        conv_in_tail = pb3[:, T - hist:, :lru_w]
    else:
        conv_in_tail = jnp.concatenate([buf0[:, T:], pb3[:, :, :lru_w]], axis=1)
    xa = (W['xa_wq'], W['xa_wo'], W['ln_g'][2], W['ln_b'][2])
    if T > 1:
        h3 = _xattn_ln(h2.reshape(n, d), mem, *xa, alpha=alpha, heads=xa_heads, rows_per_batch=T,
                       tile_rows=_tile(T, 512))
    else:
        h3 = _xattn_rows_ln(h2, *mem, *xa, alpha=alpha, nb=_tile(B, V7X_SUBLANES))
    h4 = _ffn_ln(h3, *W['ffn2'], W['ln_g'][3], W['ln_b'][3], alpha=alpha, tm=tm)
    return h4.reshape(B, T, d), s_new, pa3[:, -1], h_last.reshape(B, lru_w), conv_in_tail


def kernel(x_prompt, x_sample, mem_prompt, cache_mem_k, cache_mem_v, state_rwkv, state_rwkv_shift, state_lru, state_conv, ln_g, ln_b, ffn1_wi, ffn1_wo, ffn2_wi, ffn2_wo, w_in, shift_mu, decay_w0, decay_w2, aaa_a0, aaa_a2, gate_g2, k_k, k_a, r_k, gn_g, gn_b, conv_w, conv_b, lru_wr, lru_br, lru_wi, lru_bi, lru_lambda, w_mix_out, xa_wq, xa_wk, xa_wv, xa_wo):
    depth = ln_g.shape[0]
    alpha = (2.0 * depth) ** 0.25
    B, _, d = x_prompt.shape
    n_mem, xa_heads, xa_head = cache_mem_k.shape[2:]
    rp = shift_mu.shape[1]
    lru_w = conv_b.shape[1]
    hp, hs = x_prompt, x_sample
    outs = [[] for _ in range(10)]
    for l in range(depth):
        W = _prep_layer(l, ln_g, ln_b, ffn1_wi, ffn1_wo, ffn2_wi, ffn2_wo, w_in, shift_mu, decay_w0, decay_w2,
                        aaa_a0, aaa_a2, gate_g2, k_k, k_a, r_k.reshape(depth, -1), gn_g, gn_b, conv_w, conv_b,
                        lru_wr, lru_br, lru_wi, lru_bi, lru_lambda, w_mix_out, xa_wq, xa_wk, xa_wv, xa_wo)
        kv, mk, mv = _kv_proj(mem_prompt.reshape(B * n_mem, d), W['xa_wkv'], heads=xa_heads,
                              tm=_tile(B * n_mem, 512))
        mk = mk.reshape(B, n_mem, xa_heads, xa_head)
        mv = mv.reshape(B, n_mem, xa_heads, xa_head)
        hp, S1, sh1, h1, b1 = _layer(
            hp, kv, None, jnp.zeros((B, rp), F32), jnp.zeros((B, lru_w), F32),
            jnp.zeros((B, CONV_WIDTH - 1, lru_w), F32), W, alpha=alpha, xa_heads=xa_heads)
        hs, S2, sh2, h2, b2 = _layer(
            hs, (cache_mem_k[l], cache_mem_v[l]),
            state_rwkv[l], state_rwkv_shift[l], state_lru[l], state_conv[l], W, alpha=alpha, xa_heads=xa_heads)
        for lst, val in zip(outs, (mk, mv, S1, sh1, h1, b1, S2, sh2, h2, b2)):
            lst.append(val)
    return (hp, hs) + tuple(jnp.stack(o) for o in outs)
```

```python
import functools

import jax
import jax.numpy as jnp
from jax import lax
from jax.experimental import pallas as pl
from jax.experimental.pallas import tpu as pltpu

F32 = jnp.float32
BF16 = jnp.bfloat16

RWKV_HEAD = 64
DECAY_LORA = 64
AAA_LORA = 64
GATE_LORA = 128
GN_EPS = 64e-5
LRU_BLOCK = 64
CONV_WIDTH = 4
LRU_C = 8.0
LN_EPS = 1e-5

V7X_LANES = 128
V7X_SUBLANES = 8
V7X_MXU_DIM = 256
V7X_SCOPED_VMEM_BYTES = 60000 * 1024

RWKV_CHUNK = 64
HEAD_PAIR = 2 * RWKV_HEAD
SCAN_ROW_GROUPS = 2
LRU_GROUP = V7X_MXU_DIM
LRU_PROJ_PIECES = 8


def _cparams(semantics):
    return pltpu.CompilerParams(dimension_semantics=semantics, vmem_limit_bytes=V7X_SCOPED_VMEM_BYTES)


def _const_spec(shape):
    zeros = (0,) * len(shape)
    return pl.BlockSpec(shape, lambda *_: zeros)


def _dot(a, b):
    return jnp.dot(a.astype(BF16), b.astype(BF16), preferred_element_type=F32)


def _dot_dims(a, b, dims):
    return lax.dot_general(a.astype(BF16), b.astype(BF16), (dims, ((), ())), preferred_element_type=F32)


_NN = ((1,), (0,))
_NT = ((1,), (1,))
_TN = ((0,), (0,))


def _split2(x):
    hi = x.astype(BF16)
    lo = (x - hi.astype(F32)).astype(BF16)
    return hi, lo


def _dot_exact_lhs(a_bf16, b):
    hi, lo = _split2(b)
    d = lambda y: jnp.dot(a_bf16, y, preferred_element_type=F32)
    return d(hi) + d(lo)


def _dot_exact_rhs(a, b_bf16):
    hi, lo = _split2(a)
    d = lambda x: jnp.dot(x, b_bf16, preferred_element_type=F32)
    return d(hi) + d(lo)


def _layer_norm(x, g, b):
    mu = jnp.mean(x, axis=-1, keepdims=True)
    xc = x - mu
    var = jnp.mean(xc * xc, axis=-1, keepdims=True)
    return xc * lax.rsqrt(var + LN_EPS) * g + b


def _softplus(z):
    return jnp.maximum(z, 0.0) + jnp.log(1.0 + jnp.exp(-jnp.abs(z)))


def _head_ones(width):
    r = lax.broadcasted_iota(jnp.int32, (width, width), 0) // RWKV_HEAD
    c = lax.broadcasted_iota(jnp.int32, (width, width), 1) // RWKV_HEAD
    return (r == c).astype(BF16)


def _mm_kernel(x_ref, w_ref, o_ref):
    o_ref[...] = jnp.dot(x_ref[...].astype(BF16), w_ref[...], preferred_element_type=F32)


def _matmul(x, w, *, tm, tn):
    n, k = x.shape
    m = w.shape[1]
    assert n % tm == 0 and m % tn == 0
    return pl.pallas_call(
        _mm_kernel,
        grid=(m // tn, n // tm),
        in_specs=[pl.BlockSpec((tm, k), lambda j, i: (i, 0)),
                  pl.BlockSpec((k, tn), lambda j, i: (0, j))],
        out_specs=pl.BlockSpec((tm, tn), lambda j, i: (i, j)),
        out_shape=jax.ShapeDtypeStruct((n, m), F32),
        compiler_params=_cparams(("parallel", "parallel")),
        name="matmul",
    )(x, w)


def _ffn_kernel(x_ref, wg_ref, wu_ref, wo_ref, g_ref, b_ref, o_ref, *, alpha, row_groups):
    tm = x_ref.shape[0]
    tg = tm // row_groups
    for gi in range(row_groups):
        rows = slice(gi * tg, (gi + 1) * tg)
        x = x_ref[rows, :]
        xb = x.astype(BF16)
        gate = jnp.dot(xb, wg_ref[...], preferred_element_type=F32)
        up = jnp.dot(xb, wu_ref[...], preferred_element_type=F32)
        mid = (gate * jax.nn.sigmoid(gate) * up).astype(BF16)
        down = jnp.dot(mid, wo_ref[...], preferred_element_type=F32)
        o_ref[rows, :] = _layer_norm(alpha * x + 0.5 * down, g_ref[...], b_ref[...])


def _ffn_ln(x, wg, wu, wo, g, b, *, alpha, tm):
    n, d = x.shape
    assert n % tm == 0
    row_groups = 2 if tm % (2 * V7X_MXU_DIM) == 0 else 1
    return pl.pallas_call(
        functools.partial(_ffn_kernel, alpha=alpha, row_groups=row_groups),
        grid=(n // tm,),
        in_specs=[pl.BlockSpec((tm, d), lambda i: (i, 0)),
                  _const_spec(wg.shape), _const_spec(wu.shape), _const_spec(wo.shape),
                  _const_spec(g.shape), _const_spec(b.shape)],
        out_specs=pl.BlockSpec((tm, d), lambda i: (i, 0)),
        out_shape=jax.ShapeDtypeStruct((n, d), F32),
        compiler_params=_cparams(("parallel",)),
        name="ffn_ln",
    )(x, wg, wu, wo, g, b)


def _rwkv_pre(r, k, v, xx, w0, w2, a0, a2, g2, k_k, k_a, r_k, ones):
    xw = xx[:, :DECAY_LORA]
    xa = xx[:, DECAY_LORA:DECAY_LORA + AAA_LORA]
    xg = xx[:, DECAY_LORA + AAA_LORA:]
    z = w0 + _dot(jnp.tanh(xw), w2)
    lw = -jnp.exp(-_softplus(-z) - 0.5)
    a = jax.nn.sigmoid(a0 + _dot(xa, a2))
    g = _dot(jax.nn.sigmoid(xg), g2)
    kkr = k * k_k
    ss = _dot(kkr * kkr, ones)
    kk = kkr * lax.rsqrt(jnp.maximum(ss, 1e-24))
    kf = k * (1.0 + (a - 1.0) * k_a)
    bonus = _dot(r * kf * r_k, ones) * v
    return lw, a, g, kk, kf, bonus


def _rwkv_post(y, bonus, g, gn_g, gn_b, ones):
    inv_n = 1.0 / RWKV_HEAD
    ym = _dot_exact_rhs(y, ones) * inv_n
    yc = y - ym
    yv = _dot(yc * yc, ones) * inv_n
    yn = yc * lax.rsqrt(yv + GN_EPS) * gn_g + gn_b
    return (yn + bonus) * g


def _bdot(a, b, dims):
    dn = ((tuple(d + 1 for d in dims[0]), tuple(d + 1 for d in dims[1])), ((0,), (0,)))
    return lax.dot_general(a.astype(BF16), b.astype(BF16), dn, preferred_element_type=F32)


def _scan_operands(r, kf, v, kk, a, lw):
    tt, hw = r.shape
    C = RWKV_CHUNK
    n_pairs = hw // HEAD_PAIR
    ltri = (lax.broadcasted_iota(jnp.int32, (C, C), 0) >= lax.broadcasted_iota(jnp.int32, (C, C), 1)).astype(BF16)
    first = lax.broadcasted_iota(jnp.int32, (C, HEAD_PAIR), 1) < RWKV_HEAD

    def bd(x):
        return jnp.concatenate([jnp.where(first, x, 0.0), jnp.where(first, 0.0, x)], axis=1)

    names = ('a', 'r', 'b', 'k', 'v', 'bh', 'kh')
    ops = {n: [] for n in names}
    wcs = []
    for c in range(tt // C):
        rows = slice(c * C, (c + 1) * C)
        lw_c = lw[rows]
        L = _dot_exact_lhs(ltri, lw_c)
        Lc = L[C - 1:C, :]
        e_nl = jnp.exp(-L)
        e_c = jnp.exp(Lc - L)
        bb = kk[rows] * a[rows]
        bf = lambda x: x.astype(BF16)
        tile = dict(a=bf(-kk[rows] * jnp.exp(L - lw_c)), r=r[rows] * jnp.exp(L), b=bf(bb * e_nl),
                    k=bf(kf[rows] * e_nl), v=bf(v[rows]), bh=bf(bb * e_c), kh=bf(kf[rows] * e_c))
        wc = jnp.exp(Lc)
        for p in range(n_pairs):
            lanes = slice(p * HEAD_PAIR, (p + 1) * HEAD_PAIR)
            for n in names:
                ops[n].append(tile[n][:, lanes])
            wcs.append(wc[:, lanes])
    A, R, B, K, V, Bh, Kh = (jnp.stack(ops[n]) for n in names)
    Vbd = bd(V)
    G = _bdot(jnp.concatenate([A, R.astype(BF16)], axis=1), jnp.concatenate([bd(B), bd(K)], axis=1), _NT)
    tok = lax.broadcasted_iota(jnp.int32, (C, HEAD_PAIR), 0)
    src = lax.broadcasted_iota(jnp.int32, (C, HEAD_PAIR), 1) % RWKV_HEAD
    a_ab = jnp.where(tok > src, G[:, :C, :HEAD_PAIR], 0.0)
    a_ak = jnp.where(tok > src, G[:, :C, HEAD_PAIR:], 0.0).astype(BF16)
    a_rb = jnp.where(tok >= src, G[:, C:, :HEAD_PAIR], 0.0).astype(BF16)
    a_rk = jnp.where(tok >= src, G[:, C:, HEAD_PAIR:], 0.0).astype(BF16)
    P = jnp.where(tok == src, 1.0, 0.0) + a_ab
    N = a_ab.astype(BF16)
    N = _bdot(N, bd(N), _NN).astype(BF16)
    steps = 2
    while 2 * steps < C:
        NP = _bdot(jnp.concatenate([N, P.astype(BF16)], axis=1), bd(N), _NN)
        N = NP[:, :C].astype(BF16)
        P = P + NP[:, C:]
        steps *= 2
    P = (P + _bdot(P, bd(N), _NN)).astype(BF16)
    aV = _bdot(a_ak, Vbd, _NN).astype(BF16)
    XU = _bdot(P, jnp.concatenate([bd(A), bd(aV)], axis=2), _NN).astype(BF16)
    X1 = XU[:, :, :HEAD_PAIR]
    Uloc = XU[:, :, HEAD_PAIR:]
    Q = (R + _bdot(a_rb, bd(X1), _NN)).astype(BF16)
    Yloc = _bdot(jnp.concatenate([a_rb, a_rk], axis=2), jnp.concatenate([bd(Uloc), Vbd], axis=1), _NN)
    ri = lax.broadcasted_iota(jnp.int32, (HEAD_PAIR, HEAD_PAIR), 0) // RWKV_HEAD
    ci = lax.broadcasted_iota(jnp.int32, (HEAD_PAIR, HEAD_PAIR), 1) // RWKV_HEAD
    same_head = ri == ci
    Pm = jnp.where(same_head, _bdot(X1, Bh, _TN), 0.0).astype(BF16)
    Sloc = jnp.where(same_head, _bdot(jnp.concatenate([Uloc, V], axis=1), jnp.concatenate([Bh, Kh], axis=1), _TN),
                     0.0)
    return Q, Yloc, Pm, Sloc, jnp.stack(wcs)


def _rwkv_chunk_kernel(pr_ref, pk_ref, pv_ref, px_ref, sr_ref, sk_ref, sv_ref, sx_ref,
                       mur_ref, muk_ref, muv_ref, mux_ref, w0_ref, w2_ref, a0_ref, a2_ref, g2_ref,
                       kk_ref, ka_ref, rk_ref, gng_ref, gnb_ref,
                       out_ref, s_out_ref,
                       s_scr, cr_scr, ck_scr, cv_scr, cx_scr, y_scr):
    ti = pl.program_id(2)
    tt, hw = pr_ref.shape
    n_pairs = hw // HEAD_PAIR

    @pl.when(ti == 0)
    def _init():
        s_scr[...] = jnp.zeros(s_scr.shape, F32)
        cr_scr[...] = sr_ref[...]
        ck_scr[...] = sk_ref[...]
        cv_scr[...] = sv_ref[...]
        cx_scr[...] = sx_ref[...]

    S8 = V7X_SUBLANES

    def shifted(p_ref, c_scr, mu_ref):
        p = p_ref[...]
        first = lax.broadcasted_iota(jnp.int32, (S8, p.shape[1]), 0) == 0
        head = jnp.where(first, c_scr[...], pltpu.roll(p[:S8], 1, 0))
        prev = jnp.concatenate([head, p_ref[S8 - 1:tt - 1, :]], axis=0)
        c_scr[...] = p_ref[tt - 1:tt, :]
        return p + (prev - p) * mu_ref[...]

    r = shifted(pr_ref, cr_scr, mur_ref)
    k = shifted(pk_ref, ck_scr, muk_ref)
    v = shifted(pv_ref, cv_scr, muv_ref)
    xx = shifted(px_ref, cx_scr, mux_ref)
    ones = _head_ones(hw)
    lw, a, g, kk, kf, bonus = _rwkv_pre(r, k, v, xx, w0_ref[...], w2_ref[...], a0_ref[...], a2_ref[...],
                                        g2_ref[...], kk_ref[...], ka_ref[...], rk_ref[...], ones)
    C = RWKV_CHUNK
    S = s_scr[...]
    n_groups = SCAN_ROW_GROUPS if tt % (SCAN_ROW_GROUPS * C) == 0 else 1
    tg = tt // n_groups
    for gi in range(n_groups):
        rows = slice(gi * tg, (gi + 1) * tg)
        Q, Yloc, Pm, Sloc, wc = _scan_operands(r[rows], kf[rows], v[rows], kk[rows], a[rows], lw[rows])
        for c in range(tg // C):
            inst = slice(c * n_pairs, (c + 1) * n_pairs)
            y_c = _bdot(Q[inst], S, _NT) + Yloc[inst]
            row0 = gi * tg + c * C
            for p in range(n_pairs):
                y_scr[row0:row0 + C, p * HEAD_PAIR:(p + 1) * HEAD_PAIR] = y_c[p]
            S = S * wc[inst] + _bdot(S, Pm[inst], _NN) + Sloc[inst]
    s_scr[...] = S
    out_ref[...] = _rwkv_post(y_scr[...], bonus, g, gng_ref[...], gnb_ref[...], ones)

    @pl.when(ti == pl.num_programs(2) - 1)
    def _emit_state():
        for p in range(n_pairs):
            s_out_ref[2 * p] = S[p, :RWKV_HEAD, :RWKV_HEAD]
            s_out_ref[2 * p + 1] = S[p, RWKV_HEAD:, RWKV_HEAD:]


def _rwkv_chunked(pa, shift0, W, *, tt, hw):
    B, T, _ = pa.shape
    width = W['decay_w0'].shape[1]
    heads = width // RWKV_HEAD
    assert T % tt == 0 and tt % RWKV_CHUNK == 0 and width % hw == 0 and hw % HEAD_PAIR == 0
    nb = width // hw
    xw = DECAY_LORA + AAA_LORA + GATE_LORA
    assert (3 * width) % xw == 0
    xblk = 3 * width // xw
    col = lambda off: (lambda b, h, t: (b, t, off + h))
    vec = lambda: pl.BlockSpec((1, hw), lambda b, h, t: (0, h))
    in_specs = [
        pl.BlockSpec((None, tt, hw), col(0)), pl.BlockSpec((None, tt, hw), col(nb)),
        pl.BlockSpec((None, tt, hw), col(2 * nb)), pl.BlockSpec((None, tt, xw), lambda b, h, t: (b, t, xblk)),
        pl.BlockSpec((None, 1, hw), lambda b, h, t: (b, 0, h)), pl.BlockSpec((None, 1, hw), lambda b, h, t: (b, 0, nb + h)),
        pl.BlockSpec((None, 1, hw), lambda b, h, t: (b, 0, 2 * nb + h)), pl.BlockSpec((None, 1, xw), lambda b, h, t: (b, 0, xblk)),
        vec(), vec(), vec(), _const_spec((1, xw)),
        vec(), pl.BlockSpec((DECAY_LORA, hw), lambda b, h, t: (0, h)),
        vec(), pl.BlockSpec((AAA_LORA, hw), lambda b, h, t: (0, h)),
        pl.BlockSpec((GATE_LORA, hw), lambda b, h, t: (0, h)),
        vec(), vec(), vec(), vec(), vec(),
    ]
    out, s_new = pl.pallas_call(
        _rwkv_chunk_kernel,
        grid=(B, nb, T // tt),
        in_specs=in_specs,
        out_specs=[pl.BlockSpec((None, tt, hw), lambda b, h, t: (b, t, h)),
                   pl.BlockSpec((None, hw // RWKV_HEAD, RWKV_HEAD, RWKV_HEAD), lambda b, h, t: (b, h, 0, 0))],
        out_shape=[jax.ShapeDtypeStruct((B, T, width), F32),
                   jax.ShapeDtypeStruct((B, heads, RWKV_HEAD, RWKV_HEAD), F32)],
        scratch_shapes=[pltpu.VMEM((hw // HEAD_PAIR, HEAD_PAIR, HEAD_PAIR), F32),
                        pltpu.VMEM((1, hw), F32), pltpu.VMEM((1, hw), F32), pltpu.VMEM((1, hw), F32),
                        pltpu.VMEM((1, xw), F32), pltpu.VMEM((tt, hw), F32)],
        compiler_params=_cparams(("parallel", "parallel", "arbitrary")),
        name="rwkv_chunked",
    )(pa, pa, pa, pa, shift0, shift0, shift0, shift0,
      W['mu_r'], W['mu_k'], W['mu_v'], W['mu_x'], W['decay_w0'], W['decay_w2'], W['aaa_a0'], W['aaa_a2'],
      W['gate_g2'], W['k_k'], W['k_a'], W['r_k'], W['gn_g'], W['gn_b'])
    return out, s_new


def _rwkv_step_kernel(p_ref, s0_ref, st_ref, mu_ref, w0_ref, w2_ref, a0_ref, a2_ref, g2_ref,
                      kk_ref, ka_ref, rk_ref, gng_ref, gnb_ref,
                      out_ref, st_out_ref, vec_scr, y_scr, bonus_scr, g_scr):
    i = pl.program_id(0)
    B, width = out_ref.shape
    hg = st_ref.shape[0]
    H = RWKV_HEAD
    ones = _head_ones(width)

    @pl.when(i == 0)
    def _prologue():
        p = p_ref[...]
        xs = p + (s0_ref[...] - p) * mu_ref[...]
        r, k, v, xx = xs[:, :width], xs[:, width:2 * width], xs[:, 2 * width:3 * width], xs[:, 3 * width:]
        lw, a, g, kk, kf, bonus = _rwkv_pre(r, k, v, xx, w0_ref[...], w2_ref[...], a0_ref[...], a2_ref[...],
                                            g2_ref[...], kk_ref[...], ka_ref[...], rk_ref[...], ones)
        for j, vec in enumerate((r, kf, v, kk, kk * a, jnp.exp(lw))):
            vec_scr[j] = vec.T
        bonus_scr[...] = bonus
        g_scr[...] = g

    for hl in range(hg):
        base = pl.multiple_of((i * hg + hl) * H, H)
        r_h, kf_h, kk_h, kka_h, w_h = (vec_scr[j, pl.ds(base, H), :] for j in (0, 1, 3, 4, 5))

        def body(vi, carry):
            S = st_ref[hl, vi]
            sa = jnp.sum(S * kk_h, axis=0, keepdims=True)
            v_row = vec_scr[2, pl.ds(base + vi, 1), :]
            S2 = S * w_h - sa * kka_h + v_row * kf_h
            st_out_ref[hl, vi] = S2
            y_scr[pl.ds(base + vi, 1), :] = jnp.sum(S2 * r_h, axis=0, keepdims=True)
            return carry

        lax.fori_loop(0, H, body, 0, unroll=8)

    @pl.when(i == pl.num_programs(0) - 1)
    def _epilogue():
        out_ref[...] = _rwkv_post(y_scr[...].T, bonus_scr[...], g_scr[...], gng_ref[...], gnb_ref[...], ones)


def _rwkv_step(pa, shift0, state_t, W, *, hg):
    B, proj = pa.shape
    width = W['decay_w0'].shape[1]
    heads = width // RWKV_HEAD
    assert heads % hg == 0 and state_t.shape == (heads, RWKV_HEAD, RWKV_HEAD, B)
    st_spec = pl.BlockSpec((hg, RWKV_HEAD, RWKV_HEAD, B), lambda i: (i, 0, 0, 0))
    names = ('decay_w0', 'decay_w2', 'aaa_a0', 'aaa_a2', 'gate_g2', 'k_k', 'k_a', 'r_k', 'gn_g', 'gn_b')
    out, st_new = pl.pallas_call(
        _rwkv_step_kernel,
        grid=(heads // hg,),
        in_specs=[_const_spec((B, proj)), _const_spec((B, proj)), st_spec, _const_spec(W['shift_mu'].shape)]
                 + [_const_spec(W[n].shape) for n in names],
        out_specs=[_const_spec((B, width)), st_spec],
        out_shape=[jax.ShapeDtypeStruct((B, width), F32), jax.ShapeDtypeStruct(state_t.shape, F32)],
        scratch_shapes=[pltpu.VMEM((6, width, B), F32), pltpu.VMEM((width, B), F32),
                        pltpu.VMEM((B, width), F32), pltpu.VMEM((B, width), F32)],
        compiler_params=_cparams(("arbitrary",)),
        name="rwkv_step",
    )(pa, shift0, state_t, W['shift_mu'], *[W[n] for n in names])
    return out, st_new


def _lru_coeffs(xc, wr_ref, br_ref, wi_ref, bi_ref, lam_ref):
    xcb = xc.astype(BF16)
    n_grp = xc.shape[1] // LRU_GROUP
    grp = lambda w_ref: jnp.concatenate(
        [jnp.dot(xcb[:, i * LRU_GROUP:(i + 1) * LRU_GROUP], w_ref[i], preferred_element_type=F32)
         for i in range(n_grp)], axis=1)
    gr = jax.nn.sigmoid(grp(wr_ref) + br_ref[...])
    gi = jax.nn.sigmoid(grp(wi_ref) + bi_ref[...])
    a_t = jnp.exp(-LRU_C * gr * _softplus(-lam_ref[...]))
    return a_t, jnp.sqrt(1.0 - a_t * a_t) * gi * xc


def _lru_finish(hs, pg, g0, g1, rw, h, wmix_ref, lng_ref, lnb_ref, alpha):
    lru_out = hs * jax.nn.gelu(pg)
    merged = jax.nn.sigmoid(g0) * rw + jax.nn.sigmoid(g1) * lru_out
    mix = jnp.dot(merged.astype(BF16), wmix_ref[...], preferred_element_type=F32)
    return _layer_norm(alpha * h + mix, lng_ref[...], lnb_ref[...])


def _lru_kernel(hn_ref, rw_ref, h_ref, buf_ref, h0_ref, win_ref,
                cw_ref, cb_ref, wr_ref, br_ref, wi_ref, bi_ref, lam_ref, wmix_ref, lng_ref, lnb_ref,
                out_ref, hlast_ref, tail_ref, proj0_scr, proj1_scr, tail_scr, hc_scr, *, alpha):
    ti = pl.program_id(1)
    tt, width = rw_ref.shape
    S8 = V7X_SUBLANES
    project = lambda ref: jnp.dot(ref[...].astype(BF16), win_ref[...], preferred_element_type=F32)
    col = lambda j: slice(j * width, (j + 1) * width)

    @pl.when(ti == 0)
    def _init():
        tail_scr[...] = buf_ref[...]
        hc_scr[...] = h0_ref[...]
        proj0_scr[...] = project(h_ref)

    def step(cur_scr, nxt_scr):
        hn = hn_ref[...].astype(BF16)
        piece_w = win_ref.shape[1] // LRU_PROJ_PIECES
        pieces = iter(range(LRU_PROJ_PIECES))

        def emit(count=1):
            for _ in range(count):
                k = next(pieces, None)
                if k is not None:
                    cols = slice(k * piece_w, (k + 1) * piece_w)
                    nxt_scr[:, cols] = jnp.dot(hn, win_ref[:, cols], preferred_element_type=F32)

        x = cur_scr[:, col(0)]
        tail = tail_scr[...]
        r8 = lax.broadcasted_iota(jnp.int32, (S8, width), 0)

        def delayed(d):
            head = jnp.where(r8 < d, pltpu.roll(tail, d, 0), pltpu.roll(x[:S8], d, 0))
            return jnp.concatenate([head, cur_scr[S8 - d:tt - d, col(0)]], axis=0)

        cw = cw_ref[...]
        conv = cw[0:1] * delayed(CONV_WIDTH - 1)
        for j in range(1, CONV_WIDTH - 1):
            conv = conv + cw[j:j + 1] * delayed(CONV_WIDTH - 1 - j)
        xc = cb_ref[...] + (conv + cw[CONV_WIDTH - 1:CONV_WIDTH] * x)
        tail_scr[...] = x[tt - S8:, :]
        tail_ref[...] = x[tt - S8:, :]
        emit()

        A, Bv = _lru_coeffs(xc, wr_ref, br_ref, wi_ref, bi_ref, lam_ref)
        emit()
        A = A.reshape(tt // S8, S8, width)
        Bv = Bv.reshape(tt // S8, S8, width)
        in_grp = lax.broadcasted_iota(jnp.int32, (S8, width), 0)
        s = 1
        while s < S8:
            keep = in_grp >= s
            Bv = Bv + A * jnp.where(keep, pltpu.roll(Bv, s, 1), 0.0)
            A = A * jnp.where(keep, pltpu.roll(A, s, 1), 1.0)
            s *= 2
            emit()
        carry = hc_scr[...]
        groups = []
        for gi in range(tt // S8):
            hg = Bv[gi] + A[gi] * carry
            groups.append(hg)
            carry = hg[S8 - 1:S8, :]
        hs = jnp.concatenate(groups, axis=0)
        hc_scr[...] = carry
        hlast_ref[...] = carry
        emit()
        lru_out = hs * jax.nn.gelu(cur_scr[:, col(1)])
        emit()
        merged = jax.nn.sigmoid(cur_scr[:, col(2)]) * rw_ref[...] + jax.nn.sigmoid(cur_scr[:, col(3)]) * lru_out
        emit()
        mix = jnp.dot(merged.astype(BF16), wmix_ref[...], preferred_element_type=F32)
        out_ref[...] = _layer_norm(alpha * h_ref[...] + mix, lng_ref[...], lnb_ref[...])
        emit(LRU_PROJ_PIECES)

    pl.when(ti % 2 == 0)(lambda: step(proj0_scr, proj1_scr))
    pl.when(ti % 2 == 1)(lambda: step(proj1_scr, proj0_scr))


def _lru_step_kernel(x_ref, pg_ref, g0_ref, g1_ref, rw_ref, h_ref, buf_ref, h0_ref,
                     cw_ref, cb_ref, wr_ref, br_ref, wi_ref, bi_ref, lam_ref, wmix_ref, lng_ref, lnb_ref,
                     out_ref, hnew_ref, *, alpha):
    cw = cw_ref[...]
    conv = cw[0:1] * buf_ref[0]
    for j in range(1, CONV_WIDTH - 1):
        conv = conv + cw[j:j + 1] * buf_ref[j]
    xc = cb_ref[...] + (conv + cw[CONV_WIDTH - 1:CONV_WIDTH] * x_ref[...])
    A, Bv = _lru_coeffs(xc, wr_ref, br_ref, wi_ref, bi_ref, lam_ref)
    hs = Bv + A * h0_ref[...]
    hnew_ref[...] = hs
    out_ref[...] = _lru_finish(hs, pg_ref[...], g0_ref[...], g1_ref[...], rw_ref[...], h_ref[...],
                               wmix_ref, lng_ref, lnb_ref, alpha)


_LRU_WEIGHTS = ('conv_w', 'conv_b', 'lru_wr', 'lru_br', 'lru_wi', 'lru_bi', 'lru_lambda', 'w_mix_out')


def _lru_step_mix_ln(pb, rw, h, buf, h0, W, ln_g, ln_b, *, alpha, tm):
    B, width = rw.shape
    assert B % tm == 0
    blk = lambda j: pl.BlockSpec((tm, width), lambda i: (i, j))
    out, h_new = pl.pallas_call(
        functools.partial(_lru_step_kernel, alpha=alpha),
        grid=(B // tm,),
        in_specs=[blk(0), blk(1), blk(2), blk(3), blk(0), blk(0),
                  pl.BlockSpec((CONV_WIDTH - 1, tm, width), lambda i: (0, i, 0)), blk(0)]
                 + [_const_spec(W[n].shape) for n in _LRU_WEIGHTS] + [_const_spec(ln_g.shape), _const_spec(ln_b.shape)],
        out_specs=[blk(0), blk(0)],
        out_shape=[jax.ShapeDtypeStruct((B, width), F32), jax.ShapeDtypeStruct((B, width), F32)],
        compiler_params=_cparams(("parallel",)),
        name="lru_step_mix_ln",
    )(pb, pb, pb, pb, rw, h, buf, h0, *[W[n] for n in _LRU_WEIGHTS], ln_g, ln_b)
    return out, h_new


def _lru_mix_ln(h, rw, buf8, h0, W, ln_g, ln_b, *, alpha, tt):
    B, T, width = rw.shape
    assert T % tt == 0 and tt % V7X_SUBLANES == 0 and h.shape[2] == width
    nt = T // tt
    tile = pl.BlockSpec((None, tt, width), lambda b, t: (b, t, 0))
    nxt = pl.BlockSpec((None, tt, width), lambda b, t: (b, jnp.minimum(t + 1, nt - 1), 0))
    per_b = lambda rows: pl.BlockSpec((None, rows, width), lambda b, t: (b, 0, 0))
    win = W['w_in_b']
    out, h_last, tail = pl.pallas_call(
        functools.partial(_lru_kernel, alpha=alpha),
        grid=(B, nt),
        in_specs=[nxt, tile, tile, per_b(V7X_SUBLANES), per_b(1), _const_spec(win.shape)]
                 + [_const_spec(W[n].shape) for n in _LRU_WEIGHTS] + [_const_spec(ln_g.shape), _const_spec(ln_b.shape)],
        out_specs=[tile, per_b(1), per_b(V7X_SUBLANES)],
        out_shape=[jax.ShapeDtypeStruct((B, T, width), F32), jax.ShapeDtypeStruct((B, 1, width), F32),
                   jax.ShapeDtypeStruct((B, V7X_SUBLANES, width), F32)],
        scratch_shapes=[pltpu.VMEM((tt, win.shape[1]), F32), pltpu.VMEM((tt, win.shape[1]), F32),
                        pltpu.VMEM((V7X_SUBLANES, width), F32),
                        pltpu.VMEM((1, width), F32)],
        compiler_params=_cparams(("parallel", "arbitrary")),
        name="lru_mix_ln",
    )(h, rw, h, buf8, h0, win, *[W[n] for n in _LRU_WEIGHTS], ln_g, ln_b)
    return out, h_last, tail


def _xattn_kernel(h_ref, kv_ref, wq_ref, wo_ref, g_ref, b_ref, o_ref, *, alpha, heads):
    tm, d = h_ref.shape
    hd = d // heads
    row_groups = 2 if tm % (2 * V7X_MXU_DIM) == 0 else 1
    tg = tm // row_groups
    for gi in range(row_groups):
        rows = slice(gi * tg, (gi + 1) * tg)
        h = h_ref[rows, :]
        q = jnp.dot(h.astype(BF16), wq_ref[...], preferred_element_type=F32)
        outs = []
        for j in range(heads):
            sl = slice(j * hd, (j + 1) * hd)
            s = _dot_dims(q[:, sl], kv_ref[:, sl], _NT) * (hd ** -0.5)
            e = jnp.exp(s - jnp.max(s, axis=-1, keepdims=True))
            p = e * (1.0 / jnp.sum(e, axis=-1, keepdims=True))
            outs.append(_dot(p, kv_ref[:, d + j * hd:d + (j + 1) * hd]))
        out = jnp.dot(jnp.concatenate(outs, axis=1).astype(BF16), wo_ref[...], preferred_element_type=F32)
        o_ref[rows, :] = _layer_norm(alpha * h + out, g_ref[...], b_ref[...])


def _xattn_rows_kernel(h_ref, mk_ref, mv_ref, wq_ref, wo_ref, g_ref, b_ref, o_ref, *, alpha):
    h = h_ref[...]
    nb, m, heads, hd = mk_ref.shape
    q = jnp.dot(h.astype(BF16), wq_ref[...], preferred_element_type=F32)
    col = lax.broadcasted_iota(jnp.int32, (heads, m * heads), 1)
    own = (col % heads) == lax.broadcasted_iota(jnp.int32, (heads, m * heads), 0)
    rows = []
    for i in range(nb):
        k2 = mk_ref[i].reshape(m * heads, hd)
        v2 = mv_ref[i].reshape(m * heads, hd)
        q4 = jnp.concatenate([q[i:i + 1, j * hd:(j + 1) * hd] for j in range(heads)], axis=0)
        s = jnp.where(own, _dot_dims(q4, k2, _NT) * (hd ** -0.5), -jnp.inf)
        e = jnp.exp(s - jnp.max(s, axis=-1, keepdims=True))
        p = e / jnp.sum(e, axis=-1, keepdims=True)
        o4 = _dot(p, v2)
        rows.append(jnp.concatenate([o4[j:j + 1] for j in range(heads)], axis=1))
    out = jnp.dot(jnp.concatenate(rows, axis=0).astype(BF16), wo_ref[...], preferred_element_type=F32)
    o_ref[...] = _layer_norm(alpha * h + out, g_ref[...], b_ref[...])


def _xattn_ln(h, kv, wq, wo, g, b, *, alpha, heads, rows_per_batch, tile_rows):
    n, d = h.shape
    assert n % tile_rows == 0 and rows_per_batch % tile_rows == 0
    per = rows_per_batch // tile_rows
    m = kv.shape[0] // (n // rows_per_batch)
    return pl.pallas_call(
        functools.partial(_xattn_kernel, alpha=alpha, heads=heads),
        grid=(n // tile_rows,),
        in_specs=[pl.BlockSpec((tile_rows, d), lambda i: (i, 0)),
                  pl.BlockSpec((m, 2 * d), lambda i: (i // per, 0)),
                  _const_spec(wq.shape), _const_spec(wo.shape), _const_spec(g.shape), _const_spec(b.shape)],
        out_specs=pl.BlockSpec((tile_rows, d), lambda i: (i, 0)),
        out_shape=jax.ShapeDtypeStruct((n, d), F32),
        compiler_params=_cparams(("parallel",)),
        name="xattn_ln",
    )(h, kv, wq, wo, g, b)


def _xattn_rows_ln(h, mk, mv, wq, wo, g, b, *, alpha, nb):
    n, d = h.shape
    _, m, heads, hd = mk.shape
    assert n % nb == 0
    kv_spec = pl.BlockSpec((nb, m, heads, hd), lambda i: (i, 0, 0, 0))
    return pl.pallas_call(
        functools.partial(_xattn_rows_kernel, alpha=alpha),
        grid=(n // nb,),
        in_specs=[pl.BlockSpec((nb, d), lambda i: (i, 0)), kv_spec, kv_spec,
                  _const_spec(wq.shape), _const_spec(wo.shape), _const_spec(g.shape), _const_spec(b.shape)],
        out_specs=pl.BlockSpec((nb, d), lambda i: (i, 0)),
        out_shape=jax.ShapeDtypeStruct((n, d), F32),
        compiler_params=_cparams(("parallel",)),
        name="xattn_rows_ln",
    )(h, mk, mv, wq, wo, g, b)


def _kv_proj_kernel(x_ref, w_ref, kv_ref, k_ref, v_ref):
    heads, hd = k_ref.shape[1:]
    d = heads * hd
    kv = jnp.dot(x_ref[...].astype(BF16), w_ref[...], preferred_element_type=F32)
    kv_ref[...] = kv
    for j in range(heads):
        k_ref[:, j, :] = kv[:, j * hd:(j + 1) * hd]
        v_ref[:, j, :] = kv[:, d + j * hd:d + (j + 1) * hd]


def _kv_proj(x, wkv, *, heads, tm):
    n, d = x.shape
    assert n % tm == 0
    hd = d // heads
    out = jax.ShapeDtypeStruct((n, heads, hd), F32)
    return pl.pallas_call(
        _kv_proj_kernel,
        grid=(n // tm,),
        in_specs=[pl.BlockSpec((tm, d), lambda i: (i, 0)), _const_spec(wkv.shape)],
        out_specs=[pl.BlockSpec((tm, 2 * d), lambda i: (i, 0))] + [pl.BlockSpec((tm, heads, hd), lambda i: (i, 0, 0))] * 2,
        out_shape=[jax.ShapeDtypeStruct((n, 2 * d), F32), out, out],
        compiler_params=_cparams(("parallel",)),
        name="kv_proj",
    )(x, wkv)


def _row(v):
    return v.reshape(1, -1)


def _block_diag_groups(w):
    n, c, _ = w.shape
    per = LRU_GROUP // c
    w4 = w.reshape(n // per, per, c, c)
    bd = jnp.einsum('gjcd,jk->gjckd', w4, jnp.eye(per, dtype=w.dtype))
    return bd.reshape(n // per, LRU_GROUP, LRU_GROUP).astype(BF16)


def _prep_layer(l, ln_g, ln_b, ffn1_wi, ffn1_wo, ffn2_wi, ffn2_wo, w_in, shift_mu, decay_w0, decay_w2,
                aaa_a0, aaa_a2, gate_g2, k_k, k_a, r_k, gn_g, gn_b, conv_w, conv_b, lru_wr, lru_br,
                lru_wi, lru_bi, lru_lambda, w_mix_out, xa_wq, xa_wk, xa_wv, xa_wo):
    width = decay_w0.shape[1]
    rp = shift_mu.shape[1]
    d_ff = ffn1_wo.shape[1]
    bf = lambda w: w.astype(BF16)
    mu = shift_mu[l]
    return dict(
        ln_g=[_row(ln_g[l, i]) for i in range(4)], ln_b=[_row(ln_b[l, i]) for i in range(4)],
        ffn1=(bf(ffn1_wi[l][:, :d_ff]), bf(ffn1_wi[l][:, d_ff:]), bf(ffn1_wo[l])),
        ffn2=(bf(ffn2_wi[l][:, :d_ff]), bf(ffn2_wi[l][:, d_ff:]), bf(ffn2_wo[l])),
        w_in_a=bf(w_in[l][:, :rp]), w_in_b=bf(w_in[l][:, rp:]),
        shift_mu=_row(mu), mu_r=_row(mu[:width]), mu_k=_row(mu[width:2 * width]),
        mu_v=_row(mu[2 * width:3 * width]), mu_x=_row(mu[3 * width:]),
        decay_w0=_row(decay_w0[l]), decay_w2=bf(decay_w2[l]), aaa_a0=_row(aaa_a0[l]), aaa_a2=bf(aaa_a2[l]),
        gate_g2=bf(gate_g2[l]), k_k=_row(k_k[l]), k_a=_row(k_a[l]), r_k=_row(r_k[l]),
        gn_g=_row(gn_g[l]), gn_b=_row(gn_b[l]),
        conv_w=conv_w[l], conv_b=_row(conv_b[l]),
        lru_wr=_block_diag_groups(lru_wr[l]), lru_br=_row(lru_br[l]),
        lru_wi=_block_diag_groups(lru_wi[l]), lru_bi=_row(lru_bi[l]), lru_lambda=_row(lru_lambda[l]),
        w_mix_out=bf(w_mix_out[l]), xa_wq=bf(xa_wq[l]), xa_wo=bf(xa_wo[l]),
        xa_wkv=bf(jnp.concatenate([xa_wk[l], xa_wv[l]], axis=1)),
    )


def _tile(n, pref):
    return pref if n % pref == 0 else n


def _layer(h, mem, state, shift0, h0, buf0, W, *, alpha, xa_heads):
    B, T, d = h.shape
    n = B * T
    tm = _tile(n, 512)
    h1 = _ffn_ln(h.reshape(n, d), *W['ffn1'], W['ln_g'][0], W['ln_b'][0], alpha=alpha, tm=tm)
    pa = _matmul(h1, W['w_in_a'], tm=tm, tn=W['w_in_a'].shape[1])
    width = W['decay_w0'].shape[1]
    lru_w = W['conv_b'].shape[1]
    pa3 = pa.reshape(B, T, -1)
    hist = CONV_WIDTH - 1
    if T > 1:
        assert state is None and T >= V7X_SUBLANES
        rw, s_new = _rwkv_chunked(pa3, shift0.reshape(B, 1, -1), W, tt=_tile(T, 512), hw=_tile(width, 512))
        buf8 = jnp.concatenate([jnp.zeros((B, V7X_SUBLANES - hist, lru_w), F32), buf0], axis=1)
        h2, h_last, tail8 = _lru_mix_ln(h1.reshape(B, T, d), rw, buf8, h0.reshape(B, 1, lru_w), W,
                                        W['ln_g'][1], W['ln_b'][1], alpha=alpha, tt=_tile(T, 256))
        conv_in_tail = tail8[:, V7X_SUBLANES - hist:]
    else:
        pb = _matmul(h1, W['w_in_b'], tm=tm, tn=W['w_in_b'].shape[1])
        rw, s_new = _rwkv_step(pa, shift0, jnp.transpose(state, (1, 2, 3, 0)), W, hg=2)
        s_new = jnp.transpose(s_new, (3, 0, 1, 2))
        h2, h_last = _lru_step_mix_ln(pb, rw, h1, jnp.swapaxes(buf0, 0, 1), h0, W,
                                      W['ln_g'][1], W['ln_b'][1], alpha=alpha, tm=_tile(B, 128))
        conv_in_tail = jnp.concatenate([buf0[:, T:], pb[:, None, :lru_w]], axis=1)
    xa = (W['xa_wq'], W['xa_wo'], W['ln_g'][2], W['ln_b'][2])
    if T > 1:
        h3 = _xattn_ln(h2.reshape(n, d), mem, *xa, alpha=alpha, heads=xa_heads, rows_per_batch=T,
                       tile_rows=_tile(T, 512))
    else:
        h3 = _xattn_rows_ln(h2, *mem, *xa, alpha=alpha, nb=_tile(B, V7X_SUBLANES))
    h4 = _ffn_ln(h3, *W['ffn2'], W['ln_g'][3], W['ln_b'][3], alpha=alpha, tm=tm)
    return h4.reshape(B, T, d), s_new, pa3[:, -1], h_last.reshape(B, lru_w), conv_in_tail


def kernel(x_prompt, x_sample, mem_prompt, cache_mem_k, cache_mem_v, state_rwkv, state_rwkv_shift, state_lru, state_conv, ln_g, ln_b, ffn1_wi, ffn1_wo, ffn2_wi, ffn2_wo, w_in, shift_mu, decay_w0, decay_w2, aaa_a0, aaa_a2, gate_g2, k_k, k_a, r_k, gn_g, gn_b, conv_w, conv_b, lru_wr, lru_br, lru_wi, lru_bi, lru_lambda, w_mix_out, xa_wq, xa_wk, xa_wv, xa_wo):
    depth = ln_g.shape[0]
    alpha = (2.0 * depth) ** 0.25
    B, _, d = x_prompt.shape
    n_mem, xa_heads, xa_head = cache_mem_k.shape[2:]
    rp = shift_mu.shape[1]
    lru_w = conv_b.shape[1]
    hp, hs = x_prompt, x_sample
    outs = [[] for _ in range(10)]
    for l in range(depth):
        W = _prep_layer(l, ln_g, ln_b, ffn1_wi, ffn1_wo, ffn2_wi, ffn2_wo, w_in, shift_mu, decay_w0, decay_w2,
                        aaa_a0, aaa_a2, gate_g2, k_k, k_a, r_k.reshape(depth, -1), gn_g, gn_b, conv_w, conv_b,
                        lru_wr, lru_br, lru_wi, lru_bi, lru_lambda, w_mix_out, xa_wq, xa_wk, xa_wv, xa_wo)
        kv, mk, mv = _kv_proj(mem_prompt.reshape(B * n_mem, d), W['xa_wkv'], heads=xa_heads,
                              tm=_tile(B * n_mem, 512))
        mk = mk.reshape(B, n_mem, xa_heads, xa_head)
        mv = mv.reshape(B, n_mem, xa_heads, xa_head)
        hp, S1, sh1, h1, b1 = _layer(
            hp, kv, None, jnp.zeros((B, rp), F32), jnp.zeros((B, lru_w), F32),
            jnp.zeros((B, CONV_WIDTH - 1, lru_w), F32), W, alpha=alpha, xa_heads=xa_heads)
        hs, S2, sh2, h2, b2 = _layer(
            hs, (cache_mem_k[l], cache_mem_v[l]),
            state_rwkv[l], state_rwkv_shift[l], state_lru[l], state_conv[l], W, alpha=alpha, xa_heads=xa_heads)
        for lst, val in zip(outs, (mk, mv, S1, sh1, h1, b1, S2, sh2, h2, b2)):
            lst.append(val)
    return (hp, hs) + tuple(jnp.stack(o) for o in outs)
```

```python
import functools

import jax
import jax.numpy as jnp
from jax import lax
from jax.experimental import pallas as pl
from jax.experimental.pallas import tpu as pltpu

F32 = jnp.float32
BF16 = jnp.bfloat16

RWKV_HEAD = 64
DECAY_LORA = 64
AAA_LORA = 64
GATE_LORA = 128
GN_EPS = 64e-5
LRU_BLOCK = 64
CONV_WIDTH = 4
LRU_C = 8.0
LN_EPS = 1e-5

V7X_LANES = 128
V7X_SUBLANES = 8
V7X_MXU_DIM = 256
V7X_SCOPED_VMEM_BYTES = 60000 * 1024

RWKV_CHUNK = 64
HEAD_PAIR = 2 * RWKV_HEAD
SCAN_ROW_GROUPS = 2
LRU_GROUP = V7X_MXU_DIM
LRU_PROJ_PIECES = 8


def _cparams(semantics):
    return pltpu.CompilerParams(dimension_semantics=semantics, vmem_limit_bytes=V7X_SCOPED_VMEM_BYTES)


def _const_spec(shape):
    zeros = (0,) * len(shape)
    return pl.BlockSpec(shape, lambda *_: zeros)


def _dot(a, b):
    return jnp.dot(a.astype(BF16), b.astype(BF16), preferred_element_type=F32)


def _dot_dims(a, b, dims):
    return lax.dot_general(a.astype(BF16), b.astype(BF16), (dims, ((), ())), preferred_element_type=F32)


_NN = ((1,), (0,))
_NT = ((1,), (1,))
_TN = ((0,), (0,))


def _split2(x):
    hi = x.astype(BF16)
    lo = (x - hi.astype(F32)).astype(BF16)
    return hi, lo


def _dot_exact_lhs(a_bf16, b):
    hi, lo = _split2(b)
    d = lambda y: jnp.dot(a_bf16, y, preferred_element_type=F32)
    return d(hi) + d(lo)


def _dot_exact_rhs(a, b_bf16):
    hi, lo = _split2(a)
    d = lambda x: jnp.dot(x, b_bf16, preferred_element_type=F32)
    return d(hi) + d(lo)


def _layer_norm(x, g, b):
    mu = jnp.mean(x, axis=-1, keepdims=True)
    xc = x - mu
    var = jnp.mean(xc * xc, axis=-1, keepdims=True)
    return xc * lax.rsqrt(var + LN_EPS) * g + b


def _softplus(z):
    return jnp.maximum(z, 0.0) + jnp.log(1.0 + jnp.exp(-jnp.abs(z)))


def _head_ones(width):
    r = lax.broadcasted_iota(jnp.int32, (width, width), 0) // RWKV_HEAD
    c = lax.broadcasted_iota(jnp.int32, (width, width), 1) // RWKV_HEAD
    return (r == c).astype(BF16)


def _mm_kernel(x_ref, w_ref, o_ref):
    o_ref[...] = jnp.dot(x_ref[...].astype(BF16), w_ref[...], preferred_element_type=F32)


def _matmul(x, w, *, tm, tn):
    n, k = x.shape
    m = w.shape[1]
    assert n % tm == 0 and m % tn == 0
    return pl.pallas_call(
        _mm_kernel,
        grid=(m // tn, n // tm),
        in_specs=[pl.BlockSpec((tm, k), lambda j, i: (i, 0)),
                  pl.BlockSpec((k, tn), lambda j, i: (0, j))],
        out_specs=pl.BlockSpec((tm, tn), lambda j, i: (i, j)),
        out_shape=jax.ShapeDtypeStruct((n, m), F32),
        compiler_params=_cparams(("parallel", "parallel")),
        name="matmul",
    )(x, w)


def _ffn_kernel(x_ref, wg_ref, wu_ref, wo_ref, g_ref, b_ref, o_ref, *, alpha, row_groups):
    tm = x_ref.shape[0]
    tg = tm // row_groups
    for gi in range(row_groups):
        rows = slice(gi * tg, (gi + 1) * tg)
        x = x_ref[rows, :]
        xb = x.astype(BF16)
        gate = jnp.dot(xb, wg_ref[...], preferred_element_type=F32)
        up = jnp.dot(xb, wu_ref[...], preferred_element_type=F32)
        mid = (gate * jax.nn.sigmoid(gate) * up).astype(BF16)
        down = jnp.dot(mid, wo_ref[...], preferred_element_type=F32)
        o_ref[rows, :] = _layer_norm(alpha * x + 0.5 * down, g_ref[...], b_ref[...])


def _ffn_ln(x, wg, wu, wo, g, b, *, alpha, tm):
    n, d = x.shape
    assert n % tm == 0
    row_groups = 2 if tm % (2 * V7X_MXU_DIM) == 0 else 1
    return pl.pallas_call(
        functools.partial(_ffn_kernel, alpha=alpha, row_groups=row_groups),
        grid=(n // tm,),
        in_specs=[pl.BlockSpec((tm, d), lambda i: (i, 0)),
                  _const_spec(wg.shape), _const_spec(wu.shape), _const_spec(wo.shape),
                  _const_spec(g.shape), _const_spec(b.shape)],
        out_specs=pl.BlockSpec((tm, d), lambda i: (i, 0)),
        out_shape=jax.ShapeDtypeStruct((n, d), F32),
        compiler_params=_cparams(("parallel",)),
        name="ffn_ln",
    )(x, wg, wu, wo, g, b)


def _rwkv_pre(r, k, v, xx, w0, w2, a0, a2, g2, k_k, k_a, r_k, ones):
    xw = xx[:, :DECAY_LORA]
    xa = xx[:, DECAY_LORA:DECAY_LORA + AAA_LORA]
    xg = xx[:, DECAY_LORA + AAA_LORA:]
    z = w0 + _dot(jnp.tanh(xw), w2)
    lw = -jnp.exp(-_softplus(-z) - 0.5)
    a = jax.nn.sigmoid(a0 + _dot(xa, a2))
    g = _dot(jax.nn.sigmoid(xg), g2)
    kkr = k * k_k
    ss = _dot(kkr * kkr, ones)
    kk = kkr * lax.rsqrt(jnp.maximum(ss, 1e-24))
    kf = k * (1.0 + (a - 1.0) * k_a)
    bonus = _dot(r * kf * r_k, ones) * v
    return lw, a, g, kk, kf, bonus


def _rwkv_post(y, bonus, g, gn_g, gn_b, ones):
    inv_n = 1.0 / RWKV_HEAD
    ym = _dot_exact_rhs(y, ones) * inv_n
    yc = y - ym
    yv = _dot(yc * yc, ones) * inv_n
    yn = yc * lax.rsqrt(yv + GN_EPS) * gn_g + gn_b
    return (yn + bonus) * g


def _bdot(a, b, dims):
    dn = ((tuple(d + 1 for d in dims[0]), tuple(d + 1 for d in dims[1])), ((0,), (0,)))
    return lax.dot_general(a.astype(BF16), b.astype(BF16), dn, preferred_element_type=F32)


def _scan_operands(r, kf, v, kk, a, lw):
    tt, hw = r.shape
    C = RWKV_CHUNK
    n_pairs = hw // HEAD_PAIR
    ltri = (lax.broadcasted_iota(jnp.int32, (C, C), 0) >= lax.broadcasted_iota(jnp.int32, (C, C), 1)).astype(BF16)
    first = lax.broadcasted_iota(jnp.int32, (C, HEAD_PAIR), 1) < RWKV_HEAD

    def bd(x):
        return jnp.concatenate([jnp.where(first, x, 0.0), jnp.where(first, 0.0, x)], axis=1)

    names = ('a', 'r', 'b', 'k', 'v', 'bh', 'kh')
    ops = {n: [] for n in names}
    wcs = []
    for c in range(tt // C):
        rows = slice(c * C, (c + 1) * C)
        lw_c = lw[rows]
        L = _dot_exact_lhs(ltri, lw_c)
        Lc = L[C - 1:C, :]
        e_nl = jnp.exp(-L)
        e_c = jnp.exp(Lc - L)
        bb = kk[rows] * a[rows]
        bf = lambda x: x.astype(BF16)
        tile = dict(a=bf(-kk[rows] * jnp.exp(L - lw_c)), r=r[rows] * jnp.exp(L), b=bf(bb * e_nl),
                    k=bf(kf[rows] * e_nl), v=bf(v[rows]), bh=bf(bb * e_c), kh=bf(kf[rows] * e_c))
        wc = jnp.exp(Lc)
        for p in range(n_pairs):
            lanes = slice(p * HEAD_PAIR, (p + 1) * HEAD_PAIR)
            for n in names:
                ops[n].append(tile[n][:, lanes])
            wcs.append(wc[:, lanes])
    A, R, B, K, V, Bh, Kh = (jnp.stack(ops[n]) for n in names)
    Vbd = bd(V)
    G = _bdot(jnp.concatenate([A, R.astype(BF16)], axis=1), jnp.concatenate([bd(B), bd(K)], axis=1), _NT)
    tok = lax.broadcasted_iota(jnp.int32, (C, HEAD_PAIR), 0)
    src = lax.broadcasted_iota(jnp.int32, (C, HEAD_PAIR), 1) % RWKV_HEAD
    a_ab = jnp.where(tok > src, G[:, :C, :HEAD_PAIR], 0.0)
    a_ak = jnp.where(tok > src, G[:, :C, HEAD_PAIR:], 0.0).astype(BF16)
    a_rb = jnp.where(tok >= src, G[:, C:, :HEAD_PAIR], 0.0).astype(BF16)
    a_rk = jnp.where(tok >= src, G[:, C:, HEAD_PAIR:], 0.0).astype(BF16)
    P = jnp.where(tok == src, 1.0, 0.0) + a_ab
    N = a_ab.astype(BF16)
    N = _bdot(N, bd(N), _NN).astype(BF16)
    steps = 2
    while 2 * steps < C:
        NP = _bdot(jnp.concatenate([N, P.astype(BF16)], axis=1), bd(N), _NN)
        N = NP[:, :C].astype(BF16)
        P = P + NP[:, C:]
        steps *= 2
    P = (P + _bdot(P, bd(N), _NN)).astype(BF16)
    aV = _bdot(a_ak, Vbd, _NN).astype(BF16)
    XU = _bdot(P, jnp.concatenate([bd(A), bd(aV)], axis=2), _NN).astype(BF16)
    X1 = XU[:, :, :HEAD_PAIR]
    Uloc = XU[:, :, HEAD_PAIR:]
    Q = (R + _bdot(a_rb, bd(X1), _NN)).astype(BF16)
    Yloc = _bdot(jnp.concatenate([a_rb, a_rk], axis=2), jnp.concatenate([bd(Uloc), Vbd], axis=1), _NN)
    ri = lax.broadcasted_iota(jnp.int32, (HEAD_PAIR, HEAD_PAIR), 0) // RWKV_HEAD
    ci = lax.broadcasted_iota(jnp.int32, (HEAD_PAIR, HEAD_PAIR), 1) // RWKV_HEAD
    same_head = ri == ci
    Pm = jnp.where(same_head, _bdot(X1, Bh, _TN), 0.0).astype(BF16)
    Sloc = jnp.where(same_head, _bdot(jnp.concatenate([Uloc, V], axis=1), jnp.concatenate([Bh, Kh], axis=1), _TN),
                     0.0)
    return Q, Yloc, Pm, Sloc, jnp.stack(wcs)


def _rwkv_chunk_kernel(pr_ref, pk_ref, pv_ref, px_ref, sr_ref, sk_ref, sv_ref, sx_ref,
                       mur_ref, muk_ref, muv_ref, mux_ref, w0_ref, w2_ref, a0_ref, a2_ref, g2_ref,
                       kk_ref, ka_ref, rk_ref, gng_ref, gnb_ref, ones_ref,
                       out_ref, s_out_ref,
                       s_scr, cr_scr, ck_scr, cv_scr, cx_scr, y_scr):
    ti = pl.program_id(2)
    tt, hw = pr_ref.shape
    n_pairs = hw // HEAD_PAIR

    @pl.when(ti == 0)
    def _init():
        s_scr[...] = jnp.zeros(s_scr.shape, F32)
        cr_scr[...] = sr_ref[...]
        ck_scr[...] = sk_ref[...]
        cv_scr[...] = sv_ref[...]
        cx_scr[...] = sx_ref[...]

    S8 = V7X_SUBLANES

    def shifted(p_ref, c_scr, mu_ref):
        p = p_ref[...]
        first = lax.broadcasted_iota(jnp.int32, (S8, p.shape[1]), 0) == 0
        head = jnp.where(first, c_scr[...], pltpu.roll(p[:S8], 1, 0))
        prev = jnp.concatenate([head, p_ref[S8 - 1:tt - 1, :]], axis=0)
        c_scr[...] = p_ref[tt - 1:tt, :]
        return p + (prev - p) * mu_ref[...]

    r = shifted(pr_ref, cr_scr, mur_ref)
    k = shifted(pk_ref, ck_scr, muk_ref)
    v = shifted(pv_ref, cv_scr, muv_ref)
    xx = shifted(px_ref, cx_scr, mux_ref)
    ones = ones_ref[...]
    lw, a, g, kk, kf, bonus = _rwkv_pre(r, k, v, xx, w0_ref[...], w2_ref[...], a0_ref[...], a2_ref[...],
                                        g2_ref[...], kk_ref[...], ka_ref[...], rk_ref[...], ones)
    C = RWKV_CHUNK
    S = s_scr[...]
    n_groups = SCAN_ROW_GROUPS if tt % (SCAN_ROW_GROUPS * C) == 0 else 1
    tg = tt // n_groups
    for gi in range(n_groups):
        rows = slice(gi * tg, (gi + 1) * tg)
        Q, Yloc, Pm, Sloc, wc = _scan_operands(r[rows], kf[rows], v[rows], kk[rows], a[rows], lw[rows])
        for c in range(tg // C):
            inst = slice(c * n_pairs, (c + 1) * n_pairs)
            y_c = _bdot(Q[inst], S, _NT) + Yloc[inst]
            row0 = gi * tg + c * C
            for p in range(n_pairs):
                y_scr[row0:row0 + C, p * HEAD_PAIR:(p + 1) * HEAD_PAIR] = y_c[p]
            S = S * wc[inst] + _bdot(S, Pm[inst], _NN) + Sloc[inst]
    s_scr[...] = S
    out_ref[...] = _rwkv_post(y_scr[...], bonus, g, gng_ref[...], gnb_ref[...], ones)

    @pl.when(ti == pl.num_programs(2) - 1)
    def _emit_state():
        for p in range(n_pairs):
            s_out_ref[2 * p] = S[p, :RWKV_HEAD, :RWKV_HEAD]
            s_out_ref[2 * p + 1] = S[p, RWKV_HEAD:, RWKV_HEAD:]


def _rwkv_chunked(pa, shift0, W, *, tt, hw):
    B, T, _ = pa.shape
    width = W['decay_w0'].shape[1]
    heads = width // RWKV_HEAD
    assert T % tt == 0 and tt % RWKV_CHUNK == 0 and width % hw == 0 and hw % HEAD_PAIR == 0
    nb = width // hw
    xw = DECAY_LORA + AAA_LORA + GATE_LORA
    assert (3 * width) % xw == 0
    xblk = 3 * width // xw
    col = lambda off: (lambda b, h, t: (b, t, off + h))
    vec = lambda: pl.BlockSpec((1, hw), lambda b, h, t: (0, h))
    in_specs = [
        pl.BlockSpec((None, tt, hw), col(0)), pl.BlockSpec((None, tt, hw), col(nb)),
        pl.BlockSpec((None, tt, hw), col(2 * nb)), pl.BlockSpec((None, tt, xw), lambda b, h, t: (b, t, xblk)),
        pl.BlockSpec((None, 1, hw), lambda b, h, t: (b, 0, h)), pl.BlockSpec((None, 1, hw), lambda b, h, t: (b, 0, nb + h)),
        pl.BlockSpec((None, 1, hw), lambda b, h, t: (b, 0, 2 * nb + h)), pl.BlockSpec((None, 1, xw), lambda b, h, t: (b, 0, xblk)),
        vec(), vec(), vec(), _const_spec((1, xw)),
        vec(), pl.BlockSpec((DECAY_LORA, hw), lambda b, h, t: (0, h)),
        vec(), pl.BlockSpec((AAA_LORA, hw), lambda b, h, t: (0, h)),
        pl.BlockSpec((GATE_LORA, hw), lambda b, h, t: (0, h)),
        vec(), vec(), vec(), vec(), vec(), _const_spec((hw, hw)),
    ]
    out, s_new = pl.pallas_call(
        _rwkv_chunk_kernel,
        grid=(B, nb, T // tt),
        in_specs=in_specs,
        out_specs=[pl.BlockSpec((None, tt, hw), lambda b, h, t: (b, t, h)),
                   pl.BlockSpec((None, hw // RWKV_HEAD, RWKV_HEAD, RWKV_HEAD), lambda b, h, t: (b, h, 0, 0))],
        out_shape=[jax.ShapeDtypeStruct((B, T, width), F32),
                   jax.ShapeDtypeStruct((B, heads, RWKV_HEAD, RWKV_HEAD), F32)],
        scratch_shapes=[pltpu.VMEM((hw // HEAD_PAIR, HEAD_PAIR, HEAD_PAIR), F32),
                        pltpu.VMEM((1, hw), F32), pltpu.VMEM((1, hw), F32), pltpu.VMEM((1, hw), F32),
                        pltpu.VMEM((1, xw), F32), pltpu.VMEM((tt, hw), F32)],
        compiler_params=_cparams(("parallel", "parallel", "arbitrary")),
        name="rwkv_chunked",
    )(pa, pa, pa, pa, shift0, shift0, shift0, shift0,
      W['mu_r'], W['mu_k'], W['mu_v'], W['mu_x'], W['decay_w0'], W['decay_w2'], W['aaa_a0'], W['aaa_a2'],
      W['gate_g2'], W['k_k'], W['k_a'], W['r_k'], W['gn_g'], W['gn_b'], _head_ones(hw))
    return out, s_new


def _rwkv_step_kernel(p_ref, s0_ref, st_ref, mu_ref, w0_ref, w2_ref, a0_ref, a2_ref, g2_ref,
                      kk_ref, ka_ref, rk_ref, gng_ref, gnb_ref, ones_ref,
                      out_ref, st_out_ref, vec_scr, y_scr, bonus_scr, g_scr):
    i = pl.program_id(0)
    B, width = out_ref.shape
    hg = st_ref.shape[0]
    H = RWKV_HEAD

    @pl.when(i == 0)
    def _prologue():
        ones = ones_ref[...]
        p = p_ref[...]
        xs = p + (s0_ref[...] - p) * mu_ref[...]
        r, k, v, xx = xs[:, :width], xs[:, width:2 * width], xs[:, 2 * width:3 * width], xs[:, 3 * width:]
        lw, a, g, kk, kf, bonus = _rwkv_pre(r, k, v, xx, w0_ref[...], w2_ref[...], a0_ref[...], a2_ref[...],
                                            g2_ref[...], kk_ref[...], ka_ref[...], rk_ref[...], ones)
        for j, vec in enumerate((r, kf, v, kk, kk * a, jnp.exp(lw))):
            vec_scr[j] = vec.T
        bonus_scr[...] = bonus
        g_scr[...] = g

    for hl in range(hg):
        base = pl.multiple_of((i * hg + hl) * H, H)
        r_h, kf_h, kk_h, kka_h, w_h = (vec_scr[j, pl.ds(base, H), :] for j in (0, 1, 3, 4, 5))

        def body(vi, carry):
            S = st_ref[hl, vi]
            sa = jnp.sum(S * kk_h, axis=0, keepdims=True)
            v_row = vec_scr[2, pl.ds(base + vi, 1), :]
            S2 = S * w_h - sa * kka_h + v_row * kf_h
            st_out_ref[hl, vi] = S2
            y_scr[pl.ds(base + vi, 1), :] = jnp.sum(S2 * r_h, axis=0, keepdims=True)
            return carry

        lax.fori_loop(0, H, body, 0, unroll=8)

    @pl.when(i == pl.num_programs(0) - 1)
    def _epilogue():
        out_ref[...] = _rwkv_post(y_scr[...].T, bonus_scr[...], g_scr[...], gng_ref[...], gnb_ref[...],
                                  ones_ref[...])


def _rwkv_step(pa, shift0, state_t, W, *, hg):
    B, proj = pa.shape
    width = W['decay_w0'].shape[1]
    heads = width // RWKV_HEAD
    assert heads % hg == 0 and state_t.shape == (heads, RWKV_HEAD, RWKV_HEAD, B)
    st_spec = pl.BlockSpec((hg, RWKV_HEAD, RWKV_HEAD, B), lambda i: (i, 0, 0, 0))
    names = ('decay_w0', 'decay_w2', 'aaa_a0', 'aaa_a2', 'gate_g2', 'k_k', 'k_a', 'r_k', 'gn_g', 'gn_b')
    out, st_new = pl.pallas_call(
        _rwkv_step_kernel,
        grid=(heads // hg,),
        in_specs=[_const_spec((B, proj)), _const_spec((B, proj)), st_spec, _const_spec(W['shift_mu'].shape)]
                 + [_const_spec(W[n].shape) for n in names] + [_const_spec((width, width))],
        out_specs=[_const_spec((B, width)), st_spec],
        out_shape=[jax.ShapeDtypeStruct((B, width), F32), jax.ShapeDtypeStruct(state_t.shape, F32)],
        scratch_shapes=[pltpu.VMEM((6, width, B), F32), pltpu.VMEM((width, B), F32),
                        pltpu.VMEM((B, width), F32), pltpu.VMEM((B, width), F32)],
        compiler_params=_cparams(("arbitrary",)),
        name="rwkv_step",
    )(pa, shift0, state_t, W['shift_mu'], *[W[n] for n in names], _head_ones(width))
    return out, st_new


def _lru_coeffs(xc, wr_ref, br_ref, wi_ref, bi_ref, lam_ref):
    xcb = xc.astype(BF16)
    n_grp = xc.shape[1] // LRU_GROUP
    grp = lambda w_ref: jnp.concatenate(
        [jnp.dot(xcb[:, i * LRU_GROUP:(i + 1) * LRU_GROUP], w_ref[i], preferred_element_type=F32)
         for i in range(n_grp)], axis=1)
    gr = jax.nn.sigmoid(grp(wr_ref) + br_ref[...])
    gi = jax.nn.sigmoid(grp(wi_ref) + bi_ref[...])
    a_t = jnp.exp(-LRU_C * gr * _softplus(-lam_ref[...]))
    return a_t, jnp.sqrt(1.0 - a_t * a_t) * gi * xc


def _lru_finish(hs, pg, g0, g1, rw, h, wmix_ref, lng_ref, lnb_ref, alpha):
    lru_out = hs * jax.nn.gelu(pg)
    merged = jax.nn.sigmoid(g0) * rw + jax.nn.sigmoid(g1) * lru_out
    mix = jnp.dot(merged.astype(BF16), wmix_ref[...], preferred_element_type=F32)
    return _layer_norm(alpha * h + mix, lng_ref[...], lnb_ref[...])


def _lru_kernel(hn_ref, rw_ref, h_ref, buf_ref, h0_ref, win_ref,
                cw_ref, cb_ref, wr_ref, br_ref, wi_ref, bi_ref, lam_ref, wmix_ref, lng_ref, lnb_ref,
                out_ref, hlast_ref, tail_ref, proj_scr, tail_scr, hc_scr, *, alpha):
    ti = pl.program_id(1)
    flat = pl.program_id(0) * pl.num_programs(1) + ti
    tt, width = rw_ref.shape
    S8 = V7X_SUBLANES
    project = lambda ref: jnp.dot(ref[...].astype(BF16), win_ref[...], preferred_element_type=F32)
    col = lambda j: slice(j * width, (j + 1) * width)

    @pl.when(ti == 0)
    def _init():
        tail_scr[...] = buf_ref[...]
        hc_scr[...] = h0_ref[...]

    @pl.when(flat == 0)
    def _first_projection():
        proj_scr[0] = project(h_ref)

    def step(cur_scr, nxt_scr):
        hn = hn_ref[...].astype(BF16)
        piece_w = win_ref.shape[1] // LRU_PROJ_PIECES
        pieces = iter(range(LRU_PROJ_PIECES))

        def emit(count=1):
            for _ in range(count):
                k = next(pieces, None)
                if k is not None:
                    cols = slice(k * piece_w, (k + 1) * piece_w)
                    nxt_scr[:, cols] = jnp.dot(hn, win_ref[:, cols], preferred_element_type=F32)

        x = cur_scr[:, col(0)]
        tail = tail_scr[...]
        r8 = lax.broadcasted_iota(jnp.int32, (S8, width), 0)

        def delayed(d):
            head = jnp.where(r8 < d, pltpu.roll(tail, d, 0), pltpu.roll(x[:S8], d, 0))
            return jnp.concatenate([head, cur_scr[S8 - d:tt - d, col(0)]], axis=0)

        cw = cw_ref[...]
        conv = cw[0:1] * delayed(CONV_WIDTH - 1)
        for j in range(1, CONV_WIDTH - 1):
            conv = conv + cw[j:j + 1] * delayed(CONV_WIDTH - 1 - j)
        xc = cb_ref[...] + (conv + cw[CONV_WIDTH - 1:CONV_WIDTH] * x)
        tail_scr[...] = x[tt - S8:, :]
        tail_ref[...] = x[tt - S8:, :]
        emit()

        A, Bv = _lru_coeffs(xc, wr_ref, br_ref, wi_ref, bi_ref, lam_ref)
        emit()
        A = A.reshape(tt // S8, S8, width)
        Bv = Bv.reshape(tt // S8, S8, width)
        in_grp = lax.broadcasted_iota(jnp.int32, (S8, width), 0)
        s = 1
        while s < S8:
            keep = in_grp >= s
            Bv = Bv + A * jnp.where(keep, pltpu.roll(Bv, s, 1), 0.0)
            A = A * jnp.where(keep, pltpu.roll(A, s, 1), 1.0)
            s *= 2
            emit()
        carry = hc_scr[...]
        groups = []
        for gi in range(tt // S8):
            hg = Bv[gi] + A[gi] * carry
            groups.append(hg)
            carry = hg[S8 - 1:S8, :]
        hs = jnp.concatenate(groups, axis=0)
        hc_scr[...] = carry
        hlast_ref[...] = carry
        emit()
        lru_out = hs * jax.nn.gelu(cur_scr[:, col(1)])
        emit()
        merged = jax.nn.sigmoid(cur_scr[:, col(2)]) * rw_ref[...] + jax.nn.sigmoid(cur_scr[:, col(3)]) * lru_out
        emit()
        mix = jnp.dot(merged.astype(BF16), wmix_ref[...], preferred_element_type=F32)
        out_ref[...] = _layer_norm(alpha * h_ref[...] + mix, lng_ref[...], lnb_ref[...])
        emit(LRU_PROJ_PIECES)

    slot = flat % 2
    step(proj_scr.at[slot], proj_scr.at[1 - slot])


def _lru_step_kernel(x_ref, pg_ref, g0_ref, g1_ref, rw_ref, h_ref, buf_ref, h0_ref,
                     cw_ref, cb_ref, wr_ref, br_ref, wi_ref, bi_ref, lam_ref, wmix_ref, lng_ref, lnb_ref,
                     out_ref, hnew_ref, *, alpha):
    cw = cw_ref[...]
    conv = cw[0:1] * buf_ref[0]
    for j in range(1, CONV_WIDTH - 1):
        conv = conv + cw[j:j + 1] * buf_ref[j]
    xc = cb_ref[...] + (conv + cw[CONV_WIDTH - 1:CONV_WIDTH] * x_ref[...])
    A, Bv = _lru_coeffs(xc, wr_ref, br_ref, wi_ref, bi_ref, lam_ref)
    hs = Bv + A * h0_ref[...]
    hnew_ref[...] = hs
    out_ref[...] = _lru_finish(hs, pg_ref[...], g0_ref[...], g1_ref[...], rw_ref[...], h_ref[...],
                               wmix_ref, lng_ref, lnb_ref, alpha)


_LRU_WEIGHTS = ('conv_w', 'conv_b', 'lru_wr', 'lru_br', 'lru_wi', 'lru_bi', 'lru_lambda', 'w_mix_out')


def _lru_step_mix_ln(pb, rw, h, buf, h0, W, ln_g, ln_b, *, alpha, tm):
    B, width = rw.shape
    assert B % tm == 0
    blk = lambda j: pl.BlockSpec((tm, width), lambda i: (i, j))
    out, h_new = pl.pallas_call(
        functools.partial(_lru_step_kernel, alpha=alpha),
        grid=(B // tm,),
        in_specs=[blk(0), blk(1), blk(2), blk(3), blk(0), blk(0),
                  pl.BlockSpec((CONV_WIDTH - 1, tm, width), lambda i: (0, i, 0)), blk(0)]
                 + [_const_spec(W[n].shape) for n in _LRU_WEIGHTS] + [_const_spec(ln_g.shape), _const_spec(ln_b.shape)],
        out_specs=[blk(0), blk(0)],
        out_shape=[jax.ShapeDtypeStruct((B, width), F32), jax.ShapeDtypeStruct((B, width), F32)],
        compiler_params=_cparams(("parallel",)),
        name="lru_step_mix_ln",
    )(pb, pb, pb, pb, rw, h, buf, h0, *[W[n] for n in _LRU_WEIGHTS], ln_g, ln_b)
    return out, h_new


def _lru_mix_ln(h, rw, buf8, h0, W, ln_g, ln_b, *, alpha, tt):
    B, T, width = rw.shape
    assert T % tt == 0 and tt % V7X_SUBLANES == 0 and h.shape[2] == width
    nt = T // tt
    tile = pl.BlockSpec((None, tt, width), lambda b, t: (b, t, 0))

    def next_tile(b, t):
        f = jnp.minimum(b * nt + t + 1, B * nt - 1)
        return f // nt, f % nt, 0

    nxt = pl.BlockSpec((None, tt, width), next_tile)
    per_b = lambda rows: pl.BlockSpec((None, rows, width), lambda b, t: (b, 0, 0))
    win = W['w_in_b']
    out, h_last, tail = pl.pallas_call(
        functools.partial(_lru_kernel, alpha=alpha),
        grid=(B, nt),
        in_specs=[nxt, tile, tile, per_b(V7X_SUBLANES), per_b(1), _const_spec(win.shape)]
                 + [_const_spec(W[n].shape) for n in _LRU_WEIGHTS] + [_const_spec(ln_g.shape), _const_spec(ln_b.shape)],
        out_specs=[tile, per_b(1), per_b(V7X_SUBLANES)],
        out_shape=[jax.ShapeDtypeStruct((B, T, width), F32), jax.ShapeDtypeStruct((B, 1, width), F32),
                   jax.ShapeDtypeStruct((B, V7X_SUBLANES, width), F32)],
        scratch_shapes=[pltpu.VMEM((2, tt, win.shape[1]), F32), pltpu.VMEM((V7X_SUBLANES, width), F32),
                        pltpu.VMEM((1, width), F32)],
        compiler_params=_cparams(("arbitrary", "arbitrary")),
        name="lru_mix_ln",
    )(h, rw, h, buf8, h0, win, *[W[n] for n in _LRU_WEIGHTS], ln_g, ln_b)
    return out, h_last, tail


def _xattn_kernel(h_ref, kv_ref, wq_ref, wo_ref, g_ref, b_ref, o_ref, *, alpha, heads):
    tm, d = h_ref.shape
    hd = d // heads
    row_groups = 2 if tm % (2 * V7X_MXU_DIM) == 0 else 1
    tg = tm // row_groups
    for gi in range(row_groups):
        rows = slice(gi * tg, (gi + 1) * tg)
        h = h_ref[rows, :]
        q = jnp.dot(h.astype(BF16), wq_ref[...], preferred_element_type=F32)
        outs = []
        for j in range(heads):
            sl = slice(j * hd, (j + 1) * hd)
            s = _dot_dims(q[:, sl], kv_ref[:, sl], _NT) * (hd ** -0.5)
            e = jnp.exp(s - jnp.max(s, axis=-1, keepdims=True))
            p = e * (1.0 / jnp.sum(e, axis=-1, keepdims=True))
            outs.append(_dot(p, kv_ref[:, d + j * hd:d + (j + 1) * hd]))
        out = jnp.dot(jnp.concatenate(outs, axis=1).astype(BF16), wo_ref[...], preferred_element_type=F32)
        o_ref[rows, :] = _layer_norm(alpha * h + out, g_ref[...], b_ref[...])


def _xattn_rows_kernel(h_ref, mk_ref, mv_ref, wq_ref, wo_ref, g_ref, b_ref, o_ref, *, alpha):
    h = h_ref[...]
    nb, m, heads, hd = mk_ref.shape
    q = jnp.dot(h.astype(BF16), wq_ref[...], preferred_element_type=F32)
    col = lax.broadcasted_iota(jnp.int32, (heads, m * heads), 1)
    own = (col % heads) == lax.broadcasted_iota(jnp.int32, (heads, m * heads), 0)
    rows = []
    for i in range(nb):
        k2 = mk_ref[i].reshape(m * heads, hd)
        v2 = mv_ref[i].reshape(m * heads, hd)
        q4 = jnp.concatenate([q[i:i + 1, j * hd:(j + 1) * hd] for j in range(heads)], axis=0)
        s = jnp.where(own, _dot_dims(q4, k2, _NT) * (hd ** -0.5), -jnp.inf)
        e = jnp.exp(s - jnp.max(s, axis=-1, keepdims=True))
        p = e / jnp.sum(e, axis=-1, keepdims=True)
        o4 = _dot(p, v2)
        rows.append(jnp.concatenate([o4[j:j + 1] for j in range(heads)], axis=1))
    out = jnp.dot(jnp.concatenate(rows, axis=0).astype(BF16), wo_ref[...], preferred_element_type=F32)
    o_ref[...] = _layer_norm(alpha * h + out, g_ref[...], b_ref[...])


def _xattn_ln(h, kv, wq, wo, g, b, *, alpha, heads, rows_per_batch, tile_rows):
    n, d = h.shape
    assert n % tile_rows == 0 and rows_per_batch % tile_rows == 0
    per = rows_per_batch // tile_rows
    m = kv.shape[0] // (n // rows_per_batch)
    return pl.pallas_call(
        functools.partial(_xattn_kernel, alpha=alpha, heads=heads),
        grid=(n // tile_rows,),
        in_specs=[pl.BlockSpec((tile_rows, d), lambda i: (i, 0)),
                  pl.BlockSpec((m, 2 * d), lambda i: (i // per, 0)),
                  _const_spec(wq.shape), _const_spec(wo.shape), _const_spec(g.shape), _const_spec(b.shape)],
        out_specs=pl.BlockSpec((tile_rows, d), lambda i: (i, 0)),
        out_shape=jax.ShapeDtypeStruct((n, d), F32),
        compiler_params=_cparams(("parallel",)),
        name="xattn_ln",
    )(h, kv, wq, wo, g, b)


def _xattn_rows_ln(h, mk, mv, wq, wo, g, b, *, alpha, nb):
    n, d = h.shape
    _, m, heads, hd = mk.shape
    assert n % nb == 0
    kv_spec = pl.BlockSpec((nb, m, heads, hd), lambda i: (i, 0, 0, 0))
    return pl.pallas_call(
        functools.partial(_xattn_rows_kernel, alpha=alpha),
        grid=(n // nb,),
        in_specs=[pl.BlockSpec((nb, d), lambda i: (i, 0)), kv_spec, kv_spec,
                  _const_spec(wq.shape), _const_spec(wo.shape), _const_spec(g.shape), _const_spec(b.shape)],
        out_specs=pl.BlockSpec((nb, d), lambda i: (i, 0)),
        out_shape=jax.ShapeDtypeStruct((n, d), F32),
        compiler_params=_cparams(("parallel",)),
        name="xattn_rows_ln",
    )(h, mk, mv, wq, wo, g, b)


def _kv_proj_kernel(x_ref, w_ref, kv_ref, k_ref, v_ref):
    heads, hd = k_ref.shape[1:]
    d = heads * hd
    kv = jnp.dot(x_ref[...].astype(BF16), w_ref[...], preferred_element_type=F32)
    kv_ref[...] = kv
    for j in range(heads):
        k_ref[:, j, :] = kv[:, j * hd:(j + 1) * hd]
        v_ref[:, j, :] = kv[:, d + j * hd:d + (j + 1) * hd]


def _kv_proj(x, wkv, *, heads, tm):
    n, d = x.shape
    assert n % tm == 0
    hd = d // heads
    out = jax.ShapeDtypeStruct((n, heads, hd), F32)
    return pl.pallas_call(
        _kv_proj_kernel,
        grid=(n // tm,),
        in_specs=[pl.BlockSpec((tm, d), lambda i: (i, 0)), _const_spec(wkv.shape)],
        out_specs=[pl.BlockSpec((tm, 2 * d), lambda i: (i, 0))] + [pl.BlockSpec((tm, heads, hd), lambda i: (i, 0, 0))] * 2,
        out_shape=[jax.ShapeDtypeStruct((n, 2 * d), F32), out, out],
        compiler_params=_cparams(("parallel",)),
        name="kv_proj",
    )(x, wkv)


def _row(v):
    return v.reshape(1, -1)


def _block_diag_groups(w):
    n, c, _ = w.shape
    per = LRU_GROUP // c
    w4 = w.reshape(n // per, per, c, c)
    bd = jnp.einsum('gjcd,jk->gjckd', w4, jnp.eye(per, dtype=w.dtype))
    return bd.reshape(n // per, LRU_GROUP, LRU_GROUP).astype(BF16)


def _prep_layer(l, ln_g, ln_b, ffn1_wi, ffn1_wo, ffn2_wi, ffn2_wo, w_in, shift_mu, decay_w0, decay_w2,
                aaa_a0, aaa_a2, gate_g2, k_k, k_a, r_k, gn_g, gn_b, conv_w, conv_b, lru_wr, lru_br,
                lru_wi, lru_bi, lru_lambda, w_mix_out, xa_wq, xa_wk, xa_wv, xa_wo):
    width = decay_w0.shape[1]
    rp = shift_mu.shape[1]
    d_ff = ffn1_wo.shape[1]
    bf = lambda w: w.astype(BF16)
    mu = shift_mu[l]
    return dict(
        ln_g=[_row(ln_g[l, i]) for i in range(4)], ln_b=[_row(ln_b[l, i]) for i in range(4)],
        ffn1=(bf(ffn1_wi[l][:, :d_ff]), bf(ffn1_wi[l][:, d_ff:]), bf(ffn1_wo[l])),
        ffn2=(bf(ffn2_wi[l][:, :d_ff]), bf(ffn2_wi[l][:, d_ff:]), bf(ffn2_wo[l])),
        w_in_a=bf(w_in[l][:, :rp]), w_in_b=bf(w_in[l][:, rp:]),
        shift_mu=_row(mu), mu_r=_row(mu[:width]), mu_k=_row(mu[width:2 * width]),
        mu_v=_row(mu[2 * width:3 * width]), mu_x=_row(mu[3 * width:]),
        decay_w0=_row(decay_w0[l]), decay_w2=bf(decay_w2[l]), aaa_a0=_row(aaa_a0[l]), aaa_a2=bf(aaa_a2[l]),
        gate_g2=bf(gate_g2[l]), k_k=_row(k_k[l]), k_a=_row(k_a[l]), r_k=_row(r_k[l]),
        gn_g=_row(gn_g[l]), gn_b=_row(gn_b[l]),
        conv_w=conv_w[l], conv_b=_row(conv_b[l]),
        lru_wr=_block_diag_groups(lru_wr[l]), lru_br=_row(lru_br[l]),
        lru_wi=_block_diag_groups(lru_wi[l]), lru_bi=_row(lru_bi[l]), lru_lambda=_row(lru_lambda[l]),
        w_mix_out=bf(w_mix_out[l]), xa_wq=bf(xa_wq[l]), xa_wo=bf(xa_wo[l]),
        xa_wkv=bf(jnp.concatenate([xa_wk[l], xa_wv[l]], axis=1)),
    )


def _tile(n, pref):
    return pref if n % pref == 0 else n


def _layer(h, mem, state, shift0, h0, buf0, W, *, alpha, xa_heads):
    B, T, d = h.shape
    n = B * T
    tm = _tile(n, 512)
    h1 = _ffn_ln(h.reshape(n, d), *W['ffn1'], W['ln_g'][0], W['ln_b'][0], alpha=alpha, tm=tm)
    pa = _matmul(h1, W['w_in_a'], tm=tm, tn=W['w_in_a'].shape[1])
    width = W['decay_w0'].shape[1]
    lru_w = W['conv_b'].shape[1]
    pa3 = pa.reshape(B, T, -1)
    hist = CONV_WIDTH - 1
    if T > 1:
        assert state is None and T >= V7X_SUBLANES
        rw, s_new = _rwkv_chunked(pa3, shift0.reshape(B, 1, -1), W, tt=_tile(T, 512), hw=_tile(width, 512))
        buf8 = jnp.concatenate([jnp.zeros((B, V7X_SUBLANES - hist, lru_w), F32), buf0], axis=1)
        h2, h_last, tail8 = _lru_mix_ln(h1.reshape(B, T, d), rw, buf8, h0.reshape(B, 1, lru_w), W,
                                        W['ln_g'][1], W['ln_b'][1], alpha=alpha, tt=_tile(T, 256))
        conv_in_tail = tail8[:, V7X_SUBLANES - hist:]
    else:
        pb = _matmul(h1, W['w_in_b'], tm=tm, tn=W['w_in_b'].shape[1])
        rw, s_new = _rwkv_step(pa, shift0, jnp.transpose(state, (1, 2, 3, 0)), W, hg=2)
        s_new = jnp.transpose(s_new, (3, 0, 1, 2))
        h2, h_last = _lru_step_mix_ln(pb, rw, h1, jnp.swapaxes(buf0, 0, 1), h0, W,
                                      W['ln_g'][1], W['ln_b'][1], alpha=alpha, tm=_tile(B, 128))
        conv_in_tail = jnp.concatenate([buf0[:, T:], pb[:, None, :lru_w]], axis=1)
    xa = (W['xa_wq'], W['xa_wo'], W['ln_g'][2], W['ln_b'][2])
    if T > 1:
        h3 = _xattn_ln(h2.reshape(n, d), mem, *xa, alpha=alpha, heads=xa_heads, rows_per_batch=T,
                       tile_rows=_tile(T, 512))
    else:
        h3 = _xattn_rows_ln(h2, *mem, *xa, alpha=alpha, nb=_tile(B, V7X_SUBLANES))
    h4 = _ffn_ln(h3, *W['ffn2'], W['ln_g'][3], W['ln_b'][3], alpha=alpha, tm=tm)
    return h4.reshape(B, T, d), s_new, pa3[:, -1], h_last.reshape(B, lru_w), conv_in_tail


def kernel(x_prompt, x_sample, mem_prompt, cache_mem_k, cache_mem_v, state_rwkv, state_rwkv_shift, state_lru, state_conv, ln_g, ln_b, ffn1_wi, ffn1_wo, ffn2_wi, ffn2_wo, w_in, shift_mu, decay_w0, decay_w2, aaa_a0, aaa_a2, gate_g2, k_k, k_a, r_k, gn_g, gn_b, conv_w, conv_b, lru_wr, lru_br, lru_wi, lru_bi, lru_lambda, w_mix_out, xa_wq, xa_wk, xa_wv, xa_wo):
    depth = ln_g.shape[0]
    alpha = (2.0 * depth) ** 0.25
    B, _, d = x_prompt.shape
    n_mem, xa_heads, xa_head = cache_mem_k.shape[2:]
    rp = shift_mu.shape[1]
    lru_w = conv_b.shape[1]
    hp, hs = x_prompt, x_sample
    outs = [[] for _ in range(10)]
    for l in range(depth):
        W = _prep_layer(l, ln_g, ln_b, ffn1_wi, ffn1_wo, ffn2_wi, ffn2_wo, w_in, shift_mu, decay_w0, decay_w2,
                        aaa_a0, aaa_a2, gate_g2, k_k, k_a, r_k.reshape(depth, -1), gn_g, gn_b, conv_w, conv_b,
                        lru_wr, lru_br, lru_wi, lru_bi, lru_lambda, w_mix_out, xa_wq, xa_wk, xa_wv, xa_wo)
        kv, mk, mv = _kv_proj(mem_prompt.reshape(B * n_mem, d), W['xa_wkv'], heads=xa_heads,
                              tm=_tile(B * n_mem, 512))
        mk = mk.reshape(B, n_mem, xa_heads, xa_head)
        mv = mv.reshape(B, n_mem, xa_heads, xa_head)
        hp, S1, sh1, h1, b1 = _layer(
            hp, kv, None, jnp.zeros((B, rp), F32), jnp.zeros((B, lru_w), F32),
            jnp.zeros((B, CONV_WIDTH - 1, lru_w), F32), W, alpha=alpha, xa_heads=xa_heads)
        hs, S2, sh2, h2, b2 = _layer(
            hs, (cache_mem_k[l], cache_mem_v[l]),
            state_rwkv[l], state_rwkv_shift[l], state_lru[l], state_conv[l], W, alpha=alpha, xa_heads=xa_heads)
        for lst, val in zip(outs, (mk, mv, S1, sh1, h1, b1, S2, sh2, h2, b2)):
            lst.append(val)
    return (hp, hs) + tuple(jnp.stack(o) for o in outs)
```

```python
import functools

import jax
import jax.numpy as jnp
from jax import lax
from jax.experimental import pallas as pl
from jax.experimental.pallas import tpu as pltpu

F32 = jnp.float32
BF16 = jnp.bfloat16

RWKV_HEAD = 64
DECAY_LORA = 64
AAA_LORA = 64
GATE_LORA = 128
GN_EPS = 64e-5
LRU_BLOCK = 64
CONV_WIDTH = 4
LRU_C = 8.0
LN_EPS = 1e-5

V7X_LANES = 128
V7X_SUBLANES = 8
V7X_MXU_DIM = 256
V7X_SCOPED_VMEM_BYTES = 60000 * 1024

RWKV_CHUNK = 64
HEAD_PAIR = 2 * RWKV_HEAD
SCAN_ROW_GROUPS = 2
LRU_GROUP = V7X_MXU_DIM
LRU_PROJ_PIECES = 8


def _cparams(semantics):
    return pltpu.CompilerParams(dimension_semantics=semantics, vmem_limit_bytes=V7X_SCOPED_VMEM_BYTES)


def _const_spec(shape):
    zeros = (0,) * len(shape)
    return pl.BlockSpec(shape, lambda *_: zeros)


def _dot(a, b):
    return jnp.dot(a.astype(BF16), b.astype(BF16), preferred_element_type=F32)


def _dot_dims(a, b, dims):
    return lax.dot_general(a.astype(BF16), b.astype(BF16), (dims, ((), ())), preferred_element_type=F32)


_NN = ((1,), (0,))
_NT = ((1,), (1,))
_TN = ((0,), (0,))


def _split2(x):
    hi = x.astype(BF16)
    lo = (x - hi.astype(F32)).astype(BF16)
    return hi, lo


def _dot_exact_lhs(a_bf16, b):
    hi, lo = _split2(b)
    d = lambda y: jnp.dot(a_bf16, y, preferred_element_type=F32)
    return d(hi) + d(lo)


def _dot_exact_rhs(a, b_bf16):
    hi, lo = _split2(a)
    d = lambda x: jnp.dot(x, b_bf16, preferred_element_type=F32)
    return d(hi) + d(lo)


def _layer_norm(x, g, b):
    mu = jnp.mean(x, axis=-1, keepdims=True)
    xc = x - mu
    var = jnp.mean(xc * xc, axis=-1, keepdims=True)
    return xc * lax.rsqrt(var + LN_EPS) * g + b


def _softplus(z):
    return jnp.maximum(z, 0.0) + jnp.log(1.0 + jnp.exp(-jnp.abs(z)))


def _head_ones(width):
    r = lax.broadcasted_iota(jnp.int32, (width, width), 0) // RWKV_HEAD
    c = lax.broadcasted_iota(jnp.int32, (width, width), 1) // RWKV_HEAD
    return (r == c).astype(BF16)


def _mm_kernel(x_ref, w_ref, o_ref):
    o_ref[...] = jnp.dot(x_ref[...].astype(BF16), w_ref[...], preferred_element_type=F32)


def _matmul(x, w, *, tm, tn):
    n, k = x.shape
    m = w.shape[1]
    assert n % tm == 0 and m % tn == 0
    return pl.pallas_call(
        _mm_kernel,
        grid=(m // tn, n // tm),
        in_specs=[pl.BlockSpec((tm, k), lambda j, i: (i, 0)),
                  pl.BlockSpec((k, tn), lambda j, i: (0, j))],
        out_specs=pl.BlockSpec((tm, tn), lambda j, i: (i, j)),
        out_shape=jax.ShapeDtypeStruct((n, m), F32),
        compiler_params=_cparams(("parallel", "parallel")),
        name="matmul",
    )(x, w)


def _ffn_kernel(x_ref, wg_ref, wu_ref, wo_ref, g_ref, b_ref, o_ref, *, alpha, row_groups):
    tm = x_ref.shape[0]
    tg = tm // row_groups
    for gi in range(row_groups):
        rows = slice(gi * tg, (gi + 1) * tg)
        x = x_ref[rows, :]
        xb = x.astype(BF16)
        gate = jnp.dot(xb, wg_ref[...], preferred_element_type=F32)
        up = jnp.dot(xb, wu_ref[...], preferred_element_type=F32)
        mid = (gate * jax.nn.sigmoid(gate) * up).astype(BF16)
        down = jnp.dot(mid, wo_ref[...], preferred_element_type=F32)
        o_ref[rows, :] = _layer_norm(alpha * x + 0.5 * down, g_ref[...], b_ref[...])


def _ffn_ln(x, wg, wu, wo, g, b, *, alpha, tm):
    n, d = x.shape
    assert n % tm == 0
    row_groups = max(1, tm // V7X_MXU_DIM)
    resident = lambda w: pl.BlockSpec(w.shape, lambda i: (0, 0), pipeline_mode=pl.Buffered(1))
    return pl.pallas_call(
        functools.partial(_ffn_kernel, alpha=alpha, row_groups=row_groups),
        grid=(n // tm,),
        in_specs=[pl.BlockSpec((tm, d), lambda i: (i, 0)),
                  resident(wg), resident(wu), resident(wo),
                  _const_spec(g.shape), _const_spec(b.shape)],
        out_specs=pl.BlockSpec((tm, d), lambda i: (i, 0)),
        out_shape=jax.ShapeDtypeStruct((n, d), F32),
        compiler_params=_cparams(("parallel",)),
        name="ffn_ln",
    )(x, wg, wu, wo, g, b)


def _rwkv_pre(r, k, v, xx, w0, w2, a0, a2, g2, k_k, k_a, r_k, ones):
    xw = xx[:, :DECAY_LORA]
    xa = xx[:, DECAY_LORA:DECAY_LORA + AAA_LORA]
    xg = xx[:, DECAY_LORA + AAA_LORA:]
    z = w0 + _dot(jnp.tanh(xw), w2)
    lw = -jnp.exp(-_softplus(-z) - 0.5)
    a = jax.nn.sigmoid(a0 + _dot(xa, a2))
    g = _dot(jax.nn.sigmoid(xg), g2)
    kkr = k * k_k
    ss = _dot(kkr * kkr, ones)
    kk = kkr * lax.rsqrt(jnp.maximum(ss, 1e-24))
    kf = k * (1.0 + (a - 1.0) * k_a)
    bonus = _dot(r * kf * r_k, ones) * v
    return lw, a, g, kk, kf, bonus


def _rwkv_post(y, bonus, g, gn_g, gn_b, ones):
    inv_n = 1.0 / RWKV_HEAD
    ym = _dot_exact_rhs(y, ones) * inv_n
    yc = y - ym
    yv = _dot(yc * yc, ones) * inv_n
    yn = yc * lax.rsqrt(yv + GN_EPS) * gn_g + gn_b
    return (yn + bonus) * g


def _bdot(a, b, dims):
    dn = ((tuple(d + 1 for d in dims[0]), tuple(d + 1 for d in dims[1])), ((0,), (0,)))
    return lax.dot_general(a.astype(BF16), b.astype(BF16), dn, preferred_element_type=F32)


def _scan_operands(r, kf, v, kk, a, lw):
    tt, hw = r.shape
    C = RWKV_CHUNK
    n_pairs = hw // HEAD_PAIR
    ltri = (lax.broadcasted_iota(jnp.int32, (C, C), 0) >= lax.broadcasted_iota(jnp.int32, (C, C), 1)).astype(BF16)
    first = lax.broadcasted_iota(jnp.int32, (C, HEAD_PAIR), 1) < RWKV_HEAD

    def bd(x):
        return jnp.concatenate([jnp.where(first, x, 0.0), jnp.where(first, 0.0, x)], axis=1)

    names = ('a', 'r', 'b', 'k', 'v', 'bh', 'kh')
    ops = {n: [] for n in names}
    wcs = []
    for c in range(tt // C):
        rows = slice(c * C, (c + 1) * C)
        lw_c = lw[rows]
        L = _dot_exact_lhs(ltri, lw_c)
        Lc = L[C - 1:C, :]
        e_nl = jnp.exp(-L)
        e_c = jnp.exp(Lc - L)
        bb = kk[rows] * a[rows]
        bf = lambda x: x.astype(BF16)
        tile = dict(a=bf(-kk[rows] * jnp.exp(L - lw_c)), r=r[rows] * jnp.exp(L), b=bf(bb * e_nl),
                    k=bf(kf[rows] * e_nl), v=bf(v[rows]), bh=bf(bb * e_c), kh=bf(kf[rows] * e_c))
        wc = jnp.exp(Lc)
        for p in range(n_pairs):
            lanes = slice(p * HEAD_PAIR, (p + 1) * HEAD_PAIR)
            for n in names:
                ops[n].append(tile[n][:, lanes])
            wcs.append(wc[:, lanes])
    A, R, B, K, V, Bh, Kh = (jnp.stack(ops[n]) for n in names)
    Vbd = bd(V)
    G = _bdot(jnp.concatenate([A, R.astype(BF16)], axis=1), jnp.concatenate([bd(B), bd(K)], axis=1), _NT)
    tok = lax.broadcasted_iota(jnp.int32, (C, HEAD_PAIR), 0)
    src = lax.broadcasted_iota(jnp.int32, (C, HEAD_PAIR), 1) % RWKV_HEAD
    a_ab = jnp.where(tok > src, G[:, :C, :HEAD_PAIR], 0.0)
    a_ak = jnp.where(tok > src, G[:, :C, HEAD_PAIR:], 0.0).astype(BF16)
    a_rb = jnp.where(tok >= src, G[:, C:, :HEAD_PAIR], 0.0).astype(BF16)
    a_rk = jnp.where(tok >= src, G[:, C:, HEAD_PAIR:], 0.0).astype(BF16)
    P = jnp.where(tok == src, 1.0, 0.0) + a_ab
    N = a_ab.astype(BF16)
    N = _bdot(N, bd(N), _NN).astype(BF16)
    steps = 2
    while 2 * steps < C:
        NP = _bdot(jnp.concatenate([N, P.astype(BF16)], axis=1), bd(N), _NN)
        N = NP[:, :C].astype(BF16)
        P = P + NP[:, C:]
        steps *= 2
    P = (P + _bdot(P, bd(N), _NN)).astype(BF16)
    aV = _bdot(a_ak, Vbd, _NN).astype(BF16)
    XU = _bdot(P, jnp.concatenate([bd(A), bd(aV)], axis=2), _NN).astype(BF16)
    X1 = XU[:, :, :HEAD_PAIR]
    Uloc = XU[:, :, HEAD_PAIR:]
    Q = (R + _bdot(a_rb, bd(X1), _NN)).astype(BF16)
    Yloc = _bdot(jnp.concatenate([a_rb, a_rk], axis=2), jnp.concatenate([bd(Uloc), Vbd], axis=1), _NN)
    ri = lax.broadcasted_iota(jnp.int32, (HEAD_PAIR, HEAD_PAIR), 0) // RWKV_HEAD
    ci = lax.broadcasted_iota(jnp.int32, (HEAD_PAIR, HEAD_PAIR), 1) // RWKV_HEAD
    same_head = ri == ci
    Pm = jnp.where(same_head, _bdot(X1, Bh, _TN), 0.0).astype(BF16)
    Sloc = jnp.where(same_head, _bdot(jnp.concatenate([Uloc, V], axis=1), jnp.concatenate([Bh, Kh], axis=1), _TN),
                     0.0)
    return Q, Yloc, Pm, Sloc, jnp.stack(wcs)


def _rwkv_chunk_kernel(pr_ref, pk_ref, pv_ref, px_ref, sr_ref, sk_ref, sv_ref, sx_ref,
                       mur_ref, muk_ref, muv_ref, mux_ref, w0_ref, w2_ref, a0_ref, a2_ref, g2_ref,
                       kk_ref, ka_ref, rk_ref, gng_ref, gnb_ref, ones_ref,
                       out_ref, s_out_ref,
                       s_scr, cr_scr, ck_scr, cv_scr, cx_scr, y_scr):
    ti = pl.program_id(2)
    tt, hw = pr_ref.shape
    n_pairs = hw // HEAD_PAIR

    @pl.when(ti == 0)
    def _init():
        s_scr[...] = jnp.zeros(s_scr.shape, F32)
        cr_scr[...] = sr_ref[...]
        ck_scr[...] = sk_ref[...]
        cv_scr[...] = sv_ref[...]
        cx_scr[...] = sx_ref[...]

    S8 = V7X_SUBLANES

    def shifted(p_ref, c_scr, mu_ref):
        p = p_ref[...]
        first = lax.broadcasted_iota(jnp.int32, (S8, p.shape[1]), 0) == 0
        head = jnp.where(first, c_scr[...], pltpu.roll(p[:S8], 1, 0))
        prev = jnp.concatenate([head, p_ref[S8 - 1:tt - 1, :]], axis=0)
        c_scr[...] = p_ref[tt - 1:tt, :]
        return p + (prev - p) * mu_ref[...]

    r = shifted(pr_ref, cr_scr, mur_ref)
    k = shifted(pk_ref, ck_scr, muk_ref)
    v = shifted(pv_ref, cv_scr, muv_ref)
    xx = shifted(px_ref, cx_scr, mux_ref)
    ones = ones_ref[...]
    lw, a, g, kk, kf, bonus = _rwkv_pre(r, k, v, xx, w0_ref[...], w2_ref[...], a0_ref[...], a2_ref[...],
                                        g2_ref[...], kk_ref[...], ka_ref[...], rk_ref[...], ones)
    C = RWKV_CHUNK
    S = s_scr[...]
    n_groups = SCAN_ROW_GROUPS if tt % (SCAN_ROW_GROUPS * C) == 0 else 1
    tg = tt // n_groups
    for gi in range(n_groups):
        rows = slice(gi * tg, (gi + 1) * tg)
        Q, Yloc, Pm, Sloc, wc = _scan_operands(r[rows], kf[rows], v[rows], kk[rows], a[rows], lw[rows])
        for c in range(tg // C):
            inst = slice(c * n_pairs, (c + 1) * n_pairs)
            y_c = _bdot(Q[inst], S, _NT) + Yloc[inst]
            row0 = gi * tg + c * C
            for p in range(n_pairs):
                y_scr[row0:row0 + C, p * HEAD_PAIR:(p + 1) * HEAD_PAIR] = y_c[p]
            S = S * wc[inst] + _bdot(S, Pm[inst], _NN) + Sloc[inst]
    s_scr[...] = S
    out_ref[...] = _rwkv_post(y_scr[...], bonus, g, gng_ref[...], gnb_ref[...], ones)

    @pl.when(ti == pl.num_programs(2) - 1)
    def _emit_state():
        for p in range(n_pairs):
            s_out_ref[2 * p] = S[p, :RWKV_HEAD, :RWKV_HEAD]
            s_out_ref[2 * p + 1] = S[p, RWKV_HEAD:, RWKV_HEAD:]


def _rwkv_chunked(pa, shift0, W, *, tt, hw):
    B, T, _ = pa.shape
    width = W['decay_w0'].shape[1]
    heads = width // RWKV_HEAD
    assert T % tt == 0 and tt % RWKV_CHUNK == 0 and width % hw == 0 and hw % HEAD_PAIR == 0
    nb = width // hw
    xw = DECAY_LORA + AAA_LORA + GATE_LORA
    assert (3 * width) % xw == 0
    xblk = 3 * width // xw
    col = lambda off: (lambda b, h, t: (b, t, off + h))
    vec = lambda: pl.BlockSpec((1, hw), lambda b, h, t: (0, h))
    in_specs = [
        pl.BlockSpec((None, tt, hw), col(0)), pl.BlockSpec((None, tt, hw), col(nb)),
        pl.BlockSpec((None, tt, hw), col(2 * nb)), pl.BlockSpec((None, tt, xw), lambda b, h, t: (b, t, xblk)),
        pl.BlockSpec((None, 1, hw), lambda b, h, t: (b, 0, h)), pl.BlockSpec((None, 1, hw), lambda b, h, t: (b, 0, nb + h)),
        pl.BlockSpec((None, 1, hw), lambda b, h, t: (b, 0, 2 * nb + h)), pl.BlockSpec((None, 1, xw), lambda b, h, t: (b, 0, xblk)),
        vec(), vec(), vec(), _const_spec((1, xw)),
        vec(), pl.BlockSpec((DECAY_LORA, hw), lambda b, h, t: (0, h)),
        vec(), pl.BlockSpec((AAA_LORA, hw), lambda b, h, t: (0, h)),
        pl.BlockSpec((GATE_LORA, hw), lambda b, h, t: (0, h)),
        vec(), vec(), vec(), vec(), vec(), _const_spec((hw, hw)),
    ]
    out, s_new = pl.pallas_call(
        _rwkv_chunk_kernel,
        grid=(B, nb, T // tt),
        in_specs=in_specs,
        out_specs=[pl.BlockSpec((None, tt, hw), lambda b, h, t: (b, t, h)),
                   pl.BlockSpec((None, hw // RWKV_HEAD, RWKV_HEAD, RWKV_HEAD), lambda b, h, t: (b, h, 0, 0))],
        out_shape=[jax.ShapeDtypeStruct((B, T, width), F32),
                   jax.ShapeDtypeStruct((B, heads, RWKV_HEAD, RWKV_HEAD), F32)],
        scratch_shapes=[pltpu.VMEM((hw // HEAD_PAIR, HEAD_PAIR, HEAD_PAIR), F32),
                        pltpu.VMEM((1, hw), F32), pltpu.VMEM((1, hw), F32), pltpu.VMEM((1, hw), F32),
                        pltpu.VMEM((1, xw), F32), pltpu.VMEM((tt, hw), F32)],
        compiler_params=_cparams(("parallel", "parallel", "arbitrary")),
        name="rwkv_chunked",
    )(pa, pa, pa, pa, shift0, shift0, shift0, shift0,
      W['mu_r'], W['mu_k'], W['mu_v'], W['mu_x'], W['decay_w0'], W['decay_w2'], W['aaa_a0'], W['aaa_a2'],
      W['gate_g2'], W['k_k'], W['k_a'], W['r_k'], W['gn_g'], W['gn_b'], _head_ones(hw))
    return out, s_new


def _rwkv_step_kernel(p_ref, s0_ref, st_ref, mu_ref, w0_ref, w2_ref, a0_ref, a2_ref, g2_ref,
                      kk_ref, ka_ref, rk_ref, gng_ref, gnb_ref, ones_ref,
                      out_ref, st_out_ref, vec_scr, y_scr, bonus_scr, g_scr):
    i = pl.program_id(0)
    B, width = out_ref.shape
    hg = st_ref.shape[0]
    H = RWKV_HEAD

    @pl.when(i == 0)
    def _prologue():
        ones = ones_ref[...]
        p = p_ref[...]
        xs = p + (s0_ref[...] - p) * mu_ref[...]
        r, k, v, xx = xs[:, :width], xs[:, width:2 * width], xs[:, 2 * width:3 * width], xs[:, 3 * width:]
        lw, a, g, kk, kf, bonus = _rwkv_pre(r, k, v, xx, w0_ref[...], w2_ref[...], a0_ref[...], a2_ref[...],
                                            g2_ref[...], kk_ref[...], ka_ref[...], rk_ref[...], ones)
        for j, vec in enumerate((r, kf, v, kk, kk * a, jnp.exp(lw))):
            vec_scr[j] = vec.T
        bonus_scr[...] = bonus
        g_scr[...] = g

    for hl in range(hg):
        base = pl.multiple_of((i * hg + hl) * H, H)
        r_h, kf_h, kk_h, kka_h, w_h = (vec_scr[j, pl.ds(base, H), :] for j in (0, 1, 3, 4, 5))

        def body(vi, carry):
            S = st_ref[hl, vi]
            sa = jnp.sum(S * kk_h, axis=0, keepdims=True)
            v_row = vec_scr[2, pl.ds(base + vi, 1), :]
            S2 = S * w_h - sa * kka_h + v_row * kf_h
            st_out_ref[hl, vi] = S2
            y_scr[pl.ds(base + vi, 1), :] = jnp.sum(S2 * r_h, axis=0, keepdims=True)
            return carry

        lax.fori_loop(0, H, body, 0, unroll=8)

    @pl.when(i == pl.num_programs(0) - 1)
    def _epilogue():
        out_ref[...] = _rwkv_post(y_scr[...].T, bonus_scr[...], g_scr[...], gng_ref[...], gnb_ref[...],
                                  ones_ref[...])


def _rwkv_step(pa, shift0, state_t, W, *, hg):
    B, proj = pa.shape
    width = W['decay_w0'].shape[1]
    heads = width // RWKV_HEAD
    assert heads % hg == 0 and state_t.shape == (heads, RWKV_HEAD, RWKV_HEAD, B)
    st_spec = pl.BlockSpec((hg, RWKV_HEAD, RWKV_HEAD, B), lambda i: (i, 0, 0, 0))
    names = ('decay_w0', 'decay_w2', 'aaa_a0', 'aaa_a2', 'gate_g2', 'k_k', 'k_a', 'r_k', 'gn_g', 'gn_b')
    out, st_new = pl.pallas_call(
        _rwkv_step_kernel,
        grid=(heads // hg,),
        in_specs=[_const_spec((B, proj)), _const_spec((B, proj)), st_spec, _const_spec(W['shift_mu'].shape)]
                 + [_const_spec(W[n].shape) for n in names] + [_const_spec((width, width))],
        out_specs=[_const_spec((B, width)), st_spec],
        out_shape=[jax.ShapeDtypeStruct((B, width), F32), jax.ShapeDtypeStruct(state_t.shape, F32)],
        scratch_shapes=[pltpu.VMEM((6, width, B), F32), pltpu.VMEM((width, B), F32),
                        pltpu.VMEM((B, width), F32), pltpu.VMEM((B, width), F32)],
        compiler_params=_cparams(("arbitrary",)),
        name="rwkv_step",
    )(pa, shift0, state_t, W['shift_mu'], *[W[n] for n in names], _head_ones(width))
    return out, st_new


def _lru_coeffs(xc, wr_ref, br_ref, wi_ref, bi_ref, lam_ref):
    xcb = xc.astype(BF16)
    n_grp = xc.shape[1] // LRU_GROUP
    grp = lambda w_ref: jnp.concatenate(
        [jnp.dot(xcb[:, i * LRU_GROUP:(i + 1) * LRU_GROUP], w_ref[i], preferred_element_type=F32)
         for i in range(n_grp)], axis=1)
    gr = jax.nn.sigmoid(grp(wr_ref) + br_ref[...])
    gi = jax.nn.sigmoid(grp(wi_ref) + bi_ref[...])
    a_t = jnp.exp(-LRU_C * gr * _softplus(-lam_ref[...]))
    return a_t, jnp.sqrt(1.0 - a_t * a_t) * gi * xc


def _lru_finish(hs, pg, g0, g1, rw, h, wmix_ref, lng_ref, lnb_ref, alpha):
    lru_out = hs * jax.nn.gelu(pg)
    merged = jax.nn.sigmoid(g0) * rw + jax.nn.sigmoid(g1) * lru_out
    mix = jnp.dot(merged.astype(BF16), wmix_ref[...], preferred_element_type=F32)
    return _layer_norm(alpha * h + mix, lng_ref[...], lnb_ref[...])


def _lru_kernel(hn_ref, rw_ref, h_ref, buf_ref, h0_ref, win_ref,
                cw_ref, cb_ref, wr_ref, br_ref, wi_ref, bi_ref, lam_ref, wmix_ref, lng_ref, lnb_ref,
                out_ref, hlast_ref, tail_ref, proj_scr, tail_scr, hc_scr, *, alpha):
    ti = pl.program_id(1)
    flat = pl.program_id(0) * pl.num_programs(1) + ti
    tt, width = rw_ref.shape
    S8 = V7X_SUBLANES
    project = lambda ref: jnp.dot(ref[...].astype(BF16), win_ref[...], preferred_element_type=F32)
    col = lambda j: slice(j * width, (j + 1) * width)

    @pl.when(ti == 0)
    def _init():
        tail_scr[...] = buf_ref[...]
        hc_scr[...] = h0_ref[...]

    @pl.when(flat == 0)
    def _first_projection():
        proj_scr[0] = project(h_ref)

    def step(cur_scr, nxt_scr):
        hn = hn_ref[...].astype(BF16)
        piece_w = win_ref.shape[1] // LRU_PROJ_PIECES
        pieces = iter(range(LRU_PROJ_PIECES))

        def emit(count=1):
            for _ in range(count):
                k = next(pieces, None)
                if k is not None:
                    cols = slice(k * piece_w, (k + 1) * piece_w)
                    nxt_scr[:, cols] = jnp.dot(hn, win_ref[:, cols], preferred_element_type=F32)

        x = cur_scr[:, col(0)]
        tail = tail_scr[...]
        r8 = lax.broadcasted_iota(jnp.int32, (S8, width), 0)

        def delayed(d):
            head = jnp.where(r8 < d, pltpu.roll(tail, d, 0), pltpu.roll(x[:S8], d, 0))
            return jnp.concatenate([head, cur_scr[S8 - d:tt - d, col(0)]], axis=0)

        cw = cw_ref[...]
        conv = cw[0:1] * delayed(CONV_WIDTH - 1)
        for j in range(1, CONV_WIDTH - 1):
            conv = conv + cw[j:j + 1] * delayed(CONV_WIDTH - 1 - j)
        xc = cb_ref[...] + (conv + cw[CONV_WIDTH - 1:CONV_WIDTH] * x)
        tail_scr[...] = x[tt - S8:, :]
        tail_ref[...] = x[tt - S8:, :]
        emit()

        A, Bv = _lru_coeffs(xc, wr_ref, br_ref, wi_ref, bi_ref, lam_ref)
        emit()
        A = A.reshape(tt // S8, S8, width)
        Bv = Bv.reshape(tt // S8, S8, width)
        in_grp = lax.broadcasted_iota(jnp.int32, (S8, width), 0)
        s = 1
        while s < S8:
            keep = in_grp >= s
            Bv = Bv + A * jnp.where(keep, pltpu.roll(Bv, s, 1), 0.0)
            A = A * jnp.where(keep, pltpu.roll(A, s, 1), 1.0)
            s *= 2
            emit()
        carry = hc_scr[...]
        groups = []
        for gi in range(tt // S8):
            hg = Bv[gi] + A[gi] * carry
            groups.append(hg)
            carry = hg[S8 - 1:S8, :]
        hs = jnp.concatenate(groups, axis=0)
        hc_scr[...] = carry
        hlast_ref[...] = carry
        emit()
        lru_out = hs * jax.nn.gelu(cur_scr[:, col(1)])
        emit()
        merged = jax.nn.sigmoid(cur_scr[:, col(2)]) * rw_ref[...] + jax.nn.sigmoid(cur_scr[:, col(3)]) * lru_out
        emit()
        mix = jnp.dot(merged.astype(BF16), wmix_ref[...], preferred_element_type=F32)
        out_ref[...] = _layer_norm(alpha * h_ref[...] + mix, lng_ref[...], lnb_ref[...])
        emit(LRU_PROJ_PIECES)

    slot = flat % 2
    step(proj_scr.at[slot], proj_scr.at[1 - slot])


def _lru_step_kernel(x_ref, pg_ref, g0_ref, g1_ref, rw_ref, h_ref, buf_ref, h0_ref,
                     cw_ref, cb_ref, wr_ref, br_ref, wi_ref, bi_ref, lam_ref, wmix_ref, lng_ref, lnb_ref,
                     out_ref, hnew_ref, *, alpha):
    cw = cw_ref[...]
    conv = cw[0:1] * buf_ref[0]
    for j in range(1, CONV_WIDTH - 1):
        conv = conv + cw[j:j + 1] * buf_ref[j]
    xc = cb_ref[...] + (conv + cw[CONV_WIDTH - 1:CONV_WIDTH] * x_ref[...])
    A, Bv = _lru_coeffs(xc, wr_ref, br_ref, wi_ref, bi_ref, lam_ref)
    hs = Bv + A * h0_ref[...]
    hnew_ref[...] = hs
    out_ref[...] = _lru_finish(hs, pg_ref[...], g0_ref[...], g1_ref[...], rw_ref[...], h_ref[...],
                               wmix_ref, lng_ref, lnb_ref, alpha)


_LRU_WEIGHTS = ('conv_w', 'conv_b', 'lru_wr', 'lru_br', 'lru_wi', 'lru_bi', 'lru_lambda', 'w_mix_out')


def _lru_step_mix_ln(pb, rw, h, buf, h0, W, ln_g, ln_b, *, alpha, tm):
    B, width = rw.shape
    assert B % tm == 0
    blk = lambda j: pl.BlockSpec((tm, width), lambda i: (i, j))
    out, h_new = pl.pallas_call(
        functools.partial(_lru_step_kernel, alpha=alpha),
        grid=(B // tm,),
        in_specs=[blk(0), blk(1), blk(2), blk(3), blk(0), blk(0),
                  pl.BlockSpec((CONV_WIDTH - 1, tm, width), lambda i: (0, i, 0)), blk(0)]
                 + [_const_spec(W[n].shape) for n in _LRU_WEIGHTS] + [_const_spec(ln_g.shape), _const_spec(ln_b.shape)],
        out_specs=[blk(0), blk(0)],
        out_shape=[jax.ShapeDtypeStruct((B, width), F32), jax.ShapeDtypeStruct((B, width), F32)],
        compiler_params=_cparams(("parallel",)),
        name="lru_step_mix_ln",
    )(pb, pb, pb, pb, rw, h, buf, h0, *[W[n] for n in _LRU_WEIGHTS], ln_g, ln_b)
    return out, h_new


def _lru_mix_ln(h, rw, buf8, h0, W, ln_g, ln_b, *, alpha, tt):
    B, T, width = rw.shape
    assert T % tt == 0 and tt % V7X_SUBLANES == 0 and h.shape[2] == width
    nt = T // tt
    tile = pl.BlockSpec((None, tt, width), lambda b, t: (b, t, 0))

    def next_tile(b, t):
        f = jnp.minimum(b * nt + t + 1, B * nt - 1)
        return f // nt, f % nt, 0

    nxt = pl.BlockSpec((None, tt, width), next_tile)
    per_b = lambda rows: pl.BlockSpec((None, rows, width), lambda b, t: (b, 0, 0))
    win = W['w_in_b']
    out, h_last, tail = pl.pallas_call(
        functools.partial(_lru_kernel, alpha=alpha),
        grid=(B, nt),
        in_specs=[nxt, tile, tile, per_b(V7X_SUBLANES), per_b(1), _const_spec(win.shape)]
                 + [_const_spec(W[n].shape) for n in _LRU_WEIGHTS] + [_const_spec(ln_g.shape), _const_spec(ln_b.shape)],
        out_specs=[tile, per_b(1), per_b(V7X_SUBLANES)],
        out_shape=[jax.ShapeDtypeStruct((B, T, width), F32), jax.ShapeDtypeStruct((B, 1, width), F32),
                   jax.ShapeDtypeStruct((B, V7X_SUBLANES, width), F32)],
        scratch_shapes=[pltpu.VMEM((2, tt, win.shape[1]), F32), pltpu.VMEM((V7X_SUBLANES, width), F32),
                        pltpu.VMEM((1, width), F32)],
        compiler_params=_cparams(("arbitrary", "arbitrary")),
        name="lru_mix_ln",
    )(h, rw, h, buf8, h0, win, *[W[n] for n in _LRU_WEIGHTS], ln_g, ln_b)
    return out, h_last, tail


def _xattn_kernel(h_ref, kv_ref, wq_ref, wo_ref, g_ref, b_ref, o_ref, *, alpha, heads):
    tm, d = h_ref.shape
    hd = d // heads
    row_groups = 2 if tm % (2 * V7X_MXU_DIM) == 0 else 1
    tg = tm // row_groups
    for gi in range(row_groups):
        rows = slice(gi * tg, (gi + 1) * tg)
        h = h_ref[rows, :]
        q = jnp.dot(h.astype(BF16), wq_ref[...], preferred_element_type=F32)
        outs = []
        for j in range(heads):
            sl = slice(j * hd, (j + 1) * hd)
            s = _dot_dims(q[:, sl], kv_ref[:, sl], _NT) * (hd ** -0.5)
            e = jnp.exp(s - jnp.max(s, axis=-1, keepdims=True))
            p = e * (1.0 / jnp.sum(e, axis=-1, keepdims=True))
            outs.append(_dot(p, kv_ref[:, d + j * hd:d + (j + 1) * hd]))
        out = jnp.dot(jnp.concatenate(outs, axis=1).astype(BF16), wo_ref[...], preferred_element_type=F32)
        o_ref[rows, :] = _layer_norm(alpha * h + out, g_ref[...], b_ref[...])


def _xattn_rows_kernel(h_ref, mk_ref, mv_ref, wq_ref, wo_ref, g_ref, b_ref, o_ref, *, alpha):
    h = h_ref[...]
    nb, m, heads, hd = mk_ref.shape
    q = jnp.dot(h.astype(BF16), wq_ref[...], preferred_element_type=F32)
    col = lax.broadcasted_iota(jnp.int32, (heads, m * heads), 1)
    own = (col % heads) == lax.broadcasted_iota(jnp.int32, (heads, m * heads), 0)
    rows = []
    for i in range(nb):
        k2 = mk_ref[i].reshape(m * heads, hd)
        v2 = mv_ref[i].reshape(m * heads, hd)
        q4 = jnp.concatenate([q[i:i + 1, j * hd:(j + 1) * hd] for j in range(heads)], axis=0)
        s = jnp.where(own, _dot_dims(q4, k2, _NT) * (hd ** -0.5), -jnp.inf)
        e = jnp.exp(s - jnp.max(s, axis=-1, keepdims=True))
        p = e / jnp.sum(e, axis=-1, keepdims=True)
        o4 = _dot(p, v2)
        rows.append(jnp.concatenate([o4[j:j + 1] for j in range(heads)], axis=1))
    out = jnp.dot(jnp.concatenate(rows, axis=0).astype(BF16), wo_ref[...], preferred_element_type=F32)
    o_ref[...] = _layer_norm(alpha * h + out, g_ref[...], b_ref[...])


def _xattn_ln(h, kv, wq, wo, g, b, *, alpha, heads, rows_per_batch, tile_rows):
    n, d = h.shape
    assert n % tile_rows == 0 and rows_per_batch % tile_rows == 0
    per = rows_per_batch // tile_rows
    m = kv.shape[0] // (n // rows_per_batch)
    return pl.pallas_call(
        functools.partial(_xattn_kernel, alpha=alpha, heads=heads),
        grid=(n // tile_rows,),
        in_specs=[pl.BlockSpec((tile_rows, d), lambda i: (i, 0)),
                  pl.BlockSpec((m, 2 * d), lambda i: (i // per, 0)),
                  _const_spec(wq.shape), _const_spec(wo.shape), _const_spec(g.shape), _const_spec(b.shape)],
        out_specs=pl.BlockSpec((tile_rows, d), lambda i: (i, 0)),
        out_shape=jax.ShapeDtypeStruct((n, d), F32),
        compiler_params=_cparams(("parallel",)),
        name="xattn_ln",
    )(h, kv, wq, wo, g, b)


def _xattn_rows_ln(h, mk, mv, wq, wo, g, b, *, alpha, nb):
    n, d = h.shape
    _, m, heads, hd = mk.shape
    assert n % nb == 0
    kv_spec = pl.BlockSpec((nb, m, heads, hd), lambda i: (i, 0, 0, 0))
    return pl.pallas_call(
        functools.partial(_xattn_rows_kernel, alpha=alpha),
        grid=(n // nb,),
        in_specs=[pl.BlockSpec((nb, d), lambda i: (i, 0)), kv_spec, kv_spec,
                  _const_spec(wq.shape), _const_spec(wo.shape), _const_spec(g.shape), _const_spec(b.shape)],
        out_specs=pl.BlockSpec((nb, d), lambda i: (i, 0)),
        out_shape=jax.ShapeDtypeStruct((n, d), F32),
        compiler_params=_cparams(("parallel",)),
        name="xattn_rows_ln",
    )(h, mk, mv, wq, wo, g, b)


def _kv_proj_kernel(x_ref, w_ref, kv_ref, k_ref, v_ref):
    heads, hd = k_ref.shape[1:]
    d = heads * hd
    kv = jnp.dot(x_ref[...].astype(BF16), w_ref[...], preferred_element_type=F32)
    kv_ref[...] = kv
    for j in range(heads):
        k_ref[:, j, :] = kv[:, j * hd:(j + 1) * hd]
        v_ref[:, j, :] = kv[:, d + j * hd:d + (j + 1) * hd]


def _kv_proj(x, wkv, *, heads, tm):
    n, d = x.shape
    assert n % tm == 0
    hd = d // heads
    out = jax.ShapeDtypeStruct((n, heads, hd), F32)
    return pl.pallas_call(
        _kv_proj_kernel,
        grid=(n // tm,),
        in_specs=[pl.BlockSpec((tm, d), lambda i: (i, 0)), _const_spec(wkv.shape)],
        out_specs=[pl.BlockSpec((tm, 2 * d), lambda i: (i, 0))] + [pl.BlockSpec((tm, heads, hd), lambda i: (i, 0, 0))] * 2,
        out_shape=[jax.ShapeDtypeStruct((n, 2 * d), F32), out, out],
        compiler_params=_cparams(("parallel",)),
        name="kv_proj",
    )(x, wkv)


def _row(v):
    return v.reshape(1, -1)


def _block_diag_groups(w):
    n, c, _ = w.shape
    per = LRU_GROUP // c
    w4 = w.reshape(n // per, per, c, c)
    bd = jnp.einsum('gjcd,jk->gjckd', w4, jnp.eye(per, dtype=w.dtype))
    return bd.reshape(n // per, LRU_GROUP, LRU_GROUP).astype(BF16)


def _prep_layer(l, ln_g, ln_b, ffn1_wi, ffn1_wo, ffn2_wi, ffn2_wo, w_in, shift_mu, decay_w0, decay_w2,
                aaa_a0, aaa_a2, gate_g2, k_k, k_a, r_k, gn_g, gn_b, conv_w, conv_b, lru_wr, lru_br,
                lru_wi, lru_bi, lru_lambda, w_mix_out, xa_wq, xa_wk, xa_wv, xa_wo):
    width = decay_w0.shape[1]
    rp = shift_mu.shape[1]
    d_ff = ffn1_wo.shape[1]
    bf = lambda w: w.astype(BF16)
    mu = shift_mu[l]
    return dict(
        ln_g=[_row(ln_g[l, i]) for i in range(4)], ln_b=[_row(ln_b[l, i]) for i in range(4)],
        ffn1=(bf(ffn1_wi[l][:, :d_ff]), bf(ffn1_wi[l][:, d_ff:]), bf(ffn1_wo[l])),
        ffn2=(bf(ffn2_wi[l][:, :d_ff]), bf(ffn2_wi[l][:, d_ff:]), bf(ffn2_wo[l])),
        w_in_a=bf(w_in[l][:, :rp]), w_in_b=bf(w_in[l][:, rp:]),
        shift_mu=_row(mu), mu_r=_row(mu[:width]), mu_k=_row(mu[width:2 * width]),
        mu_v=_row(mu[2 * width:3 * width]), mu_x=_row(mu[3 * width:]),
        decay_w0=_row(decay_w0[l]), decay_w2=bf(decay_w2[l]), aaa_a0=_row(aaa_a0[l]), aaa_a2=bf(aaa_a2[l]),
        gate_g2=bf(gate_g2[l]), k_k=_row(k_k[l]), k_a=_row(k_a[l]), r_k=_row(r_k[l]),
        gn_g=_row(gn_g[l]), gn_b=_row(gn_b[l]),
        conv_w=conv_w[l], conv_b=_row(conv_b[l]),
        lru_wr=_block_diag_groups(lru_wr[l]), lru_br=_row(lru_br[l]),
        lru_wi=_block_diag_groups(lru_wi[l]), lru_bi=_row(lru_bi[l]), lru_lambda=_row(lru_lambda[l]),
        w_mix_out=bf(w_mix_out[l]), xa_wq=bf(xa_wq[l]), xa_wo=bf(xa_wo[l]),
        xa_wkv=bf(jnp.concatenate([xa_wk[l], xa_wv[l]], axis=1)),
    )


def _tile(n, pref):
    return pref if n % pref == 0 else n


def _layer(h, mem, state, shift0, h0, buf0, W, *, alpha, xa_heads):
    B, T, d = h.shape
    n = B * T
    tm = _tile(n, 512)
    h1 = _ffn_ln(h.reshape(n, d), *W['ffn1'], W['ln_g'][0], W['ln_b'][0], alpha=alpha, tm=_tile(n, 1024))
    pa = _matmul(h1, W['w_in_a'], tm=tm, tn=W['w_in_a'].shape[1])
    width = W['decay_w0'].shape[1]
    lru_w = W['conv_b'].shape[1]
    pa3 = pa.reshape(B, T, -1)
    hist = CONV_WIDTH - 1
    if T > 1:
        assert state is None and T >= V7X_SUBLANES
        rw, s_new = _rwkv_chunked(pa3, shift0.reshape(B, 1, -1), W, tt=_tile(T, 512), hw=_tile(width, 512))
        buf8 = jnp.concatenate([jnp.zeros((B, V7X_SUBLANES - hist, lru_w), F32), buf0], axis=1)
        h2, h_last, tail8 = _lru_mix_ln(h1.reshape(B, T, d), rw, buf8, h0.reshape(B, 1, lru_w), W,
                                        W['ln_g'][1], W['ln_b'][1], alpha=alpha, tt=_tile(T, 256))
        conv_in_tail = tail8[:, V7X_SUBLANES - hist:]
    else:
        pb = _matmul(h1, W['w_in_b'], tm=tm, tn=W['w_in_b'].shape[1])
        rw, s_new = _rwkv_step(pa, shift0, jnp.transpose(state, (1, 2, 3, 0)), W, hg=2)
        s_new = jnp.transpose(s_new, (3, 0, 1, 2))
        h2, h_last = _lru_step_mix_ln(pb, rw, h1, jnp.swapaxes(buf0, 0, 1), h0, W,
                                      W['ln_g'][1], W['ln_b'][1], alpha=alpha, tm=_tile(B, 128))
        conv_in_tail = jnp.concatenate([buf0[:, T:], pb[:, None, :lru_w]], axis=1)
    xa = (W['xa_wq'], W['xa_wo'], W['ln_g'][2], W['ln_b'][2])
    if T > 1:
        h3 = _xattn_ln(h2.reshape(n, d), mem, *xa, alpha=alpha, heads=xa_heads, rows_per_batch=T,
                       tile_rows=_tile(T, 512))
    else:
        h3 = _xattn_rows_ln(h2, *mem, *xa, alpha=alpha, nb=_tile(B, V7X_SUBLANES))
    h4 = _ffn_ln(h3, *W['ffn2'], W['ln_g'][3], W['ln_b'][3], alpha=alpha, tm=_tile(n, 1024))
    return h4.reshape(B, T, d), s_new, pa3[:, -1], h_last.reshape(B, lru_w), conv_in_tail


def kernel(x_prompt, x_sample, mem_prompt, cache_mem_k, cache_mem_v, state_rwkv, state_rwkv_shift, state_lru, state_conv, ln_g, ln_b, ffn1_wi, ffn1_wo, ffn2_wi, ffn2_wo, w_in, shift_mu, decay_w0, decay_w2, aaa_a0, aaa_a2, gate_g2, k_k, k_a, r_k, gn_g, gn_b, conv_w, conv_b, lru_wr, lru_br, lru_wi, lru_bi, lru_lambda, w_mix_out, xa_wq, xa_wk, xa_wv, xa_wo):
    depth = ln_g.shape[0]
    alpha = (2.0 * depth) ** 0.25
    B, _, d = x_prompt.shape
    n_mem, xa_heads, xa_head = cache_mem_k.shape[2:]
    rp = shift_mu.shape[1]
    lru_w = conv_b.shape[1]
    hp, hs = x_prompt, x_sample
    outs = [[] for _ in range(10)]
    for l in range(depth):
        W = _prep_layer(l, ln_g, ln_b, ffn1_wi, ffn1_wo, ffn2_wi, ffn2_wo, w_in, shift_mu, decay_w0, decay_w2,
                        aaa_a0, aaa_a2, gate_g2, k_k, k_a, r_k.reshape(depth, -1), gn_g, gn_b, conv_w, conv_b,
                        lru_wr, lru_br, lru_wi, lru_bi, lru_lambda, w_mix_out, xa_wq, xa_wk, xa_wv, xa_wo)
        kv, mk, mv = _kv_proj(mem_prompt.reshape(B * n_mem, d), W['xa_wkv'], heads=xa_heads,
                              tm=_tile(B * n_mem, 512))
        mk = mk.reshape(B, n_mem, xa_heads, xa_head)
        mv = mv.reshape(B, n_mem, xa_heads, xa_head)
        hp, S1, sh1, h1, b1 = _layer(
            hp, kv, None, jnp.zeros((B, rp), F32), jnp.zeros((B, lru_w), F32),
            jnp.zeros((B, CONV_WIDTH - 1, lru_w), F32), W, alpha=alpha, xa_heads=xa_heads)
        hs, S2, sh2, h2, b2 = _layer(
            hs, (cache_mem_k[l], cache_mem_v[l]),
            state_rwkv[l], state_rwkv_shift[l], state_lru[l], state_conv[l], W, alpha=alpha, xa_heads=xa_heads)
        for lst, val in zip(outs, (mk, mv, S1, sh1, h1, b1, S2, sh2, h2, b2)):
            lst.append(val)
    return (hp, hs) + tuple(jnp.stack(o) for o in outs)
```

```python
import functools

import jax
import jax.numpy as jnp
from jax import lax
from jax.experimental import pallas as pl
from jax.experimental.pallas import tpu as pltpu

F32 = jnp.float32
BF16 = jnp.bfloat16

RWKV_HEAD = 64
DECAY_LORA = 64
AAA_LORA = 64
GATE_LORA = 128
GN_EPS = 64e-5
CONV_WIDTH = 4
LRU_C = 8.0
LN_EPS = 1e-5

V7X_SUBLANES = 8
V7X_MXU_DIM = 256
V7X_SCOPED_VMEM_BYTES = 60000 * 1024

RWKV_CHUNK = 64
HEAD_PAIR = 2 * RWKV_HEAD
SCAN_ROW_GROUPS = 2
LRU_GROUP = V7X_MXU_DIM
LRU_PROJ_PIECES = 8


def _cparams(semantics):
    return pltpu.CompilerParams(dimension_semantics=semantics, vmem_limit_bytes=V7X_SCOPED_VMEM_BYTES)


def _const_spec(shape):
    zeros = (0,) * len(shape)
    return pl.BlockSpec(shape, lambda *_: zeros)


def _dot(a, b):
    return jnp.dot(a.astype(BF16), b.astype(BF16), preferred_element_type=F32)


def _dot_dims(a, b, dims):
    return lax.dot_general(a.astype(BF16), b.astype(BF16), (dims, ((), ())), preferred_element_type=F32)


_NN = ((1,), (0,))
_NT = ((1,), (1,))
_TN = ((0,), (0,))


def _split2(x):
    hi = x.astype(BF16)
    lo = (x - hi.astype(F32)).astype(BF16)
    return hi, lo


def _dot_exact_lhs(a_bf16, b):
    hi, lo = _split2(b)
    d = lambda y: jnp.dot(a_bf16, y, preferred_element_type=F32)
    return d(hi) + d(lo)


def _dot_exact_rhs(a, b_bf16):
    hi, lo = _split2(a)
    d = lambda x: jnp.dot(x, b_bf16, preferred_element_type=F32)
    return d(hi) + d(lo)


def _layer_norm(x, g, b):
    mu = jnp.mean(x, axis=-1, keepdims=True)
    xc = x - mu
    var = jnp.mean(xc * xc, axis=-1, keepdims=True)
    return xc * lax.rsqrt(var + LN_EPS) * g + b


def _softplus(z):
    return jnp.maximum(z, 0.0) + jnp.log(1.0 + jnp.exp(-jnp.abs(z)))


def _head_ones(width):
    r = lax.broadcasted_iota(jnp.int32, (width, width), 0) // RWKV_HEAD
    c = lax.broadcasted_iota(jnp.int32, (width, width), 1) // RWKV_HEAD
    return (r == c).astype(BF16)


def _mm_kernel(x_ref, w_ref, o_ref):
    o_ref[...] = jnp.dot(x_ref[...].astype(BF16), w_ref[...], preferred_element_type=F32)


def _matmul(x, w, *, tm, tn):
    n, k = x.shape
    m = w.shape[1]
    assert n % tm == 0 and m % tn == 0
    return pl.pallas_call(
        _mm_kernel,
        grid=(m // tn, n // tm),
        in_specs=[pl.BlockSpec((tm, k), lambda j, i: (i, 0)),
                  pl.BlockSpec((k, tn), lambda j, i: (0, j))],
        out_specs=pl.BlockSpec((tm, tn), lambda j, i: (i, j)),
        out_shape=jax.ShapeDtypeStruct((n, m), F32),
        compiler_params=_cparams(("parallel", "parallel")),
        name="matmul",
    )(x, w)


def _ffn_kernel(x_ref, wg_ref, wu_ref, wo_ref, g_ref, b_ref, o_ref, *, alpha, row_groups):
    tm = x_ref.shape[0]
    tg = tm // row_groups
    for gi in range(row_groups):
        rows = slice(gi * tg, (gi + 1) * tg)
        x = x_ref[rows, :]
        xb = x.astype(BF16)
        gate = jnp.dot(xb, wg_ref[...], preferred_element_type=F32)
        up = jnp.dot(xb, wu_ref[...], preferred_element_type=F32)
        mid = (gate * jax.nn.sigmoid(gate) * up).astype(BF16)
        down = jnp.dot(mid, wo_ref[...], preferred_element_type=F32)
        o_ref[rows, :] = _layer_norm(alpha * x + 0.5 * down, g_ref[...], b_ref[...])


def _ffn_ln(x, wg, wu, wo, g, b, *, alpha, tm):
    n, d = x.shape
    assert n % tm == 0
    row_groups = max(1, tm // V7X_MXU_DIM)
    resident = lambda w: pl.BlockSpec(w.shape, lambda i: (0, 0), pipeline_mode=pl.Buffered(1))
    return pl.pallas_call(
        functools.partial(_ffn_kernel, alpha=alpha, row_groups=row_groups),
        grid=(n // tm,),
        in_specs=[pl.BlockSpec((tm, d), lambda i: (i, 0)),
                  resident(wg), resident(wu), resident(wo),
                  _const_spec(g.shape), _const_spec(b.shape)],
        out_specs=pl.BlockSpec((tm, d), lambda i: (i, 0)),
        out_shape=jax.ShapeDtypeStruct((n, d), F32),
        compiler_params=_cparams(("parallel",)),
        name="ffn_ln",
    )(x, wg, wu, wo, g, b)


def _rwkv_pre(r, k, v, xx, w0, w2, a0, a2, g2, k_k, k_a, r_k, ones):
    xw = xx[:, :DECAY_LORA]
    xa = xx[:, DECAY_LORA:DECAY_LORA + AAA_LORA]
    xg = xx[:, DECAY_LORA + AAA_LORA:]
    z = w0 + _dot(jnp.tanh(xw), w2)
    lw = -jnp.exp(-_softplus(-z) - 0.5)
    a = jax.nn.sigmoid(a0 + _dot(xa, a2))
    g = _dot(jax.nn.sigmoid(xg), g2)
    kkr = k * k_k
    ss = _dot(kkr * kkr, ones)
    kk = kkr * lax.rsqrt(jnp.maximum(ss, 1e-24))
    kf = k * (1.0 + (a - 1.0) * k_a)
    bonus = _dot(r * kf * r_k, ones) * v
    return lw, a, g, kk, kf, bonus


def _rwkv_post(y, bonus, g, gn_g, gn_b, ones):
    inv_n = 1.0 / RWKV_HEAD
    ym = _dot_exact_rhs(y, ones) * inv_n
    yc = y - ym
    yv = _dot(yc * yc, ones) * inv_n
    yn = yc * lax.rsqrt(yv + GN_EPS) * gn_g + gn_b
    return (yn + bonus) * g


def _bdot(a, b, dims):
    dn = ((tuple(d + 1 for d in dims[0]), tuple(d + 1 for d in dims[1])), ((0,), (0,)))
    return lax.dot_general(a.astype(BF16), b.astype(BF16), dn, preferred_element_type=F32)


def _scan_operands(r, kf, v, kk, a, lw):
    tt, hw = r.shape
    C = RWKV_CHUNK
    n_pairs = hw // HEAD_PAIR
    ltri = (lax.broadcasted_iota(jnp.int32, (C, C), 0) >= lax.broadcasted_iota(jnp.int32, (C, C), 1)).astype(BF16)
    first = lax.broadcasted_iota(jnp.int32, (C, HEAD_PAIR), 1) < RWKV_HEAD

    def bd(x):
        return jnp.concatenate([jnp.where(first, x, 0.0), jnp.where(first, 0.0, x)], axis=1)

    names = ('a', 'r', 'b', 'k', 'v', 'bh', 'kh')
    ops = {n: [] for n in names}
    wcs = []
    for c in range(tt // C):
        rows = slice(c * C, (c + 1) * C)
        lw_c = lw[rows]
        L = _dot_exact_lhs(ltri, lw_c)
        Lc = L[C - 1:C, :]
        e_nl = jnp.exp(-L)
        e_c = jnp.exp(Lc - L)
        bb = kk[rows] * a[rows]
        bf = lambda x: x.astype(BF16)
        tile = dict(a=bf(-kk[rows] * jnp.exp(L - lw_c)), r=r[rows] * jnp.exp(L), b=bf(bb * e_nl),
                    k=bf(kf[rows] * e_nl), v=bf(v[rows]), bh=bf(bb * e_c), kh=bf(kf[rows] * e_c))
        wc = jnp.exp(Lc)
        for p in range(n_pairs):
            lanes = slice(p * HEAD_PAIR, (p + 1) * HEAD_PAIR)
            for n in names:
                ops[n].append(tile[n][:, lanes])
            wcs.append(wc[:, lanes])
    A, R, B, K, V, Bh, Kh = (jnp.stack(ops[n]) for n in names)
    Vbd = bd(V)
    G = _bdot(jnp.concatenate([A, R.astype(BF16)], axis=1), jnp.concatenate([bd(B), bd(K)], axis=1), _NT)
    tok = lax.broadcasted_iota(jnp.int32, (C, HEAD_PAIR), 0)
    src = lax.broadcasted_iota(jnp.int32, (C, HEAD_PAIR), 1) % RWKV_HEAD
    a_ab = jnp.where(tok > src, G[:, :C, :HEAD_PAIR], 0.0)
    a_ak = jnp.where(tok > src, G[:, :C, HEAD_PAIR:], 0.0).astype(BF16)
    a_rb = jnp.where(tok >= src, G[:, C:, :HEAD_PAIR], 0.0).astype(BF16)
    a_rk = jnp.where(tok >= src, G[:, C:, HEAD_PAIR:], 0.0).astype(BF16)
    P = jnp.where(tok == src, 1.0, 0.0) + a_ab
    N = a_ab.astype(BF16)
    N = _bdot(N, bd(N), _NN).astype(BF16)
    steps = 2
    while 2 * steps < C:
        NP = _bdot(jnp.concatenate([N, P.astype(BF16)], axis=1), bd(N), _NN)
        N = NP[:, :C].astype(BF16)
        P = P + NP[:, C:]
        steps *= 2
    P = (P + _bdot(P, bd(N), _NN)).astype(BF16)
    aV = _bdot(a_ak, Vbd, _NN).astype(BF16)
    XU = _bdot(P, jnp.concatenate([bd(A), bd(aV)], axis=2), _NN).astype(BF16)
    X1 = XU[:, :, :HEAD_PAIR]
    Uloc = XU[:, :, HEAD_PAIR:]
    Q = (R + _bdot(a_rb, bd(X1), _NN)).astype(BF16)
    Yloc = _bdot(jnp.concatenate([a_rb, a_rk], axis=2), jnp.concatenate([bd(Uloc), Vbd], axis=1), _NN)
    ri = lax.broadcasted_iota(jnp.int32, (HEAD_PAIR, HEAD_PAIR), 0) // RWKV_HEAD
    ci = lax.broadcasted_iota(jnp.int32, (HEAD_PAIR, HEAD_PAIR), 1) // RWKV_HEAD
    same_head = ri == ci
    Pm = jnp.where(same_head, _bdot(X1, Bh, _TN), 0.0).astype(BF16)
    Sloc = jnp.where(same_head, _bdot(jnp.concatenate([Uloc, V], axis=1), jnp.concatenate([Bh, Kh], axis=1), _TN),
                     0.0)
    return Q, Yloc, Pm, Sloc, jnp.stack(wcs)


def _rwkv_chunk_kernel(pr_ref, pk_ref, pv_ref, px_ref, sr_ref, sk_ref, sv_ref, sx_ref,
                       mur_ref, muk_ref, muv_ref, mux_ref, w0_ref, w2_ref, a0_ref, a2_ref, g2_ref,
                       kk_ref, ka_ref, rk_ref, gng_ref, gnb_ref, ones_ref,
                       out_ref, s_out_ref,
                       s_scr, cr_scr, ck_scr, cv_scr, cx_scr, y_scr):
    ti = pl.program_id(2)
    tt, hw = pr_ref.shape
    n_pairs = hw // HEAD_PAIR

    @pl.when(ti == 0)
    def _init():
        s_scr[...] = jnp.zeros(s_scr.shape, F32)
        cr_scr[...] = sr_ref[...]
        ck_scr[...] = sk_ref[...]
        cv_scr[...] = sv_ref[...]
        cx_scr[...] = sx_ref[...]

    S8 = V7X_SUBLANES

    def shifted(p_ref, c_scr, mu_ref):
        p = p_ref[...]
        first = lax.broadcasted_iota(jnp.int32, (S8, p.shape[1]), 0) == 0
        head = jnp.where(first, c_scr[...], pltpu.roll(p[:S8], 1, 0))
        prev = jnp.concatenate([head, p_ref[S8 - 1:tt - 1, :]], axis=0)
        c_scr[...] = p_ref[tt - 1:tt, :]
        return p + (prev - p) * mu_ref[...]

    r = shifted(pr_ref, cr_scr, mur_ref)
    k = shifted(pk_ref, ck_scr, muk_ref)
    v = shifted(pv_ref, cv_scr, muv_ref)
    xx = shifted(px_ref, cx_scr, mux_ref)
    ones = ones_ref[...]
    lw, a, g, kk, kf, bonus = _rwkv_pre(r, k, v, xx, w0_ref[...], w2_ref[...], a0_ref[...], a2_ref[...],
                                        g2_ref[...], kk_ref[...], ka_ref[...], rk_ref[...], ones)
    C = RWKV_CHUNK
    S = s_scr[...]
    n_groups = SCAN_ROW_GROUPS if tt % (SCAN_ROW_GROUPS * C) == 0 else 1
    tg = tt // n_groups
    for gi in range(n_groups):
        rows = slice(gi * tg, (gi + 1) * tg)
        Q, Yloc, Pm, Sloc, wc = _scan_operands(r[rows], kf[rows], v[rows], kk[rows], a[rows], lw[rows])
        for c in range(tg // C):
            inst = slice(c * n_pairs, (c + 1) * n_pairs)
            y_c = _bdot(Q[inst], S, _NT) + Yloc[inst]
            row0 = gi * tg + c * C
            for p in range(n_pairs):
                y_scr[row0:row0 + C, p * HEAD_PAIR:(p + 1) * HEAD_PAIR] = y_c[p]
            S = S * wc[inst] + _bdot(S, Pm[inst], _NN) + Sloc[inst]
    s_scr[...] = S
    out_ref[...] = _rwkv_post(y_scr[...], bonus, g, gng_ref[...], gnb_ref[...], ones)

    @pl.when(ti == pl.num_programs(2) - 1)
    def _emit_state():
        for p in range(n_pairs):
            s_out_ref[2 * p] = S[p, :RWKV_HEAD, :RWKV_HEAD]
            s_out_ref[2 * p + 1] = S[p, RWKV_HEAD:, RWKV_HEAD:]


def _rwkv_chunked(pa, shift0, W, *, tt, hw):
    B, T, _ = pa.shape
    width = W['decay_w0'].shape[1]
    heads = width // RWKV_HEAD
    assert T % tt == 0 and tt % RWKV_CHUNK == 0 and width % hw == 0 and hw % HEAD_PAIR == 0
    nb = width // hw
    xw = DECAY_LORA + AAA_LORA + GATE_LORA
    assert (3 * width) % xw == 0
    xblk = 3 * width // xw
    col = lambda off: (lambda b, h, t: (b, t, off + h))
    vec = lambda: pl.BlockSpec((1, hw), lambda b, h, t: (0, h))
    in_specs = [
        pl.BlockSpec((None, tt, hw), col(0)), pl.BlockSpec((None, tt, hw), col(nb)),
        pl.BlockSpec((None, tt, hw), col(2 * nb)), pl.BlockSpec((None, tt, xw), lambda b, h, t: (b, t, xblk)),
        pl.BlockSpec((None, 1, hw), lambda b, h, t: (b, 0, h)), pl.BlockSpec((None, 1, hw), lambda b, h, t: (b, 0, nb + h)),
        pl.BlockSpec((None, 1, hw), lambda b, h, t: (b, 0, 2 * nb + h)), pl.BlockSpec((None, 1, xw), lambda b, h, t: (b, 0, xblk)),
        vec(), vec(), vec(), _const_spec((1, xw)),
        vec(), pl.BlockSpec((DECAY_LORA, hw), lambda b, h, t: (0, h)),
        vec(), pl.BlockSpec((AAA_LORA, hw), lambda b, h, t: (0, h)),
        pl.BlockSpec((GATE_LORA, hw), lambda b, h, t: (0, h)),
        vec(), vec(), vec(), vec(), vec(), _const_spec((hw, hw)),
    ]
    out, s_new = pl.pallas_call(
        _rwkv_chunk_kernel,
        grid=(B, nb, T // tt),
        in_specs=in_specs,
        out_specs=[pl.BlockSpec((None, tt, hw), lambda b, h, t: (b, t, h)),
                   pl.BlockSpec((None, hw // RWKV_HEAD, RWKV_HEAD, RWKV_HEAD), lambda b, h, t: (b, h, 0, 0))],
        out_shape=[jax.ShapeDtypeStruct((B, T, width), F32),
                   jax.ShapeDtypeStruct((B, heads, RWKV_HEAD, RWKV_HEAD), F32)],
        scratch_shapes=[pltpu.VMEM((hw // HEAD_PAIR, HEAD_PAIR, HEAD_PAIR), F32),
                        pltpu.VMEM((1, hw), F32), pltpu.VMEM((1, hw), F32), pltpu.VMEM((1, hw), F32),
                        pltpu.VMEM((1, xw), F32), pltpu.VMEM((tt, hw), F32)],
        compiler_params=_cparams(("parallel", "parallel", "arbitrary")),
        name="rwkv_chunked",
    )(pa, pa, pa, pa, shift0, shift0, shift0, shift0,
      W['mu_r'], W['mu_k'], W['mu_v'], W['mu_x'], W['decay_w0'], W['decay_w2'], W['aaa_a0'], W['aaa_a2'],
      W['gate_g2'], W['k_k'], W['k_a'], W['r_k'], W['gn_g'], W['gn_b'], _head_ones(hw))
    return out, s_new


def _rwkv_step_kernel(p_ref, s0_ref, st_ref, mu_ref, w0_ref, w2_ref, a0_ref, a2_ref, g2_ref,
                      kk_ref, ka_ref, rk_ref, gng_ref, gnb_ref, ones_ref,
                      out_ref, st_out_ref, vec_scr, y_scr, bonus_scr, g_scr):
    i = pl.program_id(0)
    B, width = out_ref.shape
    hg = st_ref.shape[0]
    H = RWKV_HEAD

    @pl.when(i == 0)
    def _prologue():
        ones = ones_ref[...]
        p = p_ref[...]
        xs = p + (s0_ref[...] - p) * mu_ref[...]
        r, k, v, xx = xs[:, :width], xs[:, width:2 * width], xs[:, 2 * width:3 * width], xs[:, 3 * width:]
        lw, a, g, kk, kf, bonus = _rwkv_pre(r, k, v, xx, w0_ref[...], w2_ref[...], a0_ref[...], a2_ref[...],
                                            g2_ref[...], kk_ref[...], ka_ref[...], rk_ref[...], ones)
        for j, vec in enumerate((r, kf, v, kk, kk * a, jnp.exp(lw))):
            vec_scr[j] = vec.T
        bonus_scr[...] = bonus
        g_scr[...] = g

    for hl in range(hg):
        base = pl.multiple_of((i * hg + hl) * H, H)
        r_h, kf_h, kk_h, kka_h, w_h = (vec_scr[j, pl.ds(base, H), :] for j in (0, 1, 3, 4, 5))

        def body(vi, carry):
            S = st_ref[hl, vi]
            sa = jnp.sum(S * kk_h, axis=0, keepdims=True)
            v_row = vec_scr[2, pl.ds(base + vi, 1), :]
            S2 = S * w_h - sa * kka_h + v_row * kf_h
            st_out_ref[hl, vi] = S2
            y_scr[pl.ds(base + vi, 1), :] = jnp.sum(S2 * r_h, axis=0, keepdims=True)
            return carry

        lax.fori_loop(0, H, body, 0, unroll=8)

    @pl.when(i == pl.num_programs(0) - 1)
    def _epilogue():
        out_ref[...] = _rwkv_post(y_scr[...].T, bonus_scr[...], g_scr[...], gng_ref[...], gnb_ref[...],
                                  ones_ref[...])


def _rwkv_step(pa, shift0, state_t, W, *, hg):
    B, proj = pa.shape
    width = W['decay_w0'].shape[1]
    heads = width // RWKV_HEAD
    assert heads % hg == 0 and state_t.shape == (heads, RWKV_HEAD, RWKV_HEAD, B)
    st_spec = pl.BlockSpec((hg, RWKV_HEAD, RWKV_HEAD, B), lambda i: (i, 0, 0, 0))
    names = ('decay_w0', 'decay_w2', 'aaa_a0', 'aaa_a2', 'gate_g2', 'k_k', 'k_a', 'r_k', 'gn_g', 'gn_b')
    out, st_new = pl.pallas_call(
        _rwkv_step_kernel,
        grid=(heads // hg,),
        in_specs=[_const_spec((B, proj)), _const_spec((B, proj)), st_spec, _const_spec(W['shift_mu'].shape)]
                 + [_const_spec(W[n].shape) for n in names] + [_const_spec((width, width))],
        out_specs=[_const_spec((B, width)), st_spec],
        out_shape=[jax.ShapeDtypeStruct((B, width), F32), jax.ShapeDtypeStruct(state_t.shape, F32)],
        scratch_shapes=[pltpu.VMEM((6, width, B), F32), pltpu.VMEM((width, B), F32),
                        pltpu.VMEM((B, width), F32), pltpu.VMEM((B, width), F32)],
        compiler_params=_cparams(("arbitrary",)),
        name="rwkv_step",
    )(pa, shift0, state_t, W['shift_mu'], *[W[n] for n in names], _head_ones(width))
    return out, st_new


def _lru_coeffs(xc, wr_ref, br_ref, wi_ref, bi_ref, lam_ref):
    xcb = xc.astype(BF16)
    n_grp = xc.shape[1] // LRU_GROUP
    grp = lambda w_ref: jnp.concatenate(
        [jnp.dot(xcb[:, i * LRU_GROUP:(i + 1) * LRU_GROUP], w_ref[i], preferred_element_type=F32)
         for i in range(n_grp)], axis=1)
    gr = jax.nn.sigmoid(grp(wr_ref) + br_ref[...])
    gi = jax.nn.sigmoid(grp(wi_ref) + bi_ref[...])
    a_t = jnp.exp(-LRU_C * gr * _softplus(-lam_ref[...]))
    return a_t, jnp.sqrt(1.0 - a_t * a_t) * gi * xc


def _lru_finish(hs, pg, g0, g1, rw, h, wmix_ref, lng_ref, lnb_ref, alpha):
    lru_out = hs * jax.nn.gelu(pg)
    merged = jax.nn.sigmoid(g0) * rw + jax.nn.sigmoid(g1) * lru_out
    mix = jnp.dot(merged.astype(BF16), wmix_ref[...], preferred_element_type=F32)
    return _layer_norm(alpha * h + mix, lng_ref[...], lnb_ref[...])


def _lru_kernel(hn_ref, rw_ref, h_ref, buf_ref, h0_ref, win_ref,
                cw_ref, cb_ref, wr_ref, br_ref, wi_ref, bi_ref, lam_ref, wmix_ref, lng_ref, lnb_ref,
                out_ref, hlast_ref, tail_ref, proj_scr, tail_scr, hc_scr, *, alpha):
    ti = pl.program_id(1)
    flat = pl.program_id(0) * pl.num_programs(1) + ti
    tt, width = rw_ref.shape
    S8 = V7X_SUBLANES
    project = lambda ref: jnp.dot(ref[...].astype(BF16), win_ref[...], preferred_element_type=F32)
    col = lambda j: slice(j * width, (j + 1) * width)

    @pl.when(ti == 0)
    def _init():
        tail_scr[...] = buf_ref[...]
        hc_scr[...] = h0_ref[...]

    @pl.when(flat == 0)
    def _first_projection():
        proj_scr[0] = project(h_ref)

    def step(cur_scr, nxt_scr):
        hn = hn_ref[...].astype(BF16)
        piece_w = win_ref.shape[1] // LRU_PROJ_PIECES
        pieces = iter(range(LRU_PROJ_PIECES))

        def emit(count=1):
            for _ in range(count):
                k = next(pieces, None)
                if k is not None:
                    cols = slice(k * piece_w, (k + 1) * piece_w)
                    nxt_scr[:, cols] = jnp.dot(hn, win_ref[:, cols], preferred_element_type=F32)

        x = cur_scr[:, col(0)]
        tail = tail_scr[...]
        r8 = lax.broadcasted_iota(jnp.int32, (S8, width), 0)

        def delayed(d):
            head = jnp.where(r8 < d, pltpu.roll(tail, d, 0), pltpu.roll(x[:S8], d, 0))
            return jnp.concatenate([head, cur_scr[S8 - d:tt - d, col(0)]], axis=0)

        cw = cw_ref[...]
        conv = cw[0:1] * delayed(CONV_WIDTH - 1)
        for j in range(1, CONV_WIDTH - 1):
            conv = conv + cw[j:j + 1] * delayed(CONV_WIDTH - 1 - j)
        xc = cb_ref[...] + (conv + cw[CONV_WIDTH - 1:CONV_WIDTH] * x)
        tail_scr[...] = x[tt - S8:, :]
        tail_ref[...] = x[tt - S8:, :]
        emit()

        A, Bv = _lru_coeffs(xc, wr_ref, br_ref, wi_ref, bi_ref, lam_ref)
        emit()
        A = A.reshape(tt // S8, S8, width)
        Bv = Bv.reshape(tt // S8, S8, width)
        in_grp = lax.broadcasted_iota(jnp.int32, (S8, width), 0)
        s = 1
        while s < S8:
            keep = in_grp >= s
            Bv = Bv + A * jnp.where(keep, pltpu.roll(Bv, s, 1), 0.0)
            A = A * jnp.where(keep, pltpu.roll(A, s, 1), 1.0)
            s *= 2
            emit()
        carry = hc_scr[...]
        groups = []
        for gi in range(tt // S8):
            hg = Bv[gi] + A[gi] * carry
            groups.append(hg)
            carry = hg[S8 - 1:S8, :]
        hs = jnp.concatenate(groups, axis=0)
        hc_scr[...] = carry
        hlast_ref[...] = carry
        emit()
        lru_out = hs * jax.nn.gelu(cur_scr[:, col(1)])
        emit()
        merged = jax.nn.sigmoid(cur_scr[:, col(2)]) * rw_ref[...] + jax.nn.sigmoid(cur_scr[:, col(3)]) * lru_out
        emit()
        mix = jnp.dot(merged.astype(BF16), wmix_ref[...], preferred_element_type=F32)
        out_ref[...] = _layer_norm(alpha * h_ref[...] + mix, lng_ref[...], lnb_ref[...])
        emit(LRU_PROJ_PIECES)

    slot = flat % 2
    step(proj_scr.at[slot], proj_scr.at[1 - slot])


def _lru_step_kernel(x_ref, pg_ref, g0_ref, g1_ref, rw_ref, h_ref, buf_ref, h0_ref,
                     cw_ref, cb_ref, wr_ref, br_ref, wi_ref, bi_ref, lam_ref, wmix_ref, lng_ref, lnb_ref,
                     out_ref, hnew_ref, *, alpha):
    cw = cw_ref[...]
    conv = cw[0:1] * buf_ref[0]
    for j in range(1, CONV_WIDTH - 1):
        conv = conv + cw[j:j + 1] * buf_ref[j]
    xc = cb_ref[...] + (conv + cw[CONV_WIDTH - 1:CONV_WIDTH] * x_ref[...])
    A, Bv = _lru_coeffs(xc, wr_ref, br_ref, wi_ref, bi_ref, lam_ref)
    hs = Bv + A * h0_ref[...]
    hnew_ref[...] = hs
    out_ref[...] = _lru_finish(hs, pg_ref[...], g0_ref[...], g1_ref[...], rw_ref[...], h_ref[...],
                               wmix_ref, lng_ref, lnb_ref, alpha)


_LRU_WEIGHTS = ('conv_w', 'conv_b', 'lru_wr', 'lru_br', 'lru_wi', 'lru_bi', 'lru_lambda', 'w_mix_out')


def _lru_step_mix_ln(pb, rw, h, buf, h0, W, ln_g, ln_b, *, alpha, tm):
    B, width = rw.shape
    assert B % tm == 0
    blk = lambda j: pl.BlockSpec((tm, width), lambda i: (i, j))
    out, h_new = pl.pallas_call(
        functools.partial(_lru_step_kernel, alpha=alpha),
        grid=(B // tm,),
        in_specs=[blk(0), blk(1), blk(2), blk(3), blk(0), blk(0),
                  pl.BlockSpec((CONV_WIDTH - 1, tm, width), lambda i: (0, i, 0)), blk(0)]
                 + [_const_spec(W[n].shape) for n in _LRU_WEIGHTS] + [_const_spec(ln_g.shape), _const_spec(ln_b.shape)],
        out_specs=[blk(0), blk(0)],
        out_shape=[jax.ShapeDtypeStruct((B, width), F32), jax.ShapeDtypeStruct((B, width), F32)],
        compiler_params=_cparams(("parallel",)),
        name="lru_step_mix_ln",
    )(pb, pb, pb, pb, rw, h, buf, h0, *[W[n] for n in _LRU_WEIGHTS], ln_g, ln_b)
    return out, h_new


def _lru_mix_ln(h, rw, buf8, h0, W, ln_g, ln_b, *, alpha, tt):
    B, T, width = rw.shape
    assert T % tt == 0 and tt % V7X_SUBLANES == 0 and h.shape[2] == width
    nt = T // tt
    tile = pl.BlockSpec((None, tt, width), lambda b, t: (b, t, 0))

    def next_tile(b, t):
        f = jnp.minimum(b * nt + t + 1, B * nt - 1)
        return f // nt, f % nt, 0

    nxt = pl.BlockSpec((None, tt, width), next_tile)
    per_b = lambda rows: pl.BlockSpec((None, rows, width), lambda b, t: (b, 0, 0))
    win = W['w_in_b']
    out, h_last, tail = pl.pallas_call(
        functools.partial(_lru_kernel, alpha=alpha),
        grid=(B, nt),
        in_specs=[nxt, tile, tile, per_b(V7X_SUBLANES), per_b(1), _const_spec(win.shape)]
                 + [_const_spec(W[n].shape) for n in _LRU_WEIGHTS] + [_const_spec(ln_g.shape), _const_spec(ln_b.shape)],
        out_specs=[tile, per_b(1), per_b(V7X_SUBLANES)],
        out_shape=[jax.ShapeDtypeStruct((B, T, width), F32), jax.ShapeDtypeStruct((B, 1, width), F32),
                   jax.ShapeDtypeStruct((B, V7X_SUBLANES, width), F32)],
        scratch_shapes=[pltpu.VMEM((2, tt, win.shape[1]), F32), pltpu.VMEM((V7X_SUBLANES, width), F32),
                        pltpu.VMEM((1, width), F32)],
        compiler_params=_cparams(("arbitrary", "arbitrary")),
        name="lru_mix_ln",
    )(h, rw, h, buf8, h0, win, *[W[n] for n in _LRU_WEIGHTS], ln_g, ln_b)
    return out, h_last, tail


def _xattn_kernel(h_ref, kv_ref, wq_ref, wo_ref, g_ref, b_ref, o_ref, *, alpha, heads):
    tm, d = h_ref.shape
    hd = d // heads
    row_groups = max(1, tm // V7X_MXU_DIM)
    tg = tm // row_groups
    for gi in range(row_groups):
        rows = slice(gi * tg, (gi + 1) * tg)
        h = h_ref[rows, :]
        q = jnp.dot(h.astype(BF16), wq_ref[...], preferred_element_type=F32)
        outs = []
        for j in range(heads):
            sl = slice(j * hd, (j + 1) * hd)
            s = _dot_dims(q[:, sl], kv_ref[:, sl], _NT) * (hd ** -0.5)
            e = jnp.exp(s - jnp.max(s, axis=-1, keepdims=True))
            p = e * (1.0 / jnp.sum(e, axis=-1, keepdims=True))
            outs.append(_dot(p, kv_ref[:, d + j * hd:d + (j + 1) * hd]))
        out = jnp.dot(jnp.concatenate(outs, axis=1).astype(BF16), wo_ref[...], preferred_element_type=F32)
        o_ref[rows, :] = _layer_norm(alpha * h + out, g_ref[...], b_ref[...])


def _xattn_rows_kernel(h_ref, mk_ref, mv_ref, wq_ref, wo_ref, g_ref, b_ref, o_ref, *, alpha):
    h = h_ref[...]
    nb, m, heads, hd = mk_ref.shape
    q = jnp.dot(h.astype(BF16), wq_ref[...], preferred_element_type=F32)
    col = lax.broadcasted_iota(jnp.int32, (heads, m * heads), 1)
    own = (col % heads) == lax.broadcasted_iota(jnp.int32, (heads, m * heads), 0)
    rows = []
    for i in range(nb):
        k2 = mk_ref[i].reshape(m * heads, hd)
        v2 = mv_ref[i].reshape(m * heads, hd)
        q4 = jnp.concatenate([q[i:i + 1, j * hd:(j + 1) * hd] for j in range(heads)], axis=0)
        s = jnp.where(own, _dot_dims(q4, k2, _NT) * (hd ** -0.5), -jnp.inf)
        e = jnp.exp(s - jnp.max(s, axis=-1, keepdims=True))
        p = e / jnp.sum(e, axis=-1, keepdims=True)
        o4 = _dot(p, v2)
        rows.append(jnp.concatenate([o4[j:j + 1] for j in range(heads)], axis=1))
    out = jnp.dot(jnp.concatenate(rows, axis=0).astype(BF16), wo_ref[...], preferred_element_type=F32)
    o_ref[...] = _layer_norm(alpha * h + out, g_ref[...], b_ref[...])


def _xattn_ln(h, kv, wq, wo, g, b, *, alpha, heads, rows_per_batch, tile_rows):
    n, d = h.shape
    assert n % tile_rows == 0 and rows_per_batch % tile_rows == 0
    per = rows_per_batch // tile_rows
    m = kv.shape[0] // (n // rows_per_batch)
    return pl.pallas_call(
        functools.partial(_xattn_kernel, alpha=alpha, heads=heads),
        grid=(n // tile_rows,),
        in_specs=[pl.BlockSpec((tile_rows, d), lambda i: (i, 0)),
                  pl.BlockSpec((m, 2 * d), lambda i: (i // per, 0)),
                  _const_spec(wq.shape), _const_spec(wo.shape), _const_spec(g.shape), _const_spec(b.shape)],
        out_specs=pl.BlockSpec((tile_rows, d), lambda i: (i, 0)),
        out_shape=jax.ShapeDtypeStruct((n, d), F32),
        compiler_params=_cparams(("parallel",)),
        name="xattn_ln",
    )(h, kv, wq, wo, g, b)


def _xattn_rows_ln(h, mk, mv, wq, wo, g, b, *, alpha, nb):
    n, d = h.shape
    _, m, heads, hd = mk.shape
    assert n % nb == 0
    kv_spec = pl.BlockSpec((nb, m, heads, hd), lambda i: (i, 0, 0, 0))
    return pl.pallas_call(
        functools.partial(_xattn_rows_kernel, alpha=alpha),
        grid=(n // nb,),
        in_specs=[pl.BlockSpec((nb, d), lambda i: (i, 0)), kv_spec, kv_spec,
                  _const_spec(wq.shape), _const_spec(wo.shape), _const_spec(g.shape), _const_spec(b.shape)],
        out_specs=pl.BlockSpec((nb, d), lambda i: (i, 0)),
        out_shape=jax.ShapeDtypeStruct((n, d), F32),
        compiler_params=_cparams(("parallel",)),
        name="xattn_rows_ln",
    )(h, mk, mv, wq, wo, g, b)


def _kv_proj_kernel(x_ref, w_ref, kv_ref, k_ref, v_ref):
    heads, hd = k_ref.shape[1:]
    d = heads * hd
    kv = jnp.dot(x_ref[...].astype(BF16), w_ref[...], preferred_element_type=F32)
    kv_ref[...] = kv
    for j in range(heads):
        k_ref[:, j, :] = kv[:, j * hd:(j + 1) * hd]
        v_ref[:, j, :] = kv[:, d + j * hd:d + (j + 1) * hd]


def _kv_proj(x, wkv, *, heads, tm):
    n, d = x.shape
    assert n % tm == 0
    hd = d // heads
    out = jax.ShapeDtypeStruct((n, heads, hd), F32)
    return pl.pallas_call(
        _kv_proj_kernel,
        grid=(n // tm,),
        in_specs=[pl.BlockSpec((tm, d), lambda i: (i, 0)), _const_spec(wkv.shape)],
        out_specs=[pl.BlockSpec((tm, 2 * d), lambda i: (i, 0))] + [pl.BlockSpec((tm, heads, hd), lambda i: (i, 0, 0))] * 2,
        out_shape=[jax.ShapeDtypeStruct((n, 2 * d), F32), out, out],
        compiler_params=_cparams(("parallel",)),
        name="kv_proj",
    )(x, wkv)


def _row(v):
    return v.reshape(1, -1)


def _block_diag_groups(w):
    n, c, _ = w.shape
    per = LRU_GROUP // c
    w4 = w.reshape(n // per, per, c, c)
    bd = jnp.einsum('gjcd,jk->gjckd', w4, jnp.eye(per, dtype=w.dtype))
    return bd.reshape(n // per, LRU_GROUP, LRU_GROUP).astype(BF16)


def _prep_layer(l, ln_g, ln_b, ffn1_wi, ffn1_wo, ffn2_wi, ffn2_wo, w_in, shift_mu, decay_w0, decay_w2,
                aaa_a0, aaa_a2, gate_g2, k_k, k_a, r_k, gn_g, gn_b, conv_w, conv_b, lru_wr, lru_br,
                lru_wi, lru_bi, lru_lambda, w_mix_out, xa_wq, xa_wk, xa_wv, xa_wo):
    width = decay_w0.shape[1]
    rp = shift_mu.shape[1]
    d_ff = ffn1_wo.shape[1]
    bf = lambda w: w.astype(BF16)
    mu = shift_mu[l]
    return dict(
        ln_g=[_row(ln_g[l, i]) for i in range(4)], ln_b=[_row(ln_b[l, i]) for i in range(4)],
        ffn1=(bf(ffn1_wi[l][:, :d_ff]), bf(ffn1_wi[l][:, d_ff:]), bf(ffn1_wo[l])),
        ffn2=(bf(ffn2_wi[l][:, :d_ff]), bf(ffn2_wi[l][:, d_ff:]), bf(ffn2_wo[l])),
        w_in_a=bf(w_in[l][:, :rp]), w_in_b=bf(w_in[l][:, rp:]),
        shift_mu=_row(mu), mu_r=_row(mu[:width]), mu_k=_row(mu[width:2 * width]),
        mu_v=_row(mu[2 * width:3 * width]), mu_x=_row(mu[3 * width:]),
        decay_w0=_row(decay_w0[l]), decay_w2=bf(decay_w2[l]), aaa_a0=_row(aaa_a0[l]), aaa_a2=bf(aaa_a2[l]),
        gate_g2=bf(gate_g2[l]), k_k=_row(k_k[l]), k_a=_row(k_a[l]), r_k=_row(r_k[l]),
        gn_g=_row(gn_g[l]), gn_b=_row(gn_b[l]),
        conv_w=conv_w[l], conv_b=_row(conv_b[l]),
        lru_wr=_block_diag_groups(lru_wr[l]), lru_br=_row(lru_br[l]),
        lru_wi=_block_diag_groups(lru_wi[l]), lru_bi=_row(lru_bi[l]), lru_lambda=_row(lru_lambda[l]),
        w_mix_out=bf(w_mix_out[l]), xa_wq=bf(xa_wq[l]), xa_wo=bf(xa_wo[l]),
        xa_wkv=bf(jnp.concatenate([xa_wk[l], xa_wv[l]], axis=1)),
    )


def _tile(n, pref):
    return pref if n % pref == 0 else n


def _layer(h, mem, state, shift0, h0, buf0, W, *, alpha, xa_heads):
    B, T, d = h.shape
    n = B * T
    tm = _tile(n, 1024)
    h1 = _ffn_ln(h.reshape(n, d), *W['ffn1'], W['ln_g'][0], W['ln_b'][0], alpha=alpha, tm=tm)
    pa = _matmul(h1, W['w_in_a'], tm=tm, tn=W['w_in_a'].shape[1])
    width = W['decay_w0'].shape[1]
    lru_w = W['conv_b'].shape[1]
    pa3 = pa.reshape(B, T, -1)
    hist = CONV_WIDTH - 1
    if T > 1:
        assert state is None and T >= V7X_SUBLANES
        rw, s_new = _rwkv_chunked(pa3, shift0.reshape(B, 1, -1), W, tt=_tile(T, 512), hw=_tile(width, 512))
        buf8 = jnp.concatenate([jnp.zeros((B, V7X_SUBLANES - hist, lru_w), F32), buf0], axis=1)
        h2, h_last, tail8 = _lru_mix_ln(h1.reshape(B, T, d), rw, buf8, h0.reshape(B, 1, lru_w), W,
                                        W['ln_g'][1], W['ln_b'][1], alpha=alpha, tt=_tile(T, 256))
        conv_in_tail = tail8[:, V7X_SUBLANES - hist:]
    else:
        pb = _matmul(h1, W['w_in_b'], tm=tm, tn=W['w_in_b'].shape[1])
        rw, s_new = _rwkv_step(pa, shift0, jnp.transpose(state, (1, 2, 3, 0)), W, hg=2)
        s_new = jnp.transpose(s_new, (3, 0, 1, 2))
        h2, h_last = _lru_step_mix_ln(pb, rw, h1, jnp.swapaxes(buf0, 0, 1), h0, W,
                                      W['ln_g'][1], W['ln_b'][1], alpha=alpha, tm=_tile(B, 128))
        conv_in_tail = jnp.concatenate([buf0[:, T:], pb[:, None, :lru_w]], axis=1)
    xa = (W['xa_wq'], W['xa_wo'], W['ln_g'][2], W['ln_b'][2])
    if T > 1:
        h3 = _xattn_ln(h2.reshape(n, d), mem, *xa, alpha=alpha, heads=xa_heads, rows_per_batch=T,
                       tile_rows=_tile(T, 1024))
    else:
        h3 = _xattn_rows_ln(h2, *mem, *xa, alpha=alpha, nb=_tile(B, V7X_SUBLANES))
    h4 = _ffn_ln(h3, *W['ffn2'], W['ln_g'][3], W['ln_b'][3], alpha=alpha, tm=tm)
    return h4.reshape(B, T, d), s_new, pa3[:, -1], h_last.reshape(B, lru_w), conv_in_tail


def kernel(x_prompt, x_sample, mem_prompt, cache_mem_k, cache_mem_v, state_rwkv, state_rwkv_shift, state_lru, state_conv, ln_g, ln_b, ffn1_wi, ffn1_wo, ffn2_wi, ffn2_wo, w_in, shift_mu, decay_w0, decay_w2, aaa_a0, aaa_a2, gate_g2, k_k, k_a, r_k, gn_g, gn_b, conv_w, conv_b, lru_wr, lru_br, lru_wi, lru_bi, lru_lambda, w_mix_out, xa_wq, xa_wk, xa_wv, xa_wo):
    depth = ln_g.shape[0]
    alpha = (2.0 * depth) ** 0.25
    B, _, d = x_prompt.shape
    n_mem, xa_heads, xa_head = cache_mem_k.shape[2:]
    rp = shift_mu.shape[1]
    lru_w = conv_b.shape[1]
    hp, hs = x_prompt, x_sample
    outs = [[] for _ in range(10)]
    for l in range(depth):
        W = _prep_layer(l, ln_g, ln_b, ffn1_wi, ffn1_wo, ffn2_wi, ffn2_wo, w_in, shift_mu, decay_w0, decay_w2,
                        aaa_a0, aaa_a2, gate_g2, k_k, k_a, r_k.reshape(depth, -1), gn_g, gn_b, conv_w, conv_b,
                        lru_wr, lru_br, lru_wi, lru_bi, lru_lambda, w_mix_out, xa_wq, xa_wk, xa_wv, xa_wo)
        kv, mk, mv = _kv_proj(mem_prompt.reshape(B * n_mem, d), W['xa_wkv'], heads=xa_heads,
                              tm=_tile(B * n_mem, 512))
        mk = mk.reshape(B, n_mem, xa_heads, xa_head)
        mv = mv.reshape(B, n_mem, xa_heads, xa_head)
        hp, S1, sh1, h1, b1 = _layer(
            hp, kv, None, jnp.zeros((B, rp), F32), jnp.zeros((B, lru_w), F32),
            jnp.zeros((B, CONV_WIDTH - 1, lru_w), F32), W, alpha=alpha, xa_heads=xa_heads)
        hs, S2, sh2, h2, b2 = _layer(
            hs, (cache_mem_k[l], cache_mem_v[l]),
            state_rwkv[l], state_rwkv_shift[l], state_lru[l], state_conv[l], W, alpha=alpha, xa_heads=xa_heads)
        for lst, val in zip(outs, (mk, mv, S1, sh1, h1, b1, S2, sh2, h2, b2)):
            lst.append(val)
    return (hp, hs) + tuple(jnp.stack(o) for o in outs)
```

```python
import functools

import jax
import jax.numpy as jnp
from jax import lax
from jax.experimental import pallas as pl
from jax.experimental.pallas import tpu as pltpu

F32 = jnp.float32
BF16 = jnp.bfloat16

RWKV_HEAD = 64
DECAY_LORA = 64
AAA_LORA = 64
GATE_LORA = 128
GN_EPS = 64e-5
CONV_WIDTH = 4
LRU_C = 8.0
LN_EPS = 1e-5

V7X_SUBLANES = 8
V7X_MXU_DIM = 256
V7X_SCOPED_VMEM_BYTES = 60000 * 1024

RWKV_CHUNK = 64
HEAD_PAIR = 2 * RWKV_HEAD
SCAN_ROW_GROUPS = 2
LRU_GROUP = V7X_MXU_DIM
LRU_PROJ_PIECES = 8


def _cparams(semantics):
    return pltpu.CompilerParams(dimension_semantics=semantics, vmem_limit_bytes=V7X_SCOPED_VMEM_BYTES)


def _const_spec(shape):
    zeros = (0,) * len(shape)
    return pl.BlockSpec(shape, lambda *_: zeros)


def _dot(a, b):
    return jnp.dot(a.astype(BF16), b.astype(BF16), preferred_element_type=F32)


def _dot_dims(a, b, dims):
    return lax.dot_general(a.astype(BF16), b.astype(BF16), (dims, ((), ())), preferred_element_type=F32)


_NN = ((1,), (0,))
_NT = ((1,), (1,))
_TN = ((0,), (0,))


def _split2(x):
    hi = x.astype(BF16)
    lo = (x - hi.astype(F32)).astype(BF16)
    return hi, lo


def _dot_exact_lhs(a_bf16, b):
    hi, lo = _split2(b)
    d = lambda y: jnp.dot(a_bf16, y, preferred_element_type=F32)
    return d(hi) + d(lo)


def _dot_exact_rhs(a, b_bf16):
    hi, lo = _split2(a)
    d = lambda x: jnp.dot(x, b_bf16, preferred_element_type=F32)
    return d(hi) + d(lo)


def _layer_norm(x, g, b):
    mu = jnp.mean(x, axis=-1, keepdims=True)
    xc = x - mu
    var = jnp.mean(xc * xc, axis=-1, keepdims=True)
    return xc * lax.rsqrt(var + LN_EPS) * g + b


def _softplus(z):
    return jnp.maximum(z, 0.0) + jnp.log(1.0 + jnp.exp(-jnp.abs(z)))


def _head_ones(width):
    r = lax.broadcasted_iota(jnp.int32, (width, width), 0) // RWKV_HEAD
    c = lax.broadcasted_iota(jnp.int32, (width, width), 1) // RWKV_HEAD
    return (r == c).astype(BF16)


def _mm_kernel(x_ref, w_ref, o_ref):
    o_ref[...] = jnp.dot(x_ref[...].astype(BF16), w_ref[...], preferred_element_type=F32)


def _matmul(x, w, *, tm, tn):
    n, k = x.shape
    m = w.shape[1]
    assert n % tm == 0 and m % tn == 0
    return pl.pallas_call(
        _mm_kernel,
        grid=(m // tn, n // tm),
        in_specs=[pl.BlockSpec((tm, k), lambda j, i: (i, 0)),
                  pl.BlockSpec((k, tn), lambda j, i: (0, j))],
        out_specs=pl.BlockSpec((tm, tn), lambda j, i: (i, j)),
        out_shape=jax.ShapeDtypeStruct((n, m), F32),
        compiler_params=_cparams(("parallel", "parallel")),
        name="matmul",
    )(x, w)


def _ffn_kernel(x_ref, wg_ref, wu_ref, wo_ref, g_ref, b_ref, o_ref, *, alpha, row_groups):
    tm = x_ref.shape[0]
    tg = tm // row_groups
    for gi in range(row_groups):
        rows = slice(gi * tg, (gi + 1) * tg)
        x = x_ref[rows, :]
        xb = x.astype(BF16)
        gate = jnp.dot(xb, wg_ref[...], preferred_element_type=F32)
        up = jnp.dot(xb, wu_ref[...], preferred_element_type=F32)
        mid = (gate * jax.nn.sigmoid(gate) * up).astype(BF16)
        down = jnp.dot(mid, wo_ref[...], preferred_element_type=F32)
        o_ref[rows, :] = _layer_norm(alpha * x + 0.5 * down, g_ref[...], b_ref[...])


def _ffn_ln(x, wg, wu, wo, g, b, *, alpha, tm):
    n, d = x.shape
    assert n % tm == 0
    row_groups = max(1, tm // V7X_MXU_DIM)
    resident = lambda w: pl.BlockSpec(w.shape, lambda i: (0, 0), pipeline_mode=pl.Buffered(1))
    return pl.pallas_call(
        functools.partial(_ffn_kernel, alpha=alpha, row_groups=row_groups),
        grid=(n // tm,),
        in_specs=[pl.BlockSpec((tm, d), lambda i: (i, 0)),
                  resident(wg), resident(wu), resident(wo),
                  _const_spec(g.shape), _const_spec(b.shape)],
        out_specs=pl.BlockSpec((tm, d), lambda i: (i, 0)),
        out_shape=jax.ShapeDtypeStruct((n, d), F32),
        compiler_params=_cparams(("parallel",)),
        name="ffn_ln",
    )(x, wg, wu, wo, g, b)


def _rwkv_pre(r, k, v, xx, w0, w2, a0, a2, g2, k_k, k_a, r_k, ones):
    xw = xx[:, :DECAY_LORA]
    xa = xx[:, DECAY_LORA:DECAY_LORA + AAA_LORA]
    xg = xx[:, DECAY_LORA + AAA_LORA:]
    z = w0 + _dot(jnp.tanh(xw), w2)
    lw = -jnp.exp(-_softplus(-z) - 0.5)
    a = jax.nn.sigmoid(a0 + _dot(xa, a2))
    g = _dot(jax.nn.sigmoid(xg), g2)
    kkr = k * k_k
    ss = _dot(kkr * kkr, ones)
    kk = kkr * lax.rsqrt(jnp.maximum(ss, 1e-24))
    kf = k * (1.0 + (a - 1.0) * k_a)
    bonus = _dot(r * kf * r_k, ones) * v
    return lw, a, g, kk, kf, bonus


def _rwkv_post(y, bonus, g, gn_g, gn_b, ones):
    inv_n = 1.0 / RWKV_HEAD
    ym = _dot_exact_rhs(y, ones) * inv_n
    yc = y - ym
    yv = _dot(yc * yc, ones) * inv_n
    yn = yc * lax.rsqrt(yv + GN_EPS) * gn_g + gn_b
    return (yn + bonus) * g


def _bdot(a, b, dims):
    dn = ((tuple(d + 1 for d in dims[0]), tuple(d + 1 for d in dims[1])), ((0,), (0,)))
    return lax.dot_general(a.astype(BF16), b.astype(BF16), dn, preferred_element_type=F32)


def _scan_operands(r, kf, v, kk, a, lw):
    tt, hw = r.shape
    C = RWKV_CHUNK
    n_pairs = hw // HEAD_PAIR
    ltri = (lax.broadcasted_iota(jnp.int32, (C, C), 0) >= lax.broadcasted_iota(jnp.int32, (C, C), 1)).astype(BF16)
    first = lax.broadcasted_iota(jnp.int32, (C, HEAD_PAIR), 1) < RWKV_HEAD

    def bd(x):
        return jnp.concatenate([jnp.where(first, x, 0.0), jnp.where(first, 0.0, x)], axis=1)

    names = ('a', 'r', 'b', 'k', 'v', 'bh', 'kh')
    ops = {n: [] for n in names}
    wcs = []
    for c in range(tt // C):
        rows = slice(c * C, (c + 1) * C)
        lw_c = lw[rows]
        L = _dot_exact_lhs(ltri, lw_c)
        Lc = L[C - 1:C, :]
        e_nl = jnp.exp(-L)
        e_c = jnp.exp(Lc - L)
        bb = kk[rows] * a[rows]
        bf = lambda x: x.astype(BF16)
        tile = dict(a=bf(-kk[rows] * jnp.exp(L - lw_c)), r=r[rows] * jnp.exp(L), b=bf(bb * e_nl),
                    k=bf(kf[rows] * e_nl), v=bf(v[rows]), bh=bf(bb * e_c), kh=bf(kf[rows] * e_c))
        wc = jnp.exp(Lc)
        for p in range(n_pairs):
            lanes = slice(p * HEAD_PAIR, (p + 1) * HEAD_PAIR)
            for n in names:
                ops[n].append(tile[n][:, lanes])
            wcs.append(wc[:, lanes])
    A, R, B, K, V, Bh, Kh = (jnp.stack(ops[n]) for n in names)
    Vbd = bd(V)
    G = _bdot(jnp.concatenate([A, R.astype(BF16)], axis=1), jnp.concatenate([bd(B), bd(K)], axis=1), _NT)
    tok = lax.broadcasted_iota(jnp.int32, (C, HEAD_PAIR), 0)
    src = lax.broadcasted_iota(jnp.int32, (C, HEAD_PAIR), 1) % RWKV_HEAD
    a_ab = jnp.where(tok > src, G[:, :C, :HEAD_PAIR], 0.0)
    a_ak = jnp.where(tok > src, G[:, :C, HEAD_PAIR:], 0.0).astype(BF16)
    a_rb = jnp.where(tok >= src, G[:, C:, :HEAD_PAIR], 0.0).astype(BF16)
    a_rk = jnp.where(tok >= src, G[:, C:, HEAD_PAIR:], 0.0).astype(BF16)
    P = jnp.where(tok == src, 1.0, 0.0) + a_ab
    N = a_ab.astype(BF16)
    N = _bdot(N, bd(N), _NN).astype(BF16)
    steps = 2
    while 2 * steps < C:
        NP = _bdot(jnp.concatenate([N, P.astype(BF16)], axis=1), bd(N), _NN)
        N = NP[:, :C].astype(BF16)
        P = P + NP[:, C:]
        steps *= 2
    P = (P + _bdot(P, bd(N), _NN)).astype(BF16)
    aV = _bdot(a_ak, Vbd, _NN).astype(BF16)
    XU = _bdot(P, jnp.concatenate([bd(A), bd(aV)], axis=2), _NN).astype(BF16)
    X1 = XU[:, :, :HEAD_PAIR]
    Uloc = XU[:, :, HEAD_PAIR:]
    Q = (R + _bdot(a_rb, bd(X1), _NN)).astype(BF16)
    Yloc = _bdot(jnp.concatenate([a_rb, a_rk], axis=2), jnp.concatenate([bd(Uloc), Vbd], axis=1), _NN)
    ri = lax.broadcasted_iota(jnp.int32, (HEAD_PAIR, HEAD_PAIR), 0) // RWKV_HEAD
    ci = lax.broadcasted_iota(jnp.int32, (HEAD_PAIR, HEAD_PAIR), 1) // RWKV_HEAD
    same_head = ri == ci
    Pm = jnp.where(same_head, _bdot(X1, Bh, _TN), 0.0).astype(BF16)
    Sloc = jnp.where(same_head, _bdot(jnp.concatenate([Uloc, V], axis=1), jnp.concatenate([Bh, Kh], axis=1), _TN),
                     0.0)
    return Q, Yloc, Pm, Sloc, jnp.stack(wcs)


def _rwkv_chunk_kernel(pr_ref, pk_ref, pv_ref, px_ref, sr_ref, sk_ref, sv_ref, sx_ref,
                       mur_ref, muk_ref, muv_ref, mux_ref, w0_ref, w2_ref, a0_ref, a2_ref, g2_ref,
                       kk_ref, ka_ref, rk_ref, gng_ref, gnb_ref, ones_ref,
                       out_ref, s_out_ref,
                       s_scr, cr_scr, ck_scr, cv_scr, cx_scr, y_scr):
    ti = pl.program_id(2)
    tt, hw = pr_ref.shape
    n_pairs = hw // HEAD_PAIR

    @pl.when(ti == 0)
    def _init():
        s_scr[...] = jnp.zeros(s_scr.shape, F32)
        cr_scr[...] = sr_ref[...]
        ck_scr[...] = sk_ref[...]
        cv_scr[...] = sv_ref[...]
        cx_scr[...] = sx_ref[...]

    S8 = V7X_SUBLANES

    def shifted(p_ref, c_scr, mu_ref):
        p = p_ref[...]
        first = lax.broadcasted_iota(jnp.int32, (S8, p.shape[1]), 0) == 0
        head = jnp.where(first, c_scr[...], pltpu.roll(p[:S8], 1, 0))
        prev = jnp.concatenate([head, p_ref[S8 - 1:tt - 1, :]], axis=0)
        c_scr[...] = p_ref[tt - 1:tt, :]
        return p + (prev - p) * mu_ref[...]

    r = shifted(pr_ref, cr_scr, mur_ref)
    k = shifted(pk_ref, ck_scr, muk_ref)
    v = shifted(pv_ref, cv_scr, muv_ref)
    xx = shifted(px_ref, cx_scr, mux_ref)
    ones = ones_ref[...]
    lw, a, g, kk, kf, bonus = _rwkv_pre(r, k, v, xx, w0_ref[...], w2_ref[...], a0_ref[...], a2_ref[...],
                                        g2_ref[...], kk_ref[...], ka_ref[...], rk_ref[...], ones)
    C = RWKV_CHUNK
    S = s_scr[...]
    n_groups = SCAN_ROW_GROUPS if tt % (SCAN_ROW_GROUPS * C) == 0 else 1
    tg = tt // n_groups
    for gi in range(n_groups):
        rows = slice(gi * tg, (gi + 1) * tg)
        Q, Yloc, Pm, Sloc, wc = _scan_operands(r[rows], kf[rows], v[rows], kk[rows], a[rows], lw[rows])
        for c in range(tg // C):
            inst = slice(c * n_pairs, (c + 1) * n_pairs)
            y_c = _bdot(Q[inst], S, _NT) + Yloc[inst]
            row0 = gi * tg + c * C
            for p in range(n_pairs):
                y_scr[row0:row0 + C, p * HEAD_PAIR:(p + 1) * HEAD_PAIR] = y_c[p]
            S = S * wc[inst] + _bdot(S, Pm[inst], _NN) + Sloc[inst]
    s_scr[...] = S
    out_ref[...] = _rwkv_post(y_scr[...], bonus, g, gng_ref[...], gnb_ref[...], ones)

    @pl.when(ti == pl.num_programs(2) - 1)
    def _emit_state():
        for p in range(n_pairs):
            s_out_ref[2 * p] = S[p, :RWKV_HEAD, :RWKV_HEAD]
            s_out_ref[2 * p + 1] = S[p, RWKV_HEAD:, RWKV_HEAD:]


def _rwkv_chunked(pa, shift0, W, *, tt, hw):
    B, T, _ = pa.shape
    width = W['decay_w0'].shape[1]
    heads = width // RWKV_HEAD
    assert T % tt == 0 and tt % RWKV_CHUNK == 0 and width % hw == 0 and hw % HEAD_PAIR == 0
    nb = width // hw
    xw = DECAY_LORA + AAA_LORA + GATE_LORA
    assert (3 * width) % xw == 0
    xblk = 3 * width // xw
    col = lambda off: (lambda b, h, t: (b, t, off + h))
    vec = lambda: pl.BlockSpec((1, hw), lambda b, h, t: (0, h))
    in_specs = [
        pl.BlockSpec((None, tt, hw), col(0)), pl.BlockSpec((None, tt, hw), col(nb)),
        pl.BlockSpec((None, tt, hw), col(2 * nb)), pl.BlockSpec((None, tt, xw), lambda b, h, t: (b, t, xblk)),
        pl.BlockSpec((None, 1, hw), lambda b, h, t: (b, 0, h)), pl.BlockSpec((None, 1, hw), lambda b, h, t: (b, 0, nb + h)),
        pl.BlockSpec((None, 1, hw), lambda b, h, t: (b, 0, 2 * nb + h)), pl.BlockSpec((None, 1, xw), lambda b, h, t: (b, 0, xblk)),
        vec(), vec(), vec(), _const_spec((1, xw)),
        vec(), pl.BlockSpec((DECAY_LORA, hw), lambda b, h, t: (0, h)),
        vec(), pl.BlockSpec((AAA_LORA, hw), lambda b, h, t: (0, h)),
        pl.BlockSpec((GATE_LORA, hw), lambda b, h, t: (0, h)),
        vec(), vec(), vec(), vec(), vec(), _const_spec((hw, hw)),
    ]
    out, s_new = pl.pallas_call(
        _rwkv_chunk_kernel,
        grid=(B, nb, T // tt),
        in_specs=in_specs,
        out_specs=[pl.BlockSpec((None, tt, hw), lambda b, h, t: (b, t, h)),
                   pl.BlockSpec((None, hw // RWKV_HEAD, RWKV_HEAD, RWKV_HEAD), lambda b, h, t: (b, h, 0, 0))],
        out_shape=[jax.ShapeDtypeStruct((B, T, width), F32),
                   jax.ShapeDtypeStruct((B, heads, RWKV_HEAD, RWKV_HEAD), F32)],
        scratch_shapes=[pltpu.VMEM((hw // HEAD_PAIR, HEAD_PAIR, HEAD_PAIR), F32),
                        pltpu.VMEM((1, hw), F32), pltpu.VMEM((1, hw), F32), pltpu.VMEM((1, hw), F32),
                        pltpu.VMEM((1, xw), F32), pltpu.VMEM((tt, hw), F32)],
        compiler_params=_cparams(("parallel", "parallel", "arbitrary")),
        name="rwkv_chunked",
    )(pa, pa, pa, pa, shift0, shift0, shift0, shift0,
      W['mu_r'], W['mu_k'], W['mu_v'], W['mu_x'], W['decay_w0'], W['decay_w2'], W['aaa_a0'], W['aaa_a2'],
      W['gate_g2'], W['k_k'], W['k_a'], W['r_k'], W['gn_g'], W['gn_b'], _head_ones(hw))
    return out, s_new


def _rwkv_step_kernel(p_ref, s0_ref, st_ref, mu_ref, w0_ref, w2_ref, a0_ref, a2_ref, g2_ref,
                      kk_ref, ka_ref, rk_ref, gng_ref, gnb_ref, ones_ref,
                      out_ref, st_out_ref, vec_scr, y_scr, bonus_scr, g_scr):
    i = pl.program_id(0)
    B, width = out_ref.shape
    hg = st_ref.shape[0]
    H = RWKV_HEAD

    @pl.when(i == 0)
    def _prologue():
        ones = ones_ref[...]
        p = p_ref[...]
        xs = p + (s0_ref[...] - p) * mu_ref[...]
        r, k, v, xx = xs[:, :width], xs[:, width:2 * width], xs[:, 2 * width:3 * width], xs[:, 3 * width:]
        lw, a, g, kk, kf, bonus = _rwkv_pre(r, k, v, xx, w0_ref[...], w2_ref[...], a0_ref[...], a2_ref[...],
                                            g2_ref[...], kk_ref[...], ka_ref[...], rk_ref[...], ones)
        for j, vec in enumerate((r, kf, v, kk, kk * a, jnp.exp(lw))):
            vec_scr[j] = vec.T
        bonus_scr[...] = bonus
        g_scr[...] = g

    for hl in range(hg):
        base = pl.multiple_of((i * hg + hl) * H, H)
        r_h, kf_h, kk_h, kka_h, w_h = (vec_scr[j, pl.ds(base, H), :] for j in (0, 1, 3, 4, 5))

        def body(vi, carry):
            S = st_ref[hl, vi]
            sa = jnp.sum(S * kk_h, axis=0, keepdims=True)
            v_row = vec_scr[2, pl.ds(base + vi, 1), :]
            S2 = S * w_h - sa * kka_h + v_row * kf_h
            st_out_ref[hl, vi] = S2
            y_scr[pl.ds(base + vi, 1), :] = jnp.sum(S2 * r_h, axis=0, keepdims=True)
            return carry

        lax.fori_loop(0, H, body, 0, unroll=8)

    @pl.when(i == pl.num_programs(0) - 1)
    def _epilogue():
        out_ref[...] = _rwkv_post(y_scr[...].T, bonus_scr[...], g_scr[...], gng_ref[...], gnb_ref[...],
                                  ones_ref[...])


def _rwkv_step(pa, shift0, state_t, W, *, hg):
    B, proj = pa.shape
    width = W['decay_w0'].shape[1]
    heads = width // RWKV_HEAD
    assert heads % hg == 0 and state_t.shape == (heads, RWKV_HEAD, RWKV_HEAD, B)
    st_spec = pl.BlockSpec((hg, RWKV_HEAD, RWKV_HEAD, B), lambda i: (i, 0, 0, 0))
    names = ('decay_w0', 'decay_w2', 'aaa_a0', 'aaa_a2', 'gate_g2', 'k_k', 'k_a', 'r_k', 'gn_g', 'gn_b')
    out, st_new = pl.pallas_call(
        _rwkv_step_kernel,
        grid=(heads // hg,),
        in_specs=[_const_spec((B, proj)), _const_spec((B, proj)), st_spec, _const_spec(W['shift_mu'].shape)]
                 + [_const_spec(W[n].shape) for n in names] + [_const_spec((width, width))],
        out_specs=[_const_spec((B, width)), st_spec],
        out_shape=[jax.ShapeDtypeStruct((B, width), F32), jax.ShapeDtypeStruct(state_t.shape, F32)],
        scratch_shapes=[pltpu.VMEM((6, width, B), F32), pltpu.VMEM((width, B), F32),
                        pltpu.VMEM((B, width), F32), pltpu.VMEM((B, width), F32)],
        compiler_params=_cparams(("arbitrary",)),
        name="rwkv_step",
    )(pa, shift0, state_t, W['shift_mu'], *[W[n] for n in names], _head_ones(width))
    return out, st_new


def _lru_coeffs(xc, wr_ref, br_ref, wi_ref, bi_ref, lam_ref):
    xcb = xc.astype(BF16)
    n_grp = xc.shape[1] // LRU_GROUP
    grp = lambda w_ref: jnp.concatenate(
        [jnp.dot(xcb[:, i * LRU_GROUP:(i + 1) * LRU_GROUP], w_ref[i], preferred_element_type=F32)
         for i in range(n_grp)], axis=1)
    gr = jax.nn.sigmoid(grp(wr_ref) + br_ref[...])
    gi = jax.nn.sigmoid(grp(wi_ref) + bi_ref[...])
    a_t = jnp.exp(-LRU_C * gr * _softplus(-lam_ref[...]))
    return a_t, jnp.sqrt(1.0 - a_t * a_t) * gi * xc


def _lru_finish(hs, pg, g0, g1, rw, h, wmix_ref, lng_ref, lnb_ref, alpha):
    lru_out = hs * jax.nn.gelu(pg)
    merged = jax.nn.sigmoid(g0) * rw + jax.nn.sigmoid(g1) * lru_out
    mix = jnp.dot(merged.astype(BF16), wmix_ref[...], preferred_element_type=F32)
    return _layer_norm(alpha * h + mix, lng_ref[...], lnb_ref[...])


def _lru_kernel(hn_ref, rw_ref, h_ref, buf_ref, h0_ref, win_ref,
                cw_ref, cb_ref, wr_ref, br_ref, wi_ref, bi_ref, lam_ref, wmix_ref, lng_ref, lnb_ref,
                out_ref, hlast_ref, tail_ref, proj_scr, tail_scr, hc_scr, *, alpha):
    ti = pl.program_id(1)
    flat = pl.program_id(0) * pl.num_programs(1) + ti
    tt, width = rw_ref.shape
    S8 = V7X_SUBLANES
    project = lambda ref: jnp.dot(ref[...].astype(BF16), win_ref[...], preferred_element_type=F32)
    col = lambda j: slice(j * width, (j + 1) * width)

    @pl.when(ti == 0)
    def _init():
        tail_scr[...] = buf_ref[...]
        hc_scr[...] = h0_ref[...]

    @pl.when(flat == 0)
    def _first_projection():
        proj_scr[0] = project(h_ref)

    def step(cur_scr, nxt_scr):
        hn = hn_ref[...].astype(BF16)
        piece_w = win_ref.shape[1] // LRU_PROJ_PIECES
        pieces = iter(range(LRU_PROJ_PIECES))

        def emit(count=1):
            for _ in range(count):
                k = next(pieces, None)
                if k is not None:
                    cols = slice(k * piece_w, (k + 1) * piece_w)
                    nxt_scr[:, cols] = jnp.dot(hn, win_ref[:, cols], preferred_element_type=F32)

        x = cur_scr[:, col(0)]
        tail = tail_scr[...]
        r8 = lax.broadcasted_iota(jnp.int32, (S8, width), 0)

        def delayed(d):
            head = jnp.where(r8 < d, pltpu.roll(tail, d, 0), pltpu.roll(x[:S8], d, 0))
            return jnp.concatenate([head, cur_scr[S8 - d:tt - d, col(0)]], axis=0)

        cw = cw_ref[...]
        conv = cw[0:1] * delayed(CONV_WIDTH - 1)
        for j in range(1, CONV_WIDTH - 1):
            conv = conv + cw[j:j + 1] * delayed(CONV_WIDTH - 1 - j)
        xc = cb_ref[...] + (conv + cw[CONV_WIDTH - 1:CONV_WIDTH] * x)
        tail_scr[...] = x[tt - S8:, :]
        tail_ref[...] = x[tt - S8:, :]
        emit()

        A, Bv = _lru_coeffs(xc, wr_ref, br_ref, wi_ref, bi_ref, lam_ref)
        emit()
        A = A.reshape(tt // S8, S8, width)
        Bv = Bv.reshape(tt // S8, S8, width)
        in_grp = lax.broadcasted_iota(jnp.int32, (S8, width), 0)
        s = 1
        while s < S8:
            keep = in_grp >= s
            Bv = Bv + A * jnp.where(keep, pltpu.roll(Bv, s, 1), 0.0)
            A = A * jnp.where(keep, pltpu.roll(A, s, 1), 1.0)
            s *= 2
            emit()
        carry = hc_scr[...]
        groups = []
        for gi in range(tt // S8):
            hg = Bv[gi] + A[gi] * carry
            groups.append(hg)
            carry = hg[S8 - 1:S8, :]
        hs = jnp.concatenate(groups, axis=0)
        hc_scr[...] = carry
        hlast_ref[...] = carry
        emit()
        lru_out = hs * jax.nn.gelu(cur_scr[:, col(1)])
        emit()
        merged = jax.nn.sigmoid(cur_scr[:, col(2)]) * rw_ref[...] + jax.nn.sigmoid(cur_scr[:, col(3)]) * lru_out
        emit()
        mix = jnp.dot(merged.astype(BF16), wmix_ref[...], preferred_element_type=F32)
        out_ref[...] = _layer_norm(alpha * h_ref[...] + mix, lng_ref[...], lnb_ref[...])
        emit(LRU_PROJ_PIECES)

    slot = flat % 2
    step(proj_scr.at[slot], proj_scr.at[1 - slot])


def _lru_step_kernel(x_ref, pg_ref, g0_ref, g1_ref, rw_ref, h_ref, buf_ref, h0_ref,
                     cw_ref, cb_ref, wr_ref, br_ref, wi_ref, bi_ref, lam_ref, wmix_ref, lng_ref, lnb_ref,
                     out_ref, hnew_ref, *, alpha):
    cw = cw_ref[...]
    conv = cw[0:1] * buf_ref[0]
    for j in range(1, CONV_WIDTH - 1):
        conv = conv + cw[j:j + 1] * buf_ref[j]
    xc = cb_ref[...] + (conv + cw[CONV_WIDTH - 1:CONV_WIDTH] * x_ref[...])
    A, Bv = _lru_coeffs(xc, wr_ref, br_ref, wi_ref, bi_ref, lam_ref)
    hs = Bv + A * h0_ref[...]
    hnew_ref[...] = hs
    out_ref[...] = _lru_finish(hs, pg_ref[...], g0_ref[...], g1_ref[...], rw_ref[...], h_ref[...],
                               wmix_ref, lng_ref, lnb_ref, alpha)


_LRU_WEIGHTS = ('conv_w', 'conv_b', 'lru_wr', 'lru_br', 'lru_wi', 'lru_bi', 'lru_lambda', 'w_mix_out')


def _lru_step_mix_ln(pb, rw, h, buf, h0, W, ln_g, ln_b, *, alpha, tm):
    B, width = rw.shape
    assert B % tm == 0
    blk = lambda j: pl.BlockSpec((tm, width), lambda i: (i, j))
    out, h_new = pl.pallas_call(
        functools.partial(_lru_step_kernel, alpha=alpha),
        grid=(B // tm,),
        in_specs=[blk(0), blk(1), blk(2), blk(3), blk(0), blk(0),
                  pl.BlockSpec((CONV_WIDTH - 1, tm, width), lambda i: (0, i, 0)), blk(0)]
                 + [_const_spec(W[n].shape) for n in _LRU_WEIGHTS] + [_const_spec(ln_g.shape), _const_spec(ln_b.shape)],
        out_specs=[blk(0), blk(0)],
        out_shape=[jax.ShapeDtypeStruct((B, width), F32), jax.ShapeDtypeStruct((B, width), F32)],
        compiler_params=_cparams(("parallel",)),
        name="lru_step_mix_ln",
    )(pb, pb, pb, pb, rw, h, buf, h0, *[W[n] for n in _LRU_WEIGHTS], ln_g, ln_b)
    return out, h_new


def _lru_mix_ln(h, rw, buf8, h0, W, ln_g, ln_b, *, alpha, tt):
    B, T, width = rw.shape
    assert T % tt == 0 and tt % V7X_SUBLANES == 0 and h.shape[2] == width
    nt = T // tt
    tile = pl.BlockSpec((None, tt, width), lambda b, t: (b, t, 0))

    def next_tile(b, t):
        f = jnp.minimum(b * nt + t + 1, B * nt - 1)
        return f // nt, f % nt, 0

    nxt = pl.BlockSpec((None, tt, width), next_tile)
    per_b = lambda rows: pl.BlockSpec((None, rows, width), lambda b, t: (b, 0, 0))
    win = W['w_in_b']
    out, h_last, tail = pl.pallas_call(
        functools.partial(_lru_kernel, alpha=alpha),
        grid=(B, nt),
        in_specs=[nxt, tile, tile, per_b(V7X_SUBLANES), per_b(1), _const_spec(win.shape)]
                 + [_const_spec(W[n].shape) for n in _LRU_WEIGHTS] + [_const_spec(ln_g.shape), _const_spec(ln_b.shape)],
        out_specs=[tile, per_b(1), per_b(V7X_SUBLANES)],
        out_shape=[jax.ShapeDtypeStruct((B, T, width), F32), jax.ShapeDtypeStruct((B, 1, width), F32),
                   jax.ShapeDtypeStruct((B, V7X_SUBLANES, width), F32)],
        scratch_shapes=[pltpu.VMEM((2, tt, win.shape[1]), F32), pltpu.VMEM((V7X_SUBLANES, width), F32),
                        pltpu.VMEM((1, width), F32)],
        compiler_params=_cparams(("arbitrary", "arbitrary")),
        name="lru_mix_ln",
    )(h, rw, h, buf8, h0, win, *[W[n] for n in _LRU_WEIGHTS], ln_g, ln_b)
    return out, h_last, tail


def _xattn_kernel(h_ref, kv_ref, wq_ref, wo_ref, g_ref, b_ref, o_ref, *, alpha, heads):
    tm, d = h_ref.shape
    hd = d // heads
    row_groups = max(1, tm // V7X_MXU_DIM)
    tg = tm // row_groups
    for gi in range(row_groups):
        rows = slice(gi * tg, (gi + 1) * tg)
        h = h_ref[rows, :]
        q = jnp.dot(h.astype(BF16), wq_ref[...], preferred_element_type=F32)
        outs = []
        for j in range(heads):
            sl = slice(j * hd, (j + 1) * hd)
            s = _dot_dims(q[:, sl], kv_ref[:, sl], _NT) * (hd ** -0.5)
            e = jnp.exp(s - jnp.max(s, axis=-1, keepdims=True))
            p = e * (1.0 / jnp.sum(e, axis=-1, keepdims=True))
            outs.append(_dot(p, kv_ref[:, d + j * hd:d + (j + 1) * hd]))
        out = jnp.dot(jnp.concatenate(outs, axis=1).astype(BF16), wo_ref[...], preferred_element_type=F32)
        o_ref[rows, :] = _layer_norm(alpha * h + out, g_ref[...], b_ref[...])


def _xattn_rows_kernel(h_ref, mk_ref, mv_ref, wq_ref, wo_ref, g_ref, b_ref, o_ref, *, alpha):
    h = h_ref[...]
    nb, m, heads, hd = mk_ref.shape
    q = jnp.dot(h.astype(BF16), wq_ref[...], preferred_element_type=F32)
    col = lax.broadcasted_iota(jnp.int32, (heads, m * heads), 1)
    own = (col % heads) == lax.broadcasted_iota(jnp.int32, (heads, m * heads), 0)
    rows = []
    for i in range(nb):
        k2 = mk_ref[i].reshape(m * heads, hd)
        v2 = mv_ref[i].reshape(m * heads, hd)
        q4 = jnp.concatenate([q[i:i + 1, j * hd:(j + 1) * hd] for j in range(heads)], axis=0)
        s = jnp.where(own, _dot_dims(q4, k2, _NT) * (hd ** -0.5), -jnp.inf)
        e = jnp.exp(s - jnp.max(s, axis=-1, keepdims=True))
        p = e / jnp.sum(e, axis=-1, keepdims=True)
        o4 = _dot(p, v2)
        rows.append(jnp.concatenate([o4[j:j + 1] for j in range(heads)], axis=1))
    out = jnp.dot(jnp.concatenate(rows, axis=0).astype(BF16), wo_ref[...], preferred_element_type=F32)
    o_ref[...] = _layer_norm(alpha * h + out, g_ref[...], b_ref[...])


def _xattn_ln(h, kv, wq, wo, g, b, *, alpha, heads, rows_per_batch, tile_rows):
    n, d = h.shape
    assert n % tile_rows == 0 and rows_per_batch % tile_rows == 0
    per = rows_per_batch // tile_rows
    m = kv.shape[0] // (n // rows_per_batch)
    return pl.pallas_call(
        functools.partial(_xattn_kernel, alpha=alpha, heads=heads),
        grid=(n // tile_rows,),
        in_specs=[pl.BlockSpec((tile_rows, d), lambda i: (i, 0)),
                  pl.BlockSpec((m, 2 * d), lambda i: (i // per, 0)),
                  _const_spec(wq.shape), _const_spec(wo.shape), _const_spec(g.shape), _const_spec(b.shape)],
        out_specs=pl.BlockSpec((tile_rows, d), lambda i: (i, 0)),
        out_shape=jax.ShapeDtypeStruct((n, d), F32),
        compiler_params=_cparams(("parallel",)),
        name="xattn_ln",
    )(h, kv, wq, wo, g, b)


def _xattn_rows_ln(h, mk, mv, wq, wo, g, b, *, alpha, nb):
    n, d = h.shape
    _, m, heads, hd = mk.shape
    assert n % nb == 0
    kv_spec = pl.BlockSpec((nb, m, heads, hd), lambda i: (i, 0, 0, 0))
    return pl.pallas_call(
        functools.partial(_xattn_rows_kernel, alpha=alpha),
        grid=(n // nb,),
        in_specs=[pl.BlockSpec((nb, d), lambda i: (i, 0)), kv_spec, kv_spec,
                  _const_spec(wq.shape), _const_spec(wo.shape), _const_spec(g.shape), _const_spec(b.shape)],
        out_specs=pl.BlockSpec((nb, d), lambda i: (i, 0)),
        out_shape=jax.ShapeDtypeStruct((n, d), F32),
        compiler_params=_cparams(("parallel",)),
        name="xattn_rows_ln",
    )(h, mk, mv, wq, wo, g, b)


def _kv_proj_kernel(x_ref, w_ref, kv_ref, k_ref, v_ref):
    heads, hd = k_ref.shape[1:]
    d = heads * hd
    kv = jnp.dot(x_ref[...].astype(BF16), w_ref[...], preferred_element_type=F32)
    kv_ref[...] = kv
    for j in range(heads):
        k_ref[:, j, :] = kv[:, j * hd:(j + 1) * hd]
        v_ref[:, j, :] = kv[:, d + j * hd:d + (j + 1) * hd]


def _kv_proj(x, wkv, *, heads, tm):
    n, d = x.shape
    assert n % tm == 0
    hd = d // heads
    out = jax.ShapeDtypeStruct((n, heads, hd), F32)
    return pl.pallas_call(
        _kv_proj_kernel,
        grid=(n // tm,),
        in_specs=[pl.BlockSpec((tm, d), lambda i: (i, 0)), _const_spec(wkv.shape)],
        out_specs=[pl.BlockSpec((tm, 2 * d), lambda i: (i, 0))] + [pl.BlockSpec((tm, heads, hd), lambda i: (i, 0, 0))] * 2,
        out_shape=[jax.ShapeDtypeStruct((n, 2 * d), F32), out, out],
        compiler_params=_cparams(("parallel",)),
        name="kv_proj",
    )(x, wkv)


def _row(v):
    return v.reshape(1, -1)


def _block_diag_groups(w):
    n, c, _ = w.shape
    per = LRU_GROUP // c
    rows = jnp.concatenate([w.reshape(n // per, LRU_GROUP, c)] * per, axis=2)
    ri = lax.broadcasted_iota(jnp.int32, (LRU_GROUP, LRU_GROUP), 0) // c
    ci = lax.broadcasted_iota(jnp.int32, (LRU_GROUP, LRU_GROUP), 1) // c
    return jnp.where(ri == ci, rows, 0.0).astype(BF16)


def _prep_layer(l, ln_g, ln_b, ffn1_wi, ffn1_wo, ffn2_wi, ffn2_wo, w_in, shift_mu, decay_w0, decay_w2,
                aaa_a0, aaa_a2, gate_g2, k_k, k_a, r_k, gn_g, gn_b, conv_w, conv_b, lru_wr, lru_br,
                lru_wi, lru_bi, lru_lambda, w_mix_out, xa_wq, xa_wk, xa_wv, xa_wo):
    width = decay_w0.shape[1]
    rp = shift_mu.shape[1]
    d_ff = ffn1_wo.shape[1]
    bf = lambda w: w.astype(BF16)
    mu = shift_mu[l]
    return dict(
        ln_g=[_row(ln_g[l, i]) for i in range(4)], ln_b=[_row(ln_b[l, i]) for i in range(4)],
        ffn1=(bf(ffn1_wi[l][:, :d_ff]), bf(ffn1_wi[l][:, d_ff:]), bf(ffn1_wo[l])),
        ffn2=(bf(ffn2_wi[l][:, :d_ff]), bf(ffn2_wi[l][:, d_ff:]), bf(ffn2_wo[l])),
        w_in_a=bf(w_in[l][:, :rp]), w_in_b=bf(w_in[l][:, rp:]),
        shift_mu=_row(mu), mu_r=_row(mu[:width]), mu_k=_row(mu[width:2 * width]),
        mu_v=_row(mu[2 * width:3 * width]), mu_x=_row(mu[3 * width:]),
        decay_w0=_row(decay_w0[l]), decay_w2=bf(decay_w2[l]), aaa_a0=_row(aaa_a0[l]), aaa_a2=bf(aaa_a2[l]),
        gate_g2=bf(gate_g2[l]), k_k=_row(k_k[l]), k_a=_row(k_a[l]), r_k=_row(r_k[l]),
        gn_g=_row(gn_g[l]), gn_b=_row(gn_b[l]),
        conv_w=conv_w[l], conv_b=_row(conv_b[l]),
        lru_wr=_block_diag_groups(lru_wr[l]), lru_br=_row(lru_br[l]),
        lru_wi=_block_diag_groups(lru_wi[l]), lru_bi=_row(lru_bi[l]), lru_lambda=_row(lru_lambda[l]),
        w_mix_out=bf(w_mix_out[l]), xa_wq=bf(xa_wq[l]), xa_wo=bf(xa_wo[l]),
        xa_wkv=bf(jnp.concatenate([xa_wk[l], xa_wv[l]], axis=1)),
    )


def _tile(n, pref):
    return pref if n % pref == 0 else n


def _layer(h, mem, state, shift0, h0, buf0, W, *, alpha, xa_heads):
    B, T, d = h.shape
    n = B * T
    tm = _tile(n, 1024)
    h1 = _ffn_ln(h.reshape(n, d), *W['ffn1'], W['ln_g'][0], W['ln_b'][0], alpha=alpha, tm=tm)
    pa = _matmul(h1, W['w_in_a'], tm=tm, tn=W['w_in_a'].shape[1])
    width = W['decay_w0'].shape[1]
    lru_w = W['conv_b'].shape[1]
    pa3 = pa.reshape(B, T, -1)
    hist = CONV_WIDTH - 1
    if T > 1:
        assert state is None and T >= V7X_SUBLANES
        rw, s_new = _rwkv_chunked(pa3, shift0.reshape(B, 1, -1), W, tt=_tile(T, 512), hw=_tile(width, 512))
        buf8 = jnp.concatenate([jnp.zeros((B, V7X_SUBLANES - hist, lru_w), F32), buf0], axis=1)
        h2, h_last, tail8 = _lru_mix_ln(h1.reshape(B, T, d), rw, buf8, h0.reshape(B, 1, lru_w), W,
                                        W['ln_g'][1], W['ln_b'][1], alpha=alpha, tt=_tile(T, 256))
        conv_in_tail = tail8[:, V7X_SUBLANES - hist:]
    else:
        pb = _matmul(h1, W['w_in_b'], tm=tm, tn=W['w_in_b'].shape[1])
        rw, s_new = _rwkv_step(pa, shift0, jnp.transpose(state, (1, 2, 3, 0)), W, hg=2)
        s_new = jnp.transpose(s_new, (3, 0, 1, 2))
        h2, h_last = _lru_step_mix_ln(pb, rw, h1, jnp.swapaxes(buf0, 0, 1), h0, W,
                                      W['ln_g'][1], W['ln_b'][1], alpha=alpha, tm=_tile(B, 128))
        conv_in_tail = jnp.concatenate([buf0[:, T:], pb[:, None, :lru_w]], axis=1)
    xa = (W['xa_wq'], W['xa_wo'], W['ln_g'][2], W['ln_b'][2])
    if T > 1:
        h3 = _xattn_ln(h2.reshape(n, d), mem, *xa, alpha=alpha, heads=xa_heads, rows_per_batch=T,
                       tile_rows=_tile(T, 1024))
    else:
        h3 = _xattn_rows_ln(h2, *mem, *xa, alpha=alpha, nb=_tile(B, V7X_SUBLANES))
    h4 = _ffn_ln(h3, *W['ffn2'], W['ln_g'][3], W['ln_b'][3], alpha=alpha, tm=tm)
    return h4.reshape(B, T, d), s_new, pa3[:, -1], h_last.reshape(B, lru_w), conv_in_tail


def kernel(x_prompt, x_sample, mem_prompt, cache_mem_k, cache_mem_v, state_rwkv, state_rwkv_shift, state_lru, state_conv, ln_g, ln_b, ffn1_wi, ffn1_wo, ffn2_wi, ffn2_wo, w_in, shift_mu, decay_w0, decay_w2, aaa_a0, aaa_a2, gate_g2, k_k, k_a, r_k, gn_g, gn_b, conv_w, conv_b, lru_wr, lru_br, lru_wi, lru_bi, lru_lambda, w_mix_out, xa_wq, xa_wk, xa_wv, xa_wo):
    depth = ln_g.shape[0]
    alpha = (2.0 * depth) ** 0.25
    B, _, d = x_prompt.shape
    n_mem, xa_heads, xa_head = cache_mem_k.shape[2:]
    rp = shift_mu.shape[1]
    lru_w = conv_b.shape[1]
    hp, hs = x_prompt, x_sample
    outs = [[] for _ in range(10)]
    for l in range(depth):
        W = _prep_layer(l, ln_g, ln_b, ffn1_wi, ffn1_wo, ffn2_wi, ffn2_wo, w_in, shift_mu, decay_w0, decay_w2,
                        aaa_a0, aaa_a2, gate_g2, k_k, k_a, r_k.reshape(depth, -1), gn_g, gn_b, conv_w, conv_b,
                        lru_wr, lru_br, lru_wi, lru_bi, lru_lambda, w_mix_out, xa_wq, xa_wk, xa_wv, xa_wo)
        kv, mk, mv = _kv_proj(mem_prompt.reshape(B * n_mem, d), W['xa_wkv'], heads=xa_heads,
                              tm=_tile(B * n_mem, 512))
        mk = mk.reshape(B, n_mem, xa_heads, xa_head)
        mv = mv.reshape(B, n_mem, xa_heads, xa_head)
        hp, S1, sh1, h1, b1 = _layer(
            hp, kv, None, jnp.zeros((B, rp), F32), jnp.zeros((B, lru_w), F32),
            jnp.zeros((B, CONV_WIDTH - 1, lru_w), F32), W, alpha=alpha, xa_heads=xa_heads)
        hs, S2, sh2, h2, b2 = _layer(
            hs, (cache_mem_k[l], cache_mem_v[l]),
            state_rwkv[l], state_rwkv_shift[l], state_lru[l], state_conv[l], W, alpha=alpha, xa_heads=xa_heads)
        for lst, val in zip(outs, (mk, mv, S1, sh1, h1, b1, S2, sh2, h2, b2)):
            lst.append(val)
    return (hp, hs) + tuple(jnp.stack(o) for o in outs)
```

```python
import functools

import jax
import jax.numpy as jnp
from jax import lax
from jax.experimental import pallas as pl
from jax.experimental.pallas import tpu as pltpu

F32 = jnp.float32
BF16 = jnp.bfloat16

RWKV_HEAD = 64
DECAY_LORA = 64
AAA_LORA = 64
GATE_LORA = 128
GN_EPS = 64e-5
CONV_WIDTH = 4
LRU_C = 8.0
LN_EPS = 1e-5

V7X_SUBLANES = 8
V7X_MXU_DIM = 256
V7X_SCOPED_VMEM_BYTES = 60000 * 1024

RWKV_CHUNK = 64
HEAD_PAIR = 2 * RWKV_HEAD
SCAN_ROW_GROUPS = 2
LRU_GROUP = V7X_MXU_DIM
LRU_PROJ_PIECES = 8


def _cparams(semantics):
    return pltpu.CompilerParams(dimension_semantics=semantics, vmem_limit_bytes=V7X_SCOPED_VMEM_BYTES)


def _const_spec(shape):
    zeros = (0,) * len(shape)
    return pl.BlockSpec(shape, lambda *_: zeros)


def _dot(a, b):
    return jnp.dot(a.astype(BF16), b.astype(BF16), preferred_element_type=F32)


def _dot_dims(a, b, dims):
    return lax.dot_general(a.astype(BF16), b.astype(BF16), (dims, ((), ())), preferred_element_type=F32)


_NN = ((1,), (0,))
_NT = ((1,), (1,))
_TN = ((0,), (0,))


def _split2(x):
    hi = x.astype(BF16)
    lo = (x - hi.astype(F32)).astype(BF16)
    return hi, lo


def _dot_exact_lhs(a_bf16, b):
    hi, lo = _split2(b)
    d = lambda y: jnp.dot(a_bf16, y, preferred_element_type=F32)
    return d(hi) + d(lo)


def _dot_exact_rhs(a, b_bf16):
    hi, lo = _split2(a)
    d = lambda x: jnp.dot(x, b_bf16, preferred_element_type=F32)
    return d(hi) + d(lo)


def _layer_norm(x, g, b):
    mu = jnp.mean(x, axis=-1, keepdims=True)
    xc = x - mu
    var = jnp.mean(xc * xc, axis=-1, keepdims=True)
    return xc * lax.rsqrt(var + LN_EPS) * g + b


def _softplus(z):
    return jnp.maximum(z, 0.0) + jnp.log(1.0 + jnp.exp(-jnp.abs(z)))


def _head_ones(width):
    r = lax.broadcasted_iota(jnp.int32, (width, width), 0) // RWKV_HEAD
    c = lax.broadcasted_iota(jnp.int32, (width, width), 1) // RWKV_HEAD
    return (r == c).astype(BF16)


def _mm_kernel(x_ref, w_ref, o_ref):
    o_ref[...] = jnp.dot(x_ref[...].astype(BF16), w_ref[...], preferred_element_type=F32)


def _matmul(x, w, *, tm, tn, n_cols=None):
    n, k = x.shape
    m = w.shape[1] if n_cols is None else n_cols
    assert n % tm == 0 and m % tn == 0 and m <= w.shape[1]
    return pl.pallas_call(
        _mm_kernel,
        grid=(m // tn, n // tm),
        in_specs=[pl.BlockSpec((tm, k), lambda j, i: (i, 0)),
                  pl.BlockSpec((k, tn), lambda j, i: (0, j))],
        out_specs=pl.BlockSpec((tm, tn), lambda j, i: (i, j)),
        out_shape=jax.ShapeDtypeStruct((n, m), F32),
        compiler_params=_cparams(("parallel", "parallel")),
        name="matmul",
    )(x, w)


def _ffn_kernel(x_ref, wi_ref, wo_ref, g_ref, b_ref, o_ref, *, alpha, row_groups):
    tm = x_ref.shape[0]
    d_ff = wo_ref.shape[0]
    tg = tm // row_groups
    for gi in range(row_groups):
        rows = slice(gi * tg, (gi + 1) * tg)
        x = x_ref[rows, :]
        xb = x.astype(BF16)
        gate = jnp.dot(xb, wi_ref[:, :d_ff], preferred_element_type=F32)
        up = jnp.dot(xb, wi_ref[:, d_ff:], preferred_element_type=F32)
        mid = (gate * jax.nn.sigmoid(gate) * up).astype(BF16)
        down = jnp.dot(mid, wo_ref[...], preferred_element_type=F32)
        o_ref[rows, :] = _layer_norm(alpha * x + 0.5 * down, g_ref[...], b_ref[...])


def _resident_spec(shape):
    zeros = (0,) * len(shape)
    return pl.BlockSpec(shape, lambda *_: zeros, pipeline_mode=pl.Buffered(1))


def _ffn_ln(x, wi, wo, g, b, *, alpha, tm):
    n, d = x.shape
    assert n % tm == 0 and wi.shape[1] == 2 * wo.shape[0]
    row_groups = max(1, tm // V7X_MXU_DIM)
    return pl.pallas_call(
        functools.partial(_ffn_kernel, alpha=alpha, row_groups=row_groups),
        grid=(n // tm,),
        in_specs=[pl.BlockSpec((tm, d), lambda i: (i, 0)),
                  _resident_spec(wi.shape), _resident_spec(wo.shape),
                  _const_spec(g.shape), _const_spec(b.shape)],
        out_specs=pl.BlockSpec((tm, d), lambda i: (i, 0)),
        out_shape=jax.ShapeDtypeStruct((n, d), F32),
        compiler_params=_cparams(("parallel",)),
        name="ffn_ln",
    )(x, wi, wo, g, b)


def _rwkv_pre(r, k, v, xx, w0, w2, a0, a2, g2, k_k, k_a, r_k, ones):
    xw = xx[:, :DECAY_LORA]
    xa = xx[:, DECAY_LORA:DECAY_LORA + AAA_LORA]
    xg = xx[:, DECAY_LORA + AAA_LORA:]
    z = w0 + _dot(jnp.tanh(xw), w2)
    lw = -jnp.exp(-_softplus(-z) - 0.5)
    a = jax.nn.sigmoid(a0 + _dot(xa, a2))
    g = _dot(jax.nn.sigmoid(xg), g2)
    kkr = k * k_k
    ss = _dot(kkr * kkr, ones)
    kk = kkr * lax.rsqrt(jnp.maximum(ss, 1e-24))
    kf = k * (1.0 + (a - 1.0) * k_a)
    bonus = _dot(r * kf * r_k, ones) * v
    return lw, a, g, kk, kf, bonus


def _rwkv_post(y, bonus, g, gn_g, gn_b, ones):
    inv_n = 1.0 / RWKV_HEAD
    ym = _dot_exact_rhs(y, ones) * inv_n
    yc = y - ym
    yv = _dot(yc * yc, ones) * inv_n
    yn = yc * lax.rsqrt(yv + GN_EPS) * gn_g + gn_b
    return (yn + bonus) * g


def _bdot(a, b, dims):
    dn = ((tuple(d + 1 for d in dims[0]), tuple(d + 1 for d in dims[1])), ((0,), (0,)))
    return lax.dot_general(a.astype(BF16), b.astype(BF16), dn, preferred_element_type=F32)


def _scan_operands(r, kf, v, kk, a, lw):
    tt, hw = r.shape
    C = RWKV_CHUNK
    n_pairs = hw // HEAD_PAIR
    ltri = (lax.broadcasted_iota(jnp.int32, (C, C), 0) >= lax.broadcasted_iota(jnp.int32, (C, C), 1)).astype(BF16)
    first = lax.broadcasted_iota(jnp.int32, (C, HEAD_PAIR), 1) < RWKV_HEAD

    def bd(x):
        return jnp.concatenate([jnp.where(first, x, 0.0), jnp.where(first, 0.0, x)], axis=1)

    names = ('a', 'r', 'b', 'k', 'v', 'bh', 'kh')
    ops = {n: [] for n in names}
    wcs = []
    for c in range(tt // C):
        rows = slice(c * C, (c + 1) * C)
        lw_c = lw[rows]
        L = _dot_exact_lhs(ltri, lw_c)
        Lc = L[C - 1:C, :]
        e_nl = jnp.exp(-L)
        e_c = jnp.exp(Lc - L)
        bb = kk[rows] * a[rows]
        bf = lambda x: x.astype(BF16)
        tile = dict(a=bf(-kk[rows] * jnp.exp(L - lw_c)), r=r[rows] * jnp.exp(L), b=bf(bb * e_nl),
                    k=bf(kf[rows] * e_nl), v=bf(v[rows]), bh=bf(bb * e_c), kh=bf(kf[rows] * e_c))
        wc = jnp.exp(Lc)
        for p in range(n_pairs):
            lanes = slice(p * HEAD_PAIR, (p + 1) * HEAD_PAIR)
            for n in names:
                ops[n].append(tile[n][:, lanes])
            wcs.append(wc[:, lanes])
    A, R, B, K, V, Bh, Kh = (jnp.stack(ops[n]) for n in names)
    Vbd = bd(V)
    G = _bdot(jnp.concatenate([A, R.astype(BF16)], axis=1), jnp.concatenate([bd(B), bd(K)], axis=1), _NT)
    tok = lax.broadcasted_iota(jnp.int32, (C, HEAD_PAIR), 0)
    src = lax.broadcasted_iota(jnp.int32, (C, HEAD_PAIR), 1) % RWKV_HEAD
    a_ab = jnp.where(tok > src, G[:, :C, :HEAD_PAIR], 0.0)
    a_ak = jnp.where(tok > src, G[:, :C, HEAD_PAIR:], 0.0).astype(BF16)
    a_rb = jnp.where(tok >= src, G[:, C:, :HEAD_PAIR], 0.0).astype(BF16)
    a_rk = jnp.where(tok >= src, G[:, C:, HEAD_PAIR:], 0.0).astype(BF16)
    P = jnp.where(tok == src, 1.0, 0.0) + a_ab
    N = a_ab.astype(BF16)
    N = _bdot(N, bd(N), _NN).astype(BF16)
    steps = 2
    while 2 * steps < C:
        NP = _bdot(jnp.concatenate([N, P.astype(BF16)], axis=1), bd(N), _NN)
        N = NP[:, :C].astype(BF16)
        P = P + NP[:, C:]
        steps *= 2
    P = (P + _bdot(P, bd(N), _NN)).astype(BF16)
    aV = _bdot(a_ak, Vbd, _NN).astype(BF16)
    XU = _bdot(P, jnp.concatenate([bd(A), bd(aV)], axis=2), _NN).astype(BF16)
    X1 = XU[:, :, :HEAD_PAIR]
    Uloc = XU[:, :, HEAD_PAIR:]
    Q = (R + _bdot(a_rb, bd(X1), _NN)).astype(BF16)
    Yloc = _bdot(jnp.concatenate([a_rb, a_rk], axis=2), jnp.concatenate([bd(Uloc), Vbd], axis=1), _NN)
    ri = lax.broadcasted_iota(jnp.int32, (HEAD_PAIR, HEAD_PAIR), 0) // RWKV_HEAD
    ci = lax.broadcasted_iota(jnp.int32, (HEAD_PAIR, HEAD_PAIR), 1) // RWKV_HEAD
    same_head = ri == ci
    Pm = jnp.where(same_head, _bdot(X1, Bh, _TN), 0.0).astype(BF16)
    Sloc = jnp.where(same_head, _bdot(jnp.concatenate([Uloc, V], axis=1), jnp.concatenate([Bh, Kh], axis=1), _TN),
                     0.0)
    return Q, Yloc, Pm, Sloc, jnp.stack(wcs)


def _rwkv_chunk_kernel(pr_ref, pk_ref, pv_ref, px_ref, sr_ref, sk_ref, sv_ref, sx_ref,
                       mur_ref, muk_ref, muv_ref, mux_ref, w0_ref, w2_ref, a0_ref, a2_ref, g2_ref,
                       kk_ref, ka_ref, rk_ref, gng_ref, gnb_ref, ones_ref,
                       out_ref, s_out_ref,
                       s_scr, cr_scr, ck_scr, cv_scr, cx_scr, y_scr):
    ti = pl.program_id(2)
    tt, hw = pr_ref.shape
    n_pairs = hw // HEAD_PAIR

    @pl.when(ti == 0)
    def _init():
        s_scr[...] = jnp.zeros(s_scr.shape, F32)
        cr_scr[...] = sr_ref[...]
        ck_scr[...] = sk_ref[...]
        cv_scr[...] = sv_ref[...]
        cx_scr[...] = sx_ref[...]

    S8 = V7X_SUBLANES

    def shifted(p_ref, c_scr, mu_ref):
        p = p_ref[...]
        first = lax.broadcasted_iota(jnp.int32, (S8, p.shape[1]), 0) == 0
        head = jnp.where(first, c_scr[...], pltpu.roll(p[:S8], 1, 0))
        prev = jnp.concatenate([head, p_ref[S8 - 1:tt - 1, :]], axis=0)
        c_scr[...] = p_ref[tt - 1:tt, :]
        return p + (prev - p) * mu_ref[...]

    r = shifted(pr_ref, cr_scr, mur_ref)
    k = shifted(pk_ref, ck_scr, muk_ref)
    v = shifted(pv_ref, cv_scr, muv_ref)
    xx = shifted(px_ref, cx_scr, mux_ref)
    ones = ones_ref[...]
    lw, a, g, kk, kf, bonus = _rwkv_pre(r, k, v, xx, w0_ref[...], w2_ref[...], a0_ref[...], a2_ref[...],
                                        g2_ref[...], kk_ref[...], ka_ref[...], rk_ref[...], ones)
    C = RWKV_CHUNK
    S = s_scr[...]
    n_groups = SCAN_ROW_GROUPS if tt % (SCAN_ROW_GROUPS * C) == 0 else 1
    tg = tt // n_groups
    for gi in range(n_groups):
        rows = slice(gi * tg, (gi + 1) * tg)
        Q, Yloc, Pm, Sloc, wc = _scan_operands(r[rows], kf[rows], v[rows], kk[rows], a[rows], lw[rows])
        for c in range(tg // C):
            inst = slice(c * n_pairs, (c + 1) * n_pairs)
            y_c = _bdot(Q[inst], S, _NT) + Yloc[inst]
            row0 = gi * tg + c * C
            for p in range(n_pairs):
                y_scr[row0:row0 + C, p * HEAD_PAIR:(p + 1) * HEAD_PAIR] = y_c[p]
            S = S * wc[inst] + _bdot(S, Pm[inst], _NN) + Sloc[inst]
    s_scr[...] = S
    out_ref[...] = _rwkv_post(y_scr[...], bonus, g, gng_ref[...], gnb_ref[...], ones)

    @pl.when(ti == pl.num_programs(2) - 1)
    def _emit_state():
        for p in range(n_pairs):
            s_out_ref[2 * p] = S[p, :RWKV_HEAD, :RWKV_HEAD]
            s_out_ref[2 * p + 1] = S[p, RWKV_HEAD:, RWKV_HEAD:]


def _rwkv_chunked(pa, shift0, W, *, tt, hw):
    B, T, _ = pa.shape
    width = W['decay_w0'].shape[1]
    heads = width // RWKV_HEAD
    assert T % tt == 0 and tt % RWKV_CHUNK == 0 and width % hw == 0 and hw % HEAD_PAIR == 0
    nb = width // hw
    xw = DECAY_LORA + AAA_LORA + GATE_LORA
    assert (3 * width) % xw == 0
    xblk = 3 * width // xw
    col = lambda off: (lambda b, h, t: (b, t, off + h))
    vec = lambda: pl.BlockSpec((1, hw), lambda b, h, t: (0, h))
    in_specs = [
        pl.BlockSpec((None, tt, hw), col(0)), pl.BlockSpec((None, tt, hw), col(nb)),
        pl.BlockSpec((None, tt, hw), col(2 * nb)), pl.BlockSpec((None, tt, xw), lambda b, h, t: (b, t, xblk)),
        pl.BlockSpec((None, 1, hw), lambda b, h, t: (b, 0, h)), pl.BlockSpec((None, 1, hw), lambda b, h, t: (b, 0, nb + h)),
        pl.BlockSpec((None, 1, hw), lambda b, h, t: (b, 0, 2 * nb + h)), pl.BlockSpec((None, 1, xw), lambda b, h, t: (b, 0, xblk)),
        vec(), vec(), vec(), _const_spec((1, xw)),
        vec(), pl.BlockSpec((DECAY_LORA, hw), lambda b, h, t: (0, h)),
        vec(), pl.BlockSpec((AAA_LORA, hw), lambda b, h, t: (0, h)),
        pl.BlockSpec((GATE_LORA, hw), lambda b, h, t: (0, h)),
        vec(), vec(), vec(), vec(), vec(), _const_spec((hw, hw)),
    ]
    out, s_new = pl.pallas_call(
        _rwkv_chunk_kernel,
        grid=(B, nb, T // tt),
        in_specs=in_specs,
        out_specs=[pl.BlockSpec((None, tt, hw), lambda b, h, t: (b, t, h)),
                   pl.BlockSpec((None, hw // RWKV_HEAD, RWKV_HEAD, RWKV_HEAD), lambda b, h, t: (b, h, 0, 0))],
        out_shape=[jax.ShapeDtypeStruct((B, T, width), F32),
                   jax.ShapeDtypeStruct((B, heads, RWKV_HEAD, RWKV_HEAD), F32)],
        scratch_shapes=[pltpu.VMEM((hw // HEAD_PAIR, HEAD_PAIR, HEAD_PAIR), F32),
                        pltpu.VMEM((1, hw), F32), pltpu.VMEM((1, hw), F32), pltpu.VMEM((1, hw), F32),
                        pltpu.VMEM((1, xw), F32), pltpu.VMEM((tt, hw), F32)],
        compiler_params=_cparams(("parallel", "parallel", "arbitrary")),
        name="rwkv_chunked",
    )(pa, pa, pa, pa, shift0, shift0, shift0, shift0,
      W['mu_r'], W['mu_k'], W['mu_v'], W['mu_x'], W['decay_w0'], W['decay_w2'], W['aaa_a0'], W['aaa_a2'],
      W['gate_g2'], W['k_k'], W['k_a'], W['r_k'], W['gn_g'], W['gn_b'], _head_ones(hw))
    return out, s_new


def _rwkv_step_kernel(p_ref, s0_ref, st_ref, mu_ref, w0_ref, w2_ref, a0_ref, a2_ref, g2_ref,
                      kk_ref, ka_ref, rk_ref, gng_ref, gnb_ref, ones_ref,
                      out_ref, st_out_ref, vec_scr, y_scr, bonus_scr, g_scr):
    i = pl.program_id(0)
    B, width = out_ref.shape
    hg = st_ref.shape[0]
    H = RWKV_HEAD

    @pl.when(i == 0)
    def _prologue():
        ones = ones_ref[...]
        p = p_ref[...]
        xs = p + (s0_ref[...] - p) * mu_ref[...]
        r, k, v, xx = xs[:, :width], xs[:, width:2 * width], xs[:, 2 * width:3 * width], xs[:, 3 * width:]
        lw, a, g, kk, kf, bonus = _rwkv_pre(r, k, v, xx, w0_ref[...], w2_ref[...], a0_ref[...], a2_ref[...],
                                            g2_ref[...], kk_ref[...], ka_ref[...], rk_ref[...], ones)
        for j, vec in enumerate((r, kf, v, kk, kk * a, jnp.exp(lw))):
            vec_scr[j] = vec.T
        bonus_scr[...] = bonus
        g_scr[...] = g

    for hl in range(hg):
        base = pl.multiple_of((i * hg + hl) * H, H)
        r_h, kf_h, kk_h, kka_h, w_h = (vec_scr[j, pl.ds(base, H), :] for j in (0, 1, 3, 4, 5))

        def body(vi, carry):
            S = st_ref[hl, vi]
            sa = jnp.sum(S * kk_h, axis=0, keepdims=True)
            v_row = vec_scr[2, pl.ds(base + vi, 1), :]
            S2 = S * w_h - sa * kka_h + v_row * kf_h
            st_out_ref[hl, vi] = S2
            y_scr[pl.ds(base + vi, 1), :] = jnp.sum(S2 * r_h, axis=0, keepdims=True)
            return carry

        lax.fori_loop(0, H, body, 0, unroll=8)

    @pl.when(i == pl.num_programs(0) - 1)
    def _epilogue():
        out_ref[...] = _rwkv_post(y_scr[...].T, bonus_scr[...], g_scr[...], gng_ref[...], gnb_ref[...],
                                  ones_ref[...])


def _rwkv_step(pa, shift0, state_t, W, *, hg):
    B, proj = pa.shape
    width = W['decay_w0'].shape[1]
    heads = width // RWKV_HEAD
    assert heads % hg == 0 and state_t.shape == (heads, RWKV_HEAD, RWKV_HEAD, B)
    st_spec = pl.BlockSpec((hg, RWKV_HEAD, RWKV_HEAD, B), lambda i: (i, 0, 0, 0))
    names = ('decay_w0', 'decay_w2', 'aaa_a0', 'aaa_a2', 'gate_g2', 'k_k', 'k_a', 'r_k', 'gn_g', 'gn_b')
    out, st_new = pl.pallas_call(
        _rwkv_step_kernel,
        grid=(heads // hg,),
        in_specs=[_const_spec((B, proj)), _const_spec((B, proj)), st_spec, _const_spec(W['shift_mu'].shape)]
                 + [_const_spec(W[n].shape) for n in names] + [_const_spec((width, width))],
        out_specs=[_const_spec((B, width)), st_spec],
        out_shape=[jax.ShapeDtypeStruct((B, width), F32), jax.ShapeDtypeStruct(state_t.shape, F32)],
        scratch_shapes=[pltpu.VMEM((6, width, B), F32), pltpu.VMEM((width, B), F32),
                        pltpu.VMEM((B, width), F32), pltpu.VMEM((B, width), F32)],
        compiler_params=_cparams(("arbitrary",)),
        name="rwkv_step",
    )(pa, shift0, state_t, W['shift_mu'], *[W[n] for n in names], _head_ones(width))
    return out, st_new


def _lru_coeffs(xc, wr_ref, br_ref, wi_ref, bi_ref, lam_ref):
    xcb = xc.astype(BF16)
    n_grp = xc.shape[1] // LRU_GROUP
    grp = lambda w_ref: jnp.concatenate(
        [jnp.dot(xcb[:, i * LRU_GROUP:(i + 1) * LRU_GROUP], w_ref[i], preferred_element_type=F32)
         for i in range(n_grp)], axis=1)
    gr = jax.nn.sigmoid(grp(wr_ref) + br_ref[...])
    gi = jax.nn.sigmoid(grp(wi_ref) + bi_ref[...])
    a_t = jnp.exp(-LRU_C * gr * _softplus(-lam_ref[...]))
    return a_t, jnp.sqrt(1.0 - a_t * a_t) * gi * xc


def _lru_finish(hs, pg, g0, g1, rw, h, wmix_ref, lng_ref, lnb_ref, alpha):
    lru_out = hs * jax.nn.gelu(pg)
    merged = jax.nn.sigmoid(g0) * rw + jax.nn.sigmoid(g1) * lru_out
    mix = jnp.dot(merged.astype(BF16), wmix_ref[...], preferred_element_type=F32)
    return _layer_norm(alpha * h + mix, lng_ref[...], lnb_ref[...])


def _lru_kernel(hn_ref, rw_ref, h_ref, buf_ref, h0_ref, win_ref,
                cw_ref, cb_ref, wr_ref, br_ref, wi_ref, bi_ref, lam_ref, wmix_ref, lng_ref, lnb_ref,
                out_ref, hlast_ref, tail_ref, proj_scr, tail_scr, hc_scr, *, alpha):
    ti = pl.program_id(1)
    flat = pl.program_id(0) * pl.num_programs(1) + ti
    tt, width = rw_ref.shape
    S8 = V7X_SUBLANES
    n_proj = proj_scr.shape[2]
    w_off = win_ref.shape[1] - n_proj
    project = lambda ref: jnp.dot(ref[...].astype(BF16), win_ref[:, w_off:], preferred_element_type=F32)
    col = lambda j: slice(j * width, (j + 1) * width)

    @pl.when(ti == 0)
    def _init():
        tail_scr[...] = buf_ref[...]
        hc_scr[...] = h0_ref[...]

    @pl.when(flat == 0)
    def _first_projection():
        proj_scr[0] = project(h_ref)

    def step(cur_scr, nxt_scr):
        hn = hn_ref[...].astype(BF16)
        piece_w = n_proj // LRU_PROJ_PIECES
        pieces = iter(range(LRU_PROJ_PIECES))

        def emit(count=1):
            for _ in range(count):
                k = next(pieces, None)
                if k is not None:
                    c0 = k * piece_w
                    nxt_scr[:, c0:c0 + piece_w] = jnp.dot(hn, win_ref[:, w_off + c0:w_off + c0 + piece_w],
                                                          preferred_element_type=F32)

        x = cur_scr[:, col(0)]
        tail = tail_scr[...]
        r8 = lax.broadcasted_iota(jnp.int32, (S8, width), 0)

        def delayed(d):
            head = jnp.where(r8 < d, pltpu.roll(tail, d, 0), pltpu.roll(x[:S8], d, 0))
            return jnp.concatenate([head, cur_scr[S8 - d:tt - d, col(0)]], axis=0)

        cw = cw_ref[...]
        conv = cw[0:1] * delayed(CONV_WIDTH - 1)
        for j in range(1, CONV_WIDTH - 1):
            conv = conv + cw[j:j + 1] * delayed(CONV_WIDTH - 1 - j)
        xc = cb_ref[...] + (conv + cw[CONV_WIDTH - 1:CONV_WIDTH] * x)
        tail_scr[...] = x[tt - S8:, :]
        tail_ref[...] = x[tt - S8:, :]
        emit()

        A, Bv = _lru_coeffs(xc, wr_ref, br_ref, wi_ref, bi_ref, lam_ref)
        emit()
        A = A.reshape(tt // S8, S8, width)
        Bv = Bv.reshape(tt // S8, S8, width)
        in_grp = lax.broadcasted_iota(jnp.int32, (S8, width), 0)
        s = 1
        while s < S8:
            keep = in_grp >= s
            Bv = Bv + A * jnp.where(keep, pltpu.roll(Bv, s, 1), 0.0)
            A = A * jnp.where(keep, pltpu.roll(A, s, 1), 1.0)
            s *= 2
            emit()
        carry = hc_scr[...]
        groups = []
        for gi in range(tt // S8):
            hg = Bv[gi] + A[gi] * carry
            groups.append(hg)
            carry = hg[S8 - 1:S8, :]
        hs = jnp.concatenate(groups, axis=0)
        hc_scr[...] = carry
        hlast_ref[...] = carry
        emit()
        lru_out = hs * jax.nn.gelu(cur_scr[:, col(1)])
        emit()
        merged = jax.nn.sigmoid(cur_scr[:, col(2)]) * rw_ref[...] + jax.nn.sigmoid(cur_scr[:, col(3)]) * lru_out
        emit()
        mix = jnp.dot(merged.astype(BF16), wmix_ref[...], preferred_element_type=F32)
        out_ref[...] = _layer_norm(alpha * h_ref[...] + mix, lng_ref[...], lnb_ref[...])
        emit(LRU_PROJ_PIECES)

    slot = flat % 2
    step(proj_scr.at[slot], proj_scr.at[1 - slot])


def _lru_step_kernel(x_ref, pg_ref, g0_ref, g1_ref, rw_ref, h_ref, buf_ref, h0_ref,
                     cw_ref, cb_ref, wr_ref, br_ref, wi_ref, bi_ref, lam_ref, wmix_ref, lng_ref, lnb_ref,
                     out_ref, hnew_ref, *, alpha):
    cw = cw_ref[...]
    conv = cw[0:1] * buf_ref[0]
    for j in range(1, CONV_WIDTH - 1):
        conv = conv + cw[j:j + 1] * buf_ref[j]
    xc = cb_ref[...] + (conv + cw[CONV_WIDTH - 1:CONV_WIDTH] * x_ref[...])
    A, Bv = _lru_coeffs(xc, wr_ref, br_ref, wi_ref, bi_ref, lam_ref)
    hs = Bv + A * h0_ref[...]
    hnew_ref[...] = hs
    out_ref[...] = _lru_finish(hs, pg_ref[...], g0_ref[...], g1_ref[...], rw_ref[...], h_ref[...],
                               wmix_ref, lng_ref, lnb_ref, alpha)


_LRU_WEIGHTS = ('conv_w', 'conv_b', 'lru_wr', 'lru_br', 'lru_wi', 'lru_bi', 'lru_lambda', 'w_mix_out')


def _lru_step_mix_ln(pb, rw, h, buf, h0, W, ln_g, ln_b, *, alpha, tm):
    B, width = rw.shape
    assert B % tm == 0
    blk = lambda j: pl.BlockSpec((tm, width), lambda i: (i, j))
    out, h_new = pl.pallas_call(
        functools.partial(_lru_step_kernel, alpha=alpha),
        grid=(B // tm,),
        in_specs=[blk(0), blk(1), blk(2), blk(3), blk(0), blk(0),
                  pl.BlockSpec((CONV_WIDTH - 1, tm, width), lambda i: (0, i, 0)), blk(0)]
                 + [_const_spec(W[n].shape) for n in _LRU_WEIGHTS] + [_const_spec(ln_g.shape), _const_spec(ln_b.shape)],
        out_specs=[blk(0), blk(0)],
        out_shape=[jax.ShapeDtypeStruct((B, width), F32), jax.ShapeDtypeStruct((B, width), F32)],
        compiler_params=_cparams(("parallel",)),
        name="lru_step_mix_ln",
    )(pb, pb, pb, pb, rw, h, buf, h0, *[W[n] for n in _LRU_WEIGHTS], ln_g, ln_b)
    return out, h_new


def _lru_mix_ln(h, rw, buf8, h0, W, ln_g, ln_b, *, alpha, tt):
    B, T, width = rw.shape
    assert T % tt == 0 and tt % V7X_SUBLANES == 0 and h.shape[2] == width
    nt = T // tt
    tile = pl.BlockSpec((None, tt, width), lambda b, t: (b, t, 0))

    def next_tile(b, t):
        f = jnp.minimum(b * nt + t + 1, B * nt - 1)
        return f // nt, f % nt, 0

    nxt = pl.BlockSpec((None, tt, width), next_tile)
    per_b = lambda rows: pl.BlockSpec((None, rows, width), lambda b, t: (b, 0, 0))
    win = W['w_in']
    n_proj = 4 * width
    assert n_proj % (LRU_PROJ_PIECES * V7X_MXU_DIM) == 0 and (win.shape[1] - n_proj) % (V7X_MXU_DIM // 2) == 0
    out, h_last, tail = pl.pallas_call(
        functools.partial(_lru_kernel, alpha=alpha),
        grid=(B, nt),
        in_specs=[nxt, tile, tile, per_b(V7X_SUBLANES), per_b(1), _resident_spec(win.shape)]
                 + [_const_spec(W[n].shape) for n in _LRU_WEIGHTS] + [_const_spec(ln_g.shape), _const_spec(ln_b.shape)],
        out_specs=[tile, per_b(1), per_b(V7X_SUBLANES)],
        out_shape=[jax.ShapeDtypeStruct((B, T, width), F32), jax.ShapeDtypeStruct((B, 1, width), F32),
                   jax.ShapeDtypeStruct((B, V7X_SUBLANES, width), F32)],
        scratch_shapes=[pltpu.VMEM((2, tt, n_proj), F32), pltpu.VMEM((V7X_SUBLANES, width), F32),
                        pltpu.VMEM((1, width), F32)],
        compiler_params=_cparams(("arbitrary", "arbitrary")),
        name="lru_mix_ln",
    )(h, rw, h, buf8, h0, win, *[W[n] for n in _LRU_WEIGHTS], ln_g, ln_b)
    return out, h_last, tail


def _xattn_kernel(h_ref, kv_ref, wq_ref, wo_ref, g_ref, b_ref, o_ref, *, alpha, heads):
    tm, d = h_ref.shape
    hd = d // heads
    row_groups = max(1, tm // V7X_MXU_DIM)
    tg = tm // row_groups
    for gi in range(row_groups):
        rows = slice(gi * tg, (gi + 1) * tg)
        h = h_ref[rows, :]
        q = jnp.dot(h.astype(BF16), wq_ref[...], preferred_element_type=F32)
        outs = []
        for j in range(heads):
            sl = slice(j * hd, (j + 1) * hd)
            s = _dot_dims(q[:, sl], kv_ref[:, sl], _NT) * (hd ** -0.5)
            e = jnp.exp(s - jnp.max(s, axis=-1, keepdims=True))
            p = e * (1.0 / jnp.sum(e, axis=-1, keepdims=True))
            outs.append(_dot(p, kv_ref[:, d + j * hd:d + (j + 1) * hd]))
        out = jnp.dot(jnp.concatenate(outs, axis=1).astype(BF16), wo_ref[...], preferred_element_type=F32)
        o_ref[rows, :] = _layer_norm(alpha * h + out, g_ref[...], b_ref[...])


def _xattn_rows_kernel(h_ref, mk_ref, mv_ref, wq_ref, wo_ref, g_ref, b_ref, o_ref, *, alpha):
    h = h_ref[...]
    nb, m, heads, hd = mk_ref.shape
    q = jnp.dot(h.astype(BF16), wq_ref[...], preferred_element_type=F32)
    col = lax.broadcasted_iota(jnp.int32, (heads, m * heads), 1)
    own = (col % heads) == lax.broadcasted_iota(jnp.int32, (heads, m * heads), 0)
    rows = []
    for i in range(nb):
        k2 = mk_ref[i].reshape(m * heads, hd)
        v2 = mv_ref[i].reshape(m * heads, hd)
        q4 = jnp.concatenate([q[i:i + 1, j * hd:(j + 1) * hd] for j in range(heads)], axis=0)
        s = jnp.where(own, _dot_dims(q4, k2, _NT) * (hd ** -0.5), -jnp.inf)
        e = jnp.exp(s - jnp.max(s, axis=-1, keepdims=True))
        p = e / jnp.sum(e, axis=-1, keepdims=True)
        o4 = _dot(p, v2)
        rows.append(jnp.concatenate([o4[j:j + 1] for j in range(heads)], axis=1))
    out = jnp.dot(jnp.concatenate(rows, axis=0).astype(BF16), wo_ref[...], preferred_element_type=F32)
    o_ref[...] = _layer_norm(alpha * h + out, g_ref[...], b_ref[...])


def _xattn_ln(h, kv, wq, wo, g, b, *, alpha, heads, rows_per_batch, tile_rows):
    n, d = h.shape
    assert n % tile_rows == 0 and rows_per_batch % tile_rows == 0
    per = rows_per_batch // tile_rows
    m = kv.shape[0] // (n // rows_per_batch)
    return pl.pallas_call(
        functools.partial(_xattn_kernel, alpha=alpha, heads=heads),
        grid=(n // tile_rows,),
        in_specs=[pl.BlockSpec((tile_rows, d), lambda i: (i, 0)),
                  pl.BlockSpec((m, 2 * d), lambda i: (i // per, 0)),
                  _const_spec(wq.shape), _const_spec(wo.shape), _const_spec(g.shape), _const_spec(b.shape)],
        out_specs=pl.BlockSpec((tile_rows, d), lambda i: (i, 0)),
        out_shape=jax.ShapeDtypeStruct((n, d), F32),
        compiler_params=_cparams(("parallel",)),
        name="xattn_ln",
    )(h, kv, wq, wo, g, b)


def _xattn_rows_ln(h, mk, mv, wq, wo, g, b, *, alpha, nb):
    n, d = h.shape
    _, m, heads, hd = mk.shape
    assert n % nb == 0
    kv_spec = pl.BlockSpec((nb, m, heads, hd), lambda i: (i, 0, 0, 0))
    return pl.pallas_call(
        functools.partial(_xattn_rows_kernel, alpha=alpha),
        grid=(n // nb,),
        in_specs=[pl.BlockSpec((nb, d), lambda i: (i, 0)), kv_spec, kv_spec,
                  _const_spec(wq.shape), _const_spec(wo.shape), _const_spec(g.shape), _const_spec(b.shape)],
        out_specs=pl.BlockSpec((nb, d), lambda i: (i, 0)),
        out_shape=jax.ShapeDtypeStruct((n, d), F32),
        compiler_params=_cparams(("parallel",)),
        name="xattn_rows_ln",
    )(h, mk, mv, wq, wo, g, b)


def _kv_proj_kernel(x_ref, w_ref, kv_ref, k_ref, v_ref):
    heads, hd = k_ref.shape[1:]
    d = heads * hd
    kv = jnp.dot(x_ref[...].astype(BF16), w_ref[...], preferred_element_type=F32)
    kv_ref[...] = kv
    for j in range(heads):
        k_ref[:, j, :] = kv[:, j * hd:(j + 1) * hd]
        v_ref[:, j, :] = kv[:, d + j * hd:d + (j + 1) * hd]


def _kv_proj(x, wkv, *, heads, tm):
    n, d = x.shape
    assert n % tm == 0
    hd = d // heads
    out = jax.ShapeDtypeStruct((n, heads, hd), F32)
    return pl.pallas_call(
        _kv_proj_kernel,
        grid=(n // tm,),
        in_specs=[pl.BlockSpec((tm, d), lambda i: (i, 0)), _const_spec(wkv.shape)],
        out_specs=[pl.BlockSpec((tm, 2 * d), lambda i: (i, 0))] + [pl.BlockSpec((tm, heads, hd), lambda i: (i, 0, 0))] * 2,
        out_shape=[jax.ShapeDtypeStruct((n, 2 * d), F32), out, out],
        compiler_params=_cparams(("parallel",)),
        name="kv_proj",
    )(x, wkv)


def _row(v):
    return v.reshape(1, -1)


def _block_diag_groups(w):
    n, c, _ = w.shape
    per = LRU_GROUP // c
    rows = jnp.concatenate([w.reshape(n // per, LRU_GROUP, c)] * per, axis=2)
    ri = lax.broadcasted_iota(jnp.int32, (LRU_GROUP, LRU_GROUP), 0) // c
    ci = lax.broadcasted_iota(jnp.int32, (LRU_GROUP, LRU_GROUP), 1) // c
    return jnp.where(ri == ci, rows, 0.0).astype(BF16)


def _prep_layer(l, ln_g, ln_b, ffn1_wi, ffn1_wo, ffn2_wi, ffn2_wo, w_in, shift_mu, decay_w0, decay_w2,
                aaa_a0, aaa_a2, gate_g2, k_k, k_a, r_k, gn_g, gn_b, conv_w, conv_b, lru_wr, lru_br,
                lru_wi, lru_bi, lru_lambda, w_mix_out, xa_wq, xa_wk, xa_wv, xa_wo):
    width = decay_w0.shape[1]
    bf = lambda w: w.astype(BF16)
    mu = shift_mu[l]
    return dict(
        ln_g=[_row(ln_g[l, i]) for i in range(4)], ln_b=[_row(ln_b[l, i]) for i in range(4)],
        ffn1=(bf(ffn1_wi[l]), bf(ffn1_wo[l])), ffn2=(bf(ffn2_wi[l]), bf(ffn2_wo[l])),
        w_in=bf(w_in[l]),
        shift_mu=_row(mu), mu_r=_row(mu[:width]), mu_k=_row(mu[width:2 * width]),
        mu_v=_row(mu[2 * width:3 * width]), mu_x=_row(mu[3 * width:]),
        decay_w0=_row(decay_w0[l]), decay_w2=bf(decay_w2[l]), aaa_a0=_row(aaa_a0[l]), aaa_a2=bf(aaa_a2[l]),
        gate_g2=bf(gate_g2[l]), k_k=_row(k_k[l]), k_a=_row(k_a[l]), r_k=_row(r_k[l]),
        gn_g=_row(gn_g[l]), gn_b=_row(gn_b[l]),
        conv_w=conv_w[l], conv_b=_row(conv_b[l]),
        lru_wr=_block_diag_groups(lru_wr[l]), lru_br=_row(lru_br[l]),
        lru_wi=_block_diag_groups(lru_wi[l]), lru_bi=_row(lru_bi[l]), lru_lambda=_row(lru_lambda[l]),
        w_mix_out=bf(w_mix_out[l]), xa_wq=bf(xa_wq[l]), xa_wo=bf(xa_wo[l]),
        xa_wkv=bf(jnp.concatenate([xa_wk[l], xa_wv[l]], axis=1)),
    )


def _tile(n, pref):
    return pref if n % pref == 0 else n


def _layer(h, mem, state, shift0, h0, buf0, W, *, alpha, xa_heads):
    B, T, d = h.shape
    n = B * T
    tm = _tile(n, 1024)
    h1 = _ffn_ln(h.reshape(n, d), *W['ffn1'], W['ln_g'][0], W['ln_b'][0], alpha=alpha, tm=tm)
    width = W['decay_w0'].shape[1]
    lru_w = W['conv_b'].shape[1]
    rp = W['shift_mu'].shape[1]
    hist = CONV_WIDTH - 1
    if T > 1:
        assert state is None and T >= V7X_SUBLANES
        pa = _matmul(h1, W['w_in'], tm=tm, tn=rp, n_cols=rp)
        pa3 = pa.reshape(B, T, rp)
        rw, s_new = _rwkv_chunked(pa3, shift0.reshape(B, 1, -1), W, tt=_tile(T, 512), hw=_tile(width, 512))
        buf8 = jnp.concatenate([jnp.zeros((B, V7X_SUBLANES - hist, lru_w), F32), buf0], axis=1)
        h2, h_last, tail8 = _lru_mix_ln(h1.reshape(B, T, d), rw, buf8, h0.reshape(B, 1, lru_w), W,
                                        W['ln_g'][1], W['ln_b'][1], alpha=alpha, tt=_tile(T, 256))
        conv_in_tail = tail8[:, V7X_SUBLANES - hist:]
    else:
        proj = _matmul(h1, W['w_in'], tm=tm, tn=W['w_in'].shape[1])
        pa, pb = proj[:, :rp], proj[:, rp:]
        pa3 = pa.reshape(B, T, rp)
        rw, s_new = _rwkv_step(pa, shift0, jnp.transpose(state, (1, 2, 3, 0)), W, hg=2)
        s_new = jnp.transpose(s_new, (3, 0, 1, 2))
        h2, h_last = _lru_step_mix_ln(pb, rw, h1, jnp.swapaxes(buf0, 0, 1), h0, W,
                                      W['ln_g'][1], W['ln_b'][1], alpha=alpha, tm=_tile(B, 128))
        conv_in_tail = jnp.concatenate([buf0[:, T:], pb[:, None, :lru_w]], axis=1)
    xa = (W['xa_wq'], W['xa_wo'], W['ln_g'][2], W['ln_b'][2])
    if T > 1:
        h3 = _xattn_ln(h2.reshape(n, d), mem, *xa, alpha=alpha, heads=xa_heads, rows_per_batch=T,
                       tile_rows=_tile(T, 1024))
    else:
        h3 = _xattn_rows_ln(h2, *mem, *xa, alpha=alpha, nb=_tile(B, V7X_SUBLANES))
    h4 = _ffn_ln(h3, *W['ffn2'], W['ln_g'][3], W['ln_b'][3], alpha=alpha, tm=tm)
    return h4.reshape(B, T, d), s_new, pa3[:, -1], h_last.reshape(B, lru_w), conv_in_tail


def kernel(x_prompt, x_sample, mem_prompt, cache_mem_k, cache_mem_v, state_rwkv, state_rwkv_shift, state_lru, state_conv, ln_g, ln_b, ffn1_wi, ffn1_wo, ffn2_wi, ffn2_wo, w_in, shift_mu, decay_w0, decay_w2, aaa_a0, aaa_a2, gate_g2, k_k, k_a, r_k, gn_g, gn_b, conv_w, conv_b, lru_wr, lru_br, lru_wi, lru_bi, lru_lambda, w_mix_out, xa_wq, xa_wk, xa_wv, xa_wo):
    depth = ln_g.shape[0]
    alpha = (2.0 * depth) ** 0.25
    B, _, d = x_prompt.shape
    n_mem, xa_heads, xa_head = cache_mem_k.shape[2:]
    rp = shift_mu.shape[1]
    lru_w = conv_b.shape[1]
    hp, hs = x_prompt, x_sample
    outs = [[] for _ in range(10)]
    for l in range(depth):
        W = _prep_layer(l, ln_g, ln_b, ffn1_wi, ffn1_wo, ffn2_wi, ffn2_wo, w_in, shift_mu, decay_w0, decay_w2,
                        aaa_a0, aaa_a2, gate_g2, k_k, k_a, r_k.reshape(depth, -1), gn_g, gn_b, conv_w, conv_b,
                        lru_wr, lru_br, lru_wi, lru_bi, lru_lambda, w_mix_out, xa_wq, xa_wk, xa_wv, xa_wo)
        kv, mk, mv = _kv_proj(mem_prompt.reshape(B * n_mem, d), W['xa_wkv'], heads=xa_heads,
                              tm=_tile(B * n_mem, 512))
        mk = mk.reshape(B, n_mem, xa_heads, xa_head)
        mv = mv.reshape(B, n_mem, xa_heads, xa_head)
        hp, S1, sh1, h1, b1 = _layer(
            hp, kv, None, jnp.zeros((B, rp), F32), jnp.zeros((B, lru_w), F32),
            jnp.zeros((B, CONV_WIDTH - 1, lru_w), F32), W, alpha=alpha, xa_heads=xa_heads)
        hs, S2, sh2, h2, b2 = _layer(
            hs, (cache_mem_k[l], cache_mem_v[l]),
            state_rwkv[l], state_rwkv_shift[l], state_lru[l], state_conv[l], W, alpha=alpha, xa_heads=xa_heads)
        for lst, val in zip(outs, (mk, mv, S1, sh1, h1, b1, S2, sh2, h2, b2)):
            lst.append(val)
    return (hp, hs) + tuple(jnp.stack(o) for o in outs)
```

```python
import functools

import jax
import jax.numpy as jnp
from jax import lax
from jax.experimental import pallas as pl
from jax.experimental.pallas import tpu as pltpu

F32 = jnp.float32
BF16 = jnp.bfloat16

RWKV_HEAD = 64
DECAY_LORA = 64
AAA_LORA = 64
GATE_LORA = 128
GN_EPS = 64e-5
CONV_WIDTH = 4
LRU_C = 8.0
LN_EPS = 1e-5

V7X_SUBLANES = 8
V7X_MXU_DIM = 256
V7X_SCOPED_VMEM_BYTES = 60000 * 1024

RWKV_CHUNK = 64
HEAD_PAIR = 2 * RWKV_HEAD
SCAN_ROW_GROUPS = 2
LRU_GROUP = V7X_MXU_DIM
LRU_PROJ_PIECES = 8


def _cparams(semantics):
    return pltpu.CompilerParams(dimension_semantics=semantics, vmem_limit_bytes=V7X_SCOPED_VMEM_BYTES)


def _const_spec(shape):
    zeros = (0,) * len(shape)
    return pl.BlockSpec(shape, lambda *_: zeros)


def _dot(a, b):
    return jnp.dot(a.astype(BF16), b.astype(BF16), preferred_element_type=F32)


def _dot_dims(a, b, dims):
    return lax.dot_general(a.astype(BF16), b.astype(BF16), (dims, ((), ())), preferred_element_type=F32)


_NN = ((1,), (0,))
_NT = ((1,), (1,))
_TN = ((0,), (0,))


def _split2(x):
    hi = x.astype(BF16)
    lo = (x - hi.astype(F32)).astype(BF16)
    return hi, lo


def _dot_exact_lhs(a_bf16, b):
    hi, lo = _split2(b)
    d = lambda y: jnp.dot(a_bf16, y, preferred_element_type=F32)
    return d(hi) + d(lo)


def _dot_exact_rhs(a, b_bf16):
    hi, lo = _split2(a)
    d = lambda x: jnp.dot(x, b_bf16, preferred_element_type=F32)
    return d(hi) + d(lo)


def _layer_norm(x, g, b):
    mu = jnp.mean(x, axis=-1, keepdims=True)
    xc = x - mu
    var = jnp.mean(xc * xc, axis=-1, keepdims=True)
    return xc * lax.rsqrt(var + LN_EPS) * g + b


def _softplus(z):
    return jnp.maximum(z, 0.0) + jnp.log(1.0 + jnp.exp(-jnp.abs(z)))


def _head_ones(width):
    r = lax.broadcasted_iota(jnp.int32, (width, width), 0) // RWKV_HEAD
    c = lax.broadcasted_iota(jnp.int32, (width, width), 1) // RWKV_HEAD
    return (r == c).astype(BF16)


def _mm_kernel(x_ref, w_ref, o_ref):
    o_ref[...] = jnp.dot(x_ref[...].astype(BF16), w_ref[...], preferred_element_type=F32)


def _matmul(x, w, *, tm, tn, n_cols=None):
    n, k = x.shape
    m = w.shape[1] if n_cols is None else n_cols
    assert n % tm == 0 and m % tn == 0 and m <= w.shape[1]
    return pl.pallas_call(
        _mm_kernel,
        grid=(m // tn, n // tm),
        in_specs=[pl.BlockSpec((tm, k), lambda j, i: (i, 0)),
                  pl.BlockSpec((k, tn), lambda j, i: (0, j))],
        out_specs=pl.BlockSpec((tm, tn), lambda j, i: (i, j)),
        out_shape=jax.ShapeDtypeStruct((n, m), F32),
        compiler_params=_cparams(("parallel", "parallel")),
        name="matmul",
    )(x, w)


def _ffn_kernel(x_ref, wi_ref, wo_ref, g_ref, b_ref, o_ref, *, alpha, row_groups):
    tm = x_ref.shape[0]
    d_ff = wo_ref.shape[0]
    tg = tm // row_groups
    for gi in range(row_groups):
        rows = slice(gi * tg, (gi + 1) * tg)
        x = x_ref[rows, :]
        xb = x.astype(BF16)
        gate = jnp.dot(xb, wi_ref[:, :d_ff], preferred_element_type=F32)
        up = jnp.dot(xb, wi_ref[:, d_ff:], preferred_element_type=F32)
        mid = (gate * jax.nn.sigmoid(gate) * up).astype(BF16)
        down = jnp.dot(mid, wo_ref[...], preferred_element_type=F32)
        o_ref[rows, :] = _layer_norm(alpha * x + 0.5 * down, g_ref[...], b_ref[...])


def _resident_spec(shape):
    zeros = (0,) * len(shape)
    return pl.BlockSpec(shape, lambda *_: zeros, pipeline_mode=pl.Buffered(1))


def _ffn_ln(x, wi, wo, g, b, *, alpha, tm):
    n, d = x.shape
    assert n % tm == 0 and wi.shape[1] == 2 * wo.shape[0]
    row_groups = max(1, tm // V7X_MXU_DIM)
    return pl.pallas_call(
        functools.partial(_ffn_kernel, alpha=alpha, row_groups=row_groups),
        grid=(n // tm,),
        in_specs=[pl.BlockSpec((tm, d), lambda i: (i, 0)),
                  _resident_spec(wi.shape), _resident_spec(wo.shape),
                  _const_spec(g.shape), _const_spec(b.shape)],
        out_specs=pl.BlockSpec((tm, d), lambda i: (i, 0)),
        out_shape=jax.ShapeDtypeStruct((n, d), F32),
        compiler_params=_cparams(("parallel",)),
        name="ffn_ln",
    )(x, wi, wo, g, b)


def _rwkv_pre(r, k, v, xx, w0, w2, a0, a2, g2, k_k, k_a, r_k, ones):
    xw = xx[:, :DECAY_LORA]
    xa = xx[:, DECAY_LORA:DECAY_LORA + AAA_LORA]
    xg = xx[:, DECAY_LORA + AAA_LORA:]
    z = w0 + _dot(jnp.tanh(xw), w2)
    lw = -jnp.exp(-_softplus(-z) - 0.5)
    a = jax.nn.sigmoid(a0 + _dot(xa, a2))
    g = _dot(jax.nn.sigmoid(xg), g2)
    kkr = k * k_k
    ss = _dot(kkr * kkr, ones)
    kk = kkr * lax.rsqrt(jnp.maximum(ss, 1e-24))
    kf = k * (1.0 + (a - 1.0) * k_a)
    bonus = _dot(r * kf * r_k, ones) * v
    return lw, a, g, kk, kf, bonus


def _rwkv_post(y, bonus, g, gn_g, gn_b, ones):
    inv_n = 1.0 / RWKV_HEAD
    ym = _dot_exact_rhs(y, ones) * inv_n
    yc = y - ym
    yv = _dot(yc * yc, ones) * inv_n
    yn = yc * lax.rsqrt(yv + GN_EPS) * gn_g + gn_b
    return (yn + bonus) * g


def _bdot(a, b, dims):
    dn = ((tuple(d + 1 for d in dims[0]), tuple(d + 1 for d in dims[1])), ((0,), (0,)))
    return lax.dot_general(a.astype(BF16), b.astype(BF16), dn, preferred_element_type=F32)


def _scan_operands(r, kf, v, kk, a, lw):
    tt, hw = r.shape
    C = RWKV_CHUNK
    n_pairs = hw // HEAD_PAIR
    ltri = (lax.broadcasted_iota(jnp.int32, (C, C), 0) >= lax.broadcasted_iota(jnp.int32, (C, C), 1)).astype(BF16)
    first = lax.broadcasted_iota(jnp.int32, (C, HEAD_PAIR), 1) < RWKV_HEAD

    def bd(x):
        return jnp.concatenate([jnp.where(first, x, 0.0), jnp.where(first, 0.0, x)], axis=1)

    names = ('a', 'r', 'b', 'k', 'v', 'bh', 'kh')
    ops = {n: [] for n in names}
    wcs = []
    for c in range(tt // C):
        rows = slice(c * C, (c + 1) * C)
        lw_c = lw[rows]
        L = _dot_exact_lhs(ltri, lw_c)
        Lc = L[C - 1:C, :]
        e_nl = jnp.exp(-L)
        e_c = jnp.exp(Lc - L)
        bb = kk[rows] * a[rows]
        bf = lambda x: x.astype(BF16)
        tile = dict(a=bf(-kk[rows] * jnp.exp(L - lw_c)), r=r[rows] * jnp.exp(L), b=bf(bb * e_nl),
                    k=bf(kf[rows] * e_nl), v=bf(v[rows]), bh=bf(bb * e_c), kh=bf(kf[rows] * e_c))
        wc = jnp.exp(Lc)
        for p in range(n_pairs):
            lanes = slice(p * HEAD_PAIR, (p + 1) * HEAD_PAIR)
            for n in names:
                ops[n].append(tile[n][:, lanes])
            wcs.append(wc[:, lanes])
    A, R, B, K, V, Bh, Kh = (jnp.stack(ops[n]) for n in names)
    Vbd = bd(V)
    G = _bdot(jnp.concatenate([A, R.astype(BF16)], axis=1), jnp.concatenate([bd(B), bd(K)], axis=1), _NT)
    tok = lax.broadcasted_iota(jnp.int32, (C, HEAD_PAIR), 0)
    src = lax.broadcasted_iota(jnp.int32, (C, HEAD_PAIR), 1) % RWKV_HEAD
    a_ab = jnp.where(tok > src, G[:, :C, :HEAD_PAIR], 0.0)
    a_ak = jnp.where(tok > src, G[:, :C, HEAD_PAIR:], 0.0).astype(BF16)
    a_rb = jnp.where(tok >= src, G[:, C:, :HEAD_PAIR], 0.0).astype(BF16)
    a_rk = jnp.where(tok >= src, G[:, C:, HEAD_PAIR:], 0.0).astype(BF16)
    P = jnp.where(tok == src, 1.0, 0.0) + a_ab
    N = a_ab.astype(BF16)
    N = _bdot(N, bd(N), _NN).astype(BF16)
    steps = 2
    while 2 * steps < C:
        NP = _bdot(jnp.concatenate([N, P.astype(BF16)], axis=1), bd(N), _NN)
        N = NP[:, :C].astype(BF16)
        P = P + NP[:, C:]
        steps *= 2
    P = (P + _bdot(P, bd(N), _NN)).astype(BF16)
    aV = _bdot(a_ak, Vbd, _NN).astype(BF16)
    XU = _bdot(P, jnp.concatenate([bd(A), bd(aV)], axis=2), _NN).astype(BF16)
    X1 = XU[:, :, :HEAD_PAIR]
    Uloc = XU[:, :, HEAD_PAIR:]
    Q = (R + _bdot(a_rb, bd(X1), _NN)).astype(BF16)
    Yloc = _bdot(jnp.concatenate([a_rb, a_rk], axis=2), jnp.concatenate([bd(Uloc), Vbd], axis=1), _NN)
    ri = lax.broadcasted_iota(jnp.int32, (HEAD_PAIR, HEAD_PAIR), 0) // RWKV_HEAD
    ci = lax.broadcasted_iota(jnp.int32, (HEAD_PAIR, HEAD_PAIR), 1) // RWKV_HEAD
    same_head = ri == ci
    Pm = jnp.where(same_head, _bdot(X1, Bh, _TN), 0.0).astype(BF16)
    Sloc = jnp.where(same_head, _bdot(jnp.concatenate([Uloc, V], axis=1), jnp.concatenate([Bh, Kh], axis=1), _TN),
                     0.0)
    return Q, Yloc, Pm, Sloc, jnp.stack(wcs)


def _rwkv_chunk_kernel(pr_ref, pk_ref, pv_ref, px_ref, sr_ref, sk_ref, sv_ref, sx_ref,
                       mur_ref, muk_ref, muv_ref, mux_ref, w0_ref, w2_ref, a0_ref, a2_ref, g2_ref,
                       kk_ref, ka_ref, rk_ref, gng_ref, gnb_ref, ones_ref,
                       out_ref, s_out_ref,
                       s_scr, cr_scr, ck_scr, cv_scr, cx_scr, y_scr):
    ti = pl.program_id(2)
    tt, hw = pr_ref.shape
    n_pairs = hw // HEAD_PAIR

    @pl.when(ti == 0)
    def _init():
        s_scr[...] = jnp.zeros(s_scr.shape, F32)
        cr_scr[...] = sr_ref[...]
        ck_scr[...] = sk_ref[...]
        cv_scr[...] = sv_ref[...]
        cx_scr[...] = sx_ref[...]

    S8 = V7X_SUBLANES

    def shifted(p_ref, c_scr, mu_ref):
        p = p_ref[...]
        first = lax.broadcasted_iota(jnp.int32, (S8, p.shape[1]), 0) == 0
        head = jnp.where(first, c_scr[...], pltpu.roll(p[:S8], 1, 0))
        prev = jnp.concatenate([head, p_ref[S8 - 1:tt - 1, :]], axis=0)
        c_scr[...] = p_ref[tt - 1:tt, :]
        return p + (prev - p) * mu_ref[...]

    r = shifted(pr_ref, cr_scr, mur_ref)
    k = shifted(pk_ref, ck_scr, muk_ref)
    v = shifted(pv_ref, cv_scr, muv_ref)
    xx = shifted(px_ref, cx_scr, mux_ref)
    ones = ones_ref[...]
    lw, a, g, kk, kf, bonus = _rwkv_pre(r, k, v, xx, w0_ref[...], w2_ref[...], a0_ref[...], a2_ref[...],
                                        g2_ref[...], kk_ref[...], ka_ref[...], rk_ref[...], ones)
    C = RWKV_CHUNK
    S = s_scr[...]
    n_groups = SCAN_ROW_GROUPS if tt % (SCAN_ROW_GROUPS * C) == 0 else 1
    tg = tt // n_groups
    for gi in range(n_groups):
        rows = slice(gi * tg, (gi + 1) * tg)
        Q, Yloc, Pm, Sloc, wc = _scan_operands(r[rows], kf[rows], v[rows], kk[rows], a[rows], lw[rows])
        for c in range(tg // C):
            inst = slice(c * n_pairs, (c + 1) * n_pairs)
            y_c = _bdot(Q[inst], S, _NT) + Yloc[inst]
            row0 = gi * tg + c * C
            for p in range(n_pairs):
                y_scr[row0:row0 + C, p * HEAD_PAIR:(p + 1) * HEAD_PAIR] = y_c[p]
            S = S * wc[inst] + _bdot(S, Pm[inst], _NN) + Sloc[inst]
    s_scr[...] = S
    out_ref[...] = _rwkv_post(y_scr[...], bonus, g, gng_ref[...], gnb_ref[...], ones)

    @pl.when(ti == pl.num_programs(2) - 1)
    def _emit_state():
        for p in range(n_pairs):
            s_out_ref[2 * p] = S[p, :RWKV_HEAD, :RWKV_HEAD]
            s_out_ref[2 * p + 1] = S[p, RWKV_HEAD:, RWKV_HEAD:]


def _rwkv_chunked(pa, shift0, W, *, tt, hw):
    B, T, _ = pa.shape
    width = W['decay_w0'].shape[1]
    heads = width // RWKV_HEAD
    assert T % tt == 0 and tt % RWKV_CHUNK == 0 and width % hw == 0 and hw % HEAD_PAIR == 0
    nb = width // hw
    xw = DECAY_LORA + AAA_LORA + GATE_LORA
    assert (3 * width) % xw == 0
    xblk = 3 * width // xw
    col = lambda off: (lambda b, h, t: (b, t, off + h))
    vec = lambda: pl.BlockSpec((1, hw), lambda b, h, t: (0, h))
    in_specs = [
        pl.BlockSpec((None, tt, hw), col(0)), pl.BlockSpec((None, tt, hw), col(nb)),
        pl.BlockSpec((None, tt, hw), col(2 * nb)), pl.BlockSpec((None, tt, xw), lambda b, h, t: (b, t, xblk)),
        pl.BlockSpec((None, 1, hw), lambda b, h, t: (b, 0, h)), pl.BlockSpec((None, 1, hw), lambda b, h, t: (b, 0, nb + h)),
        pl.BlockSpec((None, 1, hw), lambda b, h, t: (b, 0, 2 * nb + h)), pl.BlockSpec((None, 1, xw), lambda b, h, t: (b, 0, xblk)),
        vec(), vec(), vec(), _const_spec((1, xw)),
        vec(), pl.BlockSpec((DECAY_LORA, hw), lambda b, h, t: (0, h)),
        vec(), pl.BlockSpec((AAA_LORA, hw), lambda b, h, t: (0, h)),
        pl.BlockSpec((GATE_LORA, hw), lambda b, h, t: (0, h)),
        vec(), vec(), vec(), vec(), vec(), _const_spec((hw, hw)),
    ]
    out, s_new = pl.pallas_call(
        _rwkv_chunk_kernel,
        grid=(B, nb, T // tt),
        in_specs=in_specs,
        out_specs=[pl.BlockSpec((None, tt, hw), lambda b, h, t: (b, t, h)),
                   pl.BlockSpec((None, hw // RWKV_HEAD, RWKV_HEAD, RWKV_HEAD), lambda b, h, t: (b, h, 0, 0))],
        out_shape=[jax.ShapeDtypeStruct((B, T, width), F32),
                   jax.ShapeDtypeStruct((B, heads, RWKV_HEAD, RWKV_HEAD), F32)],
        scratch_shapes=[pltpu.VMEM((hw // HEAD_PAIR, HEAD_PAIR, HEAD_PAIR), F32),
                        pltpu.VMEM((1, hw), F32), pltpu.VMEM((1, hw), F32), pltpu.VMEM((1, hw), F32),
                        pltpu.VMEM((1, xw), F32), pltpu.VMEM((tt, hw), F32)],
        compiler_params=_cparams(("parallel", "parallel", "arbitrary")),
        name="rwkv_chunked",
    )(pa, pa, pa, pa, shift0, shift0, shift0, shift0,
      W['mu_r'], W['mu_k'], W['mu_v'], W['mu_x'], W['decay_w0'], W['decay_w2'], W['aaa_a0'], W['aaa_a2'],
      W['gate_g2'], W['k_k'], W['k_a'], W['r_k'], W['gn_g'], W['gn_b'], _head_ones(hw))
    return out, s_new


def _rwkv_step_kernel(p_ref, s0_ref, st_ref, mu_ref, w0_ref, w2_ref, a0_ref, a2_ref, g2_ref,
                      kk_ref, ka_ref, rk_ref, gng_ref, gnb_ref, ones_ref,
                      out_ref, st_out_ref, vec_scr, y_scr, bonus_scr, g_scr):
    i = pl.program_id(0)
    B, width = out_ref.shape
    hg = st_ref.shape[0]
    H = RWKV_HEAD

    @pl.when(i == 0)
    def _prologue():
        ones = ones_ref[...]
        p = p_ref[...]
        xs = p + (s0_ref[...] - p) * mu_ref[...]
        r, k, v, xx = xs[:, :width], xs[:, width:2 * width], xs[:, 2 * width:3 * width], xs[:, 3 * width:]
        lw, a, g, kk, kf, bonus = _rwkv_pre(r, k, v, xx, w0_ref[...], w2_ref[...], a0_ref[...], a2_ref[...],
                                            g2_ref[...], kk_ref[...], ka_ref[...], rk_ref[...], ones)
        for j, vec in enumerate((r, kf, v, kk, kk * a, jnp.exp(lw))):
            vec_scr[j] = vec.T
        bonus_scr[...] = bonus
        g_scr[...] = g

    for hl in range(hg):
        base = pl.multiple_of((i * hg + hl) * H, H)
        r_h, kf_h, kk_h, kka_h, w_h = (vec_scr[j, pl.ds(base, H), :] for j in (0, 1, 3, 4, 5))

        def body(vi, carry):
            S = st_ref[hl, vi]
            sa = jnp.sum(S * kk_h, axis=0, keepdims=True)
            v_row = vec_scr[2, pl.ds(base + vi, 1), :]
            S2 = S * w_h - sa * kka_h + v_row * kf_h
            st_out_ref[hl, vi] = S2
            y_scr[pl.ds(base + vi, 1), :] = jnp.sum(S2 * r_h, axis=0, keepdims=True)
            return carry

        lax.fori_loop(0, H, body, 0, unroll=8)

    @pl.when(i == pl.num_programs(0) - 1)
    def _epilogue():
        out_ref[...] = _rwkv_post(y_scr[...].T, bonus_scr[...], g_scr[...], gng_ref[...], gnb_ref[...],
                                  ones_ref[...])


def _rwkv_step(pa, shift0, state_t, W, *, hg):
    B, proj = pa.shape
    width = W['decay_w0'].shape[1]
    heads = width // RWKV_HEAD
    assert heads % hg == 0 and state_t.shape == (heads, RWKV_HEAD, RWKV_HEAD, B)
    st_spec = pl.BlockSpec((hg, RWKV_HEAD, RWKV_HEAD, B), lambda i: (i, 0, 0, 0))
    names = ('decay_w0', 'decay_w2', 'aaa_a0', 'aaa_a2', 'gate_g2', 'k_k', 'k_a', 'r_k', 'gn_g', 'gn_b')
    out, st_new = pl.pallas_call(
        _rwkv_step_kernel,
        grid=(heads // hg,),
        in_specs=[_const_spec((B, proj)), _const_spec((B, proj)), st_spec, _const_spec(W['shift_mu'].shape)]
                 + [_const_spec(W[n].shape) for n in names] + [_const_spec((width, width))],
        out_specs=[_const_spec((B, width)), st_spec],
        out_shape=[jax.ShapeDtypeStruct((B, width), F32), jax.ShapeDtypeStruct(state_t.shape, F32)],
        scratch_shapes=[pltpu.VMEM((6, width, B), F32), pltpu.VMEM((width, B), F32),
                        pltpu.VMEM((B, width), F32), pltpu.VMEM((B, width), F32)],
        compiler_params=_cparams(("arbitrary",)),
        name="rwkv_step",
    )(pa, shift0, state_t, W['shift_mu'], *[W[n] for n in names], _head_ones(width))
    return out, st_new


def _lru_coeffs(xc, wr_ref, br_ref, wi_ref, bi_ref, lam_ref):
    xcb = xc.astype(BF16)
    n_grp = xc.shape[1] // LRU_GROUP
    grp = lambda w_ref: jnp.concatenate(
        [jnp.dot(xcb[:, i * LRU_GROUP:(i + 1) * LRU_GROUP], w_ref[i], preferred_element_type=F32)
         for i in range(n_grp)], axis=1)
    gr = jax.nn.sigmoid(grp(wr_ref) + br_ref[...])
    gi = jax.nn.sigmoid(grp(wi_ref) + bi_ref[...])
    a_t = jnp.exp(-LRU_C * gr * _softplus(-lam_ref[...]))
    return a_t, jnp.sqrt(1.0 - a_t * a_t) * gi * xc


def _lru_finish(hs, pg, g0, g1, rw, h, wmix_ref, lng_ref, lnb_ref, alpha):
    lru_out = hs * jax.nn.gelu(pg)
    merged = jax.nn.sigmoid(g0) * rw + jax.nn.sigmoid(g1) * lru_out
    mix = jnp.dot(merged.astype(BF16), wmix_ref[...], preferred_element_type=F32)
    return _layer_norm(alpha * h + mix, lng_ref[...], lnb_ref[...])


def _lru_kernel(hn_ref, rw_ref, h_ref, buf_ref, h0_ref, win_ref,
                cw_ref, cb_ref, wr_ref, br_ref, wi_ref, bi_ref, lam_ref, wmix_ref, lng_ref, lnb_ref,
                out_ref, hlast_ref, tail_ref, proj_scr, tail_scr, hc_scr, *, alpha):
    ti = pl.program_id(1)
    flat = pl.program_id(0) * pl.num_programs(1) + ti
    tt, width = rw_ref.shape
    S8 = V7X_SUBLANES
    project = lambda ref: jnp.dot(ref[...].astype(BF16), win_ref[...], preferred_element_type=F32)
    col = lambda j: slice(j * width, (j + 1) * width)

    @pl.when(ti == 0)
    def _init():
        tail_scr[...] = buf_ref[...]
        hc_scr[...] = h0_ref[...]

    @pl.when(flat == 0)
    def _first_projection():
        proj_scr[0] = project(h_ref)

    def step(cur_scr, nxt_scr):
        hn = hn_ref[...].astype(BF16)
        piece_w = win_ref.shape[1] // LRU_PROJ_PIECES
        pieces = iter(range(LRU_PROJ_PIECES))

        def emit(count=1):
            for _ in range(count):
                k = next(pieces, None)
                if k is not None:
                    cols = slice(k * piece_w, (k + 1) * piece_w)
                    nxt_scr[:, cols] = jnp.dot(hn, win_ref[:, cols], preferred_element_type=F32)

        x = cur_scr[:, col(0)]
        tail = tail_scr[...]
        r8 = lax.broadcasted_iota(jnp.int32, (S8, width), 0)

        def delayed(d):
            head = jnp.where(r8 < d, pltpu.roll(tail, d, 0), pltpu.roll(x[:S8], d, 0))
            return jnp.concatenate([head, cur_scr[S8 - d:tt - d, col(0)]], axis=0)

        cw = cw_ref[...]
        conv = cw[0:1] * delayed(CONV_WIDTH - 1)
        for j in range(1, CONV_WIDTH - 1):
            conv = conv + cw[j:j + 1] * delayed(CONV_WIDTH - 1 - j)
        xc = cb_ref[...] + (conv + cw[CONV_WIDTH - 1:CONV_WIDTH] * x)
        tail_scr[...] = x[tt - S8:, :]
        tail_ref[...] = x[tt - S8:, :]
        emit()

        A, Bv = _lru_coeffs(xc, wr_ref, br_ref, wi_ref, bi_ref, lam_ref)
        emit()
        A = A.reshape(tt // S8, S8, width)
        Bv = Bv.reshape(tt // S8, S8, width)
        in_grp = lax.broadcasted_iota(jnp.int32, (S8, width), 0)
        s = 1
        while s < S8:
            keep = in_grp >= s
            Bv = Bv + A * jnp.where(keep, pltpu.roll(Bv, s, 1), 0.0)
            A = A * jnp.where(keep, pltpu.roll(A, s, 1), 1.0)
            s *= 2
            emit()
        carry = hc_scr[...]
        groups = []
        for gi in range(tt // S8):
            hg = Bv[gi] + A[gi] * carry
            groups.append(hg)
            carry = hg[S8 - 1:S8, :]
        hs = jnp.concatenate(groups, axis=0)
        hc_scr[...] = carry
        hlast_ref[...] = carry
        emit()
        lru_out = hs * jax.nn.gelu(cur_scr[:, col(1)])
        emit()
        merged = jax.nn.sigmoid(cur_scr[:, col(2)]) * rw_ref[...] + jax.nn.sigmoid(cur_scr[:, col(3)]) * lru_out
        emit()
        mix = jnp.dot(merged.astype(BF16), wmix_ref[...], preferred_element_type=F32)
        out_ref[...] = _layer_norm(alpha * h_ref[...] + mix, lng_ref[...], lnb_ref[...])
        emit(LRU_PROJ_PIECES)

    slot = flat % 2
    step(proj_scr.at[slot], proj_scr.at[1 - slot])


def _lru_step_kernel(x_ref, pg_ref, g0_ref, g1_ref, rw_ref, h_ref, buf_ref, h0_ref,
                     cw_ref, cb_ref, wr_ref, br_ref, wi_ref, bi_ref, lam_ref, wmix_ref, lng_ref, lnb_ref,
                     out_ref, hnew_ref, *, alpha):
    cw = cw_ref[...]
    conv = cw[0:1] * buf_ref[0]
    for j in range(1, CONV_WIDTH - 1):
        conv = conv + cw[j:j + 1] * buf_ref[j]
    xc = cb_ref[...] + (conv + cw[CONV_WIDTH - 1:CONV_WIDTH] * x_ref[...])
    A, Bv = _lru_coeffs(xc, wr_ref, br_ref, wi_ref, bi_ref, lam_ref)
    hs = Bv + A * h0_ref[...]
    hnew_ref[...] = hs
    out_ref[...] = _lru_finish(hs, pg_ref[...], g0_ref[...], g1_ref[...], rw_ref[...], h_ref[...],
                               wmix_ref, lng_ref, lnb_ref, alpha)


_LRU_WEIGHTS = ('conv_w', 'conv_b', 'lru_wr', 'lru_br', 'lru_wi', 'lru_bi', 'lru_lambda', 'w_mix_out')


def _lru_step_mix_ln(pb, rw, h, buf, h0, W, ln_g, ln_b, *, alpha, tm):
    B, width = rw.shape
    assert B % tm == 0
    blk = lambda j: pl.BlockSpec((tm, width), lambda i: (i, j))
    out, h_new = pl.pallas_call(
        functools.partial(_lru_step_kernel, alpha=alpha),
        grid=(B // tm,),
        in_specs=[blk(0), blk(1), blk(2), blk(3), blk(0), blk(0),
                  pl.BlockSpec((CONV_WIDTH - 1, tm, width), lambda i: (0, i, 0)), blk(0)]
                 + [_const_spec(W[n].shape) for n in _LRU_WEIGHTS] + [_const_spec(ln_g.shape), _const_spec(ln_b.shape)],
        out_specs=[blk(0), blk(0)],
        out_shape=[jax.ShapeDtypeStruct((B, width), F32), jax.ShapeDtypeStruct((B, width), F32)],
        compiler_params=_cparams(("parallel",)),
        name="lru_step_mix_ln",
    )(pb, pb, pb, pb, rw, h, buf, h0, *[W[n] for n in _LRU_WEIGHTS], ln_g, ln_b)
    return out, h_new


def _lru_mix_ln(h, rw, buf8, h0, W, ln_g, ln_b, *, alpha, tt):
    B, T, width = rw.shape
    assert T % tt == 0 and tt % V7X_SUBLANES == 0 and h.shape[2] == width
    nt = T // tt
    tile = pl.BlockSpec((None, tt, width), lambda b, t: (b, t, 0))

    def next_tile(b, t):
        f = jnp.minimum(b * nt + t + 1, B * nt - 1)
        return f // nt, f % nt, 0

    nxt = pl.BlockSpec((None, tt, width), next_tile)
    per_b = lambda rows: pl.BlockSpec((None, rows, width), lambda b, t: (b, 0, 0))
    win = W['w_in_lru']
    n_proj = win.shape[1]
    assert n_proj == 4 * width and n_proj % (LRU_PROJ_PIECES * V7X_MXU_DIM) == 0
    out, h_last, tail = pl.pallas_call(
        functools.partial(_lru_kernel, alpha=alpha),
        grid=(B, nt),
        in_specs=[nxt, tile, tile, per_b(V7X_SUBLANES), per_b(1), _const_spec(win.shape)]
                 + [_const_spec(W[n].shape) for n in _LRU_WEIGHTS] + [_const_spec(ln_g.shape), _const_spec(ln_b.shape)],
        out_specs=[tile, per_b(1), per_b(V7X_SUBLANES)],
        out_shape=[jax.ShapeDtypeStruct((B, T, width), F32), jax.ShapeDtypeStruct((B, 1, width), F32),
                   jax.ShapeDtypeStruct((B, V7X_SUBLANES, width), F32)],
        scratch_shapes=[pltpu.VMEM((2, tt, n_proj), F32), pltpu.VMEM((V7X_SUBLANES, width), F32),
                        pltpu.VMEM((1, width), F32)],
        compiler_params=_cparams(("arbitrary", "arbitrary")),
        name="lru_mix_ln",
    )(h, rw, h, buf8, h0, win, *[W[n] for n in _LRU_WEIGHTS], ln_g, ln_b)
    return out, h_last, tail


def _xattn_kernel(h_ref, kv_ref, wq_ref, wo_ref, g_ref, b_ref, o_ref, *, alpha, heads):
    tm, d = h_ref.shape
    hd = d // heads
    row_groups = max(1, tm // V7X_MXU_DIM)
    tg = tm // row_groups
    for gi in range(row_groups):
        rows = slice(gi * tg, (gi + 1) * tg)
        h = h_ref[rows, :]
        q = jnp.dot(h.astype(BF16), wq_ref[...], preferred_element_type=F32)
        outs = []
        for j in range(heads):
            sl = slice(j * hd, (j + 1) * hd)
            s = _dot_dims(q[:, sl], kv_ref[:, sl], _NT) * (hd ** -0.5)
            e = jnp.exp(s - jnp.max(s, axis=-1, keepdims=True))
            p = e * (1.0 / jnp.sum(e, axis=-1, keepdims=True))
            outs.append(_dot(p, kv_ref[:, d + j * hd:d + (j + 1) * hd]))
        out = jnp.dot(jnp.concatenate(outs, axis=1).astype(BF16), wo_ref[...], preferred_element_type=F32)
        o_ref[rows, :] = _layer_norm(alpha * h + out, g_ref[...], b_ref[...])


def _xattn_rows_kernel(h_ref, mk_ref, mv_ref, wq_ref, wo_ref, g_ref, b_ref, o_ref, *, alpha):
    h = h_ref[...]
    nb, m, heads, hd = mk_ref.shape
    q = jnp.dot(h.astype(BF16), wq_ref[...], preferred_element_type=F32)
    col = lax.broadcasted_iota(jnp.int32, (heads, m * heads), 1)
    own = (col % heads) == lax.broadcasted_iota(jnp.int32, (heads, m * heads), 0)
    rows = []
    for i in range(nb):
        k2 = mk_ref[i].reshape(m * heads, hd)
        v2 = mv_ref[i].reshape(m * heads, hd)
        q4 = jnp.concatenate([q[i:i + 1, j * hd:(j + 1) * hd] for j in range(heads)], axis=0)
        s = jnp.where(own, _dot_dims(q4, k2, _NT) * (hd ** -0.5), -jnp.inf)
        e = jnp.exp(s - jnp.max(s, axis=-1, keepdims=True))
        p = e / jnp.sum(e, axis=-1, keepdims=True)
        o4 = _dot(p, v2)
        rows.append(jnp.concatenate([o4[j:j + 1] for j in range(heads)], axis=1))
    out = jnp.dot(jnp.concatenate(rows, axis=0).astype(BF16), wo_ref[...], preferred_element_type=F32)
    o_ref[...] = _layer_norm(alpha * h + out, g_ref[...], b_ref[...])


def _xattn_ln(h, kv, wq, wo, g, b, *, alpha, heads, rows_per_batch, tile_rows):
    n, d = h.shape
    assert n % tile_rows == 0 and rows_per_batch % tile_rows == 0
    per = rows_per_batch // tile_rows
    m = kv.shape[0] // (n // rows_per_batch)
    return pl.pallas_call(
        functools.partial(_xattn_kernel, alpha=alpha, heads=heads),
        grid=(n // tile_rows,),
        in_specs=[pl.BlockSpec((tile_rows, d), lambda i: (i, 0)),
                  pl.BlockSpec((m, 2 * d), lambda i: (i // per, 0)),
                  _const_spec(wq.shape), _const_spec(wo.shape), _const_spec(g.shape), _const_spec(b.shape)],
        out_specs=pl.BlockSpec((tile_rows, d), lambda i: (i, 0)),
        out_shape=jax.ShapeDtypeStruct((n, d), F32),
        compiler_params=_cparams(("parallel",)),
        name="xattn_ln",
    )(h, kv, wq, wo, g, b)


def _xattn_rows_ln(h, mk, mv, wq, wo, g, b, *, alpha, nb):
    n, d = h.shape
    _, m, heads, hd = mk.shape
    assert n % nb == 0
    kv_spec = pl.BlockSpec((nb, m, heads, hd), lambda i: (i, 0, 0, 0))
    return pl.pallas_call(
        functools.partial(_xattn_rows_kernel, alpha=alpha),
        grid=(n // nb,),
        in_specs=[pl.BlockSpec((nb, d), lambda i: (i, 0)), kv_spec, kv_spec,
                  _const_spec(wq.shape), _const_spec(wo.shape), _const_spec(g.shape), _const_spec(b.shape)],
        out_specs=pl.BlockSpec((nb, d), lambda i: (i, 0)),
        out_shape=jax.ShapeDtypeStruct((n, d), F32),
        compiler_params=_cparams(("parallel",)),
        name="xattn_rows_ln",
    )(h, mk, mv, wq, wo, g, b)


def _kv_proj_kernel(x_ref, w_ref, kv_ref, k_ref, v_ref):
    heads, hd = k_ref.shape[1:]
    d = heads * hd
    kv = jnp.dot(x_ref[...].astype(BF16), w_ref[...], preferred_element_type=F32)
    kv_ref[...] = kv
    for j in range(heads):
        k_ref[:, j, :] = kv[:, j * hd:(j + 1) * hd]
        v_ref[:, j, :] = kv[:, d + j * hd:d + (j + 1) * hd]


def _kv_proj(x, wkv, *, heads, tm):
    n, d = x.shape
    assert n % tm == 0
    hd = d // heads
    out = jax.ShapeDtypeStruct((n, heads, hd), F32)
    return pl.pallas_call(
        _kv_proj_kernel,
        grid=(n // tm,),
        in_specs=[pl.BlockSpec((tm, d), lambda i: (i, 0)), _const_spec(wkv.shape)],
        out_specs=[pl.BlockSpec((tm, 2 * d), lambda i: (i, 0))] + [pl.BlockSpec((tm, heads, hd), lambda i: (i, 0, 0))] * 2,
        out_shape=[jax.ShapeDtypeStruct((n, 2 * d), F32), out, out],
        compiler_params=_cparams(("parallel",)),
        name="kv_proj",
    )(x, wkv)


def _row(v):
    return v.reshape(1, -1)


def _block_diag_groups(w):
    n, c, _ = w.shape
    per = LRU_GROUP // c
    rows = jnp.concatenate([w.reshape(n // per, LRU_GROUP, c)] * per, axis=2)
    ri = lax.broadcasted_iota(jnp.int32, (LRU_GROUP, LRU_GROUP), 0) // c
    ci = lax.broadcasted_iota(jnp.int32, (LRU_GROUP, LRU_GROUP), 1) // c
    return jnp.where(ri == ci, rows, 0.0).astype(BF16)


def _prep_layer(l, ln_g, ln_b, ffn1_wi, ffn1_wo, ffn2_wi, ffn2_wo, w_in, shift_mu, decay_w0, decay_w2,
                aaa_a0, aaa_a2, gate_g2, k_k, k_a, r_k, gn_g, gn_b, conv_w, conv_b, lru_wr, lru_br,
                lru_wi, lru_bi, lru_lambda, w_mix_out, xa_wq, xa_wk, xa_wv, xa_wo):
    width = decay_w0.shape[1]
    bf = lambda w: w.astype(BF16)
    mu = shift_mu[l]
    return dict(
        ln_g=[_row(ln_g[l, i]) for i in range(4)], ln_b=[_row(ln_b[l, i]) for i in range(4)],
        ffn1=(bf(ffn1_wi[l]), bf(ffn1_wo[l])), ffn2=(bf(ffn2_wi[l]), bf(ffn2_wo[l])),
        w_in=bf(w_in[l]),
        w_in_lru=bf(w_in[l][:, shift_mu.shape[1]:]),
        shift_mu=_row(mu), mu_r=_row(mu[:width]), mu_k=_row(mu[width:2 * width]),
        mu_v=_row(mu[2 * width:3 * width]), mu_x=_row(mu[3 * width:]),
        decay_w0=_row(decay_w0[l]), decay_w2=bf(decay_w2[l]), aaa_a0=_row(aaa_a0[l]), aaa_a2=bf(aaa_a2[l]),
        gate_g2=bf(gate_g2[l]), k_k=_row(k_k[l]), k_a=_row(k_a[l]), r_k=_row(r_k[l]),
        gn_g=_row(gn_g[l]), gn_b=_row(gn_b[l]),
        conv_w=conv_w[l], conv_b=_row(conv_b[l]),
        lru_wr=_block_diag_groups(lru_wr[l]), lru_br=_row(lru_br[l]),
        lru_wi=_block_diag_groups(lru_wi[l]), lru_bi=_row(lru_bi[l]), lru_lambda=_row(lru_lambda[l]),
        w_mix_out=bf(w_mix_out[l]), xa_wq=bf(xa_wq[l]), xa_wo=bf(xa_wo[l]),
        xa_wkv=bf(jnp.concatenate([xa_wk[l], xa_wv[l]], axis=1)),
    )


def _tile(n, pref):
    return pref if n % pref == 0 else n


def _layer(h, mem, state, shift0, h0, buf0, W, *, alpha, xa_heads):
    B, T, d = h.shape
    n = B * T
    tm = _tile(n, 1024)
    h1 = _ffn_ln(h.reshape(n, d), *W['ffn1'], W['ln_g'][0], W['ln_b'][0], alpha=alpha, tm=tm)
    width = W['decay_w0'].shape[1]
    lru_w = W['conv_b'].shape[1]
    rp = W['shift_mu'].shape[1]
    hist = CONV_WIDTH - 1
    if T > 1:
        assert state is None and T >= V7X_SUBLANES
        pa = _matmul(h1, W['w_in'], tm=tm, tn=rp, n_cols=rp)
        pa3 = pa.reshape(B, T, rp)
        rw, s_new = _rwkv_chunked(pa3, shift0.reshape(B, 1, -1), W, tt=_tile(T, 512), hw=_tile(width, 512))
        buf8 = jnp.concatenate([jnp.zeros((B, V7X_SUBLANES - hist, lru_w), F32), buf0], axis=1)
        h2, h_last, tail8 = _lru_mix_ln(h1.reshape(B, T, d), rw, buf8, h0.reshape(B, 1, lru_w), W,
                                        W['ln_g'][1], W['ln_b'][1], alpha=alpha, tt=_tile(T, 256))
        conv_in_tail = tail8[:, V7X_SUBLANES - hist:]
    else:
        proj = _matmul(h1, W['w_in'], tm=tm, tn=W['w_in'].shape[1])
        pa, pb = proj[:, :rp], proj[:, rp:]
        pa3 = pa.reshape(B, T, rp)
        rw, s_new = _rwkv_step(pa, shift0, jnp.transpose(state, (1, 2, 3, 0)), W, hg=2)
        s_new = jnp.transpose(s_new, (3, 0, 1, 2))
        h2, h_last = _lru_step_mix_ln(pb, rw, h1, jnp.swapaxes(buf0, 0, 1), h0, W,
                                      W['ln_g'][1], W['ln_b'][1], alpha=alpha, tm=_tile(B, 128))
        conv_in_tail = jnp.concatenate([buf0[:, T:], pb[:, None, :lru_w]], axis=1)
    xa = (W['xa_wq'], W['xa_wo'], W['ln_g'][2], W['ln_b'][2])
    if T > 1:
        h3 = _xattn_ln(h2.reshape(n, d), mem, *xa, alpha=alpha, heads=xa_heads, rows_per_batch=T,
                       tile_rows=_tile(T, 1024))
    else:
        h3 = _xattn_rows_ln(h2, *mem, *xa, alpha=alpha, nb=_tile(B, V7X_SUBLANES))
    h4 = _ffn_ln(h3, *W['ffn2'], W['ln_g'][3], W['ln_b'][3], alpha=alpha, tm=tm)
    return h4.reshape(B, T, d), s_new, pa3[:, -1], h_last.reshape(B, lru_w), conv_in_tail


def kernel(x_prompt, x_sample, mem_prompt, cache_mem_k, cache_mem_v, state_rwkv, state_rwkv_shift, state_lru, state_conv, ln_g, ln_b, ffn1_wi, ffn1_wo, ffn2_wi, ffn2_wo, w_in, shift_mu, decay_w0, decay_w2, aaa_a0, aaa_a2, gate_g2, k_k, k_a, r_k, gn_g, gn_b, conv_w, conv_b, lru_wr, lru_br, lru_wi, lru_bi, lru_lambda, w_mix_out, xa_wq, xa_wk, xa_wv, xa_wo):
    depth = ln_g.shape[0]
    alpha = (2.0 * depth) ** 0.25
    B, _, d = x_prompt.shape
    n_mem, xa_heads, xa_head = cache_mem_k.shape[2:]
    rp = shift_mu.shape[1]
    lru_w = conv_b.shape[1]
    hp, hs = x_prompt, x_sample
    outs = [[] for _ in range(10)]
    for l in range(depth):
        W = _prep_layer(l, ln_g, ln_b, ffn1_wi, ffn1_wo, ffn2_wi, ffn2_wo, w_in, shift_mu, decay_w0, decay_w2,
                        aaa_a0, aaa_a2, gate_g2, k_k, k_a, r_k.reshape(depth, -1), gn_g, gn_b, conv_w, conv_b,
                        lru_wr, lru_br, lru_wi, lru_bi, lru_lambda, w_mix_out, xa_wq, xa_wk, xa_wv, xa_wo)
        kv, mk, mv = _kv_proj(mem_prompt.reshape(B * n_mem, d), W['xa_wkv'], heads=xa_heads,
                              tm=_tile(B * n_mem, 512))
        mk = mk.reshape(B, n_mem, xa_heads, xa_head)
        mv = mv.reshape(B, n_mem, xa_heads, xa_head)
        hp, S1, sh1, h1, b1 = _layer(
            hp, kv, None, jnp.zeros((B, rp), F32), jnp.zeros((B, lru_w), F32),
            jnp.zeros((B, CONV_WIDTH - 1, lru_w), F32), W, alpha=alpha, xa_heads=xa_heads)
        hs, S2, sh2, h2, b2 = _layer(
            hs, (cache_mem_k[l], cache_mem_v[l]),
            state_rwkv[l], state_rwkv_shift[l], state_lru[l], state_conv[l], W, alpha=alpha, xa_heads=xa_heads)
        for lst, val in zip(outs, (mk, mv, S1, sh1, h1, b1, S2, sh2, h2, b2)):
            lst.append(val)
    return (hp, hs) + tuple(jnp.stack(o) for o in outs)
```

```python
import functools

import jax
import jax.numpy as jnp
from jax import lax
from jax.experimental import pallas as pl
from jax.experimental.pallas import tpu as pltpu

F32 = jnp.float32
BF16 = jnp.bfloat16

RWKV_HEAD = 64
DECAY_LORA = 64
AAA_LORA = 64
GATE_LORA = 128
GN_EPS = 64e-5
CONV_WIDTH = 4
LRU_C = 8.0
LN_EPS = 1e-5

V7X_SUBLANES = 8
V7X_MXU_DIM = 256
V7X_SCOPED_VMEM_BYTES = 60000 * 1024

RWKV_CHUNK = 64
HEAD_PAIR = 2 * RWKV_HEAD
SCAN_ROW_GROUPS = 2
LRU_GROUP = V7X_MXU_DIM
LRU_PROJ_PIECES = 8


def _cparams(semantics):
    return pltpu.CompilerParams(dimension_semantics=semantics, vmem_limit_bytes=V7X_SCOPED_VMEM_BYTES)


def _const_spec(shape):
    zeros = (0,) * len(shape)
    return pl.BlockSpec(shape, lambda *_: zeros)


def _dot(a, b):
    return jnp.dot(a.astype(BF16), b.astype(BF16), preferred_element_type=F32)


def _dot_dims(a, b, dims):
    return lax.dot_general(a.astype(BF16), b.astype(BF16), (dims, ((), ())), preferred_element_type=F32)


_NN = ((1,), (0,))
_NT = ((1,), (1,))
_TN = ((0,), (0,))


def _split2(x):
    hi = x.astype(BF16)
    lo = (x - hi.astype(F32)).astype(BF16)
    return hi, lo


def _dot_exact_lhs(a_bf16, b):
    hi, lo = _split2(b)
    d = lambda y: jnp.dot(a_bf16, y, preferred_element_type=F32)
    return d(hi) + d(lo)


def _dot_exact_rhs(a, b_bf16):
    hi, lo = _split2(a)
    d = lambda x: jnp.dot(x, b_bf16, preferred_element_type=F32)
    return d(hi) + d(lo)


def _layer_norm(x, g, b):
    mu = jnp.mean(x, axis=-1, keepdims=True)
    xc = x - mu
    var = jnp.mean(xc * xc, axis=-1, keepdims=True)
    return xc * lax.rsqrt(var + LN_EPS) * g + b


def _softplus(z):
    return jnp.maximum(z, 0.0) + jnp.log(1.0 + jnp.exp(-jnp.abs(z)))


def _head_ones(width):
    r = lax.broadcasted_iota(jnp.int32, (width, width), 0) // RWKV_HEAD
    c = lax.broadcasted_iota(jnp.int32, (width, width), 1) // RWKV_HEAD
    return (r == c).astype(BF16)


def _mm_kernel(x_ref, w_ref, o_ref):
    o_ref[...] = jnp.dot(x_ref[...].astype(BF16), w_ref[...], preferred_element_type=F32)


def _matmul(x, w, *, tm, tn):
    n, k = x.shape
    m = w.shape[1]
    assert n % tm == 0 and m % tn == 0
    return pl.pallas_call(
        _mm_kernel,
        grid=(m // tn, n // tm),
        in_specs=[pl.BlockSpec((tm, k), lambda j, i: (i, 0)),
                  pl.BlockSpec((k, tn), lambda j, i: (0, j))],
        out_specs=pl.BlockSpec((tm, tn), lambda j, i: (i, j)),
        out_shape=jax.ShapeDtypeStruct((n, m), F32),
        compiler_params=_cparams(("parallel", "parallel")),
        name="matmul",
    )(x, w)


def _ffn_kernel(x_ref, wg_ref, wu_ref, wo_ref, g_ref, b_ref, o_ref, *, alpha, row_groups):
    tm = x_ref.shape[0]
    tg = tm // row_groups
    for gi in range(row_groups):
        rows = slice(gi * tg, (gi + 1) * tg)
        x = x_ref[rows, :]
        xb = x.astype(BF16)
        gate = jnp.dot(xb, wg_ref[...], preferred_element_type=F32)
        up = jnp.dot(xb, wu_ref[...], preferred_element_type=F32)
        mid = (gate * jax.nn.sigmoid(gate) * up).astype(BF16)
        down = jnp.dot(mid, wo_ref[...], preferred_element_type=F32)
        o_ref[rows, :] = _layer_norm(alpha * x + 0.5 * down, g_ref[...], b_ref[...])


def _ffn_ln(x, wg, wu, wo, g, b, *, alpha, tm):
    n, d = x.shape
    assert n % tm == 0
    row_groups = max(1, tm // V7X_MXU_DIM)
    resident = lambda w: pl.BlockSpec(w.shape, lambda i: (0, 0), pipeline_mode=pl.Buffered(1))
    return pl.pallas_call(
        functools.partial(_ffn_kernel, alpha=alpha, row_groups=row_groups),
        grid=(n // tm,),
        in_specs=[pl.BlockSpec((tm, d), lambda i: (i, 0)),
                  resident(wg), resident(wu), resident(wo),
                  _const_spec(g.shape), _const_spec(b.shape)],
        out_specs=pl.BlockSpec((tm, d), lambda i: (i, 0)),
        out_shape=jax.ShapeDtypeStruct((n, d), F32),
        compiler_params=_cparams(("parallel",)),
        name="ffn_ln",
    )(x, wg, wu, wo, g, b)


def _rwkv_pre(r, k, v, xx, w0, w2, a0, a2, g2, k_k, k_a, r_k, ones):
    xw = xx[:, :DECAY_LORA]
    xa = xx[:, DECAY_LORA:DECAY_LORA + AAA_LORA]
    xg = xx[:, DECAY_LORA + AAA_LORA:]
    z = w0 + _dot(jnp.tanh(xw), w2)
    lw = -jnp.exp(-_softplus(-z) - 0.5)
    a = jax.nn.sigmoid(a0 + _dot(xa, a2))
    g = _dot(jax.nn.sigmoid(xg), g2)
    kkr = k * k_k
    ss = _dot(kkr * kkr, ones)
    kk = kkr * lax.rsqrt(jnp.maximum(ss, 1e-24))
    kf = k * (1.0 + (a - 1.0) * k_a)
    bonus = _dot(r * kf * r_k, ones) * v
    return lw, a, g, kk, kf, bonus


def _rwkv_post(y, bonus, g, gn_g, gn_b, ones):
    inv_n = 1.0 / RWKV_HEAD
    ym = _dot_exact_rhs(y, ones) * inv_n
    yc = y - ym
    yv = _dot(yc * yc, ones) * inv_n
    yn = yc * lax.rsqrt(yv + GN_EPS) * gn_g + gn_b
    return (yn + bonus) * g


def _bdot(a, b, dims):
    dn = ((tuple(d + 1 for d in dims[0]), tuple(d + 1 for d in dims[1])), ((0,), (0,)))
    return lax.dot_general(a.astype(BF16), b.astype(BF16), dn, preferred_element_type=F32)


def _scan_operands(r, kf, v, kk, a, lw):
    tt, hw = r.shape
    C = RWKV_CHUNK
    n_pairs = hw // HEAD_PAIR
    ltri = (lax.broadcasted_iota(jnp.int32, (C, C), 0) >= lax.broadcasted_iota(jnp.int32, (C, C), 1)).astype(BF16)
    first = lax.broadcasted_iota(jnp.int32, (C, HEAD_PAIR), 1) < RWKV_HEAD

    def bd(x):
        return jnp.concatenate([jnp.where(first, x, 0.0), jnp.where(first, 0.0, x)], axis=1)

    names = ('a', 'r', 'b', 'k', 'v', 'bh', 'kh')
    ops = {n: [] for n in names}
    wcs = []
    for c in range(tt // C):
        rows = slice(c * C, (c + 1) * C)
        lw_c = lw[rows]
        L = _dot_exact_lhs(ltri, lw_c)
        Lc = L[C - 1:C, :]
        e_nl = jnp.exp(-L)
        e_c = jnp.exp(Lc - L)
        bb = kk[rows] * a[rows]
        bf = lambda x: x.astype(BF16)
        tile = dict(a=bf(-kk[rows] * jnp.exp(L - lw_c)), r=r[rows] * jnp.exp(L), b=bf(bb * e_nl),
                    k=bf(kf[rows] * e_nl), v=bf(v[rows]), bh=bf(bb * e_c), kh=bf(kf[rows] * e_c))
        wc = jnp.exp(Lc)
        for p in range(n_pairs):
            lanes = slice(p * HEAD_PAIR, (p + 1) * HEAD_PAIR)
            for n in names:
                ops[n].append(tile[n][:, lanes])
            wcs.append(wc[:, lanes])
    A, R, B, K, V, Bh, Kh = (jnp.stack(ops[n]) for n in names)
    Vbd = bd(V)
    G = _bdot(jnp.concatenate([A, R.astype(BF16)], axis=1), jnp.concatenate([bd(B), bd(K)], axis=1), _NT)
    tok = lax.broadcasted_iota(jnp.int32, (C, HEAD_PAIR), 0)
    src = lax.broadcasted_iota(jnp.int32, (C, HEAD_PAIR), 1) % RWKV_HEAD
    a_ab = jnp.where(tok > src, G[:, :C, :HEAD_PAIR], 0.0)
    a_ak = jnp.where(tok > src, G[:, :C, HEAD_PAIR:], 0.0).astype(BF16)
    a_rb = jnp.where(tok >= src, G[:, C:, :HEAD_PAIR], 0.0).astype(BF16)
    a_rk = jnp.where(tok >= src, G[:, C:, HEAD_PAIR:], 0.0).astype(BF16)
    P = jnp.where(tok == src, 1.0, 0.0) + a_ab
    N = a_ab.astype(BF16)
    N = _bdot(N, bd(N), _NN).astype(BF16)
    steps = 2
    while 2 * steps < C:
        NP = _bdot(jnp.concatenate([N, P.astype(BF16)], axis=1), bd(N), _NN)
        N = NP[:, :C].astype(BF16)
        P = P + NP[:, C:]
        steps *= 2
    P = (P + _bdot(P, bd(N), _NN)).astype(BF16)
    aV = _bdot(a_ak, Vbd, _NN).astype(BF16)
    XU = _bdot(P, jnp.concatenate([bd(A), bd(aV)], axis=2), _NN).astype(BF16)
    X1 = XU[:, :, :HEAD_PAIR]
    Uloc = XU[:, :, HEAD_PAIR:]
    Q = (R + _bdot(a_rb, bd(X1), _NN)).astype(BF16)
    Yloc = _bdot(jnp.concatenate([a_rb, a_rk], axis=2), jnp.concatenate([bd(Uloc), Vbd], axis=1), _NN)
    ri = lax.broadcasted_iota(jnp.int32, (HEAD_PAIR, HEAD_PAIR), 0) // RWKV_HEAD
    ci = lax.broadcasted_iota(jnp.int32, (HEAD_PAIR, HEAD_PAIR), 1) // RWKV_HEAD
    same_head = ri == ci
    Pm = jnp.where(same_head, _bdot(X1, Bh, _TN), 0.0).astype(BF16)
    Sloc = jnp.where(same_head, _bdot(jnp.concatenate([Uloc, V], axis=1), jnp.concatenate([Bh, Kh], axis=1), _TN),
                     0.0)
    return Q, Yloc, Pm, Sloc, jnp.stack(wcs)


def _rwkv_chunk_kernel(pr_ref, pk_ref, pv_ref, px_ref, sr_ref, sk_ref, sv_ref, sx_ref,
                       mur_ref, muk_ref, muv_ref, mux_ref, w0_ref, w2_ref, a0_ref, a2_ref, g2_ref,
                       kk_ref, ka_ref, rk_ref, gng_ref, gnb_ref, ones_ref,
                       out_ref, s_out_ref,
                       s_scr, cr_scr, ck_scr, cv_scr, cx_scr, y_scr):
    ti = pl.program_id(2)
    tt, hw = pr_ref.shape
    n_pairs = hw // HEAD_PAIR

    @pl.when(ti == 0)
    def _init():
        s_scr[...] = jnp.zeros(s_scr.shape, F32)
        cr_scr[...] = sr_ref[...]
        ck_scr[...] = sk_ref[...]
        cv_scr[...] = sv_ref[...]
        cx_scr[...] = sx_ref[...]

    S8 = V7X_SUBLANES

    def shifted(p_ref, c_scr, mu_ref):
        p = p_ref[...]
        first = lax.broadcasted_iota(jnp.int32, (S8, p.shape[1]), 0) == 0
        head = jnp.where(first, c_scr[...], pltpu.roll(p[:S8], 1, 0))
        prev = jnp.concatenate([head, p_ref[S8 - 1:tt - 1, :]], axis=0)
        c_scr[...] = p_ref[tt - 1:tt, :]
        return p + (prev - p) * mu_ref[...]

    r = shifted(pr_ref, cr_scr, mur_ref)
    k = shifted(pk_ref, ck_scr, muk_ref)
    v = shifted(pv_ref, cv_scr, muv_ref)
    xx = shifted(px_ref, cx_scr, mux_ref)
    ones = ones_ref[...]
    lw, a, g, kk, kf, bonus = _rwkv_pre(r, k, v, xx, w0_ref[...], w2_ref[...], a0_ref[...], a2_ref[...],
                                        g2_ref[...], kk_ref[...], ka_ref[...], rk_ref[...], ones)
    C = RWKV_CHUNK
    S = s_scr[...]
    n_groups = SCAN_ROW_GROUPS if tt % (SCAN_ROW_GROUPS * C) == 0 else 1
    tg = tt // n_groups
    for gi in range(n_groups):
        rows = slice(gi * tg, (gi + 1) * tg)
        Q, Yloc, Pm, Sloc, wc = _scan_operands(r[rows], kf[rows], v[rows], kk[rows], a[rows], lw[rows])
        for c in range(tg // C):
            inst = slice(c * n_pairs, (c + 1) * n_pairs)
            y_c = _bdot(Q[inst], S, _NT) + Yloc[inst]
            row0 = gi * tg + c * C
            for p in range(n_pairs):
                y_scr[row0:row0 + C, p * HEAD_PAIR:(p + 1) * HEAD_PAIR] = y_c[p]
            S = S * wc[inst] + _bdot(S, Pm[inst], _NN) + Sloc[inst]
    s_scr[...] = S
    out_ref[...] = _rwkv_post(y_scr[...], bonus, g, gng_ref[...], gnb_ref[...], ones)

    @pl.when(ti == pl.num_programs(2) - 1)
    def _emit_state():
        for p in range(n_pairs):
            s_out_ref[2 * p] = S[p, :RWKV_HEAD, :RWKV_HEAD]
            s_out_ref[2 * p + 1] = S[p, RWKV_HEAD:, RWKV_HEAD:]


def _rwkv_chunked(pa, shift0, W, *, tt, hw):
    B, T, _ = pa.shape
    width = W['decay_w0'].shape[1]
    heads = width // RWKV_HEAD
    assert T % tt == 0 and tt % RWKV_CHUNK == 0 and width % hw == 0 and hw % HEAD_PAIR == 0
    nb = width // hw
    xw = DECAY_LORA + AAA_LORA + GATE_LORA
    assert (3 * width) % xw == 0
    xblk = 3 * width // xw
    col = lambda off: (lambda b, h, t: (b, t, off + h))
    vec = lambda: pl.BlockSpec((1, hw), lambda b, h, t: (0, h))
    in_specs = [
        pl.BlockSpec((None, tt, hw), col(0)), pl.BlockSpec((None, tt, hw), col(nb)),
        pl.BlockSpec((None, tt, hw), col(2 * nb)), pl.BlockSpec((None, tt, xw), lambda b, h, t: (b, t, xblk)),
        pl.BlockSpec((None, 1, hw), lambda b, h, t: (b, 0, h)), pl.BlockSpec((None, 1, hw), lambda b, h, t: (b, 0, nb + h)),
        pl.BlockSpec((None, 1, hw), lambda b, h, t: (b, 0, 2 * nb + h)), pl.BlockSpec((None, 1, xw), lambda b, h, t: (b, 0, xblk)),
        vec(), vec(), vec(), _const_spec((1, xw)),
        vec(), pl.BlockSpec((DECAY_LORA, hw), lambda b, h, t: (0, h)),
        vec(), pl.BlockSpec((AAA_LORA, hw), lambda b, h, t: (0, h)),
        pl.BlockSpec((GATE_LORA, hw), lambda b, h, t: (0, h)),
        vec(), vec(), vec(), vec(), vec(), _const_spec((hw, hw)),
    ]
    out, s_new = pl.pallas_call(
        _rwkv_chunk_kernel,
        grid=(B, nb, T // tt),
        in_specs=in_specs,
        out_specs=[pl.BlockSpec((None, tt, hw), lambda b, h, t: (b, t, h)),
                   pl.BlockSpec((None, hw // RWKV_HEAD, RWKV_HEAD, RWKV_HEAD), lambda b, h, t: (b, h, 0, 0))],
        out_shape=[jax.ShapeDtypeStruct((B, T, width), F32),
                   jax.ShapeDtypeStruct((B, heads, RWKV_HEAD, RWKV_HEAD), F32)],
        scratch_shapes=[pltpu.VMEM((hw // HEAD_PAIR, HEAD_PAIR, HEAD_PAIR), F32),
                        pltpu.VMEM((1, hw), F32), pltpu.VMEM((1, hw), F32), pltpu.VMEM((1, hw), F32),
                        pltpu.VMEM((1, xw), F32), pltpu.VMEM((tt, hw), F32)],
        compiler_params=_cparams(("parallel", "parallel", "arbitrary")),
        name="rwkv_chunked",
    )(pa, pa, pa, pa, shift0, shift0, shift0, shift0,
      W['mu_r'], W['mu_k'], W['mu_v'], W['mu_x'], W['decay_w0'], W['decay_w2'], W['aaa_a0'], W['aaa_a2'],
      W['gate_g2'], W['k_k'], W['k_a'], W['r_k'], W['gn_g'], W['gn_b'], _head_ones(hw))
    return out, s_new


def _rwkv_step_kernel(p_ref, s0_ref, st_ref, mu_ref, w0_ref, w2_ref, a0_ref, a2_ref, g2_ref,
                      kk_ref, ka_ref, rk_ref, gng_ref, gnb_ref, ones_ref,
                      out_ref, st_out_ref, vec_scr, y_scr, bonus_scr, g_scr):
    i = pl.program_id(0)
    B, width = out_ref.shape
    hg = st_ref.shape[0]
    H = RWKV_HEAD

    @pl.when(i == 0)
    def _prologue():
        ones = ones_ref[...]
        p = p_ref[...]
        xs = p + (s0_ref[...] - p) * mu_ref[...]
        r, k, v, xx = xs[:, :width], xs[:, width:2 * width], xs[:, 2 * width:3 * width], xs[:, 3 * width:]
        lw, a, g, kk, kf, bonus = _rwkv_pre(r, k, v, xx, w0_ref[...], w2_ref[...], a0_ref[...], a2_ref[...],
                                            g2_ref[...], kk_ref[...], ka_ref[...], rk_ref[...], ones)
        for j, vec in enumerate((r, kf, v, kk, kk * a, jnp.exp(lw))):
            vec_scr[j] = vec.T
        bonus_scr[...] = bonus
        g_scr[...] = g

    for hl in range(hg):
        base = pl.multiple_of((i * hg + hl) * H, H)
        r_h, kf_h, kk_h, kka_h, w_h = (vec_scr[j, pl.ds(base, H), :] for j in (0, 1, 3, 4, 5))

        def body(vi, carry):
            S = st_ref[hl, vi]
            sa = jnp.sum(S * kk_h, axis=0, keepdims=True)
            v_row = vec_scr[2, pl.ds(base + vi, 1), :]
            S2 = S * w_h - sa * kka_h + v_row * kf_h
            st_out_ref[hl, vi] = S2
            y_scr[pl.ds(base + vi, 1), :] = jnp.sum(S2 * r_h, axis=0, keepdims=True)
            return carry

        lax.fori_loop(0, H, body, 0, unroll=8)

    @pl.when(i == pl.num_programs(0) - 1)
    def _epilogue():
        out_ref[...] = _rwkv_post(y_scr[...].T, bonus_scr[...], g_scr[...], gng_ref[...], gnb_ref[...],
                                  ones_ref[...])


def _rwkv_step(pa, shift0, state_t, W, *, hg):
    B, proj = pa.shape
    width = W['decay_w0'].shape[1]
    heads = width // RWKV_HEAD
    assert heads % hg == 0 and state_t.shape == (heads, RWKV_HEAD, RWKV_HEAD, B)
    st_spec = pl.BlockSpec((hg, RWKV_HEAD, RWKV_HEAD, B), lambda i: (i, 0, 0, 0))
    names = ('decay_w0', 'decay_w2', 'aaa_a0', 'aaa_a2', 'gate_g2', 'k_k', 'k_a', 'r_k', 'gn_g', 'gn_b')
    out, st_new = pl.pallas_call(
        _rwkv_step_kernel,
        grid=(heads // hg,),
        in_specs=[_const_spec((B, proj)), _const_spec((B, proj)), st_spec, _const_spec(W['shift_mu'].shape)]
                 + [_const_spec(W[n].shape) for n in names] + [_const_spec((width, width))],
        out_specs=[_const_spec((B, width)), st_spec],
        out_shape=[jax.ShapeDtypeStruct((B, width), F32), jax.ShapeDtypeStruct(state_t.shape, F32)],
        scratch_shapes=[pltpu.VMEM((6, width, B), F32), pltpu.VMEM((width, B), F32),
                        pltpu.VMEM((B, width), F32), pltpu.VMEM((B, width), F32)],
        compiler_params=_cparams(("arbitrary",)),
        name="rwkv_step",
    )(pa, shift0, state_t, W['shift_mu'], *[W[n] for n in names], _head_ones(width))
    return out, st_new


def _lru_coeffs(xc, wr_ref, br_ref, wi_ref, bi_ref, lam_ref):
    xcb = xc.astype(BF16)
    n_grp = xc.shape[1] // LRU_GROUP
    grp = lambda w_ref: jnp.concatenate(
        [jnp.dot(xcb[:, i * LRU_GROUP:(i + 1) * LRU_GROUP], w_ref[i], preferred_element_type=F32)
         for i in range(n_grp)], axis=1)
    gr = jax.nn.sigmoid(grp(wr_ref) + br_ref[...])
    gi = jax.nn.sigmoid(grp(wi_ref) + bi_ref[...])
    a_t = jnp.exp(-LRU_C * gr * _softplus(-lam_ref[...]))
    return a_t, jnp.sqrt(1.0 - a_t * a_t) * gi * xc


def _lru_finish(hs, pg, g0, g1, rw, h, wmix_ref, lng_ref, lnb_ref, alpha):
    lru_out = hs * jax.nn.gelu(pg)
    merged = jax.nn.sigmoid(g0) * rw + jax.nn.sigmoid(g1) * lru_out
    mix = jnp.dot(merged.astype(BF16), wmix_ref[...], preferred_element_type=F32)
    return _layer_norm(alpha * h + mix, lng_ref[...], lnb_ref[...])


def _lru_kernel(hn_ref, rw_ref, h_ref, buf_ref, h0_ref, win_ref,
                cw_ref, cb_ref, wr_ref, br_ref, wi_ref, bi_ref, lam_ref, wmix_ref, lng_ref, lnb_ref,
                out_ref, hlast_ref, tail_ref, proj_scr, tail_scr, hc_scr, *, alpha):
    ti = pl.program_id(1)
    flat = pl.program_id(0) * pl.num_programs(1) + ti
    tt, width = rw_ref.shape
    S8 = V7X_SUBLANES
    project = lambda ref: jnp.dot(ref[...].astype(BF16), win_ref[...], preferred_element_type=F32)
    col = lambda j: slice(j * width, (j + 1) * width)

    @pl.when(ti == 0)
    def _init():
        tail_scr[...] = buf_ref[...]
        hc_scr[...] = h0_ref[...]

    @pl.when(flat == 0)
    def _first_projection():
        proj_scr[0] = project(h_ref)

    def step(cur_scr, nxt_scr):
        hn = hn_ref[...].astype(BF16)
        piece_w = win_ref.shape[1] // LRU_PROJ_PIECES
        pieces = iter(range(LRU_PROJ_PIECES))

        def emit(count=1):
            for _ in range(count):
                k = next(pieces, None)
                if k is not None:
                    cols = slice(k * piece_w, (k + 1) * piece_w)
                    nxt_scr[:, cols] = jnp.dot(hn, win_ref[:, cols], preferred_element_type=F32)

        x = cur_scr[:, col(0)]
        tail = tail_scr[...]
        r8 = lax.broadcasted_iota(jnp.int32, (S8, width), 0)

        def delayed(d):
            head = jnp.where(r8 < d, pltpu.roll(tail, d, 0), pltpu.roll(x[:S8], d, 0))
            return jnp.concatenate([head, cur_scr[S8 - d:tt - d, col(0)]], axis=0)

        cw = cw_ref[...]
        conv = cw[0:1] * delayed(CONV_WIDTH - 1)
        for j in range(1, CONV_WIDTH - 1):
            conv = conv + cw[j:j + 1] * delayed(CONV_WIDTH - 1 - j)
        xc = cb_ref[...] + (conv + cw[CONV_WIDTH - 1:CONV_WIDTH] * x)
        tail_scr[...] = x[tt - S8:, :]
        tail_ref[...] = x[tt - S8:, :]
        emit()

        A, Bv = _lru_coeffs(xc, wr_ref, br_ref, wi_ref, bi_ref, lam_ref)
        emit()
        A = A.reshape(tt // S8, S8, width)
        Bv = Bv.reshape(tt // S8, S8, width)
        in_grp = lax.broadcasted_iota(jnp.int32, (S8, width), 0)
        s = 1
        while s < S8:
            keep = in_grp >= s
            Bv = Bv + A * jnp.where(keep, pltpu.roll(Bv, s, 1), 0.0)
            A = A * jnp.where(keep, pltpu.roll(A, s, 1), 1.0)
            s *= 2
            emit()
        carry = hc_scr[...]
        groups = []
        for gi in range(tt // S8):
            hg = Bv[gi] + A[gi] * carry
            groups.append(hg)
            carry = hg[S8 - 1:S8, :]
        hs = jnp.concatenate(groups, axis=0)
        hc_scr[...] = carry
        hlast_ref[...] = carry
        emit()
        lru_out = hs * jax.nn.gelu(cur_scr[:, col(1)])
        emit()
        merged = jax.nn.sigmoid(cur_scr[:, col(2)]) * rw_ref[...] + jax.nn.sigmoid(cur_scr[:, col(3)]) * lru_out
        emit()
        mix = jnp.dot(merged.astype(BF16), wmix_ref[...], preferred_element_type=F32)
        out_ref[...] = _layer_norm(alpha * h_ref[...] + mix, lng_ref[...], lnb_ref[...])
        emit(LRU_PROJ_PIECES)

    slot = flat % 2
    step(proj_scr.at[slot], proj_scr.at[1 - slot])


def _lru_step_kernel(x_ref, pg_ref, g0_ref, g1_ref, rw_ref, h_ref, buf_ref, h0_ref,
                     cw_ref, cb_ref, wr_ref, br_ref, wi_ref, bi_ref, lam_ref, wmix_ref, lng_ref, lnb_ref,
                     out_ref, hnew_ref, *, alpha):
    cw = cw_ref[...]
    conv = cw[0:1] * buf_ref[0]
    for j in range(1, CONV_WIDTH - 1):
        conv = conv + cw[j:j + 1] * buf_ref[j]
    xc = cb_ref[...] + (conv + cw[CONV_WIDTH - 1:CONV_WIDTH] * x_ref[...])
    A, Bv = _lru_coeffs(xc, wr_ref, br_ref, wi_ref, bi_ref, lam_ref)
    hs = Bv + A * h0_ref[...]
    hnew_ref[...] = hs
    out_ref[...] = _lru_finish(hs, pg_ref[...], g0_ref[...], g1_ref[...], rw_ref[...], h_ref[...],
                               wmix_ref, lng_ref, lnb_ref, alpha)


_LRU_WEIGHTS = ('conv_w', 'conv_b', 'lru_wr', 'lru_br', 'lru_wi', 'lru_bi', 'lru_lambda', 'w_mix_out')


def _lru_step_mix_ln(pb, rw, h, buf, h0, W, ln_g, ln_b, *, alpha, tm):
    B, width = rw.shape
    assert B % tm == 0
    blk = lambda j: pl.BlockSpec((tm, width), lambda i: (i, j))
    out, h_new = pl.pallas_call(
        functools.partial(_lru_step_kernel, alpha=alpha),
        grid=(B // tm,),
        in_specs=[blk(0), blk(1), blk(2), blk(3), blk(0), blk(0),
                  pl.BlockSpec((CONV_WIDTH - 1, tm, width), lambda i: (0, i, 0)), blk(0)]
                 + [_const_spec(W[n].shape) for n in _LRU_WEIGHTS] + [_const_spec(ln_g.shape), _const_spec(ln_b.shape)],
        out_specs=[blk(0), blk(0)],
        out_shape=[jax.ShapeDtypeStruct((B, width), F32), jax.ShapeDtypeStruct((B, width), F32)],
        compiler_params=_cparams(("parallel",)),
        name="lru_step_mix_ln",
    )(pb, pb, pb, pb, rw, h, buf, h0, *[W[n] for n in _LRU_WEIGHTS], ln_g, ln_b)
    return out, h_new


def _lru_mix_ln(h, rw, buf8, h0, W, ln_g, ln_b, *, alpha, tt):
    B, T, width = rw.shape
    assert T % tt == 0 and tt % V7X_SUBLANES == 0 and h.shape[2] == width
    nt = T // tt
    tile = pl.BlockSpec((None, tt, width), lambda b, t: (b, t, 0))

    def next_tile(b, t):
        f = jnp.minimum(b * nt + t + 1, B * nt - 1)
        return f // nt, f % nt, 0

    nxt = pl.BlockSpec((None, tt, width), next_tile)
    per_b = lambda rows: pl.BlockSpec((None, rows, width), lambda b, t: (b, 0, 0))
    win = W['w_in_b']
    out, h_last, tail = pl.pallas_call(
        functools.partial(_lru_kernel, alpha=alpha),
        grid=(B, nt),
        in_specs=[nxt, tile, tile, per_b(V7X_SUBLANES), per_b(1), _const_spec(win.shape)]
                 + [_const_spec(W[n].shape) for n in _LRU_WEIGHTS] + [_const_spec(ln_g.shape), _const_spec(ln_b.shape)],
        out_specs=[tile, per_b(1), per_b(V7X_SUBLANES)],
        out_shape=[jax.ShapeDtypeStruct((B, T, width), F32), jax.ShapeDtypeStruct((B, 1, width), F32),
                   jax.ShapeDtypeStruct((B, V7X_SUBLANES, width), F32)],
        scratch_shapes=[pltpu.VMEM((2, tt, win.shape[1]), F32), pltpu.VMEM((V7X_SUBLANES, width), F32),
                        pltpu.VMEM((1, width), F32)],
        compiler_params=_cparams(("arbitrary", "arbitrary")),
        name="lru_mix_ln",
    )(h, rw, h, buf8, h0, win, *[W[n] for n in _LRU_WEIGHTS], ln_g, ln_b)
    return out, h_last, tail


def _xattn_kernel(h_ref, kv_ref, wq_ref, wo_ref, g_ref, b_ref, o_ref, *, alpha, heads):
    tm, d = h_ref.shape
    hd = d // heads
    row_groups = max(1, tm // V7X_MXU_DIM)
    tg = tm // row_groups
    for gi in range(row_groups):
        rows = slice(gi * tg, (gi + 1) * tg)
        h = h_ref[rows, :]
        q = jnp.dot(h.astype(BF16), wq_ref[...], preferred_element_type=F32)
        outs = []
        for j in range(heads):
            sl = slice(j * hd, (j + 1) * hd)
            s = _dot_dims(q[:, sl], kv_ref[:, sl], _NT) * (hd ** -0.5)
            e = jnp.exp(s - jnp.max(s, axis=-1, keepdims=True))
            p = e * (1.0 / jnp.sum(e, axis=-1, keepdims=True))
            outs.append(_dot(p, kv_ref[:, d + j * hd:d + (j + 1) * hd]))
        out = jnp.dot(jnp.concatenate(outs, axis=1).astype(BF16), wo_ref[...], preferred_element_type=F32)
        o_ref[rows, :] = _layer_norm(alpha * h + out, g_ref[...], b_ref[...])


def _xattn_rows_kernel(h_ref, mk_ref, mv_ref, wq_ref, wo_ref, g_ref, b_ref, o_ref, *, alpha):
    h = h_ref[...]
    nb, m, heads, hd = mk_ref.shape
    q = jnp.dot(h.astype(BF16), wq_ref[...], preferred_element_type=F32)
    col = lax.broadcasted_iota(jnp.int32, (heads, m * heads), 1)
    own = (col % heads) == lax.broadcasted_iota(jnp.int32, (heads, m * heads), 0)
    rows = []
    for i in range(nb):
        k2 = mk_ref[i].reshape(m * heads, hd)
        v2 = mv_ref[i].reshape(m * heads, hd)
        q4 = jnp.concatenate([q[i:i + 1, j * hd:(j + 1) * hd] for j in range(heads)], axis=0)
        s = jnp.where(own, _dot_dims(q4, k2, _NT) * (hd ** -0.5), -jnp.inf)
        e = jnp.exp(s - jnp.max(s, axis=-1, keepdims=True))
        p = e / jnp.sum(e, axis=-1, keepdims=True)
        o4 = _dot(p, v2)
        rows.append(jnp.concatenate([o4[j:j + 1] for j in range(heads)], axis=1))
    out = jnp.dot(jnp.concatenate(rows, axis=0).astype(BF16), wo_ref[...], preferred_element_type=F32)
    o_ref[...] = _layer_norm(alpha * h + out, g_ref[...], b_ref[...])


def _xattn_ln(h, kv, wq, wo, g, b, *, alpha, heads, rows_per_batch, tile_rows):
    n, d = h.shape
    assert n % tile_rows == 0 and rows_per_batch % tile_rows == 0
    per = rows_per_batch // tile_rows
    m = kv.shape[0] // (n // rows_per_batch)
    return pl.pallas_call(
        functools.partial(_xattn_kernel, alpha=alpha, heads=heads),
        grid=(n // tile_rows,),
        in_specs=[pl.BlockSpec((tile_rows, d), lambda i: (i, 0)),
                  pl.BlockSpec((m, 2 * d), lambda i: (i // per, 0)),
                  _const_spec(wq.shape), _const_spec(wo.shape), _const_spec(g.shape), _const_spec(b.shape)],
        out_specs=pl.BlockSpec((tile_rows, d), lambda i: (i, 0)),
        out_shape=jax.ShapeDtypeStruct((n, d), F32),
        compiler_params=_cparams(("parallel",)),
        name="xattn_ln",
    )(h, kv, wq, wo, g, b)


def _xattn_rows_ln(h, mk, mv, wq, wo, g, b, *, alpha, nb):
    n, d = h.shape
    _, m, heads, hd = mk.shape
    assert n % nb == 0
    kv_spec = pl.BlockSpec((nb, m, heads, hd), lambda i: (i, 0, 0, 0))
    return pl.pallas_call(
        functools.partial(_xattn_rows_kernel, alpha=alpha),
        grid=(n // nb,),
        in_specs=[pl.BlockSpec((nb, d), lambda i: (i, 0)), kv_spec, kv_spec,
                  _const_spec(wq.shape), _const_spec(wo.shape), _const_spec(g.shape), _const_spec(b.shape)],
        out_specs=pl.BlockSpec((nb, d), lambda i: (i, 0)),
        out_shape=jax.ShapeDtypeStruct((n, d), F32),
        compiler_params=_cparams(("parallel",)),
        name="xattn_rows_ln",
    )(h, mk, mv, wq, wo, g, b)


def _kv_proj_kernel(x_ref, w_ref, kv_ref, k_ref, v_ref):
    heads, hd = k_ref.shape[1:]
    d = heads * hd
    kv = jnp.dot(x_ref[...].astype(BF16), w_ref[...], preferred_element_type=F32)
    kv_ref[...] = kv
    for j in range(heads):
        k_ref[:, j, :] = kv[:, j * hd:(j + 1) * hd]
        v_ref[:, j, :] = kv[:, d + j * hd:d + (j + 1) * hd]


def _kv_proj(x, wkv, *, heads, tm):
    n, d = x.shape
    assert n % tm == 0
    hd = d // heads
    out = jax.ShapeDtypeStruct((n, heads, hd), F32)
    return pl.pallas_call(
        _kv_proj_kernel,
        grid=(n // tm,),
        in_specs=[pl.BlockSpec((tm, d), lambda i: (i, 0)), _const_spec(wkv.shape)],
        out_specs=[pl.BlockSpec((tm, 2 * d), lambda i: (i, 0))] + [pl.BlockSpec((tm, heads, hd), lambda i: (i, 0, 0))] * 2,
        out_shape=[jax.ShapeDtypeStruct((n, 2 * d), F32), out, out],
        compiler_params=_cparams(("parallel",)),
        name="kv_proj",
    )(x, wkv)


def _row(v):
    return v.reshape(1, -1)


def _block_diag_groups(w):
    n, c, _ = w.shape
    per = LRU_GROUP // c
    rows = jnp.concatenate([w.reshape(n // per, LRU_GROUP, c)] * per, axis=2)
    ri = lax.broadcasted_iota(jnp.int32, (LRU_GROUP, LRU_GROUP), 0) // c
    ci = lax.broadcasted_iota(jnp.int32, (LRU_GROUP, LRU_GROUP), 1) // c
    return jnp.where(ri == ci, rows, 0.0).astype(BF16)


def _prep_layer(l, ln_g, ln_b, ffn1_wi, ffn1_wo, ffn2_wi, ffn2_wo, w_in, shift_mu, decay_w0, decay_w2,
                aaa_a0, aaa_a2, gate_g2, k_k, k_a, r_k, gn_g, gn_b, conv_w, conv_b, lru_wr, lru_br,
                lru_wi, lru_bi, lru_lambda, w_mix_out, xa_wq, xa_wk, xa_wv, xa_wo):
    width = decay_w0.shape[1]
    rp = shift_mu.shape[1]
    d_ff = ffn1_wo.shape[1]
    bf = lambda w: w.astype(BF16)
    mu = shift_mu[l]
    return dict(
        ln_g=[_row(ln_g[l, i]) for i in range(4)], ln_b=[_row(ln_b[l, i]) for i in range(4)],
        ffn1=(bf(ffn1_wi[l][:, :d_ff]), bf(ffn1_wi[l][:, d_ff:]), bf(ffn1_wo[l])),
        ffn2=(bf(ffn2_wi[l][:, :d_ff]), bf(ffn2_wi[l][:, d_ff:]), bf(ffn2_wo[l])),
        w_in_a=bf(w_in[l][:, :rp]), w_in_b=bf(w_in[l][:, rp:]),
        shift_mu=_row(mu), mu_r=_row(mu[:width]), mu_k=_row(mu[width:2 * width]),
        mu_v=_row(mu[2 * width:3 * width]), mu_x=_row(mu[3 * width:]),
        decay_w0=_row(decay_w0[l]), decay_w2=bf(decay_w2[l]), aaa_a0=_row(aaa_a0[l]), aaa_a2=bf(aaa_a2[l]),
        gate_g2=bf(gate_g2[l]), k_k=_row(k_k[l]), k_a=_row(k_a[l]), r_k=_row(r_k[l]),
        gn_g=_row(gn_g[l]), gn_b=_row(gn_b[l]),
        conv_w=conv_w[l], conv_b=_row(conv_b[l]),
        lru_wr=_block_diag_groups(lru_wr[l]), lru_br=_row(lru_br[l]),
        lru_wi=_block_diag_groups(lru_wi[l]), lru_bi=_row(lru_bi[l]), lru_lambda=_row(lru_lambda[l]),
        w_mix_out=bf(w_mix_out[l]), xa_wq=bf(xa_wq[l]), xa_wo=bf(xa_wo[l]),
        xa_wkv=bf(jnp.concatenate([xa_wk[l], xa_wv[l]], axis=1)),
    )


def _tile(n, pref):
    return pref if n % pref == 0 else n


def _layer(h, mem, state, shift0, h0, buf0, W, *, alpha, xa_heads):
    B, T, d = h.shape
    n = B * T
    tm = _tile(n, 1024)
    h1 = _ffn_ln(h.reshape(n, d), *W['ffn1'], W['ln_g'][0], W['ln_b'][0], alpha=alpha, tm=tm)
    pa = _matmul(h1, W['w_in_a'], tm=tm, tn=W['w_in_a'].shape[1])
    width = W['decay_w0'].shape[1]
    lru_w = W['conv_b'].shape[1]
    pa3 = pa.reshape(B, T, -1)
    hist = CONV_WIDTH - 1
    if T > 1:
        assert state is None and T >= V7X_SUBLANES
        rw, s_new = _rwkv_chunked(pa3, shift0.reshape(B, 1, -1), W, tt=_tile(T, 1024), hw=_tile(width, 512))
        buf8 = jnp.concatenate([jnp.zeros((B, V7X_SUBLANES - hist, lru_w), F32), buf0], axis=1)
        h2, h_last, tail8 = _lru_mix_ln(h1.reshape(B, T, d), rw, buf8, h0.reshape(B, 1, lru_w), W,
                                        W['ln_g'][1], W['ln_b'][1], alpha=alpha, tt=_tile(T, 256))
        conv_in_tail = tail8[:, V7X_SUBLANES - hist:]
    else:
        pb = _matmul(h1, W['w_in_b'], tm=tm, tn=W['w_in_b'].shape[1])
        rw, s_new = _rwkv_step(pa, shift0, jnp.transpose(state, (1, 2, 3, 0)), W, hg=2)
        s_new = jnp.transpose(s_new, (3, 0, 1, 2))
        h2, h_last = _lru_step_mix_ln(pb, rw, h1, jnp.swapaxes(buf0, 0, 1), h0, W,
                                      W['ln_g'][1], W['ln_b'][1], alpha=alpha, tm=_tile(B, 128))
        conv_in_tail = jnp.concatenate([buf0[:, T:], pb[:, None, :lru_w]], axis=1)
    xa = (W['xa_wq'], W['xa_wo'], W['ln_g'][2], W['ln_b'][2])
    if T > 1:
        h3 = _xattn_ln(h2.reshape(n, d), mem, *xa, alpha=alpha, heads=xa_heads, rows_per_batch=T,
                       tile_rows=_tile(T, 1024))
    else:
        h3 = _xattn_rows_ln(h2, *mem, *xa, alpha=alpha, nb=_tile(B, V7X_SUBLANES))
    h4 = _ffn_ln(h3, *W['ffn2'], W['ln_g'][3], W['ln_b'][3], alpha=alpha, tm=tm)
    return h4.reshape(B, T, d), s_new, pa3[:, -1], h_last.reshape(B, lru_w), conv_in_tail


def kernel(x_prompt, x_sample, mem_prompt, cache_mem_k, cache_mem_v, state_rwkv, state_rwkv_shift, state_lru, state_conv, ln_g, ln_b, ffn1_wi, ffn1_wo, ffn2_wi, ffn2_wo, w_in, shift_mu, decay_w0, decay_w2, aaa_a0, aaa_a2, gate_g2, k_k, k_a, r_k, gn_g, gn_b, conv_w, conv_b, lru_wr, lru_br, lru_wi, lru_bi, lru_lambda, w_mix_out, xa_wq, xa_wk, xa_wv, xa_wo):
    depth = ln_g.shape[0]
    alpha = (2.0 * depth) ** 0.25
    B, _, d = x_prompt.shape
    n_mem, xa_heads, xa_head = cache_mem_k.shape[2:]
    rp = shift_mu.shape[1]
    lru_w = conv_b.shape[1]
    hp, hs = x_prompt, x_sample
    outs = [[] for _ in range(10)]
    for l in range(depth):
        W = _prep_layer(l, ln_g, ln_b, ffn1_wi, ffn1_wo, ffn2_wi, ffn2_wo, w_in, shift_mu, decay_w0, decay_w2,
                        aaa_a0, aaa_a2, gate_g2, k_k, k_a, r_k.reshape(depth, -1), gn_g, gn_b, conv_w, conv_b,
                        lru_wr, lru_br, lru_wi, lru_bi, lru_lambda, w_mix_out, xa_wq, xa_wk, xa_wv, xa_wo)
        kv, mk, mv = _kv_proj(mem_prompt.reshape(B * n_mem, d), W['xa_wkv'], heads=xa_heads,
                              tm=_tile(B * n_mem, 512))
        mk = mk.reshape(B, n_mem, xa_heads, xa_head)
        mv = mv.reshape(B, n_mem, xa_heads, xa_head)
        hp, S1, sh1, h1, b1 = _layer(
            hp, kv, None, jnp.zeros((B, rp), F32), jnp.zeros((B, lru_w), F32),
            jnp.zeros((B, CONV_WIDTH - 1, lru_w), F32), W, alpha=alpha, xa_heads=xa_heads)
        hs, S2, sh2, h2, b2 = _layer(
            hs, (cache_mem_k[l], cache_mem_v[l]),
            state_rwkv[l], state_rwkv_shift[l], state_lru[l], state_conv[l], W, alpha=alpha, xa_heads=xa_heads)
        for lst, val in zip(outs, (mk, mv, S1, sh1, h1, b1, S2, sh2, h2, b2)):
            lst.append(val)
    return (hp, hs) + tuple(jnp.stack(o) for o in outs)
```

```python
import functools

import jax
import jax.numpy as jnp
from jax import lax
from jax.experimental import pallas as pl
from jax.experimental.pallas import tpu as pltpu

F32 = jnp.float32
BF16 = jnp.bfloat16

RWKV_HEAD = 64
DECAY_LORA = 64
AAA_LORA = 64
GATE_LORA = 128
GN_EPS = 64e-5
CONV_WIDTH = 4
LRU_C = 8.0
LN_EPS = 1e-5

V7X_SUBLANES = 8
V7X_MXU_DIM = 256
V7X_SCOPED_VMEM_BYTES = 60000 * 1024

RWKV_CHUNK = 64
HEAD_PAIR = 2 * RWKV_HEAD
SCAN_ROW_GROUPS = 2
LRU_GROUP = V7X_MXU_DIM
LRU_PROJ_PIECES = 8


def _cparams(semantics):
    return pltpu.CompilerParams(dimension_semantics=semantics, vmem_limit_bytes=V7X_SCOPED_VMEM_BYTES)


def _const_spec(shape):
    zeros = (0,) * len(shape)
    return pl.BlockSpec(shape, lambda *_: zeros)


def _dot(a, b):
    return jnp.dot(a.astype(BF16), b.astype(BF16), preferred_element_type=F32)


def _dot_dims(a, b, dims):
    return lax.dot_general(a.astype(BF16), b.astype(BF16), (dims, ((), ())), preferred_element_type=F32)


_NN = ((1,), (0,))
_NT = ((1,), (1,))
_TN = ((0,), (0,))


def _split2(x):
    hi = x.astype(BF16)
    lo = (x - hi.astype(F32)).astype(BF16)
    return hi, lo


def _dot_exact_lhs(a_bf16, b):
    hi, lo = _split2(b)
    d = lambda y: jnp.dot(a_bf16, y, preferred_element_type=F32)
    return d(hi) + d(lo)


def _dot_exact_rhs(a, b_bf16):
    hi, lo = _split2(a)
    d = lambda x: jnp.dot(x, b_bf16, preferred_element_type=F32)
    return d(hi) + d(lo)


def _layer_norm(x, g, b):
    mu = jnp.mean(x, axis=-1, keepdims=True)
    xc = x - mu
    var = jnp.mean(xc * xc, axis=-1, keepdims=True)
    return xc * lax.rsqrt(var + LN_EPS) * g + b


def _softplus(z):
    return jnp.maximum(z, 0.0) + jnp.log(1.0 + jnp.exp(-jnp.abs(z)))


def _head_ones(width):
    r = lax.broadcasted_iota(jnp.int32, (width, width), 0) // RWKV_HEAD
    c = lax.broadcasted_iota(jnp.int32, (width, width), 1) // RWKV_HEAD
    return (r == c).astype(BF16)


def _mm_kernel(x_ref, w_ref, o_ref):
    o_ref[...] = jnp.dot(x_ref[...].astype(BF16), w_ref[...], preferred_element_type=F32)


def _matmul(x, w, *, tm, tn):
    n, k = x.shape
    m = w.shape[1]
    assert n % tm == 0 and m % tn == 0
    return pl.pallas_call(
        _mm_kernel,
        grid=(m // tn, n // tm),
        in_specs=[pl.BlockSpec((tm, k), lambda j, i: (i, 0)),
                  pl.BlockSpec((k, tn), lambda j, i: (0, j))],
        out_specs=pl.BlockSpec((tm, tn), lambda j, i: (i, j)),
        out_shape=jax.ShapeDtypeStruct((n, m), F32),
        compiler_params=_cparams(("parallel", "parallel")),
        name="matmul",
    )(x, w)


def _ffn_kernel(x_ref, wg_ref, wu_ref, wo_ref, g_ref, b_ref, o_ref, *, alpha, row_groups):
    tm = x_ref.shape[0]
    tg = tm // row_groups
    for gi in range(row_groups):
        rows = slice(gi * tg, (gi + 1) * tg)
        x = x_ref[rows, :]
        xb = x.astype(BF16)
        gate = jnp.dot(xb, wg_ref[...], preferred_element_type=F32)
        up = jnp.dot(xb, wu_ref[...], preferred_element_type=F32)
        mid = (gate * jax.nn.sigmoid(gate) * up).astype(BF16)
        down = jnp.dot(mid, wo_ref[...], preferred_element_type=F32)
        o_ref[rows, :] = _layer_norm(alpha * x + 0.5 * down, g_ref[...], b_ref[...])


def _ffn_ln(x, wg, wu, wo, g, b, *, alpha, tm):
    n, d = x.shape
    assert n % tm == 0
    row_groups = max(1, tm // V7X_MXU_DIM)
    resident = lambda w: pl.BlockSpec(w.shape, lambda i: (0, 0), pipeline_mode=pl.Buffered(1))
    return pl.pallas_call(
        functools.partial(_ffn_kernel, alpha=alpha, row_groups=row_groups),
        grid=(n // tm,),
        in_specs=[pl.BlockSpec((tm, d), lambda i: (i, 0)),
                  resident(wg), resident(wu), resident(wo),
                  _const_spec(g.shape), _const_spec(b.shape)],
        out_specs=pl.BlockSpec((tm, d), lambda i: (i, 0)),
        out_shape=jax.ShapeDtypeStruct((n, d), F32),
        compiler_params=_cparams(("parallel",)),
        name="ffn_ln",
    )(x, wg, wu, wo, g, b)


def _rwkv_pre(r, k, v, xx, w0, w2, a0, a2, g2, k_k, k_a, r_k, ones):
    xw = xx[:, :DECAY_LORA]
    xa = xx[:, DECAY_LORA:DECAY_LORA + AAA_LORA]
    xg = xx[:, DECAY_LORA + AAA_LORA:]
    z = w0 + _dot(jnp.tanh(xw), w2)
    lw = -jnp.exp(-_softplus(-z) - 0.5)
    a = jax.nn.sigmoid(a0 + _dot(xa, a2))
    g = _dot(jax.nn.sigmoid(xg), g2)
    kkr = k * k_k
    ss = _dot(kkr * kkr, ones)
    kk = kkr * lax.rsqrt(jnp.maximum(ss, 1e-24))
    kf = k * (1.0 + (a - 1.0) * k_a)
    bonus = _dot(r * kf * r_k, ones) * v
    return lw, a, g, kk, kf, bonus


def _rwkv_post(y, bonus, g, gn_g, gn_b, ones):
    inv_n = 1.0 / RWKV_HEAD
    ym = _dot_exact_rhs(y, ones) * inv_n
    yc = y - ym
    yv = _dot(yc * yc, ones) * inv_n
    yn = yc * lax.rsqrt(yv + GN_EPS) * gn_g + gn_b
    return (yn + bonus) * g


def _bdot(a, b, dims):
    dn = ((tuple(d + 1 for d in dims[0]), tuple(d + 1 for d in dims[1])), ((0,), (0,)))
    return lax.dot_general(a.astype(BF16), b.astype(BF16), dn, preferred_element_type=F32)


def _scan_operands(r, kf, v, kk, a, lw):
    tt, hw = r.shape
    C = RWKV_CHUNK
    n_pairs = hw // HEAD_PAIR
    ltri = (lax.broadcasted_iota(jnp.int32, (C, C), 0) >= lax.broadcasted_iota(jnp.int32, (C, C), 1)).astype(BF16)
    first = lax.broadcasted_iota(jnp.int32, (C, HEAD_PAIR), 1) < RWKV_HEAD

    def bd(x):
        return jnp.concatenate([jnp.where(first, x, 0.0), jnp.where(first, 0.0, x)], axis=1)

    names = ('a', 'r', 'b', 'k', 'v', 'bh', 'kh')
    ops = {n: [] for n in names}
    wcs = []
    for c in range(tt // C):
        rows = slice(c * C, (c + 1) * C)
        lw_c = lw[rows]
        L = _dot_exact_lhs(ltri, lw_c)
        Lc = L[C - 1:C, :]
        e_nl = jnp.exp(-L)
        e_c = jnp.exp(Lc - L)
        bb = kk[rows] * a[rows]
        bf = lambda x: x.astype(BF16)
        tile = dict(a=bf(-kk[rows] * jnp.exp(L - lw_c)), r=r[rows] * jnp.exp(L), b=bf(bb * e_nl),
                    k=bf(kf[rows] * e_nl), v=bf(v[rows]), bh=bf(bb * e_c), kh=bf(kf[rows] * e_c))
        wc = jnp.exp(Lc)
        for p in range(n_pairs):
            lanes = slice(p * HEAD_PAIR, (p + 1) * HEAD_PAIR)
            for n in names:
                ops[n].append(tile[n][:, lanes])
            wcs.append(wc[:, lanes])
    A, R, B, K, V, Bh, Kh = (jnp.stack(ops[n]) for n in names)
    Vbd = bd(V)
    G = _bdot(jnp.concatenate([A, R.astype(BF16)], axis=1), jnp.concatenate([bd(B), bd(K)], axis=1), _NT)
    tok = lax.broadcasted_iota(jnp.int32, (C, HEAD_PAIR), 0)
    src = lax.broadcasted_iota(jnp.int32, (C, HEAD_PAIR), 1) % RWKV_HEAD
    a_ab = jnp.where(tok > src, G[:, :C, :HEAD_PAIR], 0.0)
    a_ak = jnp.where(tok > src, G[:, :C, HEAD_PAIR:], 0.0).astype(BF16)
    a_rb = jnp.where(tok >= src, G[:, C:, :HEAD_PAIR], 0.0).astype(BF16)
    a_rk = jnp.where(tok >= src, G[:, C:, HEAD_PAIR:], 0.0).astype(BF16)
    P = jnp.where(tok == src, 1.0, 0.0) + a_ab
    N = a_ab.astype(BF16)
    N = _bdot(N, bd(N), _NN).astype(BF16)
    steps = 2
    while 2 * steps < C:
        NP = _bdot(jnp.concatenate([N, P.astype(BF16)], axis=1), bd(N), _NN)
        N = NP[:, :C].astype(BF16)
        P = P + NP[:, C:]
        steps *= 2
    P = (P + _bdot(P, bd(N), _NN)).astype(BF16)
    aV = _bdot(a_ak, Vbd, _NN).astype(BF16)
    XU = _bdot(P, jnp.concatenate([bd(A), bd(aV)], axis=2), _NN).astype(BF16)
    X1 = XU[:, :, :HEAD_PAIR]
    Uloc = XU[:, :, HEAD_PAIR:]
    Q = (R + _bdot(a_rb, bd(X1), _NN)).astype(BF16)
    Yloc = _bdot(jnp.concatenate([a_rb, a_rk], axis=2), jnp.concatenate([bd(Uloc), Vbd], axis=1), _NN)
    ri = lax.broadcasted_iota(jnp.int32, (HEAD_PAIR, HEAD_PAIR), 0) // RWKV_HEAD
    ci = lax.broadcasted_iota(jnp.int32, (HEAD_PAIR, HEAD_PAIR), 1) // RWKV_HEAD
    same_head = ri == ci
    Pm = jnp.where(same_head, _bdot(X1, Bh, _TN), 0.0).astype(BF16)
    Sloc = jnp.where(same_head, _bdot(jnp.concatenate([Uloc, V], axis=1), jnp.concatenate([Bh, Kh], axis=1), _TN),
                     0.0)
    return Q, Yloc, Pm, Sloc, jnp.stack(wcs)


def _rwkv_chunk_kernel(pr_ref, pk_ref, pv_ref, px_ref, sr_ref, sk_ref, sv_ref, sx_ref,
                       mur_ref, muk_ref, muv_ref, mux_ref, w0_ref, w2_ref, a0_ref, a2_ref, g2_ref,
                       kk_ref, ka_ref, rk_ref, gng_ref, gnb_ref, ones_ref,
                       out_ref, s_out_ref,
                       s_scr, cr_scr, ck_scr, cv_scr, cx_scr, y_scr):
    ti = pl.program_id(2)
    tt, hw = pr_ref.shape
    n_pairs = hw // HEAD_PAIR

    @pl.when(ti == 0)
    def _init():
        s_scr[...] = jnp.zeros(s_scr.shape, F32)
        cr_scr[...] = sr_ref[...]
        ck_scr[...] = sk_ref[...]
        cv_scr[...] = sv_ref[...]
        cx_scr[...] = sx_ref[...]

    S8 = V7X_SUBLANES

    def shifted(p_ref, c_scr, mu_ref):
        p = p_ref[...]
        first = lax.broadcasted_iota(jnp.int32, (S8, p.shape[1]), 0) == 0
        head = jnp.where(first, c_scr[...], pltpu.roll(p[:S8], 1, 0))
        prev = jnp.concatenate([head, p_ref[S8 - 1:tt - 1, :]], axis=0)
        c_scr[...] = p_ref[tt - 1:tt, :]
        return p + (prev - p) * mu_ref[...]

    r = shifted(pr_ref, cr_scr, mur_ref)
    k = shifted(pk_ref, ck_scr, muk_ref)
    v = shifted(pv_ref, cv_scr, muv_ref)
    xx = shifted(px_ref, cx_scr, mux_ref)
    ones = ones_ref[...]
    lw, a, g, kk, kf, bonus = _rwkv_pre(r, k, v, xx, w0_ref[...], w2_ref[...], a0_ref[...], a2_ref[...],
                                        g2_ref[...], kk_ref[...], ka_ref[...], rk_ref[...], ones)
    C = RWKV_CHUNK
    S = s_scr[...]
    n_groups = SCAN_ROW_GROUPS if tt % (SCAN_ROW_GROUPS * C) == 0 else 1
    tg = tt // n_groups
    for gi in range(n_groups):
        rows = slice(gi * tg, (gi + 1) * tg)
        Q, Yloc, Pm, Sloc, wc = _scan_operands(r[rows], kf[rows], v[rows], kk[rows], a[rows], lw[rows])
        for c in range(tg // C):
            inst = slice(c * n_pairs, (c + 1) * n_pairs)
            y_c = _bdot(Q[inst], S, _NT) + Yloc[inst]
            row0 = gi * tg + c * C
            for p in range(n_pairs):
                y_scr[row0:row0 + C, p * HEAD_PAIR:(p + 1) * HEAD_PAIR] = y_c[p]
            S = S * wc[inst] + _bdot(S, Pm[inst], _NN) + Sloc[inst]
    s_scr[...] = S
    out_ref[...] = _rwkv_post(y_scr[...], bonus, g, gng_ref[...], gnb_ref[...], ones)

    @pl.when(ti == pl.num_programs(2) - 1)
    def _emit_state():
        for p in range(n_pairs):
            s_out_ref[2 * p] = S[p, :RWKV_HEAD, :RWKV_HEAD]
            s_out_ref[2 * p + 1] = S[p, RWKV_HEAD:, RWKV_HEAD:]


def _rwkv_chunked(pa, shift0, W, *, tt, hw):
    B, T, _ = pa.shape
    width = W['decay_w0'].shape[1]
    heads = width // RWKV_HEAD
    assert T % tt == 0 and tt % RWKV_CHUNK == 0 and width % hw == 0 and hw % HEAD_PAIR == 0
    nb = width // hw
    xw = DECAY_LORA + AAA_LORA + GATE_LORA
    assert (3 * width) % xw == 0
    xblk = 3 * width // xw
    col = lambda off: (lambda b, h, t: (b, t, off + h))
    vec = lambda: pl.BlockSpec((1, hw), lambda b, h, t: (0, h))
    in_specs = [
        pl.BlockSpec((None, tt, hw), col(0)), pl.BlockSpec((None, tt, hw), col(nb)),
        pl.BlockSpec((None, tt, hw), col(2 * nb)), pl.BlockSpec((None, tt, xw), lambda b, h, t: (b, t, xblk)),
        pl.BlockSpec((None, 1, hw), lambda b, h, t: (b, 0, h)), pl.BlockSpec((None, 1, hw), lambda b, h, t: (b, 0, nb + h)),
        pl.BlockSpec((None, 1, hw), lambda b, h, t: (b, 0, 2 * nb + h)), pl.BlockSpec((None, 1, xw), lambda b, h, t: (b, 0, xblk)),
        vec(), vec(), vec(), _const_spec((1, xw)),
        vec(), pl.BlockSpec((DECAY_LORA, hw), lambda b, h, t: (0, h)),
        vec(), pl.BlockSpec((AAA_LORA, hw), lambda b, h, t: (0, h)),
        pl.BlockSpec((GATE_LORA, hw), lambda b, h, t: (0, h)),
        vec(), vec(), vec(), vec(), vec(), _const_spec((hw, hw)),
    ]
    out, s_new = pl.pallas_call(
        _rwkv_chunk_kernel,
        grid=(B, nb, T // tt),
        in_specs=in_specs,
        out_specs=[pl.BlockSpec((None, tt, hw), lambda b, h, t: (b, t, h)),
                   pl.BlockSpec((None, hw // RWKV_HEAD, RWKV_HEAD, RWKV_HEAD), lambda b, h, t: (b, h, 0, 0))],
        out_shape=[jax.ShapeDtypeStruct((B, T, width), F32),
                   jax.ShapeDtypeStruct((B, heads, RWKV_HEAD, RWKV_HEAD), F32)],
        scratch_shapes=[pltpu.VMEM((hw // HEAD_PAIR, HEAD_PAIR, HEAD_PAIR), F32),
                        pltpu.VMEM((1, hw), F32), pltpu.VMEM((1, hw), F32), pltpu.VMEM((1, hw), F32),
                        pltpu.VMEM((1, xw), F32), pltpu.VMEM((tt, hw), F32)],
        compiler_params=_cparams(("parallel", "parallel", "arbitrary")),
        name="rwkv_chunked",
    )(pa, pa, pa, pa, shift0, shift0, shift0, shift0,
      W['mu_r'], W['mu_k'], W['mu_v'], W['mu_x'], W['decay_w0'], W['decay_w2'], W['aaa_a0'], W['aaa_a2'],
      W['gate_g2'], W['k_k'], W['k_a'], W['r_k'], W['gn_g'], W['gn_b'], _head_ones(hw))
    return out, s_new


def _rwkv_step_kernel(p_ref, s0_ref, st_ref, mu_ref, w0_ref, w2_ref, a0_ref, a2_ref, g2_ref,
                      kk_ref, ka_ref, rk_ref, gng_ref, gnb_ref, ones_ref,
                      out_ref, st_out_ref, vec_scr, y_scr, bonus_scr, g_scr):
    i = pl.program_id(0)
    B, width = out_ref.shape
    hg = st_ref.shape[0]
    H = RWKV_HEAD

    @pl.when(i == 0)
    def _prologue():
        ones = ones_ref[...]
        p = p_ref[...]
        xs = p + (s0_ref[...] - p) * mu_ref[...]
        r, k, v, xx = xs[:, :width], xs[:, width:2 * width], xs[:, 2 * width:3 * width], xs[:, 3 * width:]
        lw, a, g, kk, kf, bonus = _rwkv_pre(r, k, v, xx, w0_ref[...], w2_ref[...], a0_ref[...], a2_ref[...],
                                            g2_ref[...], kk_ref[...], ka_ref[...], rk_ref[...], ones)
        for j, vec in enumerate((r, kf, v, kk, kk * a, jnp.exp(lw))):
            vec_scr[j] = vec.T
        bonus_scr[...] = bonus
        g_scr[...] = g

    for hl in range(hg):
        base = pl.multiple_of((i * hg + hl) * H, H)
        r_h, kf_h, kk_h, kka_h, w_h = (vec_scr[j, pl.ds(base, H), :] for j in (0, 1, 3, 4, 5))

        def body(vi, carry):
            S = st_ref[hl, vi]
            sa = jnp.sum(S * kk_h, axis=0, keepdims=True)
            v_row = vec_scr[2, pl.ds(base + vi, 1), :]
            S2 = S * w_h - sa * kka_h + v_row * kf_h
            st_out_ref[hl, vi] = S2
            y_scr[pl.ds(base + vi, 1), :] = jnp.sum(S2 * r_h, axis=0, keepdims=True)
            return carry

        lax.fori_loop(0, H, body, 0, unroll=8)

    @pl.when(i == pl.num_programs(0) - 1)
    def _epilogue():
        out_ref[...] = _rwkv_post(y_scr[...].T, bonus_scr[...], g_scr[...], gng_ref[...], gnb_ref[...],
                                  ones_ref[...])


def _rwkv_step(pa, shift0, state_t, W, *, hg):
    B, proj = pa.shape
    width = W['decay_w0'].shape[1]
    heads = width // RWKV_HEAD
    assert heads % hg == 0 and state_t.shape == (heads, RWKV_HEAD, RWKV_HEAD, B)
    st_spec = pl.BlockSpec((hg, RWKV_HEAD, RWKV_HEAD, B), lambda i: (i, 0, 0, 0))
    names = ('decay_w0', 'decay_w2', 'aaa_a0', 'aaa_a2', 'gate_g2', 'k_k', 'k_a', 'r_k', 'gn_g', 'gn_b')
    out, st_new = pl.pallas_call(
        _rwkv_step_kernel,
        grid=(heads // hg,),
        in_specs=[_const_spec((B, proj)), _const_spec((B, proj)), st_spec, _const_spec(W['shift_mu'].shape)]
                 + [_const_spec(W[n].shape) for n in names] + [_const_spec((width, width))],
        out_specs=[_const_spec((B, width)), st_spec],
        out_shape=[jax.ShapeDtypeStruct((B, width), F32), jax.ShapeDtypeStruct(state_t.shape, F32)],
        scratch_shapes=[pltpu.VMEM((6, width, B), F32), pltpu.VMEM((width, B), F32),
                        pltpu.VMEM((B, width), F32), pltpu.VMEM((B, width), F32)],
        compiler_params=_cparams(("arbitrary",)),
        name="rwkv_step",
    )(pa, shift0, state_t, W['shift_mu'], *[W[n] for n in names], _head_ones(width))
    return out, st_new


def _lru_coeffs(xc, wr_ref, br_ref, wi_ref, bi_ref, lam_ref):
    xcb = xc.astype(BF16)
    n_grp = xc.shape[1] // LRU_GROUP
    grp = lambda w_ref: jnp.concatenate(
        [jnp.dot(xcb[:, i * LRU_GROUP:(i + 1) * LRU_GROUP], w_ref[i], preferred_element_type=F32)
         for i in range(n_grp)], axis=1)
    gr = jax.nn.sigmoid(grp(wr_ref) + br_ref[...])
    gi = jax.nn.sigmoid(grp(wi_ref) + bi_ref[...])
    a_t = jnp.exp(-LRU_C * gr * _softplus(-lam_ref[...]))
    return a_t, jnp.sqrt(1.0 - a_t * a_t) * gi * xc


def _lru_finish(hs, pg, g0, g1, rw, h, wmix_ref, lng_ref, lnb_ref, alpha):
    lru_out = hs * jax.nn.gelu(pg)
    merged = jax.nn.sigmoid(g0) * rw + jax.nn.sigmoid(g1) * lru_out
    mix = jnp.dot(merged.astype(BF16), wmix_ref[...], preferred_element_type=F32)
    return _layer_norm(alpha * h + mix, lng_ref[...], lnb_ref[...])


def _lru_kernel(hn_ref, rw_ref, h_ref, buf_ref, h0_ref, win_ref,
                cw_ref, cb_ref, wr_ref, br_ref, wi_ref, bi_ref, lam_ref, wmix_ref, lng_ref, lnb_ref,
                out_ref, hlast_ref, tail_ref, proj_scr, tail_scr, hc_scr, *, alpha):
    ti = pl.program_id(1)
    flat = pl.program_id(0) * pl.num_programs(1) + ti
    tt, width = rw_ref.shape
    S8 = V7X_SUBLANES
    project = lambda ref: jnp.dot(ref[...].astype(BF16), win_ref[...], preferred_element_type=F32)
    col = lambda j: slice(j * width, (j + 1) * width)

    @pl.when(ti == 0)
    def _init():
        tail_scr[...] = buf_ref[...]
        hc_scr[...] = h0_ref[...]

    @pl.when(flat == 0)
    def _first_projection():
        proj_scr[0] = project(h_ref)

    def step(cur_scr, nxt_scr):
        hn = hn_ref[...].astype(BF16)
        piece_w = win_ref.shape[1] // LRU_PROJ_PIECES
        pieces = iter(range(LRU_PROJ_PIECES))

        def emit(count=1):
            for _ in range(count):
                k = next(pieces, None)
                if k is not None:
                    cols = slice(k * piece_w, (k + 1) * piece_w)
                    nxt_scr[:, cols] = jnp.dot(hn, win_ref[:, cols], preferred_element_type=F32)

        x = cur_scr[:, col(0)]
        tail = tail_scr[...]
        r8 = lax.broadcasted_iota(jnp.int32, (S8, width), 0)

        def delayed(d):
            head = jnp.where(r8 < d, pltpu.roll(tail, d, 0), pltpu.roll(x[:S8], d, 0))
            return jnp.concatenate([head, cur_scr[S8 - d:tt - d, col(0)]], axis=0)

        cw = cw_ref[...]
        conv = cw[0:1] * delayed(CONV_WIDTH - 1)
        for j in range(1, CONV_WIDTH - 1):
            conv = conv + cw[j:j + 1] * delayed(CONV_WIDTH - 1 - j)
        xc = cb_ref[...] + (conv + cw[CONV_WIDTH - 1:CONV_WIDTH] * x)
        tail_scr[...] = x[tt - S8:, :]
        tail_ref[...] = x[tt - S8:, :]
        emit()

        A, Bv = _lru_coeffs(xc, wr_ref, br_ref, wi_ref, bi_ref, lam_ref)
        emit()
        A = A.reshape(tt // S8, S8, width)
        Bv = Bv.reshape(tt // S8, S8, width)
        in_grp = lax.broadcasted_iota(jnp.int32, (S8, width), 0)
        s = 1
        while s < S8:
            keep = in_grp >= s
            Bv = Bv + A * jnp.where(keep, pltpu.roll(Bv, s, 1), 0.0)
            A = A * jnp.where(keep, pltpu.roll(A, s, 1), 1.0)
            s *= 2
            emit()
        carry = hc_scr[...]
        groups = []
        for gi in range(tt // S8):
            hg = Bv[gi] + A[gi] * carry
            groups.append(hg)
            carry = hg[S8 - 1:S8, :]
        hs = jnp.concatenate(groups, axis=0)
        hc_scr[...] = carry
        hlast_ref[...] = carry
        emit()
        lru_out = hs * jax.nn.gelu(cur_scr[:, col(1)])
        emit()
        merged = jax.nn.sigmoid(cur_scr[:, col(2)]) * rw_ref[...] + jax.nn.sigmoid(cur_scr[:, col(3)]) * lru_out
        emit()
        mix = jnp.dot(merged.astype(BF16), wmix_ref[...], preferred_element_type=F32)
        out_ref[...] = _layer_norm(alpha * h_ref[...] + mix, lng_ref[...], lnb_ref[...])
        emit(LRU_PROJ_PIECES)

    slot = flat % 2
    step(proj_scr.at[slot], proj_scr.at[1 - slot])


def _lru_step_kernel(x_ref, pg_ref, g0_ref, g1_ref, rw_ref, h_ref, buf_ref, h0_ref,
                     cw_ref, cb_ref, wr_ref, br_ref, wi_ref, bi_ref, lam_ref, wmix_ref, lng_ref, lnb_ref,
                     out_ref, hnew_ref, *, alpha):
    cw = cw_ref[...]
    conv = cw[0:1] * buf_ref[0]
    for j in range(1, CONV_WIDTH - 1):
        conv = conv + cw[j:j + 1] * buf_ref[j]
    xc = cb_ref[...] + (conv + cw[CONV_WIDTH - 1:CONV_WIDTH] * x_ref[...])
    A, Bv = _lru_coeffs(xc, wr_ref, br_ref, wi_ref, bi_ref, lam_ref)
    hs = Bv + A * h0_ref[...]
    hnew_ref[...] = hs
    out_ref[...] = _lru_finish(hs, pg_ref[...], g0_ref[...], g1_ref[...], rw_ref[...], h_ref[...],
                               wmix_ref, lng_ref, lnb_ref, alpha)


_LRU_WEIGHTS = ('conv_w', 'conv_b', 'lru_wr', 'lru_br', 'lru_wi', 'lru_bi', 'lru_lambda', 'w_mix_out')


def _lru_step_mix_ln(pb, rw, h, buf, h0, W, ln_g, ln_b, *, alpha, tm):
    B, width = rw.shape
    assert B % tm == 0
    blk = lambda j: pl.BlockSpec((tm, width), lambda i: (i, j))
    out, h_new = pl.pallas_call(
        functools.partial(_lru_step_kernel, alpha=alpha),
        grid=(B // tm,),
        in_specs=[blk(0), blk(1), blk(2), blk(3), blk(0), blk(0),
                  pl.BlockSpec((CONV_WIDTH - 1, tm, width), lambda i: (0, i, 0)), blk(0)]
                 + [_const_spec(W[n].shape) for n in _LRU_WEIGHTS] + [_const_spec(ln_g.shape), _const_spec(ln_b.shape)],
        out_specs=[blk(0), blk(0)],
        out_shape=[jax.ShapeDtypeStruct((B, width), F32), jax.ShapeDtypeStruct((B, width), F32)],
        compiler_params=_cparams(("parallel",)),
        name="lru_step_mix_ln",
    )(pb, pb, pb, pb, rw, h, buf, h0, *[W[n] for n in _LRU_WEIGHTS], ln_g, ln_b)
    return out, h_new


def _lru_mix_ln(h, rw, buf8, h0, W, ln_g, ln_b, *, alpha, tt):
    B, T, width = rw.shape
    assert T % tt == 0 and tt % V7X_SUBLANES == 0 and h.shape[2] == width
    nt = T // tt
    tile = pl.BlockSpec((None, tt, width), lambda b, t: (b, t, 0))

    def next_tile(b, t):
        f = jnp.minimum(b * nt + t + 1, B * nt - 1)
        return f // nt, f % nt, 0

    nxt = pl.BlockSpec((None, tt, width), next_tile)
    per_b = lambda rows: pl.BlockSpec((None, rows, width), lambda b, t: (b, 0, 0))
    win = W['w_in_b']
    out, h_last, tail = pl.pallas_call(
        functools.partial(_lru_kernel, alpha=alpha),
        grid=(B, nt),
        in_specs=[nxt, tile, tile, per_b(V7X_SUBLANES), per_b(1), _const_spec(win.shape)]
                 + [_const_spec(W[n].shape) for n in _LRU_WEIGHTS] + [_const_spec(ln_g.shape), _const_spec(ln_b.shape)],
        out_specs=[tile, per_b(1), per_b(V7X_SUBLANES)],
        out_shape=[jax.ShapeDtypeStruct((B, T, width), F32), jax.ShapeDtypeStruct((B, 1, width), F32),
                   jax.ShapeDtypeStruct((B, V7X_SUBLANES, width), F32)],
        scratch_shapes=[pltpu.VMEM((2, tt, win.shape[1]), F32), pltpu.VMEM((V7X_SUBLANES, width), F32),
                        pltpu.VMEM((1, width), F32)],
        compiler_params=_cparams(("arbitrary", "arbitrary")),
        name="lru_mix_ln",
    )(h, rw, h, buf8, h0, win, *[W[n] for n in _LRU_WEIGHTS], ln_g, ln_b)
    return out, h_last, tail


def _xattn_kernel(h_ref, kv_ref, wq_ref, wo_ref, g_ref, b_ref, o_ref, *, alpha, heads):
    tm, d = h_ref.shape
    hd = d // heads
    row_groups = max(1, tm // V7X_MXU_DIM)
    tg = tm // row_groups
    for gi in range(row_groups):
        rows = slice(gi * tg, (gi + 1) * tg)
        h = h_ref[rows, :]
        q = jnp.dot(h.astype(BF16), wq_ref[...], preferred_element_type=F32)
        outs = []
        for j in range(heads):
            sl = slice(j * hd, (j + 1) * hd)
            s = _dot_dims(q[:, sl], kv_ref[:, sl], _NT) * (hd ** -0.5)
            e = jnp.exp(s - jnp.max(s, axis=-1, keepdims=True))
            p = e * (1.0 / jnp.sum(e, axis=-1, keepdims=True))
            outs.append(_dot(p, kv_ref[:, d + j * hd:d + (j + 1) * hd]))
        out = jnp.dot(jnp.concatenate(outs, axis=1).astype(BF16), wo_ref[...], preferred_element_type=F32)
        o_ref[rows, :] = _layer_norm(alpha * h + out, g_ref[...], b_ref[...])


def _xattn_rows_kernel(h_ref, mk_ref, mv_ref, wq_ref, wo_ref, g_ref, b_ref, o_ref, *, alpha):
    h = h_ref[...]
    nb, m, heads, hd = mk_ref.shape
    q = jnp.dot(h.astype(BF16), wq_ref[...], preferred_element_type=F32)
    col = lax.broadcasted_iota(jnp.int32, (heads, m * heads), 1)
    own = (col % heads) == lax.broadcasted_iota(jnp.int32, (heads, m * heads), 0)
    k2 = jnp.stack([mk_ref[i].reshape(m * heads, hd) for i in range(nb)])
    v2 = jnp.stack([mv_ref[i].reshape(m * heads, hd) for i in range(nb)])
    q4 = jnp.stack([jnp.concatenate([q[i:i + 1, j * hd:(j + 1) * hd] for j in range(heads)], axis=0)
                    for i in range(nb)])
    s = jnp.where(own, _bdot(q4, k2, _NT) * (hd ** -0.5), -jnp.inf)
    e = jnp.exp(s - jnp.max(s, axis=-1, keepdims=True))
    p = e / jnp.sum(e, axis=-1, keepdims=True)
    o4 = _bdot(p, v2, _NN)
    rows = [jnp.concatenate([o4[i, j:j + 1] for j in range(heads)], axis=1) for i in range(nb)]
    out = jnp.dot(jnp.concatenate(rows, axis=0).astype(BF16), wo_ref[...], preferred_element_type=F32)
    o_ref[...] = _layer_norm(alpha * h + out, g_ref[...], b_ref[...])


def _xattn_ln(h, kv, wq, wo, g, b, *, alpha, heads, rows_per_batch, tile_rows):
    n, d = h.shape
    assert n % tile_rows == 0 and rows_per_batch % tile_rows == 0
    per = rows_per_batch // tile_rows
    m = kv.shape[0] // (n // rows_per_batch)
    return pl.pallas_call(
        functools.partial(_xattn_kernel, alpha=alpha, heads=heads),
        grid=(n // tile_rows,),
        in_specs=[pl.BlockSpec((tile_rows, d), lambda i: (i, 0)),
                  pl.BlockSpec((m, 2 * d), lambda i: (i // per, 0)),
                  _const_spec(wq.shape), _const_spec(wo.shape), _const_spec(g.shape), _const_spec(b.shape)],
        out_specs=pl.BlockSpec((tile_rows, d), lambda i: (i, 0)),
        out_shape=jax.ShapeDtypeStruct((n, d), F32),
        compiler_params=_cparams(("parallel",)),
        name="xattn_ln",
    )(h, kv, wq, wo, g, b)


def _xattn_rows_ln(h, mk, mv, wq, wo, g, b, *, alpha, nb):
    n, d = h.shape
    _, m, heads, hd = mk.shape
    assert n % nb == 0
    kv_spec = pl.BlockSpec((nb, m, heads, hd), lambda i: (i, 0, 0, 0))
    return pl.pallas_call(
        functools.partial(_xattn_rows_kernel, alpha=alpha),
        grid=(n // nb,),
        in_specs=[pl.BlockSpec((nb, d), lambda i: (i, 0)), kv_spec, kv_spec,
                  _const_spec(wq.shape), _const_spec(wo.shape), _const_spec(g.shape), _const_spec(b.shape)],
        out_specs=pl.BlockSpec((nb, d), lambda i: (i, 0)),
        out_shape=jax.ShapeDtypeStruct((n, d), F32),
        compiler_params=_cparams(("parallel",)),
        name="xattn_rows_ln",
    )(h, mk, mv, wq, wo, g, b)


def _kv_proj_kernel(x_ref, w_ref, kv_ref, k_ref, v_ref):
    heads, hd = k_ref.shape[1:]
    d = heads * hd
    kv = jnp.dot(x_ref[...].astype(BF16), w_ref[...], preferred_element_type=F32)
    kv_ref[...] = kv
    for j in range(heads):
        k_ref[:, j, :] = kv[:, j * hd:(j + 1) * hd]
        v_ref[:, j, :] = kv[:, d + j * hd:d + (j + 1) * hd]


def _kv_proj(x, wkv, *, heads, tm):
    n, d = x.shape
    assert n % tm == 0
    hd = d // heads
    out = jax.ShapeDtypeStruct((n, heads, hd), F32)
    return pl.pallas_call(
        _kv_proj_kernel,
        grid=(n // tm,),
        in_specs=[pl.BlockSpec((tm, d), lambda i: (i, 0)), _const_spec(wkv.shape)],
        out_specs=[pl.BlockSpec((tm, 2 * d), lambda i: (i, 0))] + [pl.BlockSpec((tm, heads, hd), lambda i: (i, 0, 0))] * 2,
        out_shape=[jax.ShapeDtypeStruct((n, 2 * d), F32), out, out],
        compiler_params=_cparams(("parallel",)),
        name="kv_proj",
    )(x, wkv)


def _row(v):
    return v.reshape(1, -1)


def _block_diag_groups(w):
    n, c, _ = w.shape
    per = LRU_GROUP // c
    rows = jnp.concatenate([w.reshape(n // per, LRU_GROUP, c)] * per, axis=2)
    ri = lax.broadcasted_iota(jnp.int32, (LRU_GROUP, LRU_GROUP), 0) // c
    ci = lax.broadcasted_iota(jnp.int32, (LRU_GROUP, LRU_GROUP), 1) // c
    return jnp.where(ri == ci, rows, 0.0).astype(BF16)


def _prep_layer(l, ln_g, ln_b, ffn1_wi, ffn1_wo, ffn2_wi, ffn2_wo, w_in, shift_mu, decay_w0, decay_w2,
                aaa_a0, aaa_a2, gate_g2, k_k, k_a, r_k, gn_g, gn_b, conv_w, conv_b, lru_wr, lru_br,
                lru_wi, lru_bi, lru_lambda, w_mix_out, xa_wq, xa_wk, xa_wv, xa_wo):
    width = decay_w0.shape[1]
    rp = shift_mu.shape[1]
    d_ff = ffn1_wo.shape[1]
    bf = lambda w: w.astype(BF16)
    mu = shift_mu[l]
    return dict(
        ln_g=[_row(ln_g[l, i]) for i in range(4)], ln_b=[_row(ln_b[l, i]) for i in range(4)],
        ffn1=(bf(ffn1_wi[l][:, :d_ff]), bf(ffn1_wi[l][:, d_ff:]), bf(ffn1_wo[l])),
        ffn2=(bf(ffn2_wi[l][:, :d_ff]), bf(ffn2_wi[l][:, d_ff:]), bf(ffn2_wo[l])),
        w_in_a=bf(w_in[l][:, :rp]), w_in_b=bf(w_in[l][:, rp:]),
        shift_mu=_row(mu), mu_r=_row(mu[:width]), mu_k=_row(mu[width:2 * width]),
        mu_v=_row(mu[2 * width:3 * width]), mu_x=_row(mu[3 * width:]),
        decay_w0=_row(decay_w0[l]), decay_w2=bf(decay_w2[l]), aaa_a0=_row(aaa_a0[l]), aaa_a2=bf(aaa_a2[l]),
        gate_g2=bf(gate_g2[l]), k_k=_row(k_k[l]), k_a=_row(k_a[l]), r_k=_row(r_k[l]),
        gn_g=_row(gn_g[l]), gn_b=_row(gn_b[l]),
        conv_w=conv_w[l], conv_b=_row(conv_b[l]),
        lru_wr=_block_diag_groups(lru_wr[l]), lru_br=_row(lru_br[l]),
        lru_wi=_block_diag_groups(lru_wi[l]), lru_bi=_row(lru_bi[l]), lru_lambda=_row(lru_lambda[l]),
        w_mix_out=bf(w_mix_out[l]), xa_wq=bf(xa_wq[l]), xa_wo=bf(xa_wo[l]),
        xa_wkv=bf(jnp.concatenate([xa_wk[l], xa_wv[l]], axis=1)),
    )


def _tile(n, pref):
    return pref if n % pref == 0 else n


def _layer(h, mem, state, shift0, h0, buf0, W, *, alpha, xa_heads):
    B, T, d = h.shape
    n = B * T
    tm = _tile(n, 1024)
    h1 = _ffn_ln(h.reshape(n, d), *W['ffn1'], W['ln_g'][0], W['ln_b'][0], alpha=alpha, tm=tm)
    pa = _matmul(h1, W['w_in_a'], tm=tm, tn=W['w_in_a'].shape[1])
    width = W['decay_w0'].shape[1]
    lru_w = W['conv_b'].shape[1]
    pa3 = pa.reshape(B, T, -1)
    hist = CONV_WIDTH - 1
    if T > 1:
        assert state is None and T >= V7X_SUBLANES
        rw, s_new = _rwkv_chunked(pa3, shift0.reshape(B, 1, -1), W, tt=_tile(T, 1024), hw=_tile(width, 512))
        buf8 = jnp.concatenate([jnp.zeros((B, V7X_SUBLANES - hist, lru_w), F32), buf0], axis=1)
        h2, h_last, tail8 = _lru_mix_ln(h1.reshape(B, T, d), rw, buf8, h0.reshape(B, 1, lru_w), W,
                                        W['ln_g'][1], W['ln_b'][1], alpha=alpha, tt=_tile(T, 256))
        conv_in_tail = tail8[:, V7X_SUBLANES - hist:]
    else:
        pb = _matmul(h1, W['w_in_b'], tm=tm, tn=W['w_in_b'].shape[1])
        rw, s_new = _rwkv_step(pa, shift0, jnp.transpose(state, (1, 2, 3, 0)), W, hg=2)
        s_new = jnp.transpose(s_new, (3, 0, 1, 2))
        h2, h_last = _lru_step_mix_ln(pb, rw, h1, jnp.swapaxes(buf0, 0, 1), h0, W,
                                      W['ln_g'][1], W['ln_b'][1], alpha=alpha, tm=_tile(B, 128))
        conv_in_tail = jnp.concatenate([buf0[:, T:], pb[:, None, :lru_w]], axis=1)
    xa = (W['xa_wq'], W['xa_wo'], W['ln_g'][2], W['ln_b'][2])
    if T > 1:
        h3 = _xattn_ln(h2.reshape(n, d), mem, *xa, alpha=alpha, heads=xa_heads, rows_per_batch=T,
                       tile_rows=_tile(T, 1024))
    else:
        h3 = _xattn_rows_ln(h2, *mem, *xa, alpha=alpha, nb=_tile(B, V7X_SUBLANES))
    h4 = _ffn_ln(h3, *W['ffn2'], W['ln_g'][3], W['ln_b'][3], alpha=alpha, tm=tm)
    return h4.reshape(B, T, d), s_new, pa3[:, -1], h_last.reshape(B, lru_w), conv_in_tail


def kernel(x_prompt, x_sample, mem_prompt, cache_mem_k, cache_mem_v, state_rwkv, state_rwkv_shift, state_lru, state_conv, ln_g, ln_b, ffn1_wi, ffn1_wo, ffn2_wi, ffn2_wo, w_in, shift_mu, decay_w0, decay_w2, aaa_a0, aaa_a2, gate_g2, k_k, k_a, r_k, gn_g, gn_b, conv_w, conv_b, lru_wr, lru_br, lru_wi, lru_bi, lru_lambda, w_mix_out, xa_wq, xa_wk, xa_wv, xa_wo):
    depth = ln_g.shape[0]
    alpha = (2.0 * depth) ** 0.25
    B, _, d = x_prompt.shape
    n_mem, xa_heads, xa_head = cache_mem_k.shape[2:]
    rp = shift_mu.shape[1]
    lru_w = conv_b.shape[1]
    hp, hs = x_prompt, x_sample
    outs = [[] for _ in range(10)]
    for l in range(depth):
        W = _prep_layer(l, ln_g, ln_b, ffn1_wi, ffn1_wo, ffn2_wi, ffn2_wo, w_in, shift_mu, decay_w0, decay_w2,
                        aaa_a0, aaa_a2, gate_g2, k_k, k_a, r_k.reshape(depth, -1), gn_g, gn_b, conv_w, conv_b,
                        lru_wr, lru_br, lru_wi, lru_bi, lru_lambda, w_mix_out, xa_wq, xa_wk, xa_wv, xa_wo)
        kv, mk, mv = _kv_proj(mem_prompt.reshape(B * n_mem, d), W['xa_wkv'], heads=xa_heads,
                              tm=_tile(B * n_mem, 512))
        mk = mk.reshape(B, n_mem, xa_heads, xa_head)
        mv = mv.reshape(B, n_mem, xa_heads, xa_head)
        hp, S1, sh1, h1, b1 = _layer(
            hp, kv, None, jnp.zeros((B, rp), F32), jnp.zeros((B, lru_w), F32),
            jnp.zeros((B, CONV_WIDTH - 1, lru_w), F32), W, alpha=alpha, xa_heads=xa_heads)
        hs, S2, sh2, h2, b2 = _layer(
            hs, (cache_mem_k[l], cache_mem_v[l]),
            state_rwkv[l], state_rwkv_shift[l], state_lru[l], state_conv[l], W, alpha=alpha, xa_heads=xa_heads)
        for lst, val in zip(outs, (mk, mv, S1, sh1, h1, b1, S2, sh2, h2, b2)):
            lst.append(val)
    return (hp, hs) + tuple(jnp.stack(o) for o in outs)
```

```python
import functools

import jax
import jax.numpy as jnp
from jax import lax
from jax.experimental import pallas as pl
from jax.experimental.pallas import tpu as pltpu

F32 = jnp.float32
BF16 = jnp.bfloat16

RWKV_HEAD = 64
DECAY_LORA = 64
AAA_LORA = 64
GATE_LORA = 128
GN_EPS = 64e-5
CONV_WIDTH = 4
LRU_C = 8.0
LN_EPS = 1e-5

V7X_SUBLANES = 8
V7X_MXU_DIM = 256
V7X_SCOPED_VMEM_BYTES = 60000 * 1024

RWKV_CHUNK = 64
HEAD_PAIR = 2 * RWKV_HEAD
SCAN_ROW_GROUPS = 2
LRU_GROUP = V7X_MXU_DIM
LRU_PROJ_PIECES = 8


def _cparams(semantics):
    return pltpu.CompilerParams(dimension_semantics=semantics, vmem_limit_bytes=V7X_SCOPED_VMEM_BYTES)


def _const_spec(shape):
    zeros = (0,) * len(shape)
    return pl.BlockSpec(shape, lambda *_: zeros)


def _dot(a, b):
    return jnp.dot(a.astype(BF16), b.astype(BF16), preferred_element_type=F32)


def _dot_dims(a, b, dims):
    return lax.dot_general(a.astype(BF16), b.astype(BF16), (dims, ((), ())), preferred_element_type=F32)


_NN = ((1,), (0,))
_NT = ((1,), (1,))
_TN = ((0,), (0,))


def _split2(x):
    hi = x.astype(BF16)
    lo = (x - hi.astype(F32)).astype(BF16)
    return hi, lo


def _dot_exact_lhs(a_bf16, b):
    hi, lo = _split2(b)
    d = lambda y: jnp.dot(a_bf16, y, preferred_element_type=F32)
    return d(hi) + d(lo)


def _dot_exact_rhs(a, b_bf16):
    hi, lo = _split2(a)
    d = lambda x: jnp.dot(x, b_bf16, preferred_element_type=F32)
    return d(hi) + d(lo)


def _layer_norm(x, g, b):
    mu = jnp.mean(x, axis=-1, keepdims=True)
    xc = x - mu
    var = jnp.mean(xc * xc, axis=-1, keepdims=True)
    return xc * lax.rsqrt(var + LN_EPS) * g + b


def _softplus(z):
    return jnp.maximum(z, 0.0) + jnp.log(1.0 + jnp.exp(-jnp.abs(z)))


def _head_ones(width):
    r = lax.broadcasted_iota(jnp.int32, (width, width), 0) // RWKV_HEAD
    c = lax.broadcasted_iota(jnp.int32, (width, width), 1) // RWKV_HEAD
    return (r == c).astype(BF16)


def _mm_kernel(x_ref, w_ref, o_ref):
    o_ref[...] = jnp.dot(x_ref[...].astype(BF16), w_ref[...], preferred_element_type=F32)


def _matmul(x, w, *, tm, tn):
    n, k = x.shape
    m = w.shape[1]
    assert n % tm == 0 and m % tn == 0
    return pl.pallas_call(
        _mm_kernel,
        grid=(m // tn, n // tm),
        in_specs=[pl.BlockSpec((tm, k), lambda j, i: (i, 0)),
                  pl.BlockSpec((k, tn), lambda j, i: (0, j))],
        out_specs=pl.BlockSpec((tm, tn), lambda j, i: (i, j)),
        out_shape=jax.ShapeDtypeStruct((n, m), F32),
        compiler_params=_cparams(("parallel", "parallel")),
        name="matmul",
    )(x, w)


def _ffn_kernel(x_ref, wg_ref, wu_ref, wo_ref, g_ref, b_ref, o_ref, *, alpha, row_groups):
    tm = x_ref.shape[0]
    tg = tm // row_groups
    for gi in range(row_groups):
        rows = slice(gi * tg, (gi + 1) * tg)
        x = x_ref[rows, :]
        xb = x.astype(BF16)
        gate = jnp.dot(xb, wg_ref[...], preferred_element_type=F32)
        up = jnp.dot(xb, wu_ref[...], preferred_element_type=F32)
        mid = (gate * jax.nn.sigmoid(gate) * up).astype(BF16)
        down = jnp.dot(mid, wo_ref[...], preferred_element_type=F32)
        o_ref[rows, :] = _layer_norm(alpha * x + 0.5 * down, g_ref[...], b_ref[...])


def _ffn_ln(x, wg, wu, wo, g, b, *, alpha, tm):
    n, d = x.shape
    assert n % tm == 0
    row_groups = max(1, tm // V7X_MXU_DIM)
    resident = lambda w: pl.BlockSpec(w.shape, lambda i: (0, 0), pipeline_mode=pl.Buffered(1))
    return pl.pallas_call(
        functools.partial(_ffn_kernel, alpha=alpha, row_groups=row_groups),
        grid=(n // tm,),
        in_specs=[pl.BlockSpec((tm, d), lambda i: (i, 0)),
                  resident(wg), resident(wu), resident(wo),
                  _const_spec(g.shape), _const_spec(b.shape)],
        out_specs=pl.BlockSpec((tm, d), lambda i: (i, 0)),
        out_shape=jax.ShapeDtypeStruct((n, d), F32),
        compiler_params=_cparams(("parallel",)),
        name="ffn_ln",
    )(x, wg, wu, wo, g, b)


def _rwkv_pre(r, k, v, xx, w0, w2, a0, a2, g2, k_k, k_a, r_k, ones):
    xw = xx[:, :DECAY_LORA]
    xa = xx[:, DECAY_LORA:DECAY_LORA + AAA_LORA]
    xg = xx[:, DECAY_LORA + AAA_LORA:]
    z = w0 + _dot(jnp.tanh(xw), w2)
    lw = -jnp.exp(-_softplus(-z) - 0.5)
    a = jax.nn.sigmoid(a0 + _dot(xa, a2))
    g = _dot(jax.nn.sigmoid(xg), g2)
    kkr = k * k_k
    ss = _dot(kkr * kkr, ones)
    kk = kkr * lax.rsqrt(jnp.maximum(ss, 1e-24))
    kf = k * (1.0 + (a - 1.0) * k_a)
    bonus = _dot(r * kf * r_k, ones) * v
    return lw, a, g, kk, kf, bonus


def _rwkv_post(y, bonus, g, gn_g, gn_b, ones):
    inv_n = 1.0 / RWKV_HEAD
    ym = _dot_exact_rhs(y, ones) * inv_n
    yc = y - ym
    yv = _dot(yc * yc, ones) * inv_n
    yn = yc * lax.rsqrt(yv + GN_EPS) * gn_g + gn_b
    return (yn + bonus) * g


def _bdot(a, b, dims):
    dn = ((tuple(d + 1 for d in dims[0]), tuple(d + 1 for d in dims[1])), ((0,), (0,)))
    return lax.dot_general(a.astype(BF16), b.astype(BF16), dn, preferred_element_type=F32)


def _scan_operands(r, kf, v, kk, a, lw):
    tt, hw = r.shape
    C = RWKV_CHUNK
    n_pairs = hw // HEAD_PAIR
    ltri = (lax.broadcasted_iota(jnp.int32, (C, C), 0) >= lax.broadcasted_iota(jnp.int32, (C, C), 1)).astype(BF16)
    first = lax.broadcasted_iota(jnp.int32, (C, HEAD_PAIR), 1) < RWKV_HEAD

    def bd(x):
        return jnp.concatenate([jnp.where(first, x, 0.0), jnp.where(first, 0.0, x)], axis=1)

    names = ('a', 'r', 'b', 'k', 'v', 'bh', 'kh')
    ops = {n: [] for n in names}
    wcs = []
    for c in range(tt // C):
        rows = slice(c * C, (c + 1) * C)
        lw_c = lw[rows]
        L = _dot_exact_lhs(ltri, lw_c)
        Lc = L[C - 1:C, :]
        e_nl = jnp.exp(-L)
        e_c = jnp.exp(Lc - L)
        bb = kk[rows] * a[rows]
        bf = lambda x: x.astype(BF16)
        tile = dict(a=bf(-kk[rows] * jnp.exp(L - lw_c)), r=r[rows] * jnp.exp(L), b=bf(bb * e_nl),
                    k=bf(kf[rows] * e_nl), v=bf(v[rows]), bh=bf(bb * e_c), kh=bf(kf[rows] * e_c))
        wc = jnp.exp(Lc)
        for p in range(n_pairs):
            lanes = slice(p * HEAD_PAIR, (p + 1) * HEAD_PAIR)
            for n in names:
                ops[n].append(tile[n][:, lanes])
            wcs.append(wc[:, lanes])
    A, R, B, K, V, Bh, Kh = (jnp.stack(ops[n]) for n in names)
    Vbd = bd(V)
    G = _bdot(jnp.concatenate([A, R.astype(BF16)], axis=1), jnp.concatenate([bd(B), bd(K)], axis=1), _NT)
    tok = lax.broadcasted_iota(jnp.int32, (C, HEAD_PAIR), 0)
    src = lax.broadcasted_iota(jnp.int32, (C, HEAD_PAIR), 1) % RWKV_HEAD
    a_ab = jnp.where(tok > src, G[:, :C, :HEAD_PAIR], 0.0)
    a_ak = jnp.where(tok > src, G[:, :C, HEAD_PAIR:], 0.0).astype(BF16)
    a_rb = jnp.where(tok >= src, G[:, C:, :HEAD_PAIR], 0.0).astype(BF16)
    a_rk = jnp.where(tok >= src, G[:, C:, HEAD_PAIR:], 0.0).astype(BF16)
    P = jnp.where(tok == src, 1.0, 0.0) + a_ab
    N = a_ab.astype(BF16)
    N = _bdot(N, bd(N), _NN).astype(BF16)
    steps = 2
    while 2 * steps < C:
        NP = _bdot(jnp.concatenate([N, P.astype(BF16)], axis=1), bd(N), _NN)
        N = NP[:, :C].astype(BF16)
        P = P + NP[:, C:]
        steps *= 2
    P = (P + _bdot(P, bd(N), _NN)).astype(BF16)
    aV = _bdot(a_ak, Vbd, _NN).astype(BF16)
    XU = _bdot(P, jnp.concatenate([bd(A), bd(aV)], axis=2), _NN).astype(BF16)
    X1 = XU[:, :, :HEAD_PAIR]
    Uloc = XU[:, :, HEAD_PAIR:]
    Q = (R + _bdot(a_rb, bd(X1), _NN)).astype(BF16)
    Yloc = _bdot(jnp.concatenate([a_rb, a_rk], axis=2), jnp.concatenate([bd(Uloc), Vbd], axis=1), _NN)
    ri = lax.broadcasted_iota(jnp.int32, (HEAD_PAIR, HEAD_PAIR), 0) // RWKV_HEAD
    ci = lax.broadcasted_iota(jnp.int32, (HEAD_PAIR, HEAD_PAIR), 1) // RWKV_HEAD
    same_head = ri == ci
    Pm = jnp.where(same_head, _bdot(X1, Bh, _TN), 0.0).astype(BF16)
    Sloc = jnp.where(same_head, _bdot(jnp.concatenate([Uloc, V], axis=1), jnp.concatenate([Bh, Kh], axis=1), _TN),
                     0.0)
    return Q, Yloc, Pm, Sloc, jnp.stack(wcs)


def _rwkv_chunk_kernel(pr_ref, pk_ref, pv_ref, px_ref, sr_ref, sk_ref, sv_ref, sx_ref,
                       mur_ref, muk_ref, muv_ref, mux_ref, w0_ref, w2_ref, a0_ref, a2_ref, g2_ref,
                       kk_ref, ka_ref, rk_ref, gng_ref, gnb_ref, ones_ref,
                       out_ref, s_out_ref,
                       s_scr, cr_scr, ck_scr, cv_scr, cx_scr, y_scr):
    ti = pl.program_id(2)
    tt, hw = pr_ref.shape
    n_pairs = hw // HEAD_PAIR

    @pl.when(ti == 0)
    def _init():
        s_scr[...] = jnp.zeros(s_scr.shape, F32)
        cr_scr[...] = sr_ref[...]
        ck_scr[...] = sk_ref[...]
        cv_scr[...] = sv_ref[...]
        cx_scr[...] = sx_ref[...]

    S8 = V7X_SUBLANES

    def shifted(p_ref, c_scr, mu_ref):
        p = p_ref[...]
        first = lax.broadcasted_iota(jnp.int32, (S8, p.shape[1]), 0) == 0
        head = jnp.where(first, c_scr[...], pltpu.roll(p[:S8], 1, 0))
        prev = jnp.concatenate([head, p_ref[S8 - 1:tt - 1, :]], axis=0)
        c_scr[...] = p_ref[tt - 1:tt, :]
        return p + (prev - p) * mu_ref[...]

    r = shifted(pr_ref, cr_scr, mur_ref)
    k = shifted(pk_ref, ck_scr, muk_ref)
    v = shifted(pv_ref, cv_scr, muv_ref)
    xx = shifted(px_ref, cx_scr, mux_ref)
    ones = ones_ref[...]
    lw, a, g, kk, kf, bonus = _rwkv_pre(r, k, v, xx, w0_ref[...], w2_ref[...], a0_ref[...], a2_ref[...],
                                        g2_ref[...], kk_ref[...], ka_ref[...], rk_ref[...], ones)
    C = RWKV_CHUNK
    S = s_scr[...]
    n_groups = SCAN_ROW_GROUPS if tt % (SCAN_ROW_GROUPS * C) == 0 else 1
    tg = tt // n_groups
    for gi in range(n_groups):
        rows = slice(gi * tg, (gi + 1) * tg)
        Q, Yloc, Pm, Sloc, wc = _scan_operands(r[rows], kf[rows], v[rows], kk[rows], a[rows], lw[rows])
        for c in range(tg // C):
            inst = slice(c * n_pairs, (c + 1) * n_pairs)
            y_c = _bdot(Q[inst], S, _NT) + Yloc[inst]
            row0 = gi * tg + c * C
            for p in range(n_pairs):
                y_scr[row0:row0 + C, p * HEAD_PAIR:(p + 1) * HEAD_PAIR] = y_c[p]
            S = S * wc[inst] + _bdot(S, Pm[inst], _NN) + Sloc[inst]
    s_scr[...] = S
    out_ref[...] = _rwkv_post(y_scr[...], bonus, g, gng_ref[...], gnb_ref[...], ones)

    @pl.when(ti == pl.num_programs(2) - 1)
    def _emit_state():
        for p in range(n_pairs):
            s_out_ref[2 * p] = S[p, :RWKV_HEAD, :RWKV_HEAD]
            s_out_ref[2 * p + 1] = S[p, RWKV_HEAD:, RWKV_HEAD:]


def _rwkv_chunked(pa, shift0, W, *, tt, hw):
    B, T, _ = pa.shape
    width = W['decay_w0'].shape[1]
    heads = width // RWKV_HEAD
    assert T % tt == 0 and tt % RWKV_CHUNK == 0 and width % hw == 0 and hw % HEAD_PAIR == 0
    nb = width // hw
    xw = DECAY_LORA + AAA_LORA + GATE_LORA
    assert (3 * width) % xw == 0
    xblk = 3 * width // xw
    col = lambda off: (lambda b, h, t: (b, t, off + h))
    vec = lambda: pl.BlockSpec((1, hw), lambda b, h, t: (0, h))
    in_specs = [
        pl.BlockSpec((None, tt, hw), col(0)), pl.BlockSpec((None, tt, hw), col(nb)),
        pl.BlockSpec((None, tt, hw), col(2 * nb)), pl.BlockSpec((None, tt, xw), lambda b, h, t: (b, t, xblk)),
        pl.BlockSpec((None, 1, hw), lambda b, h, t: (b, 0, h)), pl.BlockSpec((None, 1, hw), lambda b, h, t: (b, 0, nb + h)),
        pl.BlockSpec((None, 1, hw), lambda b, h, t: (b, 0, 2 * nb + h)), pl.BlockSpec((None, 1, xw), lambda b, h, t: (b, 0, xblk)),
        vec(), vec(), vec(), _const_spec((1, xw)),
        vec(), pl.BlockSpec((DECAY_LORA, hw), lambda b, h, t: (0, h)),
        vec(), pl.BlockSpec((AAA_LORA, hw), lambda b, h, t: (0, h)),
        pl.BlockSpec((GATE_LORA, hw), lambda b, h, t: (0, h)),
        vec(), vec(), vec(), vec(), vec(), _const_spec((hw, hw)),
    ]
    out, s_new = pl.pallas_call(
        _rwkv_chunk_kernel,
        grid=(B, nb, T // tt),
        in_specs=in_specs,
        out_specs=[pl.BlockSpec((None, tt, hw), lambda b, h, t: (b, t, h)),
                   pl.BlockSpec((None, hw // RWKV_HEAD, RWKV_HEAD, RWKV_HEAD), lambda b, h, t: (b, h, 0, 0))],
        out_shape=[jax.ShapeDtypeStruct((B, T, width), F32),
                   jax.ShapeDtypeStruct((B, heads, RWKV_HEAD, RWKV_HEAD), F32)],
        scratch_shapes=[pltpu.VMEM((hw // HEAD_PAIR, HEAD_PAIR, HEAD_PAIR), F32),
                        pltpu.VMEM((1, hw), F32), pltpu.VMEM((1, hw), F32), pltpu.VMEM((1, hw), F32),
                        pltpu.VMEM((1, xw), F32), pltpu.VMEM((tt, hw), F32)],
        compiler_params=_cparams(("parallel", "parallel", "arbitrary")),
        name="rwkv_chunked",
    )(pa, pa, pa, pa, shift0, shift0, shift0, shift0,
      W['mu_r'], W['mu_k'], W['mu_v'], W['mu_x'], W['decay_w0'], W['decay_w2'], W['aaa_a0'], W['aaa_a2'],
      W['gate_g2'], W['k_k'], W['k_a'], W['r_k'], W['gn_g'], W['gn_b'], _head_ones(hw))
    return out, s_new


def _rwkv_step_kernel(p_ref, s0_ref, st_ref, mu_ref, w0_ref, w2_ref, a0_ref, a2_ref, g2_ref,
                      kk_ref, ka_ref, rk_ref, gng_ref, gnb_ref, ones_ref,
                      out_ref, st_out_ref, vec_scr, y_scr, bonus_scr, g_scr):
    i = pl.program_id(0)
    B, width = out_ref.shape
    hg = st_ref.shape[0]
    H = RWKV_HEAD

    @pl.when(i == 0)
    def _prologue():
        ones = ones_ref[...]
        p = p_ref[...]
        xs = p + (s0_ref[...] - p) * mu_ref[...]
        r, k, v, xx = xs[:, :width], xs[:, width:2 * width], xs[:, 2 * width:3 * width], xs[:, 3 * width:]
        lw, a, g, kk, kf, bonus = _rwkv_pre(r, k, v, xx, w0_ref[...], w2_ref[...], a0_ref[...], a2_ref[...],
                                            g2_ref[...], kk_ref[...], ka_ref[...], rk_ref[...], ones)
        for j, vec in enumerate((r, kf, v, kk, kk * a, jnp.exp(lw))):
            vec_scr[j] = vec.T
        bonus_scr[...] = bonus
        g_scr[...] = g

    for hl in range(hg):
        base = pl.multiple_of((i * hg + hl) * H, H)
        r_h, kf_h, kk_h, kka_h, w_h = (vec_scr[j, pl.ds(base, H), :] for j in (0, 1, 3, 4, 5))

        def body(vi, carry):
            S = st_ref[hl, vi]
            sa = jnp.sum(S * kk_h, axis=0, keepdims=True)
            v_row = vec_scr[2, pl.ds(base + vi, 1), :]
            S2 = S * w_h - sa * kka_h + v_row * kf_h
            st_out_ref[hl, vi] = S2
            y_scr[pl.ds(base + vi, 1), :] = jnp.sum(S2 * r_h, axis=0, keepdims=True)
            return carry

        lax.fori_loop(0, H, body, 0, unroll=8)

    @pl.when(i == pl.num_programs(0) - 1)
    def _epilogue():
        out_ref[...] = _rwkv_post(y_scr[...].T, bonus_scr[...], g_scr[...], gng_ref[...], gnb_ref[...],
                                  ones_ref[...])


def _rwkv_step(pa, shift0, state_t, W, *, hg):
    B, proj = pa.shape
    width = W['decay_w0'].shape[1]
    heads = width // RWKV_HEAD
    assert heads % hg == 0 and state_t.shape == (heads, RWKV_HEAD, RWKV_HEAD, B)
    st_spec = pl.BlockSpec((hg, RWKV_HEAD, RWKV_HEAD, B), lambda i: (i, 0, 0, 0))
    names = ('decay_w0', 'decay_w2', 'aaa_a0', 'aaa_a2', 'gate_g2', 'k_k', 'k_a', 'r_k', 'gn_g', 'gn_b')
    out, st_new = pl.pallas_call(
        _rwkv_step_kernel,
        grid=(heads // hg,),
        in_specs=[_const_spec((B, proj)), _const_spec((B, proj)), st_spec, _const_spec(W['shift_mu'].shape)]
                 + [_const_spec(W[n].shape) for n in names] + [_const_spec((width, width))],
        out_specs=[_const_spec((B, width)), st_spec],
        out_shape=[jax.ShapeDtypeStruct((B, width), F32), jax.ShapeDtypeStruct(state_t.shape, F32)],
        scratch_shapes=[pltpu.VMEM((6, width, B), F32), pltpu.VMEM((width, B), F32),
                        pltpu.VMEM((B, width), F32), pltpu.VMEM((B, width), F32)],
        compiler_params=_cparams(("arbitrary",)),
        name="rwkv_step",
    )(pa, shift0, state_t, W['shift_mu'], *[W[n] for n in names], _head_ones(width))
    return out, st_new


def _lru_coeffs(xc, wr_ref, br_ref, wi_ref, bi_ref, lam_ref):
    xcb = xc.astype(BF16)
    n_grp = xc.shape[1] // LRU_GROUP
    grp = lambda w_ref: jnp.concatenate(
        [jnp.dot(xcb[:, i * LRU_GROUP:(i + 1) * LRU_GROUP], w_ref[i], preferred_element_type=F32)
         for i in range(n_grp)], axis=1)
    gr = jax.nn.sigmoid(grp(wr_ref) + br_ref[...])
    gi = jax.nn.sigmoid(grp(wi_ref) + bi_ref[...])
    a_t = jnp.exp(-LRU_C * gr * _softplus(-lam_ref[...]))
    return a_t, jnp.sqrt(1.0 - a_t * a_t) * gi * xc


def _lru_finish(hs, pg, g0, g1, rw, h, wmix_ref, lng_ref, lnb_ref, alpha):
    lru_out = hs * jax.nn.gelu(pg)
    merged = jax.nn.sigmoid(g0) * rw + jax.nn.sigmoid(g1) * lru_out
    mix = jnp.dot(merged.astype(BF16), wmix_ref[...], preferred_element_type=F32)
    return _layer_norm(alpha * h + mix, lng_ref[...], lnb_ref[...])


def _lru_kernel(hn_ref, rw_ref, h_ref, buf_ref, h0_ref, win_ref,
                cw_ref, cb_ref, wr_ref, br_ref, wi_ref, bi_ref, lam_ref, wmix_ref, lng_ref, lnb_ref,
                out_ref, hlast_ref, tail_ref, proj_scr, tail_scr, hc_scr, *, alpha):
    ti = pl.program_id(1)
    flat = pl.program_id(0) * pl.num_programs(1) + ti
    tt, width = rw_ref.shape
    S8 = V7X_SUBLANES
    project = lambda ref: jnp.dot(ref[...].astype(BF16), win_ref[...], preferred_element_type=F32)
    col = lambda j: slice(j * width, (j + 1) * width)

    @pl.when(ti == 0)
    def _init():
        tail_scr[...] = buf_ref[...]
        hc_scr[...] = h0_ref[...]

    @pl.when(flat == 0)
    def _first_projection():
        proj_scr[0] = project(h_ref)

    def step(cur_scr, nxt_scr):
        hn = hn_ref[...].astype(BF16)
        piece_w = win_ref.shape[1] // LRU_PROJ_PIECES
        pieces = iter(range(LRU_PROJ_PIECES))

        def emit(count=1):
            for _ in range(count):
                k = next(pieces, None)
                if k is not None:
                    cols = slice(k * piece_w, (k + 1) * piece_w)
                    nxt_scr[:, cols] = jnp.dot(hn, win_ref[:, cols], preferred_element_type=F32)

        x = cur_scr[:, col(0)]
        tail = tail_scr[...]
        r8 = lax.broadcasted_iota(jnp.int32, (S8, width), 0)

        def delayed(d):
            head = jnp.where(r8 < d, pltpu.roll(tail, d, 0), pltpu.roll(x[:S8], d, 0))
            return jnp.concatenate([head, cur_scr[S8 - d:tt - d, col(0)]], axis=0)

        cw = cw_ref[...]
        conv = cw[0:1] * delayed(CONV_WIDTH - 1)
        for j in range(1, CONV_WIDTH - 1):
            conv = conv + cw[j:j + 1] * delayed(CONV_WIDTH - 1 - j)
        xc = cb_ref[...] + (conv + cw[CONV_WIDTH - 1:CONV_WIDTH] * x)
        tail_scr[...] = x[tt - S8:, :]
        tail_ref[...] = x[tt - S8:, :]
        emit()

        A, Bv = _lru_coeffs(xc, wr_ref, br_ref, wi_ref, bi_ref, lam_ref)
        emit()
        A = A.reshape(tt // S8, S8, width)
        Bv = Bv.reshape(tt // S8, S8, width)
        in_grp = lax.broadcasted_iota(jnp.int32, (S8, width), 0)
        s = 1
        while s < S8:
            keep = in_grp >= s
            Bv = Bv + A * jnp.where(keep, pltpu.roll(Bv, s, 1), 0.0)
            A = A * jnp.where(keep, pltpu.roll(A, s, 1), 1.0)
            s *= 2
            emit()
        carry = hc_scr[...]
        groups = []
        for gi in range(tt // S8):
            hg = Bv[gi] + A[gi] * carry
            groups.append(hg)
            carry = hg[S8 - 1:S8, :]
        hs = jnp.concatenate(groups, axis=0)
        hc_scr[...] = carry
        hlast_ref[...] = carry
        emit()
        lru_out = hs * jax.nn.gelu(cur_scr[:, col(1)])
        emit()
        merged = jax.nn.sigmoid(cur_scr[:, col(2)]) * rw_ref[...] + jax.nn.sigmoid(cur_scr[:, col(3)]) * lru_out
        emit()
        mix = jnp.dot(merged.astype(BF16), wmix_ref[...], preferred_element_type=F32)
        out_ref[...] = _layer_norm(alpha * h_ref[...] + mix, lng_ref[...], lnb_ref[...])
        emit(LRU_PROJ_PIECES)

    slot = flat % 2
    step(proj_scr.at[slot], proj_scr.at[1 - slot])


def _lru_step_kernel(x_ref, pg_ref, g0_ref, g1_ref, rw_ref, h_ref, buf_ref, h0_ref,
                     cw_ref, cb_ref, wr_ref, br_ref, wi_ref, bi_ref, lam_ref, wmix_ref, lng_ref, lnb_ref,
                     out_ref, hnew_ref, *, alpha):
    cw = cw_ref[...]
    conv = cw[0:1] * buf_ref[0]
    for j in range(1, CONV_WIDTH - 1):
        conv = conv + cw[j:j + 1] * buf_ref[j]
    xc = cb_ref[...] + (conv + cw[CONV_WIDTH - 1:CONV_WIDTH] * x_ref[...])
    A, Bv = _lru_coeffs(xc, wr_ref, br_ref, wi_ref, bi_ref, lam_ref)
    hs = Bv + A * h0_ref[...]
    hnew_ref[...] = hs
    out_ref[...] = _lru_finish(hs, pg_ref[...], g0_ref[...], g1_ref[...], rw_ref[...], h_ref[...],
                               wmix_ref, lng_ref, lnb_ref, alpha)


_LRU_WEIGHTS = ('conv_w', 'conv_b', 'lru_wr', 'lru_br', 'lru_wi', 'lru_bi', 'lru_lambda', 'w_mix_out')


def _lru_step_mix_ln(pb, rw, h, buf, h0, W, ln_g, ln_b, *, alpha, tm):
    B, width = rw.shape
    assert B % tm == 0
    blk = lambda j: pl.BlockSpec((tm, width), lambda i: (i, j))
    out, h_new = pl.pallas_call(
        functools.partial(_lru_step_kernel, alpha=alpha),
        grid=(B // tm,),
        in_specs=[blk(0), blk(1), blk(2), blk(3), blk(0), blk(0),
                  pl.BlockSpec((CONV_WIDTH - 1, tm, width), lambda i: (0, i, 0)), blk(0)]
                 + [_const_spec(W[n].shape) for n in _LRU_WEIGHTS] + [_const_spec(ln_g.shape), _const_spec(ln_b.shape)],
        out_specs=[blk(0), blk(0)],
        out_shape=[jax.ShapeDtypeStruct((B, width), F32), jax.ShapeDtypeStruct((B, width), F32)],
        compiler_params=_cparams(("parallel",)),
        name="lru_step_mix_ln",
    )(pb, pb, pb, pb, rw, h, buf, h0, *[W[n] for n in _LRU_WEIGHTS], ln_g, ln_b)
    return out, h_new


def _lru_mix_ln(h, rw, buf8, h0, W, ln_g, ln_b, *, alpha, tt):
    B, T, width = rw.shape
    assert T % tt == 0 and tt % V7X_SUBLANES == 0 and h.shape[2] == width
    nt = T // tt
    tile = pl.BlockSpec((None, tt, width), lambda b, t: (b, t, 0))

    def next_tile(b, t):
        f = jnp.minimum(b * nt + t + 1, B * nt - 1)
        return f // nt, f % nt, 0

    nxt = pl.BlockSpec((None, tt, width), next_tile)
    per_b = lambda rows: pl.BlockSpec((None, rows, width), lambda b, t: (b, 0, 0))
    win = W['w_in_b']
    out, h_last, tail = pl.pallas_call(
        functools.partial(_lru_kernel, alpha=alpha),
        grid=(B, nt),
        in_specs=[nxt, tile, tile, per_b(V7X_SUBLANES), per_b(1), _const_spec(win.shape)]
                 + [_const_spec(W[n].shape) for n in _LRU_WEIGHTS] + [_const_spec(ln_g.shape), _const_spec(ln_b.shape)],
        out_specs=[tile, per_b(1), per_b(V7X_SUBLANES)],
        out_shape=[jax.ShapeDtypeStruct((B, T, width), F32), jax.ShapeDtypeStruct((B, 1, width), F32),
                   jax.ShapeDtypeStruct((B, V7X_SUBLANES, width), F32)],
        scratch_shapes=[pltpu.VMEM((2, tt, win.shape[1]), F32), pltpu.VMEM((V7X_SUBLANES, width), F32),
                        pltpu.VMEM((1, width), F32)],
        compiler_params=_cparams(("arbitrary", "arbitrary")),
        name="lru_mix_ln",
    )(h, rw, h, buf8, h0, win, *[W[n] for n in _LRU_WEIGHTS], ln_g, ln_b)
    return out, h_last, tail


def _xattn_kernel(h_ref, kv_ref, wq_ref, wo_ref, g_ref, b_ref, o_ref, *, alpha, heads):
    tm, d = h_ref.shape
    hd = d // heads
    row_groups = max(1, tm // V7X_MXU_DIM)
    tg = tm // row_groups
    for gi in range(row_groups):
        rows = slice(gi * tg, (gi + 1) * tg)
        h = h_ref[rows, :]
        q = jnp.dot(h.astype(BF16), wq_ref[...], preferred_element_type=F32)
        per_head = lambda x, off: jnp.stack([x[:, off + j * hd:off + (j + 1) * hd] for j in range(heads)])
        s = _bdot(per_head(q, 0), per_head(kv_ref, 0), _NT) * (hd ** -0.5)
        e = jnp.exp(s - jnp.max(s, axis=-1, keepdims=True))
        p = e * (1.0 / jnp.sum(e, axis=-1, keepdims=True))
        o = _bdot(p, per_head(kv_ref, d), _NN)
        out = jnp.dot(jnp.concatenate([o[j] for j in range(heads)], axis=1).astype(BF16), wo_ref[...],
                      preferred_element_type=F32)
        o_ref[rows, :] = _layer_norm(alpha * h + out, g_ref[...], b_ref[...])


def _xattn_rows_kernel(h_ref, mk_ref, mv_ref, wq_ref, wo_ref, g_ref, b_ref, o_ref, *, alpha):
    h = h_ref[...]
    nb, m, heads, hd = mk_ref.shape
    q = jnp.dot(h.astype(BF16), wq_ref[...], preferred_element_type=F32)
    col = lax.broadcasted_iota(jnp.int32, (heads, m * heads), 1)
    own = (col % heads) == lax.broadcasted_iota(jnp.int32, (heads, m * heads), 0)
    k2 = jnp.stack([mk_ref[i].reshape(m * heads, hd) for i in range(nb)])
    v2 = jnp.stack([mv_ref[i].reshape(m * heads, hd) for i in range(nb)])
    q4 = jnp.stack([jnp.concatenate([q[i:i + 1, j * hd:(j + 1) * hd] for j in range(heads)], axis=0)
                    for i in range(nb)])
    s = jnp.where(own, _bdot(q4, k2, _NT) * (hd ** -0.5), -jnp.inf)
    e = jnp.exp(s - jnp.max(s, axis=-1, keepdims=True))
    p = e / jnp.sum(e, axis=-1, keepdims=True)
    o4 = _bdot(p, v2, _NN)
    rows = [jnp.concatenate([o4[i, j:j + 1] for j in range(heads)], axis=1) for i in range(nb)]
    out = jnp.dot(jnp.concatenate(rows, axis=0).astype(BF16), wo_ref[...], preferred_element_type=F32)
    o_ref[...] = _layer_norm(alpha * h + out, g_ref[...], b_ref[...])


def _xattn_ln(h, kv, wq, wo, g, b, *, alpha, heads, rows_per_batch, tile_rows):
    n, d = h.shape
    assert n % tile_rows == 0 and rows_per_batch % tile_rows == 0
    per = rows_per_batch // tile_rows
    m = kv.shape[0] // (n // rows_per_batch)
    return pl.pallas_call(
        functools.partial(_xattn_kernel, alpha=alpha, heads=heads),
        grid=(n // tile_rows,),
        in_specs=[pl.BlockSpec((tile_rows, d), lambda i: (i, 0)),
                  pl.BlockSpec((m, 2 * d), lambda i: (i // per, 0)),
                  _const_spec(wq.shape), _const_spec(wo.shape), _const_spec(g.shape), _const_spec(b.shape)],
        out_specs=pl.BlockSpec((tile_rows, d), lambda i: (i, 0)),
        out_shape=jax.ShapeDtypeStruct((n, d), F32),
        compiler_params=_cparams(("parallel",)),
        name="xattn_ln",
    )(h, kv, wq, wo, g, b)


def _xattn_rows_ln(h, mk, mv, wq, wo, g, b, *, alpha, nb):
    n, d = h.shape
    _, m, heads, hd = mk.shape
    assert n % nb == 0
    kv_spec = pl.BlockSpec((nb, m, heads, hd), lambda i: (i, 0, 0, 0))
    return pl.pallas_call(
        functools.partial(_xattn_rows_kernel, alpha=alpha),
        grid=(n // nb,),
        in_specs=[pl.BlockSpec((nb, d), lambda i: (i, 0)), kv_spec, kv_spec,
                  _const_spec(wq.shape), _const_spec(wo.shape), _const_spec(g.shape), _const_spec(b.shape)],
        out_specs=pl.BlockSpec((nb, d), lambda i: (i, 0)),
        out_shape=jax.ShapeDtypeStruct((n, d), F32),
        compiler_params=_cparams(("parallel",)),
        name="xattn_rows_ln",
    )(h, mk, mv, wq, wo, g, b)


def _kv_proj_kernel(x_ref, w_ref, kv_ref, k_ref, v_ref):
    heads, hd = k_ref.shape[1:]
    d = heads * hd
    kv = jnp.dot(x_ref[...].astype(BF16), w_ref[...], preferred_element_type=F32)
    kv_ref[...] = kv
    for j in range(heads):
        k_ref[:, j, :] = kv[:, j * hd:(j + 1) * hd]
        v_ref[:, j, :] = kv[:, d + j * hd:d + (j + 1) * hd]


def _kv_proj(x, wkv, *, heads, tm):
    n, d = x.shape
    assert n % tm == 0
    hd = d // heads
    out = jax.ShapeDtypeStruct((n, heads, hd), F32)
    return pl.pallas_call(
        _kv_proj_kernel,
        grid=(n // tm,),
        in_specs=[pl.BlockSpec((tm, d), lambda i: (i, 0)), _const_spec(wkv.shape)],
        out_specs=[pl.BlockSpec((tm, 2 * d), lambda i: (i, 0))] + [pl.BlockSpec((tm, heads, hd), lambda i: (i, 0, 0))] * 2,
        out_shape=[jax.ShapeDtypeStruct((n, 2 * d), F32), out, out],
        compiler_params=_cparams(("parallel",)),
        name="kv_proj",
    )(x, wkv)


def _row(v):
    return v.reshape(1, -1)


def _block_diag_groups(w):
    n, c, _ = w.shape
    per = LRU_GROUP // c
    rows = jnp.concatenate([w.reshape(n // per, LRU_GROUP, c)] * per, axis=2)
    ri = lax.broadcasted_iota(jnp.int32, (LRU_GROUP, LRU_GROUP), 0) // c
    ci = lax.broadcasted_iota(jnp.int32, (LRU_GROUP, LRU_GROUP), 1) // c
    return jnp.where(ri == ci, rows, 0.0).astype(BF16)


def _prep_layer(l, ln_g, ln_b, ffn1_wi, ffn1_wo, ffn2_wi, ffn2_wo, w_in, shift_mu, decay_w0, decay_w2,
                aaa_a0, aaa_a2, gate_g2, k_k, k_a, r_k, gn_g, gn_b, conv_w, conv_b, lru_wr, lru_br,
                lru_wi, lru_bi, lru_lambda, w_mix_out, xa_wq, xa_wk, xa_wv, xa_wo):
    width = decay_w0.shape[1]
    rp = shift_mu.shape[1]
    d_ff = ffn1_wo.shape[1]
    bf = lambda w: w.astype(BF16)
    mu = shift_mu[l]
    return dict(
        ln_g=[_row(ln_g[l, i]) for i in range(4)], ln_b=[_row(ln_b[l, i]) for i in range(4)],
        ffn1=(bf(ffn1_wi[l][:, :d_ff]), bf(ffn1_wi[l][:, d_ff:]), bf(ffn1_wo[l])),
        ffn2=(bf(ffn2_wi[l][:, :d_ff]), bf(ffn2_wi[l][:, d_ff:]), bf(ffn2_wo[l])),
        w_in_a=bf(w_in[l][:, :rp]), w_in_b=bf(w_in[l][:, rp:]),
        shift_mu=_row(mu), mu_r=_row(mu[:width]), mu_k=_row(mu[width:2 * width]),
        mu_v=_row(mu[2 * width:3 * width]), mu_x=_row(mu[3 * width:]),
        decay_w0=_row(decay_w0[l]), decay_w2=bf(decay_w2[l]), aaa_a0=_row(aaa_a0[l]), aaa_a2=bf(aaa_a2[l]),
        gate_g2=bf(gate_g2[l]), k_k=_row(k_k[l]), k_a=_row(k_a[l]), r_k=_row(r_k[l]),
        gn_g=_row(gn_g[l]), gn_b=_row(gn_b[l]),
        conv_w=conv_w[l], conv_b=_row(conv_b[l]),
        lru_wr=_block_diag_groups(lru_wr[l]), lru_br=_row(lru_br[l]),
        lru_wi=_block_diag_groups(lru_wi[l]), lru_bi=_row(lru_bi[l]), lru_lambda=_row(lru_lambda[l]),
        w_mix_out=bf(w_mix_out[l]), xa_wq=bf(xa_wq[l]), xa_wo=bf(xa_wo[l]),
        xa_wkv=bf(jnp.concatenate([xa_wk[l], xa_wv[l]], axis=1)),
    )


def _tile(n, pref):
    return pref if n % pref == 0 else n


def _layer(h, mem, state, shift0, h0, buf0, W, *, alpha, xa_heads):
    B, T, d = h.shape
    n = B * T
    tm = _tile(n, 1024)
    h1 = _ffn_ln(h.reshape(n, d), *W['ffn1'], W['ln_g'][0], W['ln_b'][0], alpha=alpha, tm=tm)
    pa = _matmul(h1, W['w_in_a'], tm=tm, tn=W['w_in_a'].shape[1])
    width = W['decay_w0'].shape[1]
    lru_w = W['conv_b'].shape[1]
    pa3 = pa.reshape(B, T, -1)
    hist = CONV_WIDTH - 1
    if T > 1:
        assert state is None and T >= V7X_SUBLANES
        rw, s_new = _rwkv_chunked(pa3, shift0.reshape(B, 1, -1), W, tt=_tile(T, 1024), hw=_tile(width, 512))
        buf8 = jnp.concatenate([jnp.zeros((B, V7X_SUBLANES - hist, lru_w), F32), buf0], axis=1)
        h2, h_last, tail8 = _lru_mix_ln(h1.reshape(B, T, d), rw, buf8, h0.reshape(B, 1, lru_w), W,
                                        W['ln_g'][1], W['ln_b'][1], alpha=alpha, tt=_tile(T, 256))
        conv_in_tail = tail8[:, V7X_SUBLANES - hist:]
    else:
        pb = _matmul(h1, W['w_in_b'], tm=tm, tn=W['w_in_b'].shape[1])
        rw, s_new = _rwkv_step(pa, shift0, jnp.transpose(state, (1, 2, 3, 0)), W, hg=2)
        s_new = jnp.transpose(s_new, (3, 0, 1, 2))
        h2, h_last = _lru_step_mix_ln(pb, rw, h1, jnp.swapaxes(buf0, 0, 1), h0, W,
                                      W['ln_g'][1], W['ln_b'][1], alpha=alpha, tm=_tile(B, 128))
        conv_in_tail = jnp.concatenate([buf0[:, T:], pb[:, None, :lru_w]], axis=1)
    xa = (W['xa_wq'], W['xa_wo'], W['ln_g'][2], W['ln_b'][2])
    if T > 1:
        h3 = _xattn_ln(h2.reshape(n, d), mem, *xa, alpha=alpha, heads=xa_heads, rows_per_batch=T,
                       tile_rows=_tile(T, 1024))
    else:
        h3 = _xattn_rows_ln(h2, *mem, *xa, alpha=alpha, nb=_tile(B, V7X_SUBLANES))
    h4 = _ffn_ln(h3, *W['ffn2'], W['ln_g'][3], W['ln_b'][3], alpha=alpha, tm=tm)
    return h4.reshape(B, T, d), s_new, pa3[:, -1], h_last.reshape(B, lru_w), conv_in_tail


def kernel(x_prompt, x_sample, mem_prompt, cache_mem_k, cache_mem_v, state_rwkv, state_rwkv_shift, state_lru, state_conv, ln_g, ln_b, ffn1_wi, ffn1_wo, ffn2_wi, ffn2_wo, w_in, shift_mu, decay_w0, decay_w2, aaa_a0, aaa_a2, gate_g2, k_k, k_a, r_k, gn_g, gn_b, conv_w, conv_b, lru_wr, lru_br, lru_wi, lru_bi, lru_lambda, w_mix_out, xa_wq, xa_wk, xa_wv, xa_wo):
    depth = ln_g.shape[0]
    alpha = (2.0 * depth) ** 0.25
    B, _, d = x_prompt.shape
    n_mem, xa_heads, xa_head = cache_mem_k.shape[2:]
    rp = shift_mu.shape[1]
    lru_w = conv_b.shape[1]
    hp, hs = x_prompt, x_sample
    outs = [[] for _ in range(10)]
    for l in range(depth):
        W = _prep_layer(l, ln_g, ln_b, ffn1_wi, ffn1_wo, ffn2_wi, ffn2_wo, w_in, shift_mu, decay_w0, decay_w2,
                        aaa_a0, aaa_a2, gate_g2, k_k, k_a, r_k.reshape(depth, -1), gn_g, gn_b, conv_w, conv_b,
                        lru_wr, lru_br, lru_wi, lru_bi, lru_lambda, w_mix_out, xa_wq, xa_wk, xa_wv, xa_wo)
        kv, mk, mv = _kv_proj(mem_prompt.reshape(B * n_mem, d), W['xa_wkv'], heads=xa_heads,
                              tm=_tile(B * n_mem, 512))
        mk = mk.reshape(B, n_mem, xa_heads, xa_head)
        mv = mv.reshape(B, n_mem, xa_heads, xa_head)
        hp, S1, sh1, h1, b1 = _layer(
            hp, kv, None, jnp.zeros((B, rp), F32), jnp.zeros((B, lru_w), F32),
            jnp.zeros((B, CONV_WIDTH - 1, lru_w), F32), W, alpha=alpha, xa_heads=xa_heads)
        hs, S2, sh2, h2, b2 = _layer(
            hs, (cache_mem_k[l], cache_mem_v[l]),
            state_rwkv[l], state_rwkv_shift[l], state_lru[l], state_conv[l], W, alpha=alpha, xa_heads=xa_heads)
        for lst, val in zip(outs, (mk, mv, S1, sh1, h1, b1, S2, sh2, h2, b2)):
            lst.append(val)
    return (hp, hs) + tuple(jnp.stack(o) for o in outs)
```

```python
import functools

import jax
import jax.numpy as jnp
from jax import lax
from jax.experimental import pallas as pl
from jax.experimental.pallas import tpu as pltpu

F32 = jnp.float32
BF16 = jnp.bfloat16

RWKV_HEAD = 64
DECAY_LORA = 64
AAA_LORA = 64
GATE_LORA = 128
GN_EPS = 64e-5
CONV_WIDTH = 4
LRU_C = 8.0
LN_EPS = 1e-5

V7X_SUBLANES = 8
V7X_MXU_DIM = 256
V7X_SCOPED_VMEM_BYTES = 60000 * 1024

RWKV_CHUNK = 64
HEAD_PAIR = 2 * RWKV_HEAD
SCAN_ROW_GROUPS = 2
LRU_GROUP = V7X_MXU_DIM
STATE_RING = 3
LRU_PROJ_PIECES = 8


def _cparams(semantics):
    return pltpu.CompilerParams(dimension_semantics=semantics, vmem_limit_bytes=V7X_SCOPED_VMEM_BYTES)


def _const_spec(shape):
    zeros = (0,) * len(shape)
    return pl.BlockSpec(shape, lambda *_: zeros)


def _dot(a, b):
    return jnp.dot(a.astype(BF16), b.astype(BF16), preferred_element_type=F32)


def _dot_dims(a, b, dims):
    return lax.dot_general(a.astype(BF16), b.astype(BF16), (dims, ((), ())), preferred_element_type=F32)


_NN = ((1,), (0,))
_NT = ((1,), (1,))
_TN = ((0,), (0,))


def _split2(x):
    hi = x.astype(BF16)
    lo = (x - hi.astype(F32)).astype(BF16)
    return hi, lo


def _dot_exact_lhs(a_bf16, b):
    hi, lo = _split2(b)
    d = lambda y: jnp.dot(a_bf16, y, preferred_element_type=F32)
    return d(hi) + d(lo)


def _dot_exact_rhs(a, b_bf16):
    hi, lo = _split2(a)
    d = lambda x: jnp.dot(x, b_bf16, preferred_element_type=F32)
    return d(hi) + d(lo)


def _layer_norm(x, g, b):
    mu = jnp.mean(x, axis=-1, keepdims=True)
    xc = x - mu
    var = jnp.mean(xc * xc, axis=-1, keepdims=True)
    return xc * lax.rsqrt(var + LN_EPS) * g + b


def _softplus(z):
    return jnp.maximum(z, 0.0) + jnp.log(1.0 + jnp.exp(-jnp.abs(z)))


def _head_ones(width):
    r = lax.broadcasted_iota(jnp.int32, (width, width), 0) // RWKV_HEAD
    c = lax.broadcasted_iota(jnp.int32, (width, width), 1) // RWKV_HEAD
    return (r == c).astype(BF16)


def _mm_kernel(x_ref, w_ref, o_ref):
    o_ref[...] = jnp.dot(x_ref[...].astype(BF16), w_ref[...], preferred_element_type=F32)


def _matmul(x, w, *, tm, tn):
    n, k = x.shape
    m = w.shape[1]
    assert n % tm == 0 and m % tn == 0
    return pl.pallas_call(
        _mm_kernel,
        grid=(m // tn, n // tm),
        in_specs=[pl.BlockSpec((tm, k), lambda j, i: (i, 0)),
                  pl.BlockSpec((k, tn), lambda j, i: (0, j))],
        out_specs=pl.BlockSpec((tm, tn), lambda j, i: (i, j)),
        out_shape=jax.ShapeDtypeStruct((n, m), F32),
        compiler_params=_cparams(("parallel", "parallel")),
        name="matmul",
    )(x, w)


def _ffn_kernel(x_ref, wg_ref, wu_ref, wo_ref, g_ref, b_ref, o_ref, *, alpha, row_groups):
    tm = x_ref.shape[0]
    tg = tm // row_groups
    for gi in range(row_groups):
        rows = slice(gi * tg, (gi + 1) * tg)
        x = x_ref[rows, :]
        xb = x.astype(BF16)
        gate = jnp.dot(xb, wg_ref[...], preferred_element_type=F32)
        up = jnp.dot(xb, wu_ref[...], preferred_element_type=F32)
        mid = (gate * jax.nn.sigmoid(gate) * up).astype(BF16)
        down = jnp.dot(mid, wo_ref[...], preferred_element_type=F32)
        o_ref[rows, :] = _layer_norm(alpha * x + 0.5 * down, g_ref[...], b_ref[...])


def _ffn_ln(x, wg, wu, wo, g, b, *, alpha, tm):
    n, d = x.shape
    assert n % tm == 0
    row_groups = max(1, tm // V7X_MXU_DIM)
    resident = lambda w: pl.BlockSpec(w.shape, lambda i: (0, 0), pipeline_mode=pl.Buffered(1))
    return pl.pallas_call(
        functools.partial(_ffn_kernel, alpha=alpha, row_groups=row_groups),
        grid=(n // tm,),
        in_specs=[pl.BlockSpec((tm, d), lambda i: (i, 0)),
                  resident(wg), resident(wu), resident(wo),
                  _const_spec(g.shape), _const_spec(b.shape)],
        out_specs=pl.BlockSpec((tm, d), lambda i: (i, 0)),
        out_shape=jax.ShapeDtypeStruct((n, d), F32),
        compiler_params=_cparams(("parallel",)),
        name="ffn_ln",
    )(x, wg, wu, wo, g, b)


def _rwkv_pre(r, k, v, xx, w0, w2, a0, a2, g2, k_k, k_a, r_k, ones):
    xw = xx[:, :DECAY_LORA]
    xa = xx[:, DECAY_LORA:DECAY_LORA + AAA_LORA]
    xg = xx[:, DECAY_LORA + AAA_LORA:]
    z = w0 + _dot(jnp.tanh(xw), w2)
    lw = -jnp.exp(-_softplus(-z) - 0.5)
    a = jax.nn.sigmoid(a0 + _dot(xa, a2))
    g = _dot(jax.nn.sigmoid(xg), g2)
    kkr = k * k_k
    ss = _dot(kkr * kkr, ones)
    kk = kkr * lax.rsqrt(jnp.maximum(ss, 1e-24))
    kf = k * (1.0 + (a - 1.0) * k_a)
    bonus = _dot(r * kf * r_k, ones) * v
    return lw, a, g, kk, kf, bonus


def _rwkv_post(y, bonus, g, gn_g, gn_b, ones):
    inv_n = 1.0 / RWKV_HEAD
    ym = _dot_exact_rhs(y, ones) * inv_n
    yc = y - ym
    yv = _dot(yc * yc, ones) * inv_n
    yn = yc * lax.rsqrt(yv + GN_EPS) * gn_g + gn_b
    return (yn + bonus) * g


def _bdot(a, b, dims):
    dn = ((tuple(d + 1 for d in dims[0]), tuple(d + 1 for d in dims[1])), ((0,), (0,)))
    return lax.dot_general(a.astype(BF16), b.astype(BF16), dn, preferred_element_type=F32)


def _scan_operands(r, kf, v, kk, a, lw):
    tt, hw = r.shape
    C = RWKV_CHUNK
    n_pairs = hw // HEAD_PAIR
    ltri = (lax.broadcasted_iota(jnp.int32, (C, C), 0) >= lax.broadcasted_iota(jnp.int32, (C, C), 1)).astype(BF16)
    first = lax.broadcasted_iota(jnp.int32, (C, HEAD_PAIR), 1) < RWKV_HEAD

    def bd(x):
        return jnp.concatenate([jnp.where(first, x, 0.0), jnp.where(first, 0.0, x)], axis=1)

    names = ('a', 'r', 'b', 'k', 'v', 'bh', 'kh')
    ops = {n: [] for n in names}
    wcs = []
    for c in range(tt // C):
        rows = slice(c * C, (c + 1) * C)
        lw_c = lw[rows]
        L = _dot_exact_lhs(ltri, lw_c)
        Lc = L[C - 1:C, :]
        e_nl = jnp.exp(-L)
        e_c = jnp.exp(Lc - L)
        bb = kk[rows] * a[rows]
        bf = lambda x: x.astype(BF16)
        tile = dict(a=bf(-kk[rows] * jnp.exp(L - lw_c)), r=r[rows] * jnp.exp(L), b=bf(bb * e_nl),
                    k=bf(kf[rows] * e_nl), v=bf(v[rows]), bh=bf(bb * e_c), kh=bf(kf[rows] * e_c))
        wc = jnp.exp(Lc)
        for p in range(n_pairs):
            lanes = slice(p * HEAD_PAIR, (p + 1) * HEAD_PAIR)
            for n in names:
                ops[n].append(tile[n][:, lanes])
            wcs.append(wc[:, lanes])
    A, R, B, K, V, Bh, Kh = (jnp.stack(ops[n]) for n in names)
    Vbd = bd(V)
    G = _bdot(jnp.concatenate([A, R.astype(BF16)], axis=1), jnp.concatenate([bd(B), bd(K)], axis=1), _NT)
    tok = lax.broadcasted_iota(jnp.int32, (C, HEAD_PAIR), 0)
    src = lax.broadcasted_iota(jnp.int32, (C, HEAD_PAIR), 1) % RWKV_HEAD
    a_ab = jnp.where(tok > src, G[:, :C, :HEAD_PAIR], 0.0)
    a_ak = jnp.where(tok > src, G[:, :C, HEAD_PAIR:], 0.0).astype(BF16)
    a_rb = jnp.where(tok >= src, G[:, C:, :HEAD_PAIR], 0.0).astype(BF16)
    a_rk = jnp.where(tok >= src, G[:, C:, HEAD_PAIR:], 0.0).astype(BF16)
    P = jnp.where(tok == src, 1.0, 0.0) + a_ab
    N = a_ab.astype(BF16)
    N = _bdot(N, bd(N), _NN).astype(BF16)
    steps = 2
    while 2 * steps < C:
        NP = _bdot(jnp.concatenate([N, P.astype(BF16)], axis=1), bd(N), _NN)
        N = NP[:, :C].astype(BF16)
        P = P + NP[:, C:]
        steps *= 2
    P = (P + _bdot(P, bd(N), _NN)).astype(BF16)
    aV = _bdot(a_ak, Vbd, _NN).astype(BF16)
    XU = _bdot(P, jnp.concatenate([bd(A), bd(aV)], axis=2), _NN).astype(BF16)
    X1 = XU[:, :, :HEAD_PAIR]
    Uloc = XU[:, :, HEAD_PAIR:]
    Q = (R + _bdot(a_rb, bd(X1), _NN)).astype(BF16)
    Yloc = _bdot(jnp.concatenate([a_rb, a_rk], axis=2), jnp.concatenate([bd(Uloc), Vbd], axis=1), _NN)
    ri = lax.broadcasted_iota(jnp.int32, (HEAD_PAIR, HEAD_PAIR), 0) // RWKV_HEAD
    ci = lax.broadcasted_iota(jnp.int32, (HEAD_PAIR, HEAD_PAIR), 1) // RWKV_HEAD
    same_head = ri == ci
    Pm = jnp.where(same_head, _bdot(X1, Bh, _TN), 0.0).astype(BF16)
    Sloc = jnp.where(same_head, _bdot(jnp.concatenate([Uloc, V], axis=1), jnp.concatenate([Bh, Kh], axis=1), _TN),
                     0.0)
    return Q, Yloc, Pm, Sloc, jnp.stack(wcs)


def _rwkv_chunk_kernel(pr_ref, pk_ref, pv_ref, px_ref, sr_ref, sk_ref, sv_ref, sx_ref,
                       mur_ref, muk_ref, muv_ref, mux_ref, w0_ref, w2_ref, a0_ref, a2_ref, g2_ref,
                       kk_ref, ka_ref, rk_ref, gng_ref, gnb_ref, ones_ref,
                       out_ref, s_out_ref,
                       s_scr, cr_scr, ck_scr, cv_scr, cx_scr, y_scr):
    ti = pl.program_id(2)
    tt, hw = pr_ref.shape
    n_pairs = hw // HEAD_PAIR

    @pl.when(ti == 0)
    def _init():
        s_scr[...] = jnp.zeros(s_scr.shape, F32)
        cr_scr[...] = sr_ref[...]
        ck_scr[...] = sk_ref[...]
        cv_scr[...] = sv_ref[...]
        cx_scr[...] = sx_ref[...]

    S8 = V7X_SUBLANES

    def shifted(p_ref, c_scr, mu_ref):
        p = p_ref[...]
        first = lax.broadcasted_iota(jnp.int32, (S8, p.shape[1]), 0) == 0
        head = jnp.where(first, c_scr[...], pltpu.roll(p[:S8], 1, 0))
        prev = jnp.concatenate([head, p_ref[S8 - 1:tt - 1, :]], axis=0)
        c_scr[...] = p_ref[tt - 1:tt, :]
        return p + (prev - p) * mu_ref[...]

    r = shifted(pr_ref, cr_scr, mur_ref)
    k = shifted(pk_ref, ck_scr, muk_ref)
    v = shifted(pv_ref, cv_scr, muv_ref)
    xx = shifted(px_ref, cx_scr, mux_ref)
    ones = ones_ref[...]
    lw, a, g, kk, kf, bonus = _rwkv_pre(r, k, v, xx, w0_ref[...], w2_ref[...], a0_ref[...], a2_ref[...],
                                        g2_ref[...], kk_ref[...], ka_ref[...], rk_ref[...], ones)
    C = RWKV_CHUNK
    S = s_scr[...]
    n_groups = SCAN_ROW_GROUPS if tt % (SCAN_ROW_GROUPS * C) == 0 else 1
    tg = tt // n_groups
    for gi in range(n_groups):
        rows = slice(gi * tg, (gi + 1) * tg)
        Q, Yloc, Pm, Sloc, wc = _scan_operands(r[rows], kf[rows], v[rows], kk[rows], a[rows], lw[rows])
        for c in range(tg // C):
            inst = slice(c * n_pairs, (c + 1) * n_pairs)
            y_c = _bdot(Q[inst], S, _NT) + Yloc[inst]
            row0 = gi * tg + c * C
            for p in range(n_pairs):
                y_scr[row0:row0 + C, p * HEAD_PAIR:(p + 1) * HEAD_PAIR] = y_c[p]
            S = S * wc[inst] + _bdot(S, Pm[inst], _NN) + Sloc[inst]
    s_scr[...] = S
    out_ref[...] = _rwkv_post(y_scr[...], bonus, g, gng_ref[...], gnb_ref[...], ones)

    @pl.when(ti == pl.num_programs(2) - 1)
    def _emit_state():
        for p in range(n_pairs):
            s_out_ref[2 * p] = S[p, :RWKV_HEAD, :RWKV_HEAD]
            s_out_ref[2 * p + 1] = S[p, RWKV_HEAD:, RWKV_HEAD:]


def _rwkv_chunked(pa, shift0, W, *, tt, hw):
    B, T, _ = pa.shape
    width = W['decay_w0'].shape[1]
    heads = width // RWKV_HEAD
    assert T % tt == 0 and tt % RWKV_CHUNK == 0 and width % hw == 0 and hw % HEAD_PAIR == 0
    nb = width // hw
    xw = DECAY_LORA + AAA_LORA + GATE_LORA
    assert (3 * width) % xw == 0
    xblk = 3 * width // xw
    col = lambda off: (lambda b, h, t: (b, t, off + h))
    vec = lambda: pl.BlockSpec((1, hw), lambda b, h, t: (0, h))
    in_specs = [
        pl.BlockSpec((None, tt, hw), col(0)), pl.BlockSpec((None, tt, hw), col(nb)),
        pl.BlockSpec((None, tt, hw), col(2 * nb)), pl.BlockSpec((None, tt, xw), lambda b, h, t: (b, t, xblk)),
        pl.BlockSpec((None, 1, hw), lambda b, h, t: (b, 0, h)), pl.BlockSpec((None, 1, hw), lambda b, h, t: (b, 0, nb + h)),
        pl.BlockSpec((None, 1, hw), lambda b, h, t: (b, 0, 2 * nb + h)), pl.BlockSpec((None, 1, xw), lambda b, h, t: (b, 0, xblk)),
        vec(), vec(), vec(), _const_spec((1, xw)),
        vec(), pl.BlockSpec((DECAY_LORA, hw), lambda b, h, t: (0, h)),
        vec(), pl.BlockSpec((AAA_LORA, hw), lambda b, h, t: (0, h)),
        pl.BlockSpec((GATE_LORA, hw), lambda b, h, t: (0, h)),
        vec(), vec(), vec(), vec(), vec(), _const_spec((hw, hw)),
    ]
    out, s_new = pl.pallas_call(
        _rwkv_chunk_kernel,
        grid=(B, nb, T // tt),
        in_specs=in_specs,
        out_specs=[pl.BlockSpec((None, tt, hw), lambda b, h, t: (b, t, h)),
                   pl.BlockSpec((None, hw // RWKV_HEAD, RWKV_HEAD, RWKV_HEAD), lambda b, h, t: (b, h, 0, 0))],
        out_shape=[jax.ShapeDtypeStruct((B, T, width), F32),
                   jax.ShapeDtypeStruct((B, heads, RWKV_HEAD, RWKV_HEAD), F32)],
        scratch_shapes=[pltpu.VMEM((hw // HEAD_PAIR, HEAD_PAIR, HEAD_PAIR), F32),
                        pltpu.VMEM((1, hw), F32), pltpu.VMEM((1, hw), F32), pltpu.VMEM((1, hw), F32),
                        pltpu.VMEM((1, xw), F32), pltpu.VMEM((tt, hw), F32)],
        compiler_params=_cparams(("parallel", "parallel", "arbitrary")),
        name="rwkv_chunked",
    )(pa, pa, pa, pa, shift0, shift0, shift0, shift0,
      W['mu_r'], W['mu_k'], W['mu_v'], W['mu_x'], W['decay_w0'], W['decay_w2'], W['aaa_a0'], W['aaa_a2'],
      W['gate_g2'], W['k_k'], W['k_a'], W['r_k'], W['gn_g'], W['gn_b'], _head_ones(hw))
    return out, s_new


def _rwkv_step_kernel(p_ref, s0_ref, st_hbm, mu_ref, w0_ref, w2_ref, a0_ref, a2_ref, g2_ref,
                      kk_ref, ka_ref, rk_ref, gng_ref, gnb_ref, ones_ref,
                      out_ref, st_out_ref, vec_scr, y_scr, bonus_scr, g_scr, st_buf, st_sem):
    i = pl.program_id(0)
    B, width = out_ref.shape
    hg = st_out_ref.shape[0]
    n_steps = st_hbm.shape[0] // hg
    ahead = STATE_RING - 1
    H = RWKV_HEAD

    def fetch(step):
        slot = step % STATE_RING
        return pltpu.make_async_copy(st_hbm.at[pl.ds(step * hg, hg)], st_buf.at[slot], st_sem.at[slot])

    @pl.when(i == 0)
    def _prefill():
        for s in range(min(ahead, n_steps)):
            fetch(s).start()

    @pl.when(i + ahead < n_steps)
    def _fetch_ahead():
        fetch(i + ahead).start()

    @pl.when(i == 0)
    def _prologue():
        ones = ones_ref[...]
        p = p_ref[...]
        xs = p + (s0_ref[...] - p) * mu_ref[...]
        r, k, v, xx = xs[:, :width], xs[:, width:2 * width], xs[:, 2 * width:3 * width], xs[:, 3 * width:]
        lw, a, g, kk, kf, bonus = _rwkv_pre(r, k, v, xx, w0_ref[...], w2_ref[...], a0_ref[...], a2_ref[...],
                                            g2_ref[...], kk_ref[...], ka_ref[...], rk_ref[...], ones)
        for j, vec in enumerate((r, kf, v, kk, kk * a, jnp.exp(lw))):
            vec_scr[j] = vec.T
        bonus_scr[...] = bonus
        g_scr[...] = g

    fetch(i).wait()
    slot = i % STATE_RING
    for hl in range(hg):
        base = pl.multiple_of((i * hg + hl) * H, H)
        r_h, kf_h, kk_h, kka_h, w_h = (vec_scr[j, pl.ds(base, H), :] for j in (0, 1, 3, 4, 5))

        def body(vi, carry):
            S = st_buf[slot, hl, vi]
            sa = jnp.sum(S * kk_h, axis=0, keepdims=True)
            v_row = vec_scr[2, pl.ds(base + vi, 1), :]
            S2 = S * w_h - sa * kka_h + v_row * kf_h
            st_out_ref[hl, vi] = S2
            y_scr[pl.ds(base + vi, 1), :] = jnp.sum(S2 * r_h, axis=0, keepdims=True)
            return carry

        lax.fori_loop(0, H, body, 0, unroll=8)

    @pl.when(i == pl.num_programs(0) - 1)
    def _epilogue():
        out_ref[...] = _rwkv_post(y_scr[...].T, bonus_scr[...], g_scr[...], gng_ref[...], gnb_ref[...],
                                  ones_ref[...])


def _rwkv_step(pa, shift0, state_t, W, *, hg):
    B, proj = pa.shape
    width = W['decay_w0'].shape[1]
    heads = width // RWKV_HEAD
    assert heads % hg == 0 and state_t.shape == (heads, RWKV_HEAD, RWKV_HEAD, B)
    st_spec = pl.BlockSpec((hg, RWKV_HEAD, RWKV_HEAD, B), lambda i: (i, 0, 0, 0))
    names = ('decay_w0', 'decay_w2', 'aaa_a0', 'aaa_a2', 'gate_g2', 'k_k', 'k_a', 'r_k', 'gn_g', 'gn_b')
    out, st_new = pl.pallas_call(
        _rwkv_step_kernel,
        grid=(heads // hg,),
        in_specs=[_const_spec((B, proj)), _const_spec((B, proj)), pl.BlockSpec(memory_space=pl.ANY),
                  _const_spec(W['shift_mu'].shape)]
                 + [_const_spec(W[n].shape) for n in names] + [_const_spec((width, width))],
        out_specs=[_const_spec((B, width)), st_spec],
        out_shape=[jax.ShapeDtypeStruct((B, width), F32), jax.ShapeDtypeStruct(state_t.shape, F32)],
        scratch_shapes=[pltpu.VMEM((6, width, B), F32), pltpu.VMEM((width, B), F32),
                        pltpu.VMEM((B, width), F32), pltpu.VMEM((B, width), F32),
                        pltpu.VMEM((STATE_RING, hg, RWKV_HEAD, RWKV_HEAD, B), F32),
                        pltpu.SemaphoreType.DMA((STATE_RING,))],
        compiler_params=_cparams(("arbitrary",)),
        name="rwkv_step",
    )(pa, shift0, state_t, W['shift_mu'], *[W[n] for n in names], _head_ones(width))
    return out, st_new


def _lru_coeffs(xc, wr_ref, br_ref, wi_ref, bi_ref, lam_ref):
    xcb = xc.astype(BF16)
    n_grp = xc.shape[1] // LRU_GROUP
    grp = lambda w_ref: jnp.concatenate(
        [jnp.dot(xcb[:, i * LRU_GROUP:(i + 1) * LRU_GROUP], w_ref[i], preferred_element_type=F32)
         for i in range(n_grp)], axis=1)
    gr = jax.nn.sigmoid(grp(wr_ref) + br_ref[...])
    gi = jax.nn.sigmoid(grp(wi_ref) + bi_ref[...])
    a_t = jnp.exp(-LRU_C * gr * _softplus(-lam_ref[...]))
    return a_t, jnp.sqrt(1.0 - a_t * a_t) * gi * xc


def _lru_finish(hs, pg, g0, g1, rw, h, wmix_ref, lng_ref, lnb_ref, alpha):
    lru_out = hs * jax.nn.gelu(pg)
    merged = jax.nn.sigmoid(g0) * rw + jax.nn.sigmoid(g1) * lru_out
    mix = jnp.dot(merged.astype(BF16), wmix_ref[...], preferred_element_type=F32)
    return _layer_norm(alpha * h + mix, lng_ref[...], lnb_ref[...])


def _lru_kernel(hn_ref, rw_ref, h_ref, buf_ref, h0_ref, win_ref,
                cw_ref, cb_ref, wr_ref, br_ref, wi_ref, bi_ref, lam_ref, wmix_ref, lng_ref, lnb_ref,
                out_ref, hlast_ref, tail_ref, proj_scr, tail_scr, hc_scr, *, alpha):
    ti = pl.program_id(1)
    flat = pl.program_id(0) * pl.num_programs(1) + ti
    tt, width = rw_ref.shape
    S8 = V7X_SUBLANES
    project = lambda ref: jnp.dot(ref[...].astype(BF16), win_ref[...], preferred_element_type=F32)
    col = lambda j: slice(j * width, (j + 1) * width)

    @pl.when(ti == 0)
    def _init():
        tail_scr[...] = buf_ref[...]
        hc_scr[...] = h0_ref[...]

    @pl.when(flat == 0)
    def _first_projection():
        proj_scr[0] = project(h_ref)

    def step(cur_scr, nxt_scr):
        hn = hn_ref[...].astype(BF16)
        piece_w = win_ref.shape[1] // LRU_PROJ_PIECES
        pieces = iter(range(LRU_PROJ_PIECES))

        def emit(count=1):
            for _ in range(count):
                k = next(pieces, None)
                if k is not None:
                    cols = slice(k * piece_w, (k + 1) * piece_w)
                    nxt_scr[:, cols] = jnp.dot(hn, win_ref[:, cols], preferred_element_type=F32)

        x = cur_scr[:, col(0)]
        tail = tail_scr[...]
        r8 = lax.broadcasted_iota(jnp.int32, (S8, width), 0)

        def delayed(d):
            head = jnp.where(r8 < d, pltpu.roll(tail, d, 0), pltpu.roll(x[:S8], d, 0))
            return jnp.concatenate([head, cur_scr[S8 - d:tt - d, col(0)]], axis=0)

        cw = cw_ref[...]
        conv = cw[0:1] * delayed(CONV_WIDTH - 1)
        for j in range(1, CONV_WIDTH - 1):
            conv = conv + cw[j:j + 1] * delayed(CONV_WIDTH - 1 - j)
        xc = cb_ref[...] + (conv + cw[CONV_WIDTH - 1:CONV_WIDTH] * x)
        tail_scr[...] = x[tt - S8:, :]
        tail_ref[...] = x[tt - S8:, :]
        emit()

        A, Bv = _lru_coeffs(xc, wr_ref, br_ref, wi_ref, bi_ref, lam_ref)
        emit()
        A = A.reshape(tt // S8, S8, width)
        Bv = Bv.reshape(tt // S8, S8, width)
        in_grp = lax.broadcasted_iota(jnp.int32, (S8, width), 0)
        s = 1
        while s < S8:
            keep = in_grp >= s
            Bv = Bv + A * jnp.where(keep, pltpu.roll(Bv, s, 1), 0.0)
            A = A * jnp.where(keep, pltpu.roll(A, s, 1), 1.0)
            s *= 2
            emit()
        carry = hc_scr[...]
        groups = []
        for gi in range(tt // S8):
            hg = Bv[gi] + A[gi] * carry
            groups.append(hg)
            carry = hg[S8 - 1:S8, :]
        hs = jnp.concatenate(groups, axis=0)
        hc_scr[...] = carry
        hlast_ref[...] = carry
        emit()
        lru_out = hs * jax.nn.gelu(cur_scr[:, col(1)])
        emit()
        merged = jax.nn.sigmoid(cur_scr[:, col(2)]) * rw_ref[...] + jax.nn.sigmoid(cur_scr[:, col(3)]) * lru_out
        emit()
        mix = jnp.dot(merged.astype(BF16), wmix_ref[...], preferred_element_type=F32)
        out_ref[...] = _layer_norm(alpha * h_ref[...] + mix, lng_ref[...], lnb_ref[...])
        emit(LRU_PROJ_PIECES)

    slot = flat % 2
    step(proj_scr.at[slot], proj_scr.at[1 - slot])


def _lru_step_kernel(x_ref, pg_ref, g0_ref, g1_ref, rw_ref, h_ref, buf_ref, h0_ref,
                     cw_ref, cb_ref, wr_ref, br_ref, wi_ref, bi_ref, lam_ref, wmix_ref, lng_ref, lnb_ref,
                     out_ref, hnew_ref, *, alpha):
    cw = cw_ref[...]
    conv = cw[0:1] * buf_ref[0]
    for j in range(1, CONV_WIDTH - 1):
        conv = conv + cw[j:j + 1] * buf_ref[j]
    xc = cb_ref[...] + (conv + cw[CONV_WIDTH - 1:CONV_WIDTH] * x_ref[...])
    A, Bv = _lru_coeffs(xc, wr_ref, br_ref, wi_ref, bi_ref, lam_ref)
    hs = Bv + A * h0_ref[...]
    hnew_ref[...] = hs
    out_ref[...] = _lru_finish(hs, pg_ref[...], g0_ref[...], g1_ref[...], rw_ref[...], h_ref[...],
                               wmix_ref, lng_ref, lnb_ref, alpha)


_LRU_WEIGHTS = ('conv_w', 'conv_b', 'lru_wr', 'lru_br', 'lru_wi', 'lru_bi', 'lru_lambda', 'w_mix_out')


def _lru_step_mix_ln(pb, rw, h, buf, h0, W, ln_g, ln_b, *, alpha, tm):
    B, width = rw.shape
    assert B % tm == 0
    blk = lambda j: pl.BlockSpec((tm, width), lambda i: (i, j))
    out, h_new = pl.pallas_call(
        functools.partial(_lru_step_kernel, alpha=alpha),
        grid=(B // tm,),
        in_specs=[blk(0), blk(1), blk(2), blk(3), blk(0), blk(0),
                  pl.BlockSpec((CONV_WIDTH - 1, tm, width), lambda i: (0, i, 0)), blk(0)]
                 + [_const_spec(W[n].shape) for n in _LRU_WEIGHTS] + [_const_spec(ln_g.shape), _const_spec(ln_b.shape)],
        out_specs=[blk(0), blk(0)],
        out_shape=[jax.ShapeDtypeStruct((B, width), F32), jax.ShapeDtypeStruct((B, width), F32)],
        compiler_params=_cparams(("parallel",)),
        name="lru_step_mix_ln",
    )(pb, pb, pb, pb, rw, h, buf, h0, *[W[n] for n in _LRU_WEIGHTS], ln_g, ln_b)
    return out, h_new


def _lru_mix_ln(h, rw, buf8, h0, W, ln_g, ln_b, *, alpha, tt):
    B, T, width = rw.shape
    assert T % tt == 0 and tt % V7X_SUBLANES == 0 and h.shape[2] == width
    nt = T // tt
    tile = pl.BlockSpec((None, tt, width), lambda b, t: (b, t, 0))

    def next_tile(b, t):
        f = jnp.minimum(b * nt + t + 1, B * nt - 1)
        return f // nt, f % nt, 0

    nxt = pl.BlockSpec((None, tt, width), next_tile)
    per_b = lambda rows: pl.BlockSpec((None, rows, width), lambda b, t: (b, 0, 0))
    win = W['w_in_b']
    out, h_last, tail = pl.pallas_call(
        functools.partial(_lru_kernel, alpha=alpha),
        grid=(B, nt),
        in_specs=[nxt, tile, tile, per_b(V7X_SUBLANES), per_b(1), _const_spec(win.shape)]
                 + [_const_spec(W[n].shape) for n in _LRU_WEIGHTS] + [_const_spec(ln_g.shape), _const_spec(ln_b.shape)],
        out_specs=[tile, per_b(1), per_b(V7X_SUBLANES)],
        out_shape=[jax.ShapeDtypeStruct((B, T, width), F32), jax.ShapeDtypeStruct((B, 1, width), F32),
                   jax.ShapeDtypeStruct((B, V7X_SUBLANES, width), F32)],
        scratch_shapes=[pltpu.VMEM((2, tt, win.shape[1]), F32), pltpu.VMEM((V7X_SUBLANES, width), F32),
                        pltpu.VMEM((1, width), F32)],
        compiler_params=_cparams(("arbitrary", "arbitrary")),
        name="lru_mix_ln",
    )(h, rw, h, buf8, h0, win, *[W[n] for n in _LRU_WEIGHTS], ln_g, ln_b)
    return out, h_last, tail


def _xattn_kernel(h_ref, kv_ref, wq_ref, wo_ref, g_ref, b_ref, o_ref, *, alpha, heads):
    tm, d = h_ref.shape
    hd = d // heads
    row_groups = max(1, tm // V7X_MXU_DIM)
    tg = tm // row_groups
    for gi in range(row_groups):
        rows = slice(gi * tg, (gi + 1) * tg)
        h = h_ref[rows, :]
        q = jnp.dot(h.astype(BF16), wq_ref[...], preferred_element_type=F32)
        per_head = lambda x, off: jnp.stack([x[:, off + j * hd:off + (j + 1) * hd] for j in range(heads)])
        s = _bdot(per_head(q, 0), per_head(kv_ref, 0), _NT) * (hd ** -0.5)
        e = jnp.exp(s - jnp.max(s, axis=-1, keepdims=True))
        p = e * (1.0 / jnp.sum(e, axis=-1, keepdims=True))
        o = _bdot(p, per_head(kv_ref, d), _NN)
        out = jnp.dot(jnp.concatenate([o[j] for j in range(heads)], axis=1).astype(BF16), wo_ref[...],
                      preferred_element_type=F32)
        o_ref[rows, :] = _layer_norm(alpha * h + out, g_ref[...], b_ref[...])


def _xattn_rows_kernel(h_ref, mk_ref, mv_ref, wq_ref, wo_ref, g_ref, b_ref, o_ref, *, alpha):
    h = h_ref[...]
    nb, m, heads, hd = mk_ref.shape
    q = jnp.dot(h.astype(BF16), wq_ref[...], preferred_element_type=F32)
    col = lax.broadcasted_iota(jnp.int32, (heads, m * heads), 1)
    own = (col % heads) == lax.broadcasted_iota(jnp.int32, (heads, m * heads), 0)
    k2 = jnp.stack([mk_ref[i].reshape(m * heads, hd) for i in range(nb)])
    v2 = jnp.stack([mv_ref[i].reshape(m * heads, hd) for i in range(nb)])
    q4 = jnp.stack([jnp.concatenate([q[i:i + 1, j * hd:(j + 1) * hd] for j in range(heads)], axis=0)
                    for i in range(nb)])
    s = jnp.where(own, _bdot(q4, k2, _NT) * (hd ** -0.5), -jnp.inf)
    e = jnp.exp(s - jnp.max(s, axis=-1, keepdims=True))
    p = e / jnp.sum(e, axis=-1, keepdims=True)
    o4 = _bdot(p, v2, _NN)
    rows = [jnp.concatenate([o4[i, j:j + 1] for j in range(heads)], axis=1) for i in range(nb)]
    out = jnp.dot(jnp.concatenate(rows, axis=0).astype(BF16), wo_ref[...], preferred_element_type=F32)
    o_ref[...] = _layer_norm(alpha * h + out, g_ref[...], b_ref[...])


def _xattn_ln(h, kv, wq, wo, g, b, *, alpha, heads, rows_per_batch, tile_rows):
    n, d = h.shape
    assert n % tile_rows == 0 and rows_per_batch % tile_rows == 0
    per = rows_per_batch // tile_rows
    m = kv.shape[0] // (n // rows_per_batch)
    return pl.pallas_call(
        functools.partial(_xattn_kernel, alpha=alpha, heads=heads),
        grid=(n // tile_rows,),
        in_specs=[pl.BlockSpec((tile_rows, d), lambda i: (i, 0)),
                  pl.BlockSpec((m, 2 * d), lambda i: (i // per, 0)),
                  _const_spec(wq.shape), _const_spec(wo.shape), _const_spec(g.shape), _const_spec(b.shape)],
        out_specs=pl.BlockSpec((tile_rows, d), lambda i: (i, 0)),
        out_shape=jax.ShapeDtypeStruct((n, d), F32),
        compiler_params=_cparams(("parallel",)),
        name="xattn_ln",
    )(h, kv, wq, wo, g, b)


def _xattn_rows_ln(h, mk, mv, wq, wo, g, b, *, alpha, nb):
    n, d = h.shape
    _, m, heads, hd = mk.shape
    assert n % nb == 0
    kv_spec = pl.BlockSpec((nb, m, heads, hd), lambda i: (i, 0, 0, 0))
    return pl.pallas_call(
        functools.partial(_xattn_rows_kernel, alpha=alpha),
        grid=(n // nb,),
        in_specs=[pl.BlockSpec((nb, d), lambda i: (i, 0)), kv_spec, kv_spec,
                  _const_spec(wq.shape), _const_spec(wo.shape), _const_spec(g.shape), _const_spec(b.shape)],
        out_specs=pl.BlockSpec((nb, d), lambda i: (i, 0)),
        out_shape=jax.ShapeDtypeStruct((n, d), F32),
        compiler_params=_cparams(("parallel",)),
        name="xattn_rows_ln",
    )(h, mk, mv, wq, wo, g, b)


def _kv_proj_kernel(x_ref, w_ref, kv_ref, k_ref, v_ref):
    heads, hd = k_ref.shape[1:]
    d = heads * hd
    kv = jnp.dot(x_ref[...].astype(BF16), w_ref[...], preferred_element_type=F32)
    kv_ref[...] = kv
    for j in range(heads):
        k_ref[:, j, :] = kv[:, j * hd:(j + 1) * hd]
        v_ref[:, j, :] = kv[:, d + j * hd:d + (j + 1) * hd]


def _kv_proj(x, wkv, *, heads, tm):
    n, d = x.shape
    assert n % tm == 0
    hd = d // heads
    out = jax.ShapeDtypeStruct((n, heads, hd), F32)
    return pl.pallas_call(
        _kv_proj_kernel,
        grid=(n // tm,),
        in_specs=[pl.BlockSpec((tm, d), lambda i: (i, 0)), _const_spec(wkv.shape)],
        out_specs=[pl.BlockSpec((tm, 2 * d), lambda i: (i, 0))] + [pl.BlockSpec((tm, heads, hd), lambda i: (i, 0, 0))] * 2,
        out_shape=[jax.ShapeDtypeStruct((n, 2 * d), F32), out, out],
        compiler_params=_cparams(("parallel",)),
        name="kv_proj",
    )(x, wkv)


def _row(v):
    return v.reshape(1, -1)


def _block_diag_groups(w):
    n, c, _ = w.shape
    per = LRU_GROUP // c
    rows = jnp.concatenate([w.reshape(n // per, LRU_GROUP, c)] * per, axis=2)
    ri = lax.broadcasted_iota(jnp.int32, (LRU_GROUP, LRU_GROUP), 0) // c
    ci = lax.broadcasted_iota(jnp.int32, (LRU_GROUP, LRU_GROUP), 1) // c
    return jnp.where(ri == ci, rows, 0.0).astype(BF16)


def _prep_layer(l, ln_g, ln_b, ffn1_wi, ffn1_wo, ffn2_wi, ffn2_wo, w_in, shift_mu, decay_w0, decay_w2,
                aaa_a0, aaa_a2, gate_g2, k_k, k_a, r_k, gn_g, gn_b, conv_w, conv_b, lru_wr, lru_br,
                lru_wi, lru_bi, lru_lambda, w_mix_out, xa_wq, xa_wk, xa_wv, xa_wo):
    width = decay_w0.shape[1]
    rp = shift_mu.shape[1]
    d_ff = ffn1_wo.shape[1]
    bf = lambda w: w.astype(BF16)
    mu = shift_mu[l]
    return dict(
        ln_g=[_row(ln_g[l, i]) for i in range(4)], ln_b=[_row(ln_b[l, i]) for i in range(4)],
        ffn1=(bf(ffn1_wi[l][:, :d_ff]), bf(ffn1_wi[l][:, d_ff:]), bf(ffn1_wo[l])),
        ffn2=(bf(ffn2_wi[l][:, :d_ff]), bf(ffn2_wi[l][:, d_ff:]), bf(ffn2_wo[l])),
        w_in_a=bf(w_in[l][:, :rp]), w_in_b=bf(w_in[l][:, rp:]),
        shift_mu=_row(mu), mu_r=_row(mu[:width]), mu_k=_row(mu[width:2 * width]),
        mu_v=_row(mu[2 * width:3 * width]), mu_x=_row(mu[3 * width:]),
        decay_w0=_row(decay_w0[l]), decay_w2=bf(decay_w2[l]), aaa_a0=_row(aaa_a0[l]), aaa_a2=bf(aaa_a2[l]),
        gate_g2=bf(gate_g2[l]), k_k=_row(k_k[l]), k_a=_row(k_a[l]), r_k=_row(r_k[l]),
        gn_g=_row(gn_g[l]), gn_b=_row(gn_b[l]),
        conv_w=conv_w[l], conv_b=_row(conv_b[l]),
        lru_wr=_block_diag_groups(lru_wr[l]), lru_br=_row(lru_br[l]),
        lru_wi=_block_diag_groups(lru_wi[l]), lru_bi=_row(lru_bi[l]), lru_lambda=_row(lru_lambda[l]),
        w_mix_out=bf(w_mix_out[l]), xa_wq=bf(xa_wq[l]), xa_wo=bf(xa_wo[l]),
        xa_wkv=bf(jnp.concatenate([xa_wk[l], xa_wv[l]], axis=1)),
    )


def _tile(n, pref):
    return pref if n % pref == 0 else n


def _layer(h, mem, state, shift0, h0, buf0, W, *, alpha, xa_heads):
    B, T, d = h.shape
    n = B * T
    tm = _tile(n, 1024)
    h1 = _ffn_ln(h.reshape(n, d), *W['ffn1'], W['ln_g'][0], W['ln_b'][0], alpha=alpha, tm=tm)
    pa = _matmul(h1, W['w_in_a'], tm=tm, tn=W['w_in_a'].shape[1])
    width = W['decay_w0'].shape[1]
    lru_w = W['conv_b'].shape[1]
    pa3 = pa.reshape(B, T, -1)
    hist = CONV_WIDTH - 1
    if T > 1:
        assert state is None and T >= V7X_SUBLANES
        rw, s_new = _rwkv_chunked(pa3, shift0.reshape(B, 1, -1), W, tt=_tile(T, 1024), hw=_tile(width, 512))
        buf8 = jnp.concatenate([jnp.zeros((B, V7X_SUBLANES - hist, lru_w), F32), buf0], axis=1)
        h2, h_last, tail8 = _lru_mix_ln(h1.reshape(B, T, d), rw, buf8, h0.reshape(B, 1, lru_w), W,
                                        W['ln_g'][1], W['ln_b'][1], alpha=alpha, tt=_tile(T, 256))
        conv_in_tail = tail8[:, V7X_SUBLANES - hist:]
    else:
        pb = _matmul(h1, W['w_in_b'], tm=tm, tn=W['w_in_b'].shape[1])
        rw, s_new = _rwkv_step(pa, shift0, jnp.transpose(state, (1, 2, 3, 0)), W, hg=2)
        s_new = jnp.transpose(s_new, (3, 0, 1, 2))
        h2, h_last = _lru_step_mix_ln(pb, rw, h1, jnp.swapaxes(buf0, 0, 1), h0, W,
                                      W['ln_g'][1], W['ln_b'][1], alpha=alpha, tm=_tile(B, 128))
        conv_in_tail = jnp.concatenate([buf0[:, T:], pb[:, None, :lru_w]], axis=1)
    xa = (W['xa_wq'], W['xa_wo'], W['ln_g'][2], W['ln_b'][2])
    if T > 1:
        h3 = _xattn_ln(h2.reshape(n, d), mem, *xa, alpha=alpha, heads=xa_heads, rows_per_batch=T,
                       tile_rows=_tile(T, 1024))
    else:
        h3 = _xattn_rows_ln(h2, *mem, *xa, alpha=alpha, nb=_tile(B, V7X_SUBLANES))
    h4 = _ffn_ln(h3, *W['ffn2'], W['ln_g'][3], W['ln_b'][3], alpha=alpha, tm=tm)
    return h4.reshape(B, T, d), s_new, pa3[:, -1], h_last.reshape(B, lru_w), conv_in_tail


def kernel(x_prompt, x_sample, mem_prompt, cache_mem_k, cache_mem_v, state_rwkv, state_rwkv_shift, state_lru, state_conv, ln_g, ln_b, ffn1_wi, ffn1_wo, ffn2_wi, ffn2_wo, w_in, shift_mu, decay_w0, decay_w2, aaa_a0, aaa_a2, gate_g2, k_k, k_a, r_k, gn_g, gn_b, conv_w, conv_b, lru_wr, lru_br, lru_wi, lru_bi, lru_lambda, w_mix_out, xa_wq, xa_wk, xa_wv, xa_wo):
    depth = ln_g.shape[0]
    alpha = (2.0 * depth) ** 0.25
    B, _, d = x_prompt.shape
    n_mem, xa_heads, xa_head = cache_mem_k.shape[2:]
    rp = shift_mu.shape[1]
    lru_w = conv_b.shape[1]
    hp, hs = x_prompt, x_sample
    outs = [[] for _ in range(10)]
    for l in range(depth):
        W = _prep_layer(l, ln_g, ln_b, ffn1_wi, ffn1_wo, ffn2_wi, ffn2_wo, w_in, shift_mu, decay_w0, decay_w2,
                        aaa_a0, aaa_a2, gate_g2, k_k, k_a, r_k.reshape(depth, -1), gn_g, gn_b, conv_w, conv_b,
                        lru_wr, lru_br, lru_wi, lru_bi, lru_lambda, w_mix_out, xa_wq, xa_wk, xa_wv, xa_wo)
        kv, mk, mv = _kv_proj(mem_prompt.reshape(B * n_mem, d), W['xa_wkv'], heads=xa_heads,
                              tm=_tile(B * n_mem, 512))
        mk = mk.reshape(B, n_mem, xa_heads, xa_head)
        mv = mv.reshape(B, n_mem, xa_heads, xa_head)
        hp, S1, sh1, h1, b1 = _layer(
            hp, kv, None, jnp.zeros((B, rp), F32), jnp.zeros((B, lru_w), F32),
            jnp.zeros((B, CONV_WIDTH - 1, lru_w), F32), W, alpha=alpha, xa_heads=xa_heads)
        hs, S2, sh2, h2, b2 = _layer(
            hs, (cache_mem_k[l], cache_mem_v[l]),
            state_rwkv[l], state_rwkv_shift[l], state_lru[l], state_conv[l], W, alpha=alpha, xa_heads=xa_heads)
        for lst, val in zip(outs, (mk, mv, S1, sh1, h1, b1, S2, sh2, h2, b2)):
            lst.append(val)
    return (hp, hs) + tuple(jnp.stack(o) for o in outs)
```

```python
import functools

import jax
import jax.numpy as jnp
from jax import lax
from jax.experimental import pallas as pl
from jax.experimental.pallas import tpu as pltpu

F32 = jnp.float32
BF16 = jnp.bfloat16

RWKV_HEAD = 64
DECAY_LORA = 64
AAA_LORA = 64
GATE_LORA = 128
GN_EPS = 64e-5
CONV_WIDTH = 4
LRU_C = 8.0
LN_EPS = 1e-5

V7X_SUBLANES = 8
V7X_MXU_DIM = 256
V7X_SCOPED_VMEM_BYTES = 60000 * 1024

RWKV_CHUNK = 64
HEAD_PAIR = 2 * RWKV_HEAD
SCAN_ROW_GROUPS = 2
LRU_GROUP = V7X_MXU_DIM
STATE_RING = 3
LRU_PROJ_PIECES = 8


def _cparams(semantics):
    return pltpu.CompilerParams(dimension_semantics=semantics, vmem_limit_bytes=V7X_SCOPED_VMEM_BYTES)


def _const_spec(shape):
    zeros = (0,) * len(shape)
    return pl.BlockSpec(shape, lambda *_: zeros)


def _dot(a, b):
    return jnp.dot(a.astype(BF16), b.astype(BF16), preferred_element_type=F32)


def _dot_dims(a, b, dims):
    return lax.dot_general(a.astype(BF16), b.astype(BF16), (dims, ((), ())), preferred_element_type=F32)


_NN = ((1,), (0,))
_NT = ((1,), (1,))
_TN = ((0,), (0,))


def _split2(x):
    hi = x.astype(BF16)
    lo = (x - hi.astype(F32)).astype(BF16)
    return hi, lo


def _dot_exact_lhs(a_bf16, b):
    hi, lo = _split2(b)
    d = lambda y: jnp.dot(a_bf16, y, preferred_element_type=F32)
    return d(hi) + d(lo)


def _dot_exact_rhs(a, b_bf16):
    hi, lo = _split2(a)
    d = lambda x: jnp.dot(x, b_bf16, preferred_element_type=F32)
    return d(hi) + d(lo)


def _layer_norm(x, g, b):
    mu = jnp.mean(x, axis=-1, keepdims=True)
    xc = x - mu
    var = jnp.mean(xc * xc, axis=-1, keepdims=True)
    return xc * lax.rsqrt(var + LN_EPS) * g + b


def _softplus(z):
    return jnp.maximum(z, 0.0) + jnp.log(1.0 + jnp.exp(-jnp.abs(z)))


def _head_ones(width):
    r = lax.broadcasted_iota(jnp.int32, (width, width), 0) // RWKV_HEAD
    c = lax.broadcasted_iota(jnp.int32, (width, width), 1) // RWKV_HEAD
    return (r == c).astype(BF16)


def _mm_kernel(x_ref, w_ref, o_ref):
    o_ref[...] = jnp.dot(x_ref[...].astype(BF16), w_ref[...], preferred_element_type=F32)


def _matmul(x, w, *, tm, tn):
    n, k = x.shape
    m = w.shape[1]
    assert n % tm == 0 and m % tn == 0
    return pl.pallas_call(
        _mm_kernel,
        grid=(m // tn, n // tm),
        in_specs=[pl.BlockSpec((tm, k), lambda j, i: (i, 0)),
                  pl.BlockSpec((k, tn), lambda j, i: (0, j))],
        out_specs=pl.BlockSpec((tm, tn), lambda j, i: (i, j)),
        out_shape=jax.ShapeDtypeStruct((n, m), F32),
        compiler_params=_cparams(("parallel", "parallel")),
        name="matmul",
    )(x, w)


def _ffn_kernel(x_ref, wg_hbm, wu_hbm, wo_hbm, g_ref, b_ref, o_ref, wg_ref, wu_ref, wo_ref, w_sem,
                *, alpha, row_groups):
    tm = x_ref.shape[0]
    tg = tm // row_groups
    copies = [pltpu.make_async_copy(src, dst, w_sem.at[j])
              for j, (src, dst) in enumerate(((wg_hbm, wg_ref), (wu_hbm, wu_ref), (wo_hbm, wo_ref)))]

    def body(first_step):
        for gi in range(row_groups):
            arrive = first_step and gi == 0
            rows = slice(gi * tg, (gi + 1) * tg)
            x = x_ref[rows, :]
            xb = x.astype(BF16)
            if arrive:
                copies[0].wait()
            gate = jnp.dot(xb, wg_ref[...], preferred_element_type=F32)
            if arrive:
                copies[1].wait()
            up = jnp.dot(xb, wu_ref[...], preferred_element_type=F32)
            mid = (gate * jax.nn.sigmoid(gate) * up).astype(BF16)
            if arrive:
                copies[2].wait()
            down = jnp.dot(mid, wo_ref[...], preferred_element_type=F32)
            o_ref[rows, :] = _layer_norm(alpha * x + 0.5 * down, g_ref[...], b_ref[...])

    @pl.when(pl.program_id(0) == 0)
    def _first():
        for c in copies:
            c.start()
        body(True)

    @pl.when(pl.program_id(0) != 0)
    def _rest():
        body(False)


def _ffn_ln(x, wg, wu, wo, g, b, *, alpha, tm):
    n, d = x.shape
    assert n % tm == 0
    row_groups = max(1, tm // V7X_MXU_DIM)
    hbm = pl.BlockSpec(memory_space=pl.ANY)
    return pl.pallas_call(
        functools.partial(_ffn_kernel, alpha=alpha, row_groups=row_groups),
        grid=(n // tm,),
        in_specs=[pl.BlockSpec((tm, d), lambda i: (i, 0)), hbm, hbm, hbm,
                  _const_spec(g.shape), _const_spec(b.shape)],
        out_specs=pl.BlockSpec((tm, d), lambda i: (i, 0)),
        out_shape=jax.ShapeDtypeStruct((n, d), F32),
        scratch_shapes=[pltpu.VMEM(wg.shape, wg.dtype), pltpu.VMEM(wu.shape, wu.dtype), pltpu.VMEM(wo.shape, wo.dtype),
                        pltpu.SemaphoreType.DMA((3,))],
        compiler_params=_cparams(("arbitrary",)),
        name="ffn_ln",
    )(x, wg, wu, wo, g, b)


def _rwkv_pre(r, k, v, xx, w0, w2, a0, a2, g2, k_k, k_a, r_k, ones):
    xw = xx[:, :DECAY_LORA]
    xa = xx[:, DECAY_LORA:DECAY_LORA + AAA_LORA]
    xg = xx[:, DECAY_LORA + AAA_LORA:]
    z = w0 + _dot(jnp.tanh(xw), w2)
    lw = -jnp.exp(-_softplus(-z) - 0.5)
    a = jax.nn.sigmoid(a0 + _dot(xa, a2))
    g = _dot(jax.nn.sigmoid(xg), g2)
    kkr = k * k_k
    ss = _dot(kkr * kkr, ones)
    kk = kkr * lax.rsqrt(jnp.maximum(ss, 1e-24))
    kf = k * (1.0 + (a - 1.0) * k_a)
    bonus = _dot(r * kf * r_k, ones) * v
    return lw, a, g, kk, kf, bonus


def _rwkv_post(y, bonus, g, gn_g, gn_b, ones):
    inv_n = 1.0 / RWKV_HEAD
    ym = _dot_exact_rhs(y, ones) * inv_n
    yc = y - ym
    yv = _dot(yc * yc, ones) * inv_n
    yn = yc * lax.rsqrt(yv + GN_EPS) * gn_g + gn_b
    return (yn + bonus) * g


def _bdot(a, b, dims):
    dn = ((tuple(d + 1 for d in dims[0]), tuple(d + 1 for d in dims[1])), ((0,), (0,)))
    return lax.dot_general(a.astype(BF16), b.astype(BF16), dn, preferred_element_type=F32)


def _scan_operands(r, kf, v, kk, a, lw):
    tt, hw = r.shape
    C = RWKV_CHUNK
    n_pairs = hw // HEAD_PAIR
    ltri = (lax.broadcasted_iota(jnp.int32, (C, C), 0) >= lax.broadcasted_iota(jnp.int32, (C, C), 1)).astype(BF16)
    first = lax.broadcasted_iota(jnp.int32, (C, HEAD_PAIR), 1) < RWKV_HEAD

    def bd(x):
        return jnp.concatenate([jnp.where(first, x, 0.0), jnp.where(first, 0.0, x)], axis=1)

    names = ('a', 'r', 'b', 'k', 'v', 'bh', 'kh')
    ops = {n: [] for n in names}
    wcs = []
    for c in range(tt // C):
        rows = slice(c * C, (c + 1) * C)
        lw_c = lw[rows]
        L = _dot_exact_lhs(ltri, lw_c)
        Lc = L[C - 1:C, :]
        e_nl = jnp.exp(-L)
        e_c = jnp.exp(Lc - L)
        bb = kk[rows] * a[rows]
        bf = lambda x: x.astype(BF16)
        tile = dict(a=bf(-kk[rows] * jnp.exp(L - lw_c)), r=r[rows] * jnp.exp(L), b=bf(bb * e_nl),
                    k=bf(kf[rows] * e_nl), v=bf(v[rows]), bh=bf(bb * e_c), kh=bf(kf[rows] * e_c))
        wc = jnp.exp(Lc)
        for p in range(n_pairs):
            lanes = slice(p * HEAD_PAIR, (p + 1) * HEAD_PAIR)
            for n in names:
                ops[n].append(tile[n][:, lanes])
            wcs.append(wc[:, lanes])
    A, R, B, K, V, Bh, Kh = (jnp.stack(ops[n]) for n in names)
    Vbd = bd(V)
    G = _bdot(jnp.concatenate([A, R.astype(BF16)], axis=1), jnp.concatenate([bd(B), bd(K)], axis=1), _NT)
    tok = lax.broadcasted_iota(jnp.int32, (C, HEAD_PAIR), 0)
    src = lax.broadcasted_iota(jnp.int32, (C, HEAD_PAIR), 1) % RWKV_HEAD
    a_ab = jnp.where(tok > src, G[:, :C, :HEAD_PAIR], 0.0)
    a_ak = jnp.where(tok > src, G[:, :C, HEAD_PAIR:], 0.0).astype(BF16)
    a_rb = jnp.where(tok >= src, G[:, C:, :HEAD_PAIR], 0.0).astype(BF16)
    a_rk = jnp.where(tok >= src, G[:, C:, HEAD_PAIR:], 0.0).astype(BF16)
    P = jnp.where(tok == src, 1.0, 0.0) + a_ab
    N = a_ab.astype(BF16)
    N = _bdot(N, bd(N), _NN).astype(BF16)
    steps = 2
    while 2 * steps < C:
        NP = _bdot(jnp.concatenate([N, P.astype(BF16)], axis=1), bd(N), _NN)
        N = NP[:, :C].astype(BF16)
        P = P + NP[:, C:]
        steps *= 2
    P = (P + _bdot(P, bd(N), _NN)).astype(BF16)
    aV = _bdot(a_ak, Vbd, _NN).astype(BF16)
    XU = _bdot(P, jnp.concatenate([bd(A), bd(aV)], axis=2), _NN).astype(BF16)
    X1 = XU[:, :, :HEAD_PAIR]
    Uloc = XU[:, :, HEAD_PAIR:]
    Q = (R + _bdot(a_rb, bd(X1), _NN)).astype(BF16)
    Yloc = _bdot(jnp.concatenate([a_rb, a_rk], axis=2), jnp.concatenate([bd(Uloc), Vbd], axis=1), _NN)
    ri = lax.broadcasted_iota(jnp.int32, (HEAD_PAIR, HEAD_PAIR), 0) // RWKV_HEAD
    ci = lax.broadcasted_iota(jnp.int32, (HEAD_PAIR, HEAD_PAIR), 1) // RWKV_HEAD
    same_head = ri == ci
    Pm = jnp.where(same_head, _bdot(X1, Bh, _TN), 0.0).astype(BF16)
    Sloc = jnp.where(same_head, _bdot(jnp.concatenate([Uloc, V], axis=1), jnp.concatenate([Bh, Kh], axis=1), _TN),
                     0.0)
    return Q, Yloc, Pm, Sloc, jnp.stack(wcs)


def _rwkv_chunk_kernel(pr_ref, pk_ref, pv_ref, px_ref, sr_ref, sk_ref, sv_ref, sx_ref,
                       mur_ref, muk_ref, muv_ref, mux_ref, w0_ref, w2_ref, a0_ref, a2_ref, g2_ref,
                       kk_ref, ka_ref, rk_ref, gng_ref, gnb_ref, ones_ref,
                       out_ref, s_out_ref,
                       s_scr, cr_scr, ck_scr, cv_scr, cx_scr, y_scr):
    ti = pl.program_id(2)
    tt, hw = pr_ref.shape
    n_pairs = hw // HEAD_PAIR

    @pl.when(ti == 0)
    def _init():
        s_scr[...] = jnp.zeros(s_scr.shape, F32)
        cr_scr[...] = sr_ref[...]
        ck_scr[...] = sk_ref[...]
        cv_scr[...] = sv_ref[...]
        cx_scr[...] = sx_ref[...]

    S8 = V7X_SUBLANES

    def shifted(p_ref, c_scr, mu_ref):
        p = p_ref[...]
        first = lax.broadcasted_iota(jnp.int32, (S8, p.shape[1]), 0) == 0
        head = jnp.where(first, c_scr[...], pltpu.roll(p[:S8], 1, 0))
        prev = jnp.concatenate([head, p_ref[S8 - 1:tt - 1, :]], axis=0)
        c_scr[...] = p_ref[tt - 1:tt, :]
        return p + (prev - p) * mu_ref[...]

    r = shifted(pr_ref, cr_scr, mur_ref)
    k = shifted(pk_ref, ck_scr, muk_ref)
    v = shifted(pv_ref, cv_scr, muv_ref)
    xx = shifted(px_ref, cx_scr, mux_ref)
    ones = ones_ref[...]
    lw, a, g, kk, kf, bonus = _rwkv_pre(r, k, v, xx, w0_ref[...], w2_ref[...], a0_ref[...], a2_ref[...],
                                        g2_ref[...], kk_ref[...], ka_ref[...], rk_ref[...], ones)
    C = RWKV_CHUNK
    S = s_scr[...]
    n_groups = SCAN_ROW_GROUPS if tt % (SCAN_ROW_GROUPS * C) == 0 else 1
    tg = tt // n_groups
    for gi in range(n_groups):
        rows = slice(gi * tg, (gi + 1) * tg)
        Q, Yloc, Pm, Sloc, wc = _scan_operands(r[rows], kf[rows], v[rows], kk[rows], a[rows], lw[rows])
        for c in range(tg // C):
            inst = slice(c * n_pairs, (c + 1) * n_pairs)
            y_c = _bdot(Q[inst], S, _NT) + Yloc[inst]
            row0 = gi * tg + c * C
            for p in range(n_pairs):
                y_scr[row0:row0 + C, p * HEAD_PAIR:(p + 1) * HEAD_PAIR] = y_c[p]
            S = S * wc[inst] + _bdot(S, Pm[inst], _NN) + Sloc[inst]
    s_scr[...] = S
    out_ref[...] = _rwkv_post(y_scr[...], bonus, g, gng_ref[...], gnb_ref[...], ones)

    @pl.when(ti == pl.num_programs(2) - 1)
    def _emit_state():
        for p in range(n_pairs):
            s_out_ref[2 * p] = S[p, :RWKV_HEAD, :RWKV_HEAD]
            s_out_ref[2 * p + 1] = S[p, RWKV_HEAD:, RWKV_HEAD:]


def _rwkv_chunked(pa, shift0, W, *, tt, hw):
    B, T, _ = pa.shape
    width = W['decay_w0'].shape[1]
    heads = width // RWKV_HEAD
    assert T % tt == 0 and tt % RWKV_CHUNK == 0 and width % hw == 0 and hw % HEAD_PAIR == 0
    nb = width // hw
    xw = DECAY_LORA + AAA_LORA + GATE_LORA
    assert (3 * width) % xw == 0
    xblk = 3 * width // xw
    col = lambda off: (lambda b, h, t: (b, t, off + h))
    vec = lambda: pl.BlockSpec((1, hw), lambda b, h, t: (0, h))
    in_specs = [
        pl.BlockSpec((None, tt, hw), col(0)), pl.BlockSpec((None, tt, hw), col(nb)),
        pl.BlockSpec((None, tt, hw), col(2 * nb)), pl.BlockSpec((None, tt, xw), lambda b, h, t: (b, t, xblk)),
        pl.BlockSpec((None, 1, hw), lambda b, h, t: (b, 0, h)), pl.BlockSpec((None, 1, hw), lambda b, h, t: (b, 0, nb + h)),
        pl.BlockSpec((None, 1, hw), lambda b, h, t: (b, 0, 2 * nb + h)), pl.BlockSpec((None, 1, xw), lambda b, h, t: (b, 0, xblk)),
        vec(), vec(), vec(), _const_spec((1, xw)),
        vec(), pl.BlockSpec((DECAY_LORA, hw), lambda b, h, t: (0, h)),
        vec(), pl.BlockSpec((AAA_LORA, hw), lambda b, h, t: (0, h)),
        pl.BlockSpec((GATE_LORA, hw), lambda b, h, t: (0, h)),
        vec(), vec(), vec(), vec(), vec(), _const_spec((hw, hw)),
    ]
    out, s_new = pl.pallas_call(
        _rwkv_chunk_kernel,
        grid=(B, nb, T // tt),
        in_specs=in_specs,
        out_specs=[pl.BlockSpec((None, tt, hw), lambda b, h, t: (b, t, h)),
                   pl.BlockSpec((None, hw // RWKV_HEAD, RWKV_HEAD, RWKV_HEAD), lambda b, h, t: (b, h, 0, 0))],
        out_shape=[jax.ShapeDtypeStruct((B, T, width), F32),
                   jax.ShapeDtypeStruct((B, heads, RWKV_HEAD, RWKV_HEAD), F32)],
        scratch_shapes=[pltpu.VMEM((hw // HEAD_PAIR, HEAD_PAIR, HEAD_PAIR), F32),
                        pltpu.VMEM((1, hw), F32), pltpu.VMEM((1, hw), F32), pltpu.VMEM((1, hw), F32),
                        pltpu.VMEM((1, xw), F32), pltpu.VMEM((tt, hw), F32)],
        compiler_params=_cparams(("parallel", "parallel", "arbitrary")),
        name="rwkv_chunked",
    )(pa, pa, pa, pa, shift0, shift0, shift0, shift0,
      W['mu_r'], W['mu_k'], W['mu_v'], W['mu_x'], W['decay_w0'], W['decay_w2'], W['aaa_a0'], W['aaa_a2'],
      W['gate_g2'], W['k_k'], W['k_a'], W['r_k'], W['gn_g'], W['gn_b'], _head_ones(hw))
    return out, s_new


def _rwkv_step_kernel(p_ref, s0_ref, st_hbm, mu_ref, w0_ref, w2_ref, a0_ref, a2_ref, g2_ref,
                      kk_ref, ka_ref, rk_ref, gng_ref, gnb_ref, ones_ref,
                      out_ref, st_out_ref, vec_scr, y_scr, bonus_scr, g_scr, st_buf, st_sem):
    i = pl.program_id(0)
    B, width = out_ref.shape
    hg = st_out_ref.shape[0]
    n_steps = st_hbm.shape[0] // hg
    ahead = STATE_RING - 1
    H = RWKV_HEAD

    def fetch(step):
        slot = step % STATE_RING
        return pltpu.make_async_copy(st_hbm.at[pl.ds(step * hg, hg)], st_buf.at[slot], st_sem.at[slot])

    @pl.when(i == 0)
    def _prefill():
        for s in range(min(ahead, n_steps)):
            fetch(s).start()

    @pl.when(i + ahead < n_steps)
    def _fetch_ahead():
        fetch(i + ahead).start()

    @pl.when(i == 0)
    def _prologue():
        ones = ones_ref[...]
        p = p_ref[...]
        xs = p + (s0_ref[...] - p) * mu_ref[...]
        r, k, v, xx = xs[:, :width], xs[:, width:2 * width], xs[:, 2 * width:3 * width], xs[:, 3 * width:]
        lw, a, g, kk, kf, bonus = _rwkv_pre(r, k, v, xx, w0_ref[...], w2_ref[...], a0_ref[...], a2_ref[...],
                                            g2_ref[...], kk_ref[...], ka_ref[...], rk_ref[...], ones)
        for j, vec in enumerate((r, kf, v, kk, kk * a, jnp.exp(lw))):
            vec_scr[j] = vec.T
        bonus_scr[...] = bonus
        g_scr[...] = g

    fetch(i).wait()
    slot = i % STATE_RING
    for hl in range(hg):
        base = pl.multiple_of((i * hg + hl) * H, H)
        r_h, kf_h, kk_h, kka_h, w_h = (vec_scr[j, pl.ds(base, H), :] for j in (0, 1, 3, 4, 5))

        def body(vi, carry):
            S = st_buf[slot, hl, vi]
            sa = jnp.sum(S * kk_h, axis=0, keepdims=True)
            v_row = vec_scr[2, pl.ds(base + vi, 1), :]
            S2 = S * w_h - sa * kka_h + v_row * kf_h
            st_out_ref[hl, vi] = S2
            y_scr[pl.ds(base + vi, 1), :] = jnp.sum(S2 * r_h, axis=0, keepdims=True)
            return carry

        lax.fori_loop(0, H, body, 0, unroll=8)

    @pl.when(i == pl.num_programs(0) - 1)
    def _epilogue():
        out_ref[...] = _rwkv_post(y_scr[...].T, bonus_scr[...], g_scr[...], gng_ref[...], gnb_ref[...],
                                  ones_ref[...])


def _rwkv_step(pa, shift0, state_t, W, *, hg):
    B, proj = pa.shape
    width = W['decay_w0'].shape[1]
    heads = width // RWKV_HEAD
    assert heads % hg == 0 and state_t.shape == (heads, RWKV_HEAD, RWKV_HEAD, B)
    st_spec = pl.BlockSpec((hg, RWKV_HEAD, RWKV_HEAD, B), lambda i: (i, 0, 0, 0))
    names = ('decay_w0', 'decay_w2', 'aaa_a0', 'aaa_a2', 'gate_g2', 'k_k', 'k_a', 'r_k', 'gn_g', 'gn_b')
    out, st_new = pl.pallas_call(
        _rwkv_step_kernel,
        grid=(heads // hg,),
        in_specs=[_const_spec((B, proj)), _const_spec((B, proj)), pl.BlockSpec(memory_space=pl.ANY),
                  _const_spec(W['shift_mu'].shape)]
                 + [_const_spec(W[n].shape) for n in names] + [_const_spec((width, width))],
        out_specs=[_const_spec((B, width)), st_spec],
        out_shape=[jax.ShapeDtypeStruct((B, width), F32), jax.ShapeDtypeStruct(state_t.shape, F32)],
        scratch_shapes=[pltpu.VMEM((6, width, B), F32), pltpu.VMEM((width, B), F32),
                        pltpu.VMEM((B, width), F32), pltpu.VMEM((B, width), F32),
                        pltpu.VMEM((STATE_RING, hg, RWKV_HEAD, RWKV_HEAD, B), F32),
                        pltpu.SemaphoreType.DMA((STATE_RING,))],
        compiler_params=_cparams(("arbitrary",)),
        name="rwkv_step",
    )(pa, shift0, state_t, W['shift_mu'], *[W[n] for n in names], _head_ones(width))
    return out, st_new


def _lru_coeffs(xc, wr_ref, br_ref, wi_ref, bi_ref, lam_ref):
    xcb = xc.astype(BF16)
    n_grp = xc.shape[1] // LRU_GROUP
    grp = lambda w_ref: jnp.concatenate(
        [jnp.dot(xcb[:, i * LRU_GROUP:(i + 1) * LRU_GROUP], w_ref[i], preferred_element_type=F32)
         for i in range(n_grp)], axis=1)
    gr = jax.nn.sigmoid(grp(wr_ref) + br_ref[...])
    gi = jax.nn.sigmoid(grp(wi_ref) + bi_ref[...])
    a_t = jnp.exp(-LRU_C * gr * _softplus(-lam_ref[...]))
    return a_t, jnp.sqrt(1.0 - a_t * a_t) * gi * xc


def _lru_finish(hs, pg, g0, g1, rw, h, wmix_ref, lng_ref, lnb_ref, alpha):
    lru_out = hs * jax.nn.gelu(pg)
    merged = jax.nn.sigmoid(g0) * rw + jax.nn.sigmoid(g1) * lru_out
    mix = jnp.dot(merged.astype(BF16), wmix_ref[...], preferred_element_type=F32)
    return _layer_norm(alpha * h + mix, lng_ref[...], lnb_ref[...])


def _lru_kernel(hn_ref, rw_ref, h_ref, buf_ref, h0_ref, win_ref,
                cw_ref, cb_ref, wr_ref, br_ref, wi_ref, bi_ref, lam_ref, wmix_ref, lng_ref, lnb_ref,
                out_ref, hlast_ref, tail_ref, proj_scr, tail_scr, hc_scr, *, alpha):
    ti = pl.program_id(1)
    flat = pl.program_id(0) * pl.num_programs(1) + ti
    tt, width = rw_ref.shape
    S8 = V7X_SUBLANES
    project = lambda ref: jnp.dot(ref[...].astype(BF16), win_ref[...], preferred_element_type=F32)
    col = lambda j: slice(j * width, (j + 1) * width)

    @pl.when(ti == 0)
    def _init():
        tail_scr[...] = buf_ref[...]
        hc_scr[...] = h0_ref[...]

    @pl.when(flat == 0)
    def _first_projection():
        proj_scr[0] = project(h_ref)

    def step(cur_scr, nxt_scr):
        hn = hn_ref[...].astype(BF16)
        piece_w = win_ref.shape[1] // LRU_PROJ_PIECES
        pieces = iter(range(LRU_PROJ_PIECES))

        def emit(count=1):
            for _ in range(count):
                k = next(pieces, None)
                if k is not None:
                    cols = slice(k * piece_w, (k + 1) * piece_w)
                    nxt_scr[:, cols] = jnp.dot(hn, win_ref[:, cols], preferred_element_type=F32)

        x = cur_scr[:, col(0)]
        tail = tail_scr[...]
        r8 = lax.broadcasted_iota(jnp.int32, (S8, width), 0)

        def delayed(d):
            head = jnp.where(r8 < d, pltpu.roll(tail, d, 0), pltpu.roll(x[:S8], d, 0))
            return jnp.concatenate([head, cur_scr[S8 - d:tt - d, col(0)]], axis=0)

        cw = cw_ref[...]
        conv = cw[0:1] * delayed(CONV_WIDTH - 1)
        for j in range(1, CONV_WIDTH - 1):
            conv = conv + cw[j:j + 1] * delayed(CONV_WIDTH - 1 - j)
        xc = cb_ref[...] + (conv + cw[CONV_WIDTH - 1:CONV_WIDTH] * x)
        tail_scr[...] = x[tt - S8:, :]
        tail_ref[...] = x[tt - S8:, :]
        emit()

        A, Bv = _lru_coeffs(xc, wr_ref, br_ref, wi_ref, bi_ref, lam_ref)
        emit()
        A = A.reshape(tt // S8, S8, width)
        Bv = Bv.reshape(tt // S8, S8, width)
        in_grp = lax.broadcasted_iota(jnp.int32, (S8, width), 0)
        s = 1
        while s < S8:
            keep = in_grp >= s
            Bv = Bv + A * jnp.where(keep, pltpu.roll(Bv, s, 1), 0.0)
            A = A * jnp.where(keep, pltpu.roll(A, s, 1), 1.0)
            s *= 2
            emit()
        carry = hc_scr[...]
        groups = []
        for gi in range(tt // S8):
            hg = Bv[gi] + A[gi] * carry
            groups.append(hg)
            carry = hg[S8 - 1:S8, :]
        hs = jnp.concatenate(groups, axis=0)
        hc_scr[...] = carry
        hlast_ref[...] = carry
        emit()
        lru_out = hs * jax.nn.gelu(cur_scr[:, col(1)])
        emit()
        merged = jax.nn.sigmoid(cur_scr[:, col(2)]) * rw_ref[...] + jax.nn.sigmoid(cur_scr[:, col(3)]) * lru_out
        emit()
        mix = jnp.dot(merged.astype(BF16), wmix_ref[...], preferred_element_type=F32)
        out_ref[...] = _layer_norm(alpha * h_ref[...] + mix, lng_ref[...], lnb_ref[...])
        emit(LRU_PROJ_PIECES)

    slot = flat % 2
    step(proj_scr.at[slot], proj_scr.at[1 - slot])


def _lru_step_kernel(x_ref, pg_ref, g0_ref, g1_ref, rw_ref, h_ref, buf_ref, h0_ref,
                     cw_ref, cb_ref, wr_ref, br_ref, wi_ref, bi_ref, lam_ref, wmix_ref, lng_ref, lnb_ref,
                     out_ref, hnew_ref, *, alpha):
    cw = cw_ref[...]
    conv = cw[0:1] * buf_ref[0]
    for j in range(1, CONV_WIDTH - 1):
        conv = conv + cw[j:j + 1] * buf_ref[j]
    xc = cb_ref[...] + (conv + cw[CONV_WIDTH - 1:CONV_WIDTH] * x_ref[...])
    A, Bv = _lru_coeffs(xc, wr_ref, br_ref, wi_ref, bi_ref, lam_ref)
    hs = Bv + A * h0_ref[...]
    hnew_ref[...] = hs
    out_ref[...] = _lru_finish(hs, pg_ref[...], g0_ref[...], g1_ref[...], rw_ref[...], h_ref[...],
                               wmix_ref, lng_ref, lnb_ref, alpha)


_LRU_WEIGHTS = ('conv_w', 'conv_b', 'lru_wr', 'lru_br', 'lru_wi', 'lru_bi', 'lru_lambda', 'w_mix_out')


def _lru_step_mix_ln(pb, rw, h, buf, h0, W, ln_g, ln_b, *, alpha, tm):
    B, width = rw.shape
    assert B % tm == 0
    blk = lambda j: pl.BlockSpec((tm, width), lambda i: (i, j))
    out, h_new = pl.pallas_call(
        functools.partial(_lru_step_kernel, alpha=alpha),
        grid=(B // tm,),
        in_specs=[blk(0), blk(1), blk(2), blk(3), blk(0), blk(0),
                  pl.BlockSpec((CONV_WIDTH - 1, tm, width), lambda i: (0, i, 0)), blk(0)]
                 + [_const_spec(W[n].shape) for n in _LRU_WEIGHTS] + [_const_spec(ln_g.shape), _const_spec(ln_b.shape)],
        out_specs=[blk(0), blk(0)],
        out_shape=[jax.ShapeDtypeStruct((B, width), F32), jax.ShapeDtypeStruct((B, width), F32)],
        compiler_params=_cparams(("parallel",)),
        name="lru_step_mix_ln",
    )(pb, pb, pb, pb, rw, h, buf, h0, *[W[n] for n in _LRU_WEIGHTS], ln_g, ln_b)
    return out, h_new


def _lru_mix_ln(h, rw, buf8, h0, W, ln_g, ln_b, *, alpha, tt):
    B, T, width = rw.shape
    assert T % tt == 0 and tt % V7X_SUBLANES == 0 and h.shape[2] == width
    nt = T // tt
    tile = pl.BlockSpec((None, tt, width), lambda b, t: (b, t, 0))

    def next_tile(b, t):
        f = jnp.minimum(b * nt + t + 1, B * nt - 1)
        return f // nt, f % nt, 0

    nxt = pl.BlockSpec((None, tt, width), next_tile)
    per_b = lambda rows: pl.BlockSpec((None, rows, width), lambda b, t: (b, 0, 0))
    win = W['w_in_b']
    out, h_last, tail = pl.pallas_call(
        functools.partial(_lru_kernel, alpha=alpha),
        grid=(B, nt),
        in_specs=[nxt, tile, tile, per_b(V7X_SUBLANES), per_b(1), _const_spec(win.shape)]
                 + [_const_spec(W[n].shape) for n in _LRU_WEIGHTS] + [_const_spec(ln_g.shape), _const_spec(ln_b.shape)],
        out_specs=[tile, per_b(1), per_b(V7X_SUBLANES)],
        out_shape=[jax.ShapeDtypeStruct((B, T, width), F32), jax.ShapeDtypeStruct((B, 1, width), F32),
                   jax.ShapeDtypeStruct((B, V7X_SUBLANES, width), F32)],
        scratch_shapes=[pltpu.VMEM((2, tt, win.shape[1]), F32), pltpu.VMEM((V7X_SUBLANES, width), F32),
                        pltpu.VMEM((1, width), F32)],
        compiler_params=_cparams(("arbitrary", "arbitrary")),
        name="lru_mix_ln",
    )(h, rw, h, buf8, h0, win, *[W[n] for n in _LRU_WEIGHTS], ln_g, ln_b)
    return out, h_last, tail


def _xattn_kernel(h_ref, kv_ref, wq_ref, wo_ref, g_ref, b_ref, o_ref, *, alpha, heads):
    tm, d = h_ref.shape
    hd = d // heads
    row_groups = max(1, tm // V7X_MXU_DIM)
    tg = tm // row_groups
    for gi in range(row_groups):
        rows = slice(gi * tg, (gi + 1) * tg)
        h = h_ref[rows, :]
        q = jnp.dot(h.astype(BF16), wq_ref[...], preferred_element_type=F32)
        per_head = lambda x, off: jnp.stack([x[:, off + j * hd:off + (j + 1) * hd] for j in range(heads)])
        s = _bdot(per_head(q, 0), per_head(kv_ref, 0), _NT) * (hd ** -0.5)
        e = jnp.exp(s - jnp.max(s, axis=-1, keepdims=True))
        p = e * (1.0 / jnp.sum(e, axis=-1, keepdims=True))
        o = _bdot(p, per_head(kv_ref, d), _NN)
        out = jnp.dot(jnp.concatenate([o[j] for j in range(heads)], axis=1).astype(BF16), wo_ref[...],
                      preferred_element_type=F32)
        o_ref[rows, :] = _layer_norm(alpha * h + out, g_ref[...], b_ref[...])


def _xattn_rows_kernel(h_ref, mk_ref, mv_ref, wq_ref, wo_ref, g_ref, b_ref, o_ref, *, alpha):
    h = h_ref[...]
    nb, m, heads, hd = mk_ref.shape
    q = jnp.dot(h.astype(BF16), wq_ref[...], preferred_element_type=F32)
    col = lax.broadcasted_iota(jnp.int32, (heads, m * heads), 1)
    own = (col % heads) == lax.broadcasted_iota(jnp.int32, (heads, m * heads), 0)
    k2 = jnp.stack([mk_ref[i].reshape(m * heads, hd) for i in range(nb)])
    v2 = jnp.stack([mv_ref[i].reshape(m * heads, hd) for i in range(nb)])
    q4 = jnp.stack([jnp.concatenate([q[i:i + 1, j * hd:(j + 1) * hd] for j in range(heads)], axis=0)
                    for i in range(nb)])
    s = jnp.where(own, _bdot(q4, k2, _NT) * (hd ** -0.5), -jnp.inf)
    e = jnp.exp(s - jnp.max(s, axis=-1, keepdims=True))
    p = e / jnp.sum(e, axis=-1, keepdims=True)
    o4 = _bdot(p, v2, _NN)
    rows = [jnp.concatenate([o4[i, j:j + 1] for j in range(heads)], axis=1) for i in range(nb)]
    out = jnp.dot(jnp.concatenate(rows, axis=0).astype(BF16), wo_ref[...], preferred_element_type=F32)
    o_ref[...] = _layer_norm(alpha * h + out, g_ref[...], b_ref[...])


def _xattn_ln(h, kv, wq, wo, g, b, *, alpha, heads, rows_per_batch, tile_rows):
    n, d = h.shape
    assert n % tile_rows == 0 and rows_per_batch % tile_rows == 0
    per = rows_per_batch // tile_rows
    m = kv.shape[0] // (n // rows_per_batch)
    return pl.pallas_call(
        functools.partial(_xattn_kernel, alpha=alpha, heads=heads),
        grid=(n // tile_rows,),
        in_specs=[pl.BlockSpec((tile_rows, d), lambda i: (i, 0)),
                  pl.BlockSpec((m, 2 * d), lambda i: (i // per, 0)),
                  _const_spec(wq.shape), _const_spec(wo.shape), _const_spec(g.shape), _const_spec(b.shape)],
        out_specs=pl.BlockSpec((tile_rows, d), lambda i: (i, 0)),
        out_shape=jax.ShapeDtypeStruct((n, d), F32),
        compiler_params=_cparams(("parallel",)),
        name="xattn_ln",
    )(h, kv, wq, wo, g, b)


def _xattn_rows_ln(h, mk, mv, wq, wo, g, b, *, alpha, nb):
    n, d = h.shape
    _, m, heads, hd = mk.shape
    assert n % nb == 0
    kv_spec = pl.BlockSpec((nb, m, heads, hd), lambda i: (i, 0, 0, 0))
    return pl.pallas_call(
        functools.partial(_xattn_rows_kernel, alpha=alpha),
        grid=(n // nb,),
        in_specs=[pl.BlockSpec((nb, d), lambda i: (i, 0)), kv_spec, kv_spec,
                  _const_spec(wq.shape), _const_spec(wo.shape), _const_spec(g.shape), _const_spec(b.shape)],
        out_specs=pl.BlockSpec((nb, d), lambda i: (i, 0)),
        out_shape=jax.ShapeDtypeStruct((n, d), F32),
        compiler_params=_cparams(("parallel",)),
        name="xattn_rows_ln",
    )(h, mk, mv, wq, wo, g, b)


def _kv_proj_kernel(x_ref, w_ref, kv_ref, k_ref, v_ref):
    heads, hd = k_ref.shape[1:]
    d = heads * hd
    kv = jnp.dot(x_ref[...].astype(BF16), w_ref[...], preferred_element_type=F32)
    kv_ref[...] = kv
    for j in range(heads):
        k_ref[:, j, :] = kv[:, j * hd:(j + 1) * hd]
        v_ref[:, j, :] = kv[:, d + j * hd:d + (j + 1) * hd]


def _kv_proj(x, wkv, *, heads, tm):
    n, d = x.shape
    assert n % tm == 0
    hd = d // heads
    out = jax.ShapeDtypeStruct((n, heads, hd), F32)
    return pl.pallas_call(
        _kv_proj_kernel,
        grid=(n // tm,),
        in_specs=[pl.BlockSpec((tm, d), lambda i: (i, 0)), _const_spec(wkv.shape)],
        out_specs=[pl.BlockSpec((tm, 2 * d), lambda i: (i, 0))] + [pl.BlockSpec((tm, heads, hd), lambda i: (i, 0, 0))] * 2,
        out_shape=[jax.ShapeDtypeStruct((n, 2 * d), F32), out, out],
        compiler_params=_cparams(("parallel",)),
        name="kv_proj",
    )(x, wkv)


def _row(v):
    return v.reshape(1, -1)


def _block_diag_groups(w):
    n, c, _ = w.shape
    per = LRU_GROUP // c
    rows = jnp.concatenate([w.reshape(n // per, LRU_GROUP, c)] * per, axis=2)
    ri = lax.broadcasted_iota(jnp.int32, (LRU_GROUP, LRU_GROUP), 0) // c
    ci = lax.broadcasted_iota(jnp.int32, (LRU_GROUP, LRU_GROUP), 1) // c
    return jnp.where(ri == ci, rows, 0.0).astype(BF16)


def _prep_layer(l, ln_g, ln_b, ffn1_wi, ffn1_wo, ffn2_wi, ffn2_wo, w_in, shift_mu, decay_w0, decay_w2,
                aaa_a0, aaa_a2, gate_g2, k_k, k_a, r_k, gn_g, gn_b, conv_w, conv_b, lru_wr, lru_br,
                lru_wi, lru_bi, lru_lambda, w_mix_out, xa_wq, xa_wk, xa_wv, xa_wo):
    width = decay_w0.shape[1]
    rp = shift_mu.shape[1]
    d_ff = ffn1_wo.shape[1]
    bf = lambda w: w.astype(BF16)
    mu = shift_mu[l]
    return dict(
        ln_g=[_row(ln_g[l, i]) for i in range(4)], ln_b=[_row(ln_b[l, i]) for i in range(4)],
        ffn1=(bf(ffn1_wi[l][:, :d_ff]), bf(ffn1_wi[l][:, d_ff:]), bf(ffn1_wo[l])),
        ffn2=(bf(ffn2_wi[l][:, :d_ff]), bf(ffn2_wi[l][:, d_ff:]), bf(ffn2_wo[l])),
        w_in_a=bf(w_in[l][:, :rp]), w_in_b=bf(w_in[l][:, rp:]),
        shift_mu=_row(mu), mu_r=_row(mu[:width]), mu_k=_row(mu[width:2 * width]),
        mu_v=_row(mu[2 * width:3 * width]), mu_x=_row(mu[3 * width:]),
        decay_w0=_row(decay_w0[l]), decay_w2=bf(decay_w2[l]), aaa_a0=_row(aaa_a0[l]), aaa_a2=bf(aaa_a2[l]),
        gate_g2=bf(gate_g2[l]), k_k=_row(k_k[l]), k_a=_row(k_a[l]), r_k=_row(r_k[l]),
        gn_g=_row(gn_g[l]), gn_b=_row(gn_b[l]),
        conv_w=conv_w[l], conv_b=_row(conv_b[l]),
        lru_wr=_block_diag_groups(lru_wr[l]), lru_br=_row(lru_br[l]),
        lru_wi=_block_diag_groups(lru_wi[l]), lru_bi=_row(lru_bi[l]), lru_lambda=_row(lru_lambda[l]),
        w_mix_out=bf(w_mix_out[l]), xa_wq=bf(xa_wq[l]), xa_wo=bf(xa_wo[l]),
        xa_wkv=bf(jnp.concatenate([xa_wk[l], xa_wv[l]], axis=1)),
    )


def _tile(n, pref):
    return pref if n % pref == 0 else n


def _layer(h, mem, state, shift0, h0, buf0, W, *, alpha, xa_heads):
    B, T, d = h.shape
    n = B * T
    tm = _tile(n, 1024)
    h1 = _ffn_ln(h.reshape(n, d), *W['ffn1'], W['ln_g'][0], W['ln_b'][0], alpha=alpha, tm=tm)
    pa = _matmul(h1, W['w_in_a'], tm=tm, tn=W['w_in_a'].shape[1])
    width = W['decay_w0'].shape[1]
    lru_w = W['conv_b'].shape[1]
    pa3 = pa.reshape(B, T, -1)
    hist = CONV_WIDTH - 1
    if T > 1:
        assert state is None and T >= V7X_SUBLANES
        rw, s_new = _rwkv_chunked(pa3, shift0.reshape(B, 1, -1), W, tt=_tile(T, 1024), hw=_tile(width, 512))
        buf8 = jnp.concatenate([jnp.zeros((B, V7X_SUBLANES - hist, lru_w), F32), buf0], axis=1)
        h2, h_last, tail8 = _lru_mix_ln(h1.reshape(B, T, d), rw, buf8, h0.reshape(B, 1, lru_w), W,
                                        W['ln_g'][1], W['ln_b'][1], alpha=alpha, tt=_tile(T, 256))
        conv_in_tail = tail8[:, V7X_SUBLANES - hist:]
    else:
        pb = _matmul(h1, W['w_in_b'], tm=tm, tn=W['w_in_b'].shape[1])
        rw, s_new = _rwkv_step(pa, shift0, jnp.transpose(state, (1, 2, 3, 0)), W, hg=2)
        s_new = jnp.transpose(s_new, (3, 0, 1, 2))
        h2, h_last = _lru_step_mix_ln(pb, rw, h1, jnp.swapaxes(buf0, 0, 1), h0, W,
                                      W['ln_g'][1], W['ln_b'][1], alpha=alpha, tm=_tile(B, 128))
        conv_in_tail = jnp.concatenate([buf0[:, T:], pb[:, None, :lru_w]], axis=1)
    xa = (W['xa_wq'], W['xa_wo'], W['ln_g'][2], W['ln_b'][2])
    if T > 1:
        h3 = _xattn_ln(h2.reshape(n, d), mem, *xa, alpha=alpha, heads=xa_heads, rows_per_batch=T,
                       tile_rows=_tile(T, 1024))
    else:
        h3 = _xattn_rows_ln(h2, *mem, *xa, alpha=alpha, nb=_tile(B, V7X_SUBLANES))
    h4 = _ffn_ln(h3, *W['ffn2'], W['ln_g'][3], W['ln_b'][3], alpha=alpha, tm=tm)
    return h4.reshape(B, T, d), s_new, pa3[:, -1], h_last.reshape(B, lru_w), conv_in_tail


def kernel(x_prompt, x_sample, mem_prompt, cache_mem_k, cache_mem_v, state_rwkv, state_rwkv_shift, state_lru, state_conv, ln_g, ln_b, ffn1_wi, ffn1_wo, ffn2_wi, ffn2_wo, w_in, shift_mu, decay_w0, decay_w2, aaa_a0, aaa_a2, gate_g2, k_k, k_a, r_k, gn_g, gn_b, conv_w, conv_b, lru_wr, lru_br, lru_wi, lru_bi, lru_lambda, w_mix_out, xa_wq, xa_wk, xa_wv, xa_wo):
    depth = ln_g.shape[0]
    alpha = (2.0 * depth) ** 0.25
    B, _, d = x_prompt.shape
    n_mem, xa_heads, xa_head = cache_mem_k.shape[2:]
    rp = shift_mu.shape[1]
    lru_w = conv_b.shape[1]
    hp, hs = x_prompt, x_sample
    outs = [[] for _ in range(10)]
    for l in range(depth):
        W = _prep_layer(l, ln_g, ln_b, ffn1_wi, ffn1_wo, ffn2_wi, ffn2_wo, w_in, shift_mu, decay_w0, decay_w2,
                        aaa_a0, aaa_a2, gate_g2, k_k, k_a, r_k.reshape(depth, -1), gn_g, gn_b, conv_w, conv_b,
                        lru_wr, lru_br, lru_wi, lru_bi, lru_lambda, w_mix_out, xa_wq, xa_wk, xa_wv, xa_wo)
        kv, mk, mv = _kv_proj(mem_prompt.reshape(B * n_mem, d), W['xa_wkv'], heads=xa_heads,
                              tm=_tile(B * n_mem, 512))
        mk = mk.reshape(B, n_mem, xa_heads, xa_head)
        mv = mv.reshape(B, n_mem, xa_heads, xa_head)
        hp, S1, sh1, h1, b1 = _layer(
            hp, kv, None, jnp.zeros((B, rp), F32), jnp.zeros((B, lru_w), F32),
            jnp.zeros((B, CONV_WIDTH - 1, lru_w), F32), W, alpha=alpha, xa_heads=xa_heads)
        hs, S2, sh2, h2, b2 = _layer(
            hs, (cache_mem_k[l], cache_mem_v[l]),
            state_rwkv[l], state_rwkv_shift[l], state_lru[l], state_conv[l], W, alpha=alpha, xa_heads=xa_heads)
        for lst, val in zip(outs, (mk, mv, S1, sh1, h1, b1, S2, sh2, h2, b2)):
            lst.append(val)
    return (hp, hs) + tuple(jnp.stack(o) for o in outs)
```

```python
import functools

import jax
import jax.numpy as jnp
from jax import lax
from jax.experimental import pallas as pl
from jax.experimental.pallas import tpu as pltpu

F32 = jnp.float32
BF16 = jnp.bfloat16

RWKV_HEAD = 64
DECAY_LORA = 64
AAA_LORA = 64
GATE_LORA = 128
GN_EPS = 64e-5
CONV_WIDTH = 4
LRU_C = 8.0
LN_EPS = 1e-5

V7X_SUBLANES = 8
V7X_MXU_DIM = 256
V7X_SCOPED_VMEM_BYTES = 60000 * 1024

RWKV_CHUNK = 64
HEAD_PAIR = 2 * RWKV_HEAD
SCAN_ROW_GROUPS = 2
LRU_GROUP = V7X_MXU_DIM
STATE_RING = 3
LRU_PROJ_PIECES = 8


def _cparams(semantics):
    return pltpu.CompilerParams(dimension_semantics=semantics, vmem_limit_bytes=V7X_SCOPED_VMEM_BYTES)


def _const_spec(shape):
    zeros = (0,) * len(shape)
    return pl.BlockSpec(shape, lambda *_: zeros)


def _dot(a, b):
    return jnp.dot(a.astype(BF16), b.astype(BF16), preferred_element_type=F32)


def _dot_dims(a, b, dims):
    return lax.dot_general(a.astype(BF16), b.astype(BF16), (dims, ((), ())), preferred_element_type=F32)


_NN = ((1,), (0,))
_NT = ((1,), (1,))
_TN = ((0,), (0,))


def _split2(x):
    hi = x.astype(BF16)
    lo = (x - hi.astype(F32)).astype(BF16)
    return hi, lo


def _dot_exact_lhs(a_bf16, b):
    hi, lo = _split2(b)
    d = lambda y: jnp.dot(a_bf16, y, preferred_element_type=F32)
    return d(hi) + d(lo)


def _dot_exact_rhs(a, b_bf16):
    hi, lo = _split2(a)
    d = lambda x: jnp.dot(x, b_bf16, preferred_element_type=F32)
    return d(hi) + d(lo)


def _layer_norm(x, g, b):
    mu = jnp.mean(x, axis=-1, keepdims=True)
    xc = x - mu
    var = jnp.mean(xc * xc, axis=-1, keepdims=True)
    return xc * lax.rsqrt(var + LN_EPS) * g + b


def _softplus(z):
    return jnp.maximum(z, 0.0) + jnp.log(1.0 + jnp.exp(-jnp.abs(z)))


def _head_ones(width):
    r = lax.broadcasted_iota(jnp.int32, (width, width), 0) // RWKV_HEAD
    c = lax.broadcasted_iota(jnp.int32, (width, width), 1) // RWKV_HEAD
    return (r == c).astype(BF16)


def _mm_kernel(x_ref, w_ref, o_ref):
    o_ref[...] = jnp.dot(x_ref[...].astype(BF16), w_ref[...], preferred_element_type=F32)


def _matmul(x, w, *, tm, tn):
    n, k = x.shape
    m = w.shape[1]
    assert n % tm == 0 and m % tn == 0
    return pl.pallas_call(
        _mm_kernel,
        grid=(m // tn, n // tm),
        in_specs=[pl.BlockSpec((tm, k), lambda j, i: (i, 0)),
                  pl.BlockSpec((k, tn), lambda j, i: (0, j))],
        out_specs=pl.BlockSpec((tm, tn), lambda j, i: (i, j)),
        out_shape=jax.ShapeDtypeStruct((n, m), F32),
        compiler_params=_cparams(("parallel", "parallel")),
        name="matmul",
    )(x, w)


def _ffn_kernel(x_ref, wg_ref, wu_ref, wo_ref, g_ref, b_ref, o_ref, *, alpha, row_groups):
    tm = x_ref.shape[0]
    tg = tm // row_groups
    for gi in range(row_groups):
        rows = slice(gi * tg, (gi + 1) * tg)
        x = x_ref[rows, :]
        xb = x.astype(BF16)
        gate = jnp.dot(xb, wg_ref[...], preferred_element_type=F32)
        up = jnp.dot(xb, wu_ref[...], preferred_element_type=F32)
        mid = (gate * jax.nn.sigmoid(gate) * up).astype(BF16)
        down = jnp.dot(mid, wo_ref[...], preferred_element_type=F32)
        o_ref[rows, :] = _layer_norm(alpha * x + 0.5 * down, g_ref[...], b_ref[...])


def _ffn_ln(x, wg, wu, wo, g, b, *, alpha, tm):
    n, d = x.shape
    assert n % tm == 0
    row_groups = max(1, tm // V7X_MXU_DIM)
    resident = lambda w: pl.BlockSpec(w.shape, lambda i: (0, 0), pipeline_mode=pl.Buffered(1))
    return pl.pallas_call(
        functools.partial(_ffn_kernel, alpha=alpha, row_groups=row_groups),
        grid=(n // tm,),
        in_specs=[pl.BlockSpec((tm, d), lambda i: (i, 0)),
                  resident(wg), resident(wu), resident(wo),
                  _const_spec(g.shape), _const_spec(b.shape)],
        out_specs=pl.BlockSpec((tm, d), lambda i: (i, 0)),
        out_shape=jax.ShapeDtypeStruct((n, d), F32),
        compiler_params=_cparams(("parallel",)),
        name="ffn_ln",
    )(x, wg, wu, wo, g, b)


def _rwkv_pre(r, k, v, xx, w0, w2, a0, a2, g2, k_k, k_a, r_k, ones):
    xw = xx[:, :DECAY_LORA]
    xa = xx[:, DECAY_LORA:DECAY_LORA + AAA_LORA]
    xg = xx[:, DECAY_LORA + AAA_LORA:]
    z = w0 + _dot(jnp.tanh(xw), w2)
    lw = -jnp.exp(-_softplus(-z) - 0.5)
    a = jax.nn.sigmoid(a0 + _dot(xa, a2))
    g = _dot(jax.nn.sigmoid(xg), g2)
    kkr = k * k_k
    ss = _dot(kkr * kkr, ones)
    kk = kkr * lax.rsqrt(jnp.maximum(ss, 1e-24))
    kf = k * (1.0 + (a - 1.0) * k_a)
    bonus = _dot(r * kf * r_k, ones) * v
    return lw, a, g, kk, kf, bonus


def _rwkv_post(y, bonus, g, gn_g, gn_b, ones):
    inv_n = 1.0 / RWKV_HEAD
    ym = _dot_exact_rhs(y, ones) * inv_n
    yc = y - ym
    yv = _dot(yc * yc, ones) * inv_n
    yn = yc * lax.rsqrt(yv + GN_EPS) * gn_g + gn_b
    return (yn + bonus) * g


def _bdot(a, b, dims):
    dn = ((tuple(d + 1 for d in dims[0]), tuple(d + 1 for d in dims[1])), ((0,), (0,)))
    return lax.dot_general(a.astype(BF16), b.astype(BF16), dn, preferred_element_type=F32)


def _scan_operands(r, kf, v, kk, a, lw):
    tt, hw = r.shape
    C = RWKV_CHUNK
    n_pairs = hw // HEAD_PAIR
    ltri = (lax.broadcasted_iota(jnp.int32, (C, C), 0) >= lax.broadcasted_iota(jnp.int32, (C, C), 1)).astype(BF16)
    first = lax.broadcasted_iota(jnp.int32, (C, HEAD_PAIR), 1) < RWKV_HEAD

    def bd(x):
        return jnp.concatenate([jnp.where(first, x, 0.0), jnp.where(first, 0.0, x)], axis=1)

    names = ('a', 'r', 'b', 'k', 'v', 'bh', 'kh')
    ops = {n: [] for n in names}
    wcs = []
    for c in range(tt // C):
        rows = slice(c * C, (c + 1) * C)
        lw_c = lw[rows]
        L = _dot_exact_lhs(ltri, lw_c)
        Lc = L[C - 1:C, :]
        e_nl = jnp.exp(-L)
        e_c = jnp.exp(Lc - L)
        bb = kk[rows] * a[rows]
        bf = lambda x: x.astype(BF16)
        tile = dict(a=bf(-kk[rows] * jnp.exp(L - lw_c)), r=r[rows] * jnp.exp(L), b=bf(bb * e_nl),
                    k=bf(kf[rows] * e_nl), v=bf(v[rows]), bh=bf(bb * e_c), kh=bf(kf[rows] * e_c))
        wc = jnp.exp(Lc)
        for p in range(n_pairs):
            lanes = slice(p * HEAD_PAIR, (p + 1) * HEAD_PAIR)
            for n in names:
                ops[n].append(tile[n][:, lanes])
            wcs.append(wc[:, lanes])
    A, R, B, K, V, Bh, Kh = (jnp.stack(ops[n]) for n in names)
    Vbd = bd(V)
    G = _bdot(jnp.concatenate([A, R.astype(BF16)], axis=1), jnp.concatenate([bd(B), bd(K)], axis=1), _NT)
    tok = lax.broadcasted_iota(jnp.int32, (C, HEAD_PAIR), 0)
    src = lax.broadcasted_iota(jnp.int32, (C, HEAD_PAIR), 1) % RWKV_HEAD
    a_ab = jnp.where(tok > src, G[:, :C, :HEAD_PAIR], 0.0)
    a_ak = jnp.where(tok > src, G[:, :C, HEAD_PAIR:], 0.0).astype(BF16)
    a_rb = jnp.where(tok >= src, G[:, C:, :HEAD_PAIR], 0.0).astype(BF16)
    a_rk = jnp.where(tok >= src, G[:, C:, HEAD_PAIR:], 0.0).astype(BF16)
    P = jnp.where(tok == src, 1.0, 0.0) + a_ab
    N = a_ab.astype(BF16)
    N = _bdot(N, bd(N), _NN).astype(BF16)
    steps = 2
    while 2 * steps < C:
        NP = _bdot(jnp.concatenate([N, P.astype(BF16)], axis=1), bd(N), _NN)
        N = NP[:, :C].astype(BF16)
        P = P + NP[:, C:]
        steps *= 2
    P = (P + _bdot(P, bd(N), _NN)).astype(BF16)
    aV = _bdot(a_ak, Vbd, _NN).astype(BF16)
    XU = _bdot(P, jnp.concatenate([bd(A), bd(aV)], axis=2), _NN).astype(BF16)
    X1 = XU[:, :, :HEAD_PAIR]
    Uloc = XU[:, :, HEAD_PAIR:]
    Q = (R + _bdot(a_rb, bd(X1), _NN)).astype(BF16)
    Yloc = _bdot(jnp.concatenate([a_rb, a_rk], axis=2), jnp.concatenate([bd(Uloc), Vbd], axis=1), _NN)
    ri = lax.broadcasted_iota(jnp.int32, (HEAD_PAIR, HEAD_PAIR), 0) // RWKV_HEAD
    ci = lax.broadcasted_iota(jnp.int32, (HEAD_PAIR, HEAD_PAIR), 1) // RWKV_HEAD
    same_head = ri == ci
    Pm = jnp.where(same_head, _bdot(X1, Bh, _TN), 0.0).astype(BF16)
    Sloc = jnp.where(same_head, _bdot(jnp.concatenate([Uloc, V], axis=1), jnp.concatenate([Bh, Kh], axis=1), _TN),
                     0.0)
    return Q, Yloc, Pm, Sloc, jnp.stack(wcs)


def _rwkv_chunk_kernel(pr_ref, pk_ref, pv_ref, px_ref, sr_ref, sk_ref, sv_ref, sx_ref,
                       mur_ref, muk_ref, muv_ref, mux_ref, w0_ref, w2_ref, a0_ref, a2_ref, g2_ref,
                       kk_ref, ka_ref, rk_ref, gng_ref, gnb_ref, ones_ref,
                       out_ref, s_out_ref,
                       s_scr, cr_scr, ck_scr, cv_scr, cx_scr, y_scr):
    ti = pl.program_id(2)
    tt, hw = pr_ref.shape
    n_pairs = hw // HEAD_PAIR

    @pl.when(ti == 0)
    def _init():
        s_scr[...] = jnp.zeros(s_scr.shape, F32)
        cr_scr[...] = sr_ref[...]
        ck_scr[...] = sk_ref[...]
        cv_scr[...] = sv_ref[...]
        cx_scr[...] = sx_ref[...]

    S8 = V7X_SUBLANES

    def shifted(p_ref, c_scr, mu_ref):
        p = p_ref[...]
        first = lax.broadcasted_iota(jnp.int32, (S8, p.shape[1]), 0) == 0
        head = jnp.where(first, c_scr[...], pltpu.roll(p[:S8], 1, 0))
        prev = jnp.concatenate([head, p_ref[S8 - 1:tt - 1, :]], axis=0)
        c_scr[...] = p_ref[tt - 1:tt, :]
        return p + (prev - p) * mu_ref[...]

    r = shifted(pr_ref, cr_scr, mur_ref)
    k = shifted(pk_ref, ck_scr, muk_ref)
    v = shifted(pv_ref, cv_scr, muv_ref)
    xx = shifted(px_ref, cx_scr, mux_ref)
    ones = ones_ref[...]
    lw, a, g, kk, kf, bonus = _rwkv_pre(r, k, v, xx, w0_ref[...], w2_ref[...], a0_ref[...], a2_ref[...],
                                        g2_ref[...], kk_ref[...], ka_ref[...], rk_ref[...], ones)
    C = RWKV_CHUNK
    S = s_scr[...]
    n_groups = SCAN_ROW_GROUPS if tt % (SCAN_ROW_GROUPS * C) == 0 else 1
    tg = tt // n_groups
    for gi in range(n_groups):
        rows = slice(gi * tg, (gi + 1) * tg)
        Q, Yloc, Pm, Sloc, wc = _scan_operands(r[rows], kf[rows], v[rows], kk[rows], a[rows], lw[rows])
        for c in range(tg // C):
            inst = slice(c * n_pairs, (c + 1) * n_pairs)
            y_c = _bdot(Q[inst], S, _NT) + Yloc[inst]
            row0 = gi * tg + c * C
            for p in range(n_pairs):
                y_scr[row0:row0 + C, p * HEAD_PAIR:(p + 1) * HEAD_PAIR] = y_c[p]
            S = S * wc[inst] + _bdot(S, Pm[inst], _NN) + Sloc[inst]
    s_scr[...] = S
    out_ref[...] = _rwkv_post(y_scr[...], bonus, g, gng_ref[...], gnb_ref[...], ones)

    @pl.when(ti == pl.num_programs(2) - 1)
    def _emit_state():
        for p in range(n_pairs):
            s_out_ref[2 * p] = S[p, :RWKV_HEAD, :RWKV_HEAD]
            s_out_ref[2 * p + 1] = S[p, RWKV_HEAD:, RWKV_HEAD:]


def _rwkv_chunked(pa, shift0, W, *, tt, hw):
    B, T, _ = pa.shape
    width = W['decay_w0'].shape[1]
    heads = width // RWKV_HEAD
    assert T % tt == 0 and tt % RWKV_CHUNK == 0 and width % hw == 0 and hw % HEAD_PAIR == 0
    nb = width // hw
    xw = DECAY_LORA + AAA_LORA + GATE_LORA
    assert (3 * width) % xw == 0
    xblk = 3 * width // xw
    col = lambda off: (lambda b, h, t: (b, t, off + h))
    vec = lambda: pl.BlockSpec((1, hw), lambda b, h, t: (0, h))
    in_specs = [
        pl.BlockSpec((None, tt, hw), col(0)), pl.BlockSpec((None, tt, hw), col(nb)),
        pl.BlockSpec((None, tt, hw), col(2 * nb)), pl.BlockSpec((None, tt, xw), lambda b, h, t: (b, t, xblk)),
        pl.BlockSpec((None, 1, hw), lambda b, h, t: (b, 0, h)), pl.BlockSpec((None, 1, hw), lambda b, h, t: (b, 0, nb + h)),
        pl.BlockSpec((None, 1, hw), lambda b, h, t: (b, 0, 2 * nb + h)), pl.BlockSpec((None, 1, xw), lambda b, h, t: (b, 0, xblk)),
        vec(), vec(), vec(), _const_spec((1, xw)),
        vec(), pl.BlockSpec((DECAY_LORA, hw), lambda b, h, t: (0, h)),
        vec(), pl.BlockSpec((AAA_LORA, hw), lambda b, h, t: (0, h)),
        pl.BlockSpec((GATE_LORA, hw), lambda b, h, t: (0, h)),
        vec(), vec(), vec(), vec(), vec(), _const_spec((hw, hw)),
    ]
    out, s_new = pl.pallas_call(
        _rwkv_chunk_kernel,
        grid=(B, nb, T // tt),
        in_specs=in_specs,
        out_specs=[pl.BlockSpec((None, tt, hw), lambda b, h, t: (b, t, h)),
                   pl.BlockSpec((None, hw // RWKV_HEAD, RWKV_HEAD, RWKV_HEAD), lambda b, h, t: (b, h, 0, 0))],
        out_shape=[jax.ShapeDtypeStruct((B, T, width), F32),
                   jax.ShapeDtypeStruct((B, heads, RWKV_HEAD, RWKV_HEAD), F32)],
        scratch_shapes=[pltpu.VMEM((hw // HEAD_PAIR, HEAD_PAIR, HEAD_PAIR), F32),
                        pltpu.VMEM((1, hw), F32), pltpu.VMEM((1, hw), F32), pltpu.VMEM((1, hw), F32),
                        pltpu.VMEM((1, xw), F32), pltpu.VMEM((tt, hw), F32)],
        compiler_params=_cparams(("parallel", "parallel", "arbitrary")),
        name="rwkv_chunked",
    )(pa, pa, pa, pa, shift0, shift0, shift0, shift0,
      W['mu_r'], W['mu_k'], W['mu_v'], W['mu_x'], W['decay_w0'], W['decay_w2'], W['aaa_a0'], W['aaa_a2'],
      W['gate_g2'], W['k_k'], W['k_a'], W['r_k'], W['gn_g'], W['gn_b'], _head_ones(hw))
    return out, s_new


def _rwkv_step_kernel(p_ref, s0_ref, st_hbm, mu_ref, w0_ref, w2_ref, a0_ref, a2_ref, g2_ref,
                      kk_ref, ka_ref, rk_ref, gng_ref, gnb_ref, ones_ref,
                      out_ref, st_out_ref, vec_scr, y_scr, bonus_scr, g_scr, st_buf, st_sem):
    i = pl.program_id(0)
    B, width = out_ref.shape
    hg = st_out_ref.shape[0]
    n_steps = st_hbm.shape[0] // hg
    ahead = STATE_RING - 1
    H = RWKV_HEAD

    def fetch(step):
        slot = step % STATE_RING
        return pltpu.make_async_copy(st_hbm.at[pl.ds(step * hg, hg)], st_buf.at[slot], st_sem.at[slot])

    @pl.when(i == 0)
    def _prefill():
        for s in range(min(ahead, n_steps)):
            fetch(s).start()

    @pl.when(i + ahead < n_steps)
    def _fetch_ahead():
        fetch(i + ahead).start()

    @pl.when(i == 0)
    def _prologue():
        ones = ones_ref[...]
        p = p_ref[...]
        xs = p + (s0_ref[...] - p) * mu_ref[...]
        r, k, v, xx = xs[:, :width], xs[:, width:2 * width], xs[:, 2 * width:3 * width], xs[:, 3 * width:]
        lw, a, g, kk, kf, bonus = _rwkv_pre(r, k, v, xx, w0_ref[...], w2_ref[...], a0_ref[...], a2_ref[...],
                                            g2_ref[...], kk_ref[...], ka_ref[...], rk_ref[...], ones)
        for j, vec in enumerate((r, kf, v, kk, kk * a, jnp.exp(lw))):
            vec_scr[j] = vec.T
        bonus_scr[...] = bonus
        g_scr[...] = g

    fetch(i).wait()
    slot = i % STATE_RING
    for hl in range(hg):
        base = pl.multiple_of((i * hg + hl) * H, H)
        r_h, kf_h, kk_h, kka_h, w_h = (vec_scr[j, pl.ds(base, H), :] for j in (0, 1, 3, 4, 5))

        def body(vi, carry):
            S = st_buf[slot, hl, vi]
            sa = jnp.sum(S * kk_h, axis=0, keepdims=True)
            v_row = vec_scr[2, pl.ds(base + vi, 1), :]
            S2 = S * w_h - sa * kka_h + v_row * kf_h
            st_out_ref[hl, vi] = S2
            y_scr[pl.ds(base + vi, 1), :] = jnp.sum(S2 * r_h, axis=0, keepdims=True)
            return carry

        lax.fori_loop(0, H, body, 0, unroll=8)

    @pl.when(i == pl.num_programs(0) - 1)
    def _epilogue():
        out_ref[...] = _rwkv_post(y_scr[...].T, bonus_scr[...], g_scr[...], gng_ref[...], gnb_ref[...],
                                  ones_ref[...])


def _rwkv_step(pa, shift0, state_t, W, *, hg):
    B, proj = pa.shape
    width = W['decay_w0'].shape[1]
    heads = width // RWKV_HEAD
    assert heads % hg == 0 and state_t.shape == (heads, RWKV_HEAD, RWKV_HEAD, B)
    st_spec = pl.BlockSpec((hg, RWKV_HEAD, RWKV_HEAD, B), lambda i: (i, 0, 0, 0))
    names = ('decay_w0', 'decay_w2', 'aaa_a0', 'aaa_a2', 'gate_g2', 'k_k', 'k_a', 'r_k', 'gn_g', 'gn_b')
    out, st_new = pl.pallas_call(
        _rwkv_step_kernel,
        grid=(heads // hg,),
        in_specs=[_const_spec((B, proj)), _const_spec((B, proj)), pl.BlockSpec(memory_space=pl.ANY),
                  _const_spec(W['shift_mu'].shape)]
                 + [_const_spec(W[n].shape) for n in names] + [_const_spec((width, width))],
        out_specs=[_const_spec((B, width)), st_spec],
        out_shape=[jax.ShapeDtypeStruct((B, width), F32), jax.ShapeDtypeStruct(state_t.shape, F32)],
        scratch_shapes=[pltpu.VMEM((6, width, B), F32), pltpu.VMEM((width, B), F32),
                        pltpu.VMEM((B, width), F32), pltpu.VMEM((B, width), F32),
                        pltpu.VMEM((STATE_RING, hg, RWKV_HEAD, RWKV_HEAD, B), F32),
                        pltpu.SemaphoreType.DMA((STATE_RING,))],
        compiler_params=_cparams(("arbitrary",)),
        name="rwkv_step",
    )(pa, shift0, state_t, W['shift_mu'], *[W[n] for n in names], _head_ones(width))
    return out, st_new


def _lru_coeffs(xc, wr_ref, br_ref, wi_ref, bi_ref, lam_ref):
    xcb = xc.astype(BF16)
    n_grp = xc.shape[1] // LRU_GROUP
    grp = lambda w_ref: jnp.concatenate(
        [jnp.dot(xcb[:, i * LRU_GROUP:(i + 1) * LRU_GROUP], w_ref[i], preferred_element_type=F32)
         for i in range(n_grp)], axis=1)
    gr = jax.nn.sigmoid(grp(wr_ref) + br_ref[...])
    gi = jax.nn.sigmoid(grp(wi_ref) + bi_ref[...])
    a_t = jnp.exp(-LRU_C * gr * _softplus(-lam_ref[...]))
    return a_t, jnp.sqrt(1.0 - a_t * a_t) * gi * xc


def _lru_finish(hs, pg, g0, g1, rw, h, wmix_ref, lng_ref, lnb_ref, alpha):
    lru_out = hs * jax.nn.gelu(pg)
    merged = jax.nn.sigmoid(g0) * rw + jax.nn.sigmoid(g1) * lru_out
    mix = jnp.dot(merged.astype(BF16), wmix_ref[...], preferred_element_type=F32)
    return _layer_norm(alpha * h + mix, lng_ref[...], lnb_ref[...])


def _lru_kernel(hn_ref, rw_ref, h_ref, buf_ref, h0_ref, win_ref,
                cw_ref, cb_ref, wr_ref, br_ref, wi_ref, bi_ref, lam_ref, wmix_ref, lng_ref, lnb_ref,
                out_ref, hlast_ref, tail_ref, proj_scr, tail_scr, hc_scr, *, alpha):
    ti = pl.program_id(1)
    flat = pl.program_id(0) * pl.num_programs(1) + ti
    tt, width = rw_ref.shape
    S8 = V7X_SUBLANES
    project = lambda ref: jnp.dot(ref[...].astype(BF16), win_ref[...], preferred_element_type=F32)
    col = lambda j: slice(j * width, (j + 1) * width)

    @pl.when(ti == 0)
    def _init():
        tail_scr[...] = buf_ref[...]
        hc_scr[...] = h0_ref[...]

    @pl.when(flat == 0)
    def _first_projection():
        proj_scr[0] = project(h_ref)

    def step(cur_scr, nxt_scr):
        hn = hn_ref[...].astype(BF16)
        piece_w = win_ref.shape[1] // LRU_PROJ_PIECES
        pieces = iter(range(LRU_PROJ_PIECES))

        def emit(count=1):
            for _ in range(count):
                k = next(pieces, None)
                if k is not None:
                    cols = slice(k * piece_w, (k + 1) * piece_w)
                    nxt_scr[:, cols] = jnp.dot(hn, win_ref[:, cols], preferred_element_type=F32)

        x = cur_scr[:, col(0)]
        tail = tail_scr[...]
        r8 = lax.broadcasted_iota(jnp.int32, (S8, width), 0)

        def delayed(d):
            head = jnp.where(r8 < d, pltpu.roll(tail, d, 0), pltpu.roll(x[:S8], d, 0))
            return jnp.concatenate([head, cur_scr[S8 - d:tt - d, col(0)]], axis=0)

        cw = cw_ref[...]
        conv = cw[0:1] * delayed(CONV_WIDTH - 1)
        for j in range(1, CONV_WIDTH - 1):
            conv = conv + cw[j:j + 1] * delayed(CONV_WIDTH - 1 - j)
        xc = cb_ref[...] + (conv + cw[CONV_WIDTH - 1:CONV_WIDTH] * x)
        tail_scr[...] = x[tt - S8:, :]
        tail_ref[...] = x[tt - S8:, :]
        emit()

        A, Bv = _lru_coeffs(xc, wr_ref, br_ref, wi_ref, bi_ref, lam_ref)
        emit()
        A = A.reshape(tt // S8, S8, width)
        Bv = Bv.reshape(tt // S8, S8, width)
        in_grp = lax.broadcasted_iota(jnp.int32, (S8, width), 0)
        s = 1
        while s < S8:
            keep = in_grp >= s
            Bv = Bv + A * jnp.where(keep, pltpu.roll(Bv, s, 1), 0.0)
            A = A * jnp.where(keep, pltpu.roll(A, s, 1), 1.0)
            s *= 2
            emit()
        carry = hc_scr[...]
        groups = []
        for gi in range(tt // S8):
            hg = Bv[gi] + A[gi] * carry
            groups.append(hg)
            carry = hg[S8 - 1:S8, :]
        hs = jnp.concatenate(groups, axis=0)
        hc_scr[...] = carry
        hlast_ref[...] = carry
        emit()
        lru_out = hs * jax.nn.gelu(cur_scr[:, col(1)])
        emit()
        merged = jax.nn.sigmoid(cur_scr[:, col(2)]) * rw_ref[...] + jax.nn.sigmoid(cur_scr[:, col(3)]) * lru_out
        emit()
        mix = jnp.dot(merged.astype(BF16), wmix_ref[...], preferred_element_type=F32)
        out_ref[...] = _layer_norm(alpha * h_ref[...] + mix, lng_ref[...], lnb_ref[...])
        emit(LRU_PROJ_PIECES)

    slot = flat % 2
    step(proj_scr.at[slot], proj_scr.at[1 - slot])


def _lru_step_kernel(x_ref, pg_ref, g0_ref, g1_ref, rw_ref, h_ref, buf_ref, h0_ref,
                     cw_ref, cb_ref, wr_ref, br_ref, wi_ref, bi_ref, lam_ref, wmix_ref, lng_ref, lnb_ref,
                     out_ref, hnew_ref, *, alpha):
    cw = cw_ref[...]
    conv = cw[0:1] * buf_ref[0]
    for j in range(1, CONV_WIDTH - 1):
        conv = conv + cw[j:j + 1] * buf_ref[j]
    xc = cb_ref[...] + (conv + cw[CONV_WIDTH - 1:CONV_WIDTH] * x_ref[...])
    A, Bv = _lru_coeffs(xc, wr_ref, br_ref, wi_ref, bi_ref, lam_ref)
    hs = Bv + A * h0_ref[...]
    hnew_ref[...] = hs
    out_ref[...] = _lru_finish(hs, pg_ref[...], g0_ref[...], g1_ref[...], rw_ref[...], h_ref[...],
                               wmix_ref, lng_ref, lnb_ref, alpha)


_LRU_WEIGHTS = ('conv_w', 'conv_b', 'lru_wr', 'lru_br', 'lru_wi', 'lru_bi', 'lru_lambda', 'w_mix_out')


def _lru_step_mix_ln(pb, rw, h, buf, h0, W, ln_g, ln_b, *, alpha, tm):
    B, width = rw.shape
    assert B % tm == 0
    blk = lambda j: pl.BlockSpec((tm, width), lambda i: (i, j))
    out, h_new = pl.pallas_call(
        functools.partial(_lru_step_kernel, alpha=alpha),
        grid=(B // tm,),
        in_specs=[blk(0), blk(1), blk(2), blk(3), blk(0), blk(0),
                  pl.BlockSpec((CONV_WIDTH - 1, tm, width), lambda i: (0, i, 0)), blk(0)]
                 + [_const_spec(W[n].shape) for n in _LRU_WEIGHTS] + [_const_spec(ln_g.shape), _const_spec(ln_b.shape)],
        out_specs=[blk(0), blk(0)],
        out_shape=[jax.ShapeDtypeStruct((B, width), F32), jax.ShapeDtypeStruct((B, width), F32)],
        compiler_params=_cparams(("parallel",)),
        name="lru_step_mix_ln",
    )(pb, pb, pb, pb, rw, h, buf, h0, *[W[n] for n in _LRU_WEIGHTS], ln_g, ln_b)
    return out, h_new


def _lru_mix_ln(h, rw, buf8, h0, W, ln_g, ln_b, *, alpha, tt):
    B, T, width = rw.shape
    assert T % tt == 0 and tt % V7X_SUBLANES == 0 and h.shape[2] == width
    nt = T // tt
    tile = pl.BlockSpec((None, tt, width), lambda b, t: (b, t, 0))

    def next_tile(b, t):
        f = jnp.minimum(b * nt + t + 1, B * nt - 1)
        return f // nt, f % nt, 0

    nxt = pl.BlockSpec((None, tt, width), next_tile)
    per_b = lambda rows: pl.BlockSpec((None, rows, width), lambda b, t: (b, 0, 0))
    win = W['w_in_b']
    out, h_last, tail = pl.pallas_call(
        functools.partial(_lru_kernel, alpha=alpha),
        grid=(B, nt),
        in_specs=[nxt, tile, tile, per_b(V7X_SUBLANES), per_b(1), _const_spec(win.shape)]
                 + [_const_spec(W[n].shape) for n in _LRU_WEIGHTS] + [_const_spec(ln_g.shape), _const_spec(ln_b.shape)],
        out_specs=[tile, per_b(1), per_b(V7X_SUBLANES)],
        out_shape=[jax.ShapeDtypeStruct((B, T, width), F32), jax.ShapeDtypeStruct((B, 1, width), F32),
                   jax.ShapeDtypeStruct((B, V7X_SUBLANES, width), F32)],
        scratch_shapes=[pltpu.VMEM((2, tt, win.shape[1]), F32), pltpu.VMEM((V7X_SUBLANES, width), F32),
                        pltpu.VMEM((1, width), F32)],
        compiler_params=_cparams(("arbitrary", "arbitrary")),
        name="lru_mix_ln",
    )(h, rw, h, buf8, h0, win, *[W[n] for n in _LRU_WEIGHTS], ln_g, ln_b)
    return out, h_last, tail


def _xattn_kernel(h_ref, kv_ref, wq_ref, wo_ref, g_ref, b_ref, o_ref, *, alpha, heads):
    tm, d = h_ref.shape
    hd = d // heads
    row_groups = max(1, tm // V7X_MXU_DIM)
    tg = tm // row_groups
    for gi in range(row_groups):
        rows = slice(gi * tg, (gi + 1) * tg)
        h = h_ref[rows, :]
        q = jnp.dot(h.astype(BF16), wq_ref[...], preferred_element_type=F32)
        per_head = lambda x, off: jnp.stack([x[:, off + j * hd:off + (j + 1) * hd] for j in range(heads)])
        s = _bdot(per_head(q, 0), per_head(kv_ref, 0), _NT) * (hd ** -0.5)
        e = jnp.exp(s - jnp.max(s, axis=-1, keepdims=True))
        p = e * (1.0 / jnp.sum(e, axis=-1, keepdims=True))
        o = _bdot(p, per_head(kv_ref, d), _NN)
        out = jnp.dot(jnp.concatenate([o[j] for j in range(heads)], axis=1).astype(BF16), wo_ref[...],
                      preferred_element_type=F32)
        o_ref[rows, :] = _layer_norm(alpha * h + out, g_ref[...], b_ref[...])


def _xattn_rows_kernel(h_ref, mk_hbm, mv_hbm, wq_ref, wo_ref, g_ref, b_ref, o_ref, mk_buf, mv_buf, kv_sem, *, alpha):
    step = pl.program_id(0)
    _, nb, m, heads, hd = mk_buf.shape
    n_steps = mk_hbm.shape[0] // nb
    ahead = STATE_RING - 1

    def fetch(s):
        slot = s % STATE_RING
        return [pltpu.make_async_copy(src.at[pl.ds(s * nb, nb)], dst.at[slot], kv_sem.at[j, slot])
                for j, (src, dst) in enumerate(((mk_hbm, mk_buf), (mv_hbm, mv_buf)))]

    @pl.when(step == 0)
    def _prefill():
        for s in range(min(ahead, n_steps)):
            for c in fetch(s):
                c.start()

    @pl.when(step + ahead < n_steps)
    def _fetch_ahead():
        for c in fetch(step + ahead):
            c.start()

    h = h_ref[...]
    q = jnp.dot(h.astype(BF16), wq_ref[...], preferred_element_type=F32)
    for c in fetch(step):
        c.wait()
    mk_ref = mk_buf.at[step % STATE_RING]
    mv_ref = mv_buf.at[step % STATE_RING]
    col = lax.broadcasted_iota(jnp.int32, (heads, m * heads), 1)
    own = (col % heads) == lax.broadcasted_iota(jnp.int32, (heads, m * heads), 0)
    k2 = jnp.stack([mk_ref[i].reshape(m * heads, hd) for i in range(nb)])
    v2 = jnp.stack([mv_ref[i].reshape(m * heads, hd) for i in range(nb)])
    q4 = jnp.stack([jnp.concatenate([q[i:i + 1, j * hd:(j + 1) * hd] for j in range(heads)], axis=0)
                    for i in range(nb)])
    s = jnp.where(own, _bdot(q4, k2, _NT) * (hd ** -0.5), -jnp.inf)
    e = jnp.exp(s - jnp.max(s, axis=-1, keepdims=True))
    p = e / jnp.sum(e, axis=-1, keepdims=True)
    o4 = _bdot(p, v2, _NN)
    rows = [jnp.concatenate([o4[i, j:j + 1] for j in range(heads)], axis=1) for i in range(nb)]
    out = jnp.dot(jnp.concatenate(rows, axis=0).astype(BF16), wo_ref[...], preferred_element_type=F32)
    o_ref[...] = _layer_norm(alpha * h + out, g_ref[...], b_ref[...])


def _xattn_ln(h, kv, wq, wo, g, b, *, alpha, heads, rows_per_batch, tile_rows):
    n, d = h.shape
    assert n % tile_rows == 0 and rows_per_batch % tile_rows == 0
    per = rows_per_batch // tile_rows
    m = kv.shape[0] // (n // rows_per_batch)
    return pl.pallas_call(
        functools.partial(_xattn_kernel, alpha=alpha, heads=heads),
        grid=(n // tile_rows,),
        in_specs=[pl.BlockSpec((tile_rows, d), lambda i: (i, 0)),
                  pl.BlockSpec((m, 2 * d), lambda i: (i // per, 0)),
                  _const_spec(wq.shape), _const_spec(wo.shape), _const_spec(g.shape), _const_spec(b.shape)],
        out_specs=pl.BlockSpec((tile_rows, d), lambda i: (i, 0)),
        out_shape=jax.ShapeDtypeStruct((n, d), F32),
        compiler_params=_cparams(("parallel",)),
        name="xattn_ln",
    )(h, kv, wq, wo, g, b)


def _xattn_rows_ln(h, mk, mv, wq, wo, g, b, *, alpha, nb):
    n, d = h.shape
    _, m, heads, hd = mk.shape
    assert n % nb == 0
    hbm = pl.BlockSpec(memory_space=pl.ANY)
    return pl.pallas_call(
        functools.partial(_xattn_rows_kernel, alpha=alpha),
        grid=(n // nb,),
        in_specs=[pl.BlockSpec((nb, d), lambda i: (i, 0)), hbm, hbm,
                  _const_spec(wq.shape), _const_spec(wo.shape), _const_spec(g.shape), _const_spec(b.shape)],
        out_specs=pl.BlockSpec((nb, d), lambda i: (i, 0)),
        out_shape=jax.ShapeDtypeStruct((n, d), F32),
        scratch_shapes=[pltpu.VMEM((STATE_RING, nb, m, heads, hd), F32), pltpu.VMEM((STATE_RING, nb, m, heads, hd), F32),
                        pltpu.SemaphoreType.DMA((2, STATE_RING))],
        compiler_params=_cparams(("arbitrary",)),
        name="xattn_rows_ln",
    )(h, mk, mv, wq, wo, g, b)


def _kv_proj_kernel(x_ref, w_ref, kv_ref, k_ref, v_ref):
    heads, hd = k_ref.shape[1:]
    d = heads * hd
    kv = jnp.dot(x_ref[...].astype(BF16), w_ref[...], preferred_element_type=F32)
    kv_ref[...] = kv
    for j in range(heads):
        k_ref[:, j, :] = kv[:, j * hd:(j + 1) * hd]
        v_ref[:, j, :] = kv[:, d + j * hd:d + (j + 1) * hd]


def _kv_proj(x, wkv, *, heads, tm):
    n, d = x.shape
    assert n % tm == 0
    hd = d // heads
    out = jax.ShapeDtypeStruct((n, heads, hd), F32)
    return pl.pallas_call(
        _kv_proj_kernel,
        grid=(n // tm,),
        in_specs=[pl.BlockSpec((tm, d), lambda i: (i, 0)), _const_spec(wkv.shape)],
        out_specs=[pl.BlockSpec((tm, 2 * d), lambda i: (i, 0))] + [pl.BlockSpec((tm, heads, hd), lambda i: (i, 0, 0))] * 2,
        out_shape=[jax.ShapeDtypeStruct((n, 2 * d), F32), out, out],
        compiler_params=_cparams(("parallel",)),
        name="kv_proj",
    )(x, wkv)


def _row(v):
    return v.reshape(1, -1)


def _block_diag_groups(w):
    n, c, _ = w.shape
    per = LRU_GROUP // c
    rows = jnp.concatenate([w.reshape(n // per, LRU_GROUP, c)] * per, axis=2)
    ri = lax.broadcasted_iota(jnp.int32, (LRU_GROUP, LRU_GROUP), 0) // c
    ci = lax.broadcasted_iota(jnp.int32, (LRU_GROUP, LRU_GROUP), 1) // c
    return jnp.where(ri == ci, rows, 0.0).astype(BF16)


def _prep_layer(l, ln_g, ln_b, ffn1_wi, ffn1_wo, ffn2_wi, ffn2_wo, w_in, shift_mu, decay_w0, decay_w2,
                aaa_a0, aaa_a2, gate_g2, k_k, k_a, r_k, gn_g, gn_b, conv_w, conv_b, lru_wr, lru_br,
                lru_wi, lru_bi, lru_lambda, w_mix_out, xa_wq, xa_wk, xa_wv, xa_wo):
    width = decay_w0.shape[1]
    rp = shift_mu.shape[1]
    d_ff = ffn1_wo.shape[1]
    bf = lambda w: w.astype(BF16)
    mu = shift_mu[l]
    return dict(
        ln_g=[_row(ln_g[l, i]) for i in range(4)], ln_b=[_row(ln_b[l, i]) for i in range(4)],
        ffn1=(bf(ffn1_wi[l][:, :d_ff]), bf(ffn1_wi[l][:, d_ff:]), bf(ffn1_wo[l])),
        ffn2=(bf(ffn2_wi[l][:, :d_ff]), bf(ffn2_wi[l][:, d_ff:]), bf(ffn2_wo[l])),
        w_in_a=bf(w_in[l][:, :rp]), w_in_b=bf(w_in[l][:, rp:]),
        shift_mu=_row(mu), mu_r=_row(mu[:width]), mu_k=_row(mu[width:2 * width]),
        mu_v=_row(mu[2 * width:3 * width]), mu_x=_row(mu[3 * width:]),
        decay_w0=_row(decay_w0[l]), decay_w2=bf(decay_w2[l]), aaa_a0=_row(aaa_a0[l]), aaa_a2=bf(aaa_a2[l]),
        gate_g2=bf(gate_g2[l]), k_k=_row(k_k[l]), k_a=_row(k_a[l]), r_k=_row(r_k[l]),
        gn_g=_row(gn_g[l]), gn_b=_row(gn_b[l]),
        conv_w=conv_w[l], conv_b=_row(conv_b[l]),
        lru_wr=_block_diag_groups(lru_wr[l]), lru_br=_row(lru_br[l]),
        lru_wi=_block_diag_groups(lru_wi[l]), lru_bi=_row(lru_bi[l]), lru_lambda=_row(lru_lambda[l]),
        w_mix_out=bf(w_mix_out[l]), xa_wq=bf(xa_wq[l]), xa_wo=bf(xa_wo[l]),
        xa_wkv=bf(jnp.concatenate([xa_wk[l], xa_wv[l]], axis=1)),
    )


def _tile(n, pref):
    return pref if n % pref == 0 else n


def _layer(h, mem, state, shift0, h0, buf0, W, *, alpha, xa_heads):
    B, T, d = h.shape
    n = B * T
    tm = _tile(n, 1024)
    h1 = _ffn_ln(h.reshape(n, d), *W['ffn1'], W['ln_g'][0], W['ln_b'][0], alpha=alpha, tm=tm)
    pa = _matmul(h1, W['w_in_a'], tm=tm, tn=W['w_in_a'].shape[1])
    width = W['decay_w0'].shape[1]
    lru_w = W['conv_b'].shape[1]
    pa3 = pa.reshape(B, T, -1)
    hist = CONV_WIDTH - 1
    if T > 1:
        assert state is None and T >= V7X_SUBLANES
        rw, s_new = _rwkv_chunked(pa3, shift0.reshape(B, 1, -1), W, tt=_tile(T, 1024), hw=_tile(width, 512))
        buf8 = jnp.concatenate([jnp.zeros((B, V7X_SUBLANES - hist, lru_w), F32), buf0], axis=1)
        h2, h_last, tail8 = _lru_mix_ln(h1.reshape(B, T, d), rw, buf8, h0.reshape(B, 1, lru_w), W,
                                        W['ln_g'][1], W['ln_b'][1], alpha=alpha, tt=_tile(T, 256))
        conv_in_tail = tail8[:, V7X_SUBLANES - hist:]
    else:
        pb = _matmul(h1, W['w_in_b'], tm=tm, tn=W['w_in_b'].shape[1])
        rw, s_new = _rwkv_step(pa, shift0, jnp.transpose(state, (1, 2, 3, 0)), W, hg=2)
        s_new = jnp.transpose(s_new, (3, 0, 1, 2))
        h2, h_last = _lru_step_mix_ln(pb, rw, h1, jnp.swapaxes(buf0, 0, 1), h0, W,
                                      W['ln_g'][1], W['ln_b'][1], alpha=alpha, tm=_tile(B, 128))
        conv_in_tail = jnp.concatenate([buf0[:, T:], pb[:, None, :lru_w]], axis=1)
    xa = (W['xa_wq'], W['xa_wo'], W['ln_g'][2], W['ln_b'][2])
    if T > 1:
        h3 = _xattn_ln(h2.reshape(n, d), mem, *xa, alpha=alpha, heads=xa_heads, rows_per_batch=T,
                       tile_rows=_tile(T, 1024))
    else:
        h3 = _xattn_rows_ln(h2, *mem, *xa, alpha=alpha, nb=_tile(B, V7X_SUBLANES))
    h4 = _ffn_ln(h3, *W['ffn2'], W['ln_g'][3], W['ln_b'][3], alpha=alpha, tm=tm)
    return h4.reshape(B, T, d), s_new, pa3[:, -1], h_last.reshape(B, lru_w), conv_in_tail


def kernel(x_prompt, x_sample, mem_prompt, cache_mem_k, cache_mem_v, state_rwkv, state_rwkv_shift, state_lru, state_conv, ln_g, ln_b, ffn1_wi, ffn1_wo, ffn2_wi, ffn2_wo, w_in, shift_mu, decay_w0, decay_w2, aaa_a0, aaa_a2, gate_g2, k_k, k_a, r_k, gn_g, gn_b, conv_w, conv_b, lru_wr, lru_br, lru_wi, lru_bi, lru_lambda, w_mix_out, xa_wq, xa_wk, xa_wv, xa_wo):
    depth = ln_g.shape[0]
    alpha = (2.0 * depth) ** 0.25
    B, _, d = x_prompt.shape
    n_mem, xa_heads, xa_head = cache_mem_k.shape[2:]
    rp = shift_mu.shape[1]
    lru_w = conv_b.shape[1]
    hp, hs = x_prompt, x_sample
    outs = [[] for _ in range(10)]
    for l in range(depth):
        W = _prep_layer(l, ln_g, ln_b, ffn1_wi, ffn1_wo, ffn2_wi, ffn2_wo, w_in, shift_mu, decay_w0, decay_w2,
                        aaa_a0, aaa_a2, gate_g2, k_k, k_a, r_k.reshape(depth, -1), gn_g, gn_b, conv_w, conv_b,
                        lru_wr, lru_br, lru_wi, lru_bi, lru_lambda, w_mix_out, xa_wq, xa_wk, xa_wv, xa_wo)
        kv, mk, mv = _kv_proj(mem_prompt.reshape(B * n_mem, d), W['xa_wkv'], heads=xa_heads,
                              tm=_tile(B * n_mem, 512))
        mk = mk.reshape(B, n_mem, xa_heads, xa_head)
        mv = mv.reshape(B, n_mem, xa_heads, xa_head)
        hp, S1, sh1, h1, b1 = _layer(
            hp, kv, None, jnp.zeros((B, rp), F32), jnp.zeros((B, lru_w), F32),
            jnp.zeros((B, CONV_WIDTH - 1, lru_w), F32), W, alpha=alpha, xa_heads=xa_heads)
        hs, S2, sh2, h2, b2 = _layer(
            hs, (cache_mem_k[l], cache_mem_v[l]),
            state_rwkv[l], state_rwkv_shift[l], state_lru[l], state_conv[l], W, alpha=alpha, xa_heads=xa_heads)
        for lst, val in zip(outs, (mk, mv, S1, sh1, h1, b1, S2, sh2, h2, b2)):
            lst.append(val)
    return (hp, hs) + tuple(jnp.stack(o) for o in outs)
```
